```python
import math
import jax, jax.numpy as jnp
from jax import lax
import numpy as np

D_MODEL = 1024
BATCH = 1
SEQ = 16384
DEPTH = 1

D_MIX = D_MODEL
CONV_WIDTH = D_MIX // 2
ATTN_WIDTH = D_MIX - CONV_WIDTH
DIFF_HEAD_DIM = 64
N_DIFF_HEADS = ATTN_WIDTH // (2 * DIFF_HEAD_DIM)
ROT_DIM = DIFF_HEAD_DIM // 4
ROPE_THETA = 500000.0
CONV_KERNEL = 31
D_FF = 2816
Q_BLOCK = 128
N_SUBLAYERS = 3
N_MOD = 3 * N_SUBLAYERS
IN_COLS = 2 * CONV_WIDTH + 3 * ATTN_WIDTH
ALPHA = (2.0 * DEPTH) ** 0.25
BETA = (8.0 * DEPTH) ** -0.25
LN_EPS = 1e-5

kernel_name = "hybrid_conformer_diffattn_macaron_deepnorm_adaln"


def layer_norm(x, g, b):
    xf = x.astype(jnp.float32)
    mu = jnp.mean(xf, axis=-1, keepdims=True)
    var = jnp.mean(jnp.square(xf - mu), axis=-1, keepdims=True)
    y = (xf - mu) * lax.rsqrt(var + LN_EPS)
    return (y * g.astype(jnp.float32) + b.astype(jnp.float32)).astype(x.dtype)


def rms_norm(x, g):
    xf = x.astype(jnp.float32)
    y = xf * lax.rsqrt(jnp.mean(jnp.square(xf), axis=-1, keepdims=True) + LN_EPS)
    return (y * g.astype(jnp.float32)).astype(x.dtype)


def swiglu(h, w_in, w_out):
    gate, up = jnp.split(h @ w_in, 2, axis=-1)
    return (jax.nn.silu(gate) * up) @ w_out


def partial_rope(t, cos, sin):
    rot, rest = t[..., :ROT_DIM], t[..., ROT_DIM:]
    r1, r2 = rot[..., :ROT_DIM // 2], rot[..., ROT_DIM // 2:]
    cos = cos.astype(t.dtype)[None, :, None, None, :]
    sin = sin.astype(t.dtype)[None, :, None, None, :]
    rot = jnp.concatenate([r1 * cos - r2 * sin, r2 * cos + r1 * sin], axis=-1)
    return jnp.concatenate([rot, rest], axis=-1)


def conformer_conv(glu_a, glu_b, conv_w, conv_b, ln_g, ln_b):
    u = glu_a * jax.nn.sigmoid(glu_b)
    pad = (CONV_KERNEL - 1) // 2
    u = lax.conv_general_dilated(
        u, conv_w[:, None, :].astype(u.dtype), window_strides=(1,),
        padding=[(pad, pad)], dimension_numbers=("NWC", "WIO", "NWC"),
        feature_group_count=CONV_WIDTH) + conv_b
    return jax.nn.silu(layer_norm(u, ln_g, ln_b))


def diff_attention(q, k, v, lq1, lk1, lq2, lk2, subln_g, cos, sin, lam_init):
    B, S, _ = q.shape
    H, Dh = N_DIFF_HEADS, DIFF_HEAD_DIM
    q = partial_rope(q.reshape(B, S, H, 2, Dh), cos, sin)
    k = partial_rope(k.reshape(B, S, H, 2, Dh), cos, sin)
    v = v.reshape(B, S, H, 2 * Dh).transpose(0, 2, 1, 3)
    q = q.transpose(0, 2, 3, 1, 4)
    k = k.transpose(0, 2, 3, 1, 4)
    lam = (jnp.exp(jnp.sum(lq1.astype(jnp.float32) * lk1.astype(jnp.float32)))
           - jnp.exp(jnp.sum(lq2.astype(jnp.float32) * lk2.astype(jnp.float32)))
           + lam_init)
    scale = 1.0 / math.sqrt(Dh)
    n_blk = S // Q_BLOCK
    q_blocks = jnp.moveaxis(q.reshape(B, H, 2, n_blk, Q_BLOCK, Dh), 3, 0)

    def attend(qb):
        s = jnp.einsum("bhmqd,bhmkd->bhmqk", qb, k).astype(jnp.float32) * scale
        p = jax.nn.softmax(s, axis=-1)
        a = p[:, :, 0] - lam * p[:, :, 1]
        return jnp.einsum("bhqk,bhkd->bhqd", a.astype(v.dtype), v)

    out = lax.map(attend, q_blocks)
    out = out.transpose(1, 0, 3, 2, 4).reshape(B, S, H, 2 * Dh)
    out = rms_norm(out, subln_g) * (1.0 - lam_init)
    return out.reshape(B, S, H * 2 * Dh)


def hybrid_mixer(h, w_in, conv_w, conv_b, conv_ln_g, conv_ln_b,
                 lq1, lk1, lq2, lk2, subln_g, w_out, cos, sin, lam_init):
    proj = h @ w_in
    c0 = CONV_WIDTH
    glu_a, glu_b, q, k, v = jnp.split(
        proj, [c0, 2 * c0, 2 * c0 + ATTN_WIDTH, 2 * c0 + 2 * ATTN_WIDTH], axis=-1)
    conv_out = conformer_conv(glu_a, glu_b, conv_w, conv_b, conv_ln_g, conv_ln_b)
    attn_out = diff_attention(q, k, v, lq1, lk1, lq2, lk2, subln_g, cos, sin, lam_init)
    return jnp.concatenate([conv_out, attn_out], axis=-1) @ w_out


def post_norm_sublayer(x, mod, weight, f, g, b):
    shift, scale, gate = mod[:, 0, None, :], mod[:, 1, None, :], mod[:, 2, None, :]
    y = f(x * (1.0 + scale) + shift)
    return layer_norm(ALPHA * x + weight * (1.0 + gate) * y, g, b)


def setup_inputs(seed: int = 0) -> dict:
    key = jax.random.key(seed)
    ks = jax.random.split(key, 32)
    L, D = DEPTH, D_MODEL

    def nrm(k, shape, s):
        return jax.random.normal(k, shape, jnp.float32) * s

    def gain(k, shape):
        return 1.0 + nrm(k, shape, 0.02)

    return {
        "x": nrm(ks[0], (BATCH, SEQ, D), 1.0),
        "c": nrm(ks[1], (BATCH, D), 1.0),
        "w_ada": nrm(ks[2], (L, D, N_MOD * D), 0.1 * D ** -0.5),
        "b_ada": nrm(ks[3], (L, N_MOD * D), 0.01),
        "ffn1_w_in": nrm(ks[4], (L, D, 2 * D_FF), D ** -0.5),
        "ffn1_w_out": nrm(ks[5], (L, D_FF, D), BETA * D_FF ** -0.5),
        "ln1_g": gain(ks[6], (L, D)),
        "ln1_b": nrm(ks[7], (L, D), 0.02),
        "mix_w_in": nrm(ks[8], (L, D, IN_COLS), D ** -0.5),
        "conv_w": nrm(ks[9], (L, CONV_KERNEL, CONV_WIDTH), CONV_KERNEL ** -0.5),
        "conv_b": nrm(ks[10], (L, CONV_WIDTH), 0.02),
        "conv_ln_g": gain(ks[11], (L, CONV_WIDTH)),
        "conv_ln_b": nrm(ks[12], (L, CONV_WIDTH), 0.02),
        "lambda_q1": nrm(ks[13], (L, DIFF_HEAD_DIM), 0.1),
        "lambda_k1": nrm(ks[14], (L, DIFF_HEAD_DIM), 0.1),
        "lambda_q2": nrm(ks[15], (L, DIFF_HEAD_DIM), 0.1),
        "lambda_k2": nrm(ks[16], (L, DIFF_HEAD_DIM), 0.1),
        "subln_g": gain(ks[17], (L, 2 * DIFF_HEAD_DIM)),
        "mix_w_out": nrm(ks[18], (L, D_MIX, D), BETA * D_MIX ** -0.5),
        "ln2_g": gain(ks[19], (L, D)),
        "ln2_b": nrm(ks[20], (L, D), 0.02),
        "ffn2_w_in": nrm(ks[21], (L, D, 2 * D_FF), D ** -0.5),
        "ffn2_w_out": nrm(ks[22], (L, D_FF, D), BETA * D_FF ** -0.5),
        "ln3_g": gain(ks[23], (L, D)),
        "ln3_b": nrm(ks[24], (L, D), 0.02),
    }


def reference(x, c, w_ada, b_ada, ffn1_w_in, ffn1_w_out, ln1_g, ln1_b,
              mix_w_in, conv_w, conv_b, conv_ln_g, conv_ln_b,
              lambda_q1, lambda_k1, lambda_q2, lambda_k2, subln_g, mix_w_out, ln2_g, ln2_b,
              ffn2_w_in, ffn2_w_out, ln3_g, ln3_b):
    B, S, D = x.shape
    pos = jnp.arange(S, dtype=jnp.float32)
    inv_freq = ROPE_THETA ** (-jnp.arange(0, ROT_DIM, 2, dtype=jnp.float32) / ROT_DIM)
    ang = pos[:, None] * inv_freq[None, :]
    cos, sin = jnp.cos(ang), jnp.sin(ang)
    c_act = jax.nn.silu(c)

    for l in range(DEPTH):
        lam_init = 0.8 - 0.6 * math.exp(-0.3 * l)
        mod = (c_act @ w_ada[l] + b_ada[l]).reshape(B, N_SUBLAYERS, 3, D)
        x = post_norm_sublayer(
            x, mod[:, 0], 0.5,
            lambda h: swiglu(h, ffn1_w_in[l], ffn1_w_out[l]), ln1_g[l], ln1_b[l])
        x = post_norm_sublayer(
            x, mod[:, 1], 1.0,
            lambda h: hybrid_mixer(h, mix_w_in[l], conv_w[l], conv_b[l], conv_ln_g[l], conv_ln_b[l],
                                   lambda_q1[l], lambda_k1[l], lambda_q2[l], lambda_k2[l],
                                   subln_g[l], mix_w_out[l], cos, sin, lam_init),
            ln2_g[l], ln2_b[l])
        x = post_norm_sublayer(
            x, mod[:, 2], 0.5,
            lambda h: swiglu(h, ffn2_w_in[l], ffn2_w_out[l]), ln3_g[l], ln3_b[l])
    return x
```

```python
import functools
import math

import jax
import jax.numpy as jnp
from jax import lax
from jax.experimental import pallas as pl
from jax.experimental.pallas import tpu as pltpu

F32 = jnp.float32
BF16 = jnp.bfloat16

DEPTH = 1
ALPHA = (2.0 * DEPTH) ** 0.25
LN_EPS = 1e-5
DIFF_HEAD_DIM = 64
ROT_DIM = DIFF_HEAD_DIM // 4
ROPE_THETA = 500000.0
CONV_KERNEL = 31
CONV_PAD = (CONV_KERNEL - 1) // 2
CONV_HALO = 16
LANES = 128
VMEM_LIMIT = 56 * 1024 * 1024


def _sigmoid(x):
    return 1.0 / (1.0 + jnp.exp(-x))


def _layer_norm(z, g, b):
    mu = jnp.mean(z, axis=-1, keepdims=True)
    zc = z - mu
    var = jnp.mean(zc * zc, axis=-1, keepdims=True)
    return zc * lax.rsqrt(var + LN_EPS) * g + b


def _modulate(x, mod_ref, sub):
    shift = mod_ref[3 * sub:3 * sub + 1, :]
    scale = mod_ref[3 * sub + 1:3 * sub + 2, :]
    return x * (1.0 + scale) + shift


def _const_spec(shape):
    return pl.BlockSpec(shape, lambda *_: (0,) * len(shape), pipeline_mode=pl.Buffered(1))


def _ada_kernel(c_ref, w_ref, b_ref, o_ref):
    c = c_ref[...]
    ca = c * _sigmoid(c)
    o_ref[...] = jnp.sum(ca * w_ref[...], axis=0, keepdims=True) + b_ref[...]


def _ada(c_col, w, b_row, tn):
    d, n = w.shape
    return pl.pallas_call(
        _ada_kernel,
        grid=(n // tn,),
        in_specs=[pl.BlockSpec((d, 1), lambda j: (0, 0)),
                  pl.BlockSpec((d, tn), lambda j: (0, j)),
                  pl.BlockSpec((1, tn), lambda j: (0, j))],
        out_specs=pl.BlockSpec((1, tn), lambda j: (0, j)),
        out_shape=jax.ShapeDtypeStruct((1, n), F32),
        compiler_params=pltpu.CompilerParams(dimension_semantics=("arbitrary",),
                                             vmem_limit_bytes=VMEM_LIMIT),
        name="ada",
    )(c_col, w, b_row)


def _ffn_kernel(x_ref, mod_ref, win_ref, wout_ref, g_ref, b_ref, o_ref, act_ref, *, sub, weight, tf):
    x = x_ref[...]
    d_ff = wout_ref.shape[0]
    h = _modulate(x, mod_ref, sub).astype(BF16)
    for c in range(d_ff // tf):
        gate = jnp.dot(h, win_ref[:, c * tf:(c + 1) * tf], preferred_element_type=F32)
        up = jnp.dot(h, win_ref[:, d_ff + c * tf:d_ff + (c + 1) * tf], preferred_element_type=F32)
        act_ref[:, c * tf:(c + 1) * tf] = (gate * _sigmoid(gate) * up).astype(BF16)
    y = jnp.dot(act_ref[...], wout_ref[...], preferred_element_type=F32)
    gate_c = mod_ref[3 * sub + 2:3 * sub + 3, :]
    z = ALPHA * x + weight * (1.0 + gate_c) * y
    o_ref[...] = _layer_norm(z, g_ref[...], b_ref[...])


def _ffn(x, mod9, w_in, w_out, g, b, *, sub, weight, tm, tf):
    s, d = x.shape
    d_ff = w_out.shape[0]
    return pl.pallas_call(
        functools.partial(_ffn_kernel, sub=sub, weight=weight, tf=tf),
        grid=(s // tm,),
        in_specs=[pl.BlockSpec((tm, d), lambda i: (i, 0)),
                  _const_spec(mod9.shape),
                  _const_spec(w_in.shape),
                  _const_spec(w_out.shape),
                  _const_spec(g.shape),
                  _const_spec(b.shape)],
        out_specs=pl.BlockSpec((tm, d), lambda i: (i, 0)),
        out_shape=jax.ShapeDtypeStruct((s, d), F32),
        scratch_shapes=[pltpu.VMEM((tm, d_ff), BF16)],
        compiler_params=pltpu.CompilerParams(dimension_semantics=("arbitrary",),
                                             vmem_limit_bytes=VMEM_LIMIT),
        name=f"ffn{sub}",
    )(x, mod9, w_in, w_out, g, b)


def _mix_in_kernel(x_ref, mod_ref, w_ref, rc_ref, rs1_ref, rs2_ref,
                   u_ref, q_ref, k_ref, v_ref, *, sub, cw, aw, q_scale):
    h = _modulate(x_ref[...], mod_ref, sub).astype(BF16)
    proj = jnp.dot(h, w_ref[...], preferred_element_type=F32)
    u_ref[...] = proj[:, :cw] * _sigmoid(proj[:, cw:2 * cw])
    rc, rs1, rs2 = rc_ref[...], rs1_ref[...], rs2_ref[...]
    half = ROT_DIM // 2

    def rope(t):
        return t * rc + pltpu.roll(t, LANES - half, 1) * rs1 + pltpu.roll(t, half, 1) * rs2

    q0, k0, v0 = 2 * cw, 2 * cw + aw, 2 * cw + 2 * aw
    for g in range(aw // LANES):
        sl = slice(g * LANES, (g + 1) * LANES)
        q_ref[:, sl] = (rope(proj[:, q0 + g * LANES:q0 + (g + 1) * LANES]) * q_scale).astype(BF16)
        k_ref[:, sl] = rope(proj[:, k0 + g * LANES:k0 + (g + 1) * LANES]).astype(BF16)
    v_ref[...] = proj[:, v0:].astype(BF16)


def _mix_in(x, mod9, w, rc, rs1, rs2, *, cw, aw, q_scale, tm):
    s, d = x.shape
    row = lambda i: (i, 0)
    return pl.pallas_call(
        functools.partial(_mix_in_kernel, sub=1, cw=cw, aw=aw, q_scale=q_scale),
        grid=(s // tm,),
        in_specs=[pl.BlockSpec((tm, d), row),
                  _const_spec(mod9.shape),
                  _const_spec(w.shape),
                  pl.BlockSpec((tm, LANES), row),
                  pl.BlockSpec((tm, LANES), row),
                  pl.BlockSpec((tm, LANES), row)],
        out_specs=[pl.BlockSpec((tm, cw), row),
                   pl.BlockSpec((tm, aw), row),
                   pl.BlockSpec((tm, aw), row),
                   pl.BlockSpec((tm, aw), row)],
        out_shape=[jax.ShapeDtypeStruct((s, cw), F32),
                   jax.ShapeDtypeStruct((s, aw), BF16),
                   jax.ShapeDtypeStruct((s, aw), BF16),
                   jax.ShapeDtypeStruct((s, aw), BF16)],
        compiler_params=pltpu.CompilerParams(dimension_semantics=("arbitrary",),
                                             vmem_limit_bytes=VMEM_LIMIT),
        name="mix_in",
    )(x, mod9, w, rc, rs1, rs2)


def _conv_kernel(prev_ref, cur_ref, next_ref, w_ref, cb_ref, g_ref, b_ref, o_ref, ext_ref, y_ref, *, rows):
    i = pl.program_id(0)
    tm, cw = cur_ref.shape
    ext_ref[0:CONV_HALO, :] = jnp.where(i > 0, prev_ref[...], 0.0)
    ext_ref[CONV_HALO:CONV_HALO + tm, :] = cur_ref[...]
    ext_ref[CONV_HALO + tm:, :] = jnp.where(i < pl.num_programs(0) - 1, next_ref[...], 0.0)
    base = CONV_HALO - CONV_PAD
    for lc in range(cw // LANES):
        ls = slice(lc * LANES, (lc + 1) * LANES)
        for rc in range(tm // rows):
            r0 = rc * rows
            acc = jnp.zeros((rows, LANES), F32)
            for t in range(CONV_KERNEL):
                acc = acc + ext_ref[r0 + base + t:r0 + base + t + rows, ls] * w_ref[t:t + 1, ls]
            y_ref[r0:r0 + rows, ls] = acc
    y = _layer_norm(y_ref[...] + cb_ref[...], g_ref[...], b_ref[...])
    o_ref[...] = (y * _sigmoid(y)).astype(BF16)


def _conv(u, w, cb, g, b, *, tm, rows):
    s, cw = u.shape
    nh = tm // CONV_HALO
    last = s // CONV_HALO - 1
    return pl.pallas_call(
        functools.partial(_conv_kernel, rows=rows),
        grid=(s // tm,),
        in_specs=[pl.BlockSpec((CONV_HALO, cw), lambda i: (jnp.maximum(i * nh - 1, 0), 0)),
                  pl.BlockSpec((tm, cw), lambda i: (i, 0)),
                  pl.BlockSpec((CONV_HALO, cw), lambda i: (jnp.minimum((i + 1) * nh, last), 0)),
                  _const_spec(w.shape),
                  _const_spec(cb.shape),
                  _const_spec(g.shape),
                  _const_spec(b.shape)],
        out_specs=pl.BlockSpec((tm, cw), lambda i: (i, 0)),
        out_shape=jax.ShapeDtypeStruct((s, cw), BF16),
        scratch_shapes=[pltpu.VMEM((tm + 2 * CONV_HALO, cw), F32),
                        pltpu.VMEM((tm, cw), F32)],
        compiler_params=pltpu.CompilerParams(dimension_semantics=("arbitrary",),
                                             vmem_limit_bytes=VMEM_LIMIT),
        name="conv",
    )(u, u, u, w, cb, g, b)


def _attn_kernel(qt_ref, k_ref, vt_ref, lq1_ref, lk1_ref, lq2_ref, lk2_ref, g_ref, o_ref,
                 rhs_ref, acc_ref, *, tk, lam_init):
    hd2, tq = qt_ref.shape[1], qt_ref.shape[2]
    n_kv = k_ref.shape[0] // tk
    qt = qt_ref[0]
    row = lax.broadcasted_iota(jnp.int32, qt.shape, 0)
    zero = jnp.zeros_like(qt)
    rhs_ref[:, :tq] = jnp.where(row < DIFF_HEAD_DIM, qt, zero)
    rhs_ref[:, tq:] = jnp.where(row >= DIFF_HEAD_DIM, qt, zero)
    acc_ref[...] = jnp.zeros_like(acc_ref)

    def body(j, carry):
        m, l = carry
        k = k_ref[pl.ds(pl.multiple_of(j * tk, tk), tk), :]
        s = jnp.dot(k, rhs_ref[...], preferred_element_type=F32)
        m_new = jnp.maximum(m, jnp.max(s, axis=0, keepdims=True))
        alpha = jnp.exp2(m - m_new)
        e = jnp.exp2(s - m_new)
        l = alpha * l + jnp.sum(e, axis=0, keepdims=True)
        pv = jnp.dot(vt_ref[0, j], e.astype(BF16), preferred_element_type=F32)
        acc_ref[...] = alpha * acc_ref[...] + pv
        return m_new, l

    init = (jnp.full((1, 2 * tq), -jnp.inf, F32), jnp.zeros((1, 2 * tq), F32))
    _, l = lax.fori_loop(0, n_kv, body, init)
    o = acc_ref[...] * (1.0 / l)
    lam = (jnp.exp(jnp.sum(lq1_ref[...] * lk1_ref[...])) - jnp.exp(jnp.sum(lq2_ref[...] * lk2_ref[...]))
           + lam_init)
    o = o[:, :tq] - lam * o[:, tq:]
    ms = jnp.mean(o * o, axis=0, keepdims=True)
    o = o * lax.rsqrt(ms + LN_EPS) * g_ref[...] * (1.0 - lam_init)
    o_ref[...] = o.T.astype(BF16)


def _attn(qt, k, vt, lq1, lk1, lq2, lk2, g_col, *, tq, tk, lam_init):
    n_heads, hd2, s = qt.shape
    lam_spec = _const_spec(lq1.shape)
    return pl.pallas_call(
        functools.partial(_attn_kernel, tk=tk, lam_init=lam_init),
        grid=(n_heads, s // tq),
        in_specs=[pl.BlockSpec((1, hd2, tq), lambda h, i: (h, 0, i)),
                  pl.BlockSpec((s, hd2), lambda h, i: (0, h)),
                  pl.BlockSpec((1, s // tk, hd2, tk), lambda h, i: (h, 0, 0, 0)),
                  lam_spec, lam_spec, lam_spec, lam_spec,
                  _const_spec(g_col.shape)],
        out_specs=pl.BlockSpec((tq, hd2), lambda h, i: (i, h)),
        out_shape=jax.ShapeDtypeStruct((s, n_heads * hd2), BF16),
        scratch_shapes=[pltpu.VMEM((hd2, 2 * tq), BF16),
                        pltpu.VMEM((hd2, 2 * tq), F32)],
        compiler_params=pltpu.CompilerParams(dimension_semantics=("arbitrary", "arbitrary"),
                                             vmem_limit_bytes=VMEM_LIMIT),
        name="attn",
    )(qt, k, vt, lq1, lk1, lq2, lk2, g_col)


def _mix_out_kernel(conv_ref, attn_ref, x_ref, mod_ref, w_ref, g_ref, b_ref, o_ref, *, sub):
    cw = conv_ref.shape[1]
    y = (jnp.dot(conv_ref[...], w_ref[:cw, :], preferred_element_type=F32)
         + jnp.dot(attn_ref[...], w_ref[cw:, :], preferred_element_type=F32))
    gate_c = mod_ref[3 * sub + 2:3 * sub + 3, :]
    z = ALPHA * x_ref[...] + (1.0 + gate_c) * y
    o_ref[...] = _layer_norm(z, g_ref[...], b_ref[...])


def _mix_out(conv, attn, x, mod9, w, g, b, *, tm):
    s, d = x.shape
    row = lambda i: (i, 0)
    return pl.pallas_call(
        functools.partial(_mix_out_kernel, sub=1),
        grid=(s // tm,),
        in_specs=[pl.BlockSpec((tm, conv.shape[1]), row),
                  pl.BlockSpec((tm, attn.shape[1]), row),
                  pl.BlockSpec((tm, d), row),
                  _const_spec(mod9.shape),
                  _const_spec(w.shape),
                  _const_spec(g.shape),
                  _const_spec(b.shape)],
        out_specs=pl.BlockSpec((tm, d), row),
        out_shape=jax.ShapeDtypeStruct((s, d), F32),
        compiler_params=pltpu.CompilerParams(dimension_semantics=("arbitrary",),
                                             vmem_limit_bytes=VMEM_LIMIT),
        name="mix_out",
    )(conv, attn, x, mod9, w, g, b)


def _rope_tables(s):
    pos = jnp.arange(s, dtype=F32)
    inv_freq = ROPE_THETA ** (-jnp.arange(0, ROT_DIM, 2, dtype=F32) / ROT_DIM)
    ang = pos[:, None] * inv_freq[None, :]
    cos, sin = jnp.cos(ang), jnp.sin(ang)
    half = ROT_DIM // 2
    pad = DIFF_HEAD_DIM - ROT_DIM
    ones = jnp.ones((s, pad), F32)
    zeros_h = jnp.zeros((s, half), F32)
    zeros_p = jnp.zeros((s, pad), F32)
    reps = LANES // DIFF_HEAD_DIM
    rc = jnp.tile(jnp.concatenate([cos, cos, ones], axis=1), (1, reps))
    rs1 = jnp.tile(jnp.concatenate([-sin, zeros_h, zeros_p], axis=1), (1, reps))
    rs2 = jnp.tile(jnp.concatenate([zeros_h, sin, zeros_p], axis=1), (1, reps))
    return rc, rs1, rs2


def kernel(x, c, w_ada, b_ada, ffn1_w_in, ffn1_w_out, ln1_g, ln1_b, mix_w_in, conv_w, conv_b, conv_ln_g,
           conv_ln_b, lambda_q1, lambda_k1, lambda_q2, lambda_k2, subln_g, mix_w_out, ln2_g, ln2_b,
           ffn2_w_in, ffn2_w_out, ln3_g, ln3_b):
    batch, s, d = x.shape
    assert batch == 1 and w_ada.shape[0] == DEPTH == 1
    cw = conv_w.shape[2]
    aw = (mix_w_in.shape[2] - 2 * cw) // 3
    hd2 = 2 * DIFF_HEAD_DIM
    n_heads = aw // hd2
    lam_init = 0.8 - 0.6 * math.exp(-0.3 * 0)
    q_scale = math.log2(math.e) / math.sqrt(DIFF_HEAD_DIM)
    tq, tk = 256, 512

    mod9 = _ada(c.reshape(d, 1), w_ada[0], b_ada, tn=1152).reshape(9, d)
    x0 = x[0]
    x1 = _ffn(x0, mod9, ffn1_w_in[0].astype(BF16), ffn1_w_out[0].astype(BF16), ln1_g, ln1_b,
              sub=0, weight=0.5, tm=512, tf=256)

    rc, rs1, rs2 = _rope_tables(s)
    u, q, k, v = _mix_in(x1, mod9, mix_w_in[0].astype(BF16), rc, rs1, rs2,
                         cw=cw, aw=aw, q_scale=q_scale, tm=512)
    conv = _conv(u, conv_w[0], conv_b, conv_ln_g, conv_ln_b, tm=256, rows=128)
    qt = q.reshape(s, n_heads, hd2).transpose(1, 2, 0)
    vt = v.reshape(s // tk, tk, n_heads, hd2).transpose(2, 0, 3, 1)
    attn = _attn(qt, k, vt, lambda_q1, lambda_k1, lambda_q2, lambda_k2, subln_g.reshape(hd2, 1),
                 tq=tq, tk=tk, lam_init=lam_init)
    x2 = _mix_out(conv, attn, x1, mod9, mix_w_out[0].astype(BF16), ln2_g, ln2_b, tm=512)

    x3 = _ffn(x2, mod9, ffn2_w_in[0].astype(BF16), ffn2_w_out[0].astype(BF16), ln3_g, ln3_b,
              sub=2, weight=0.5, tm=512, tf=256)
    return x3[None]
```

```python
import functools
import math

import jax
import jax.numpy as jnp
from jax import lax
from jax.experimental import pallas as pl
from jax.experimental.pallas import tpu as pltpu

F32 = jnp.float32
BF16 = jnp.bfloat16

DEPTH = 1
ALPHA = (2.0 * DEPTH) ** 0.25
LN_EPS = 1e-5
DIFF_HEAD_DIM = 64
ROT_DIM = DIFF_HEAD_DIM // 4
ROPE_THETA = 500000.0
CONV_KERNEL = 31
CONV_PAD = (CONV_KERNEL - 1) // 2
CONV_HALO = 16
LANES = 128
VMEM_LIMIT = 56 * 1024 * 1024


def _sigmoid(x):
    return 1.0 / (1.0 + jnp.exp(-x))


def _layer_norm(z, g, b):
    mu = jnp.mean(z, axis=-1, keepdims=True)
    zc = z - mu
    var = jnp.mean(zc * zc, axis=-1, keepdims=True)
    return zc * lax.rsqrt(var + LN_EPS) * g + b


def _modulate(x, mod_ref, sub):
    shift = mod_ref[3 * sub:3 * sub + 1, :]
    scale = mod_ref[3 * sub + 1:3 * sub + 2, :]
    return x * (1.0 + scale) + shift


def _const_spec(shape):
    return pl.BlockSpec(shape, lambda *_: (0,) * len(shape), pipeline_mode=pl.Buffered(1))


def _ada_kernel(c_ref, w_ref, b_ref, o_ref):
    c = c_ref[...]
    ca = c * _sigmoid(c)
    o_ref[...] = jnp.sum(ca * w_ref[...], axis=0, keepdims=True) + b_ref[...]


def _ada(c_col, w, b_row, tn):
    d, n = w.shape
    return pl.pallas_call(
        _ada_kernel,
        grid=(n // tn,),
        in_specs=[pl.BlockSpec((d, 1), lambda j: (0, 0)),
                  pl.BlockSpec((d, tn), lambda j: (0, j)),
                  pl.BlockSpec((1, tn), lambda j: (0, j))],
        out_specs=pl.BlockSpec((1, tn), lambda j: (0, j)),
        out_shape=jax.ShapeDtypeStruct((1, n), F32),
        compiler_params=pltpu.CompilerParams(dimension_semantics=("arbitrary",),
                                             vmem_limit_bytes=VMEM_LIMIT),
        name="ada",
    )(c_col, w, b_row)


def _ffn_kernel(x_ref, mod_ref, win_ref, wout_ref, g_ref, b_ref, o_ref, act_ref, *, sub, weight, tf):
    x = x_ref[...]
    d_ff = wout_ref.shape[0]
    h = _modulate(x, mod_ref, sub).astype(BF16)
    for c in range(d_ff // tf):
        gate = jnp.dot(h, win_ref[:, c * tf:(c + 1) * tf], preferred_element_type=F32)
        up = jnp.dot(h, win_ref[:, d_ff + c * tf:d_ff + (c + 1) * tf], preferred_element_type=F32)
        act_ref[:, c * tf:(c + 1) * tf] = (gate * _sigmoid(gate) * up).astype(BF16)
    y = jnp.dot(act_ref[...], wout_ref[...], preferred_element_type=F32)
    gate_c = mod_ref[3 * sub + 2:3 * sub + 3, :]
    z = ALPHA * x + weight * (1.0 + gate_c) * y
    o_ref[...] = _layer_norm(z, g_ref[...], b_ref[...])


def _ffn(x, mod9, w_in, w_out, g, b, *, sub, weight, tm, tf):
    s, d = x.shape
    d_ff = w_out.shape[0]
    return pl.pallas_call(
        functools.partial(_ffn_kernel, sub=sub, weight=weight, tf=tf),
        grid=(s // tm,),
        in_specs=[pl.BlockSpec((tm, d), lambda i: (i, 0)),
                  _const_spec(mod9.shape),
                  _const_spec(w_in.shape),
                  _const_spec(w_out.shape),
                  _const_spec(g.shape),
                  _const_spec(b.shape)],
        out_specs=pl.BlockSpec((tm, d), lambda i: (i, 0)),
        out_shape=jax.ShapeDtypeStruct((s, d), F32),
        scratch_shapes=[pltpu.VMEM((tm, d_ff), BF16)],
        compiler_params=pltpu.CompilerParams(dimension_semantics=("arbitrary",),
                                             vmem_limit_bytes=VMEM_LIMIT),
        name=f"ffn{sub}",
    )(x, mod9, w_in, w_out, g, b)


def _mix_in_kernel(x_ref, mod_ref, w_ref, rc_ref, rs1_ref, rs2_ref,
                   u_ref, q_ref, k_ref, v_ref, *, sub, cw, aw, q_scale):
    h = _modulate(x_ref[...], mod_ref, sub).astype(BF16)
    proj = jnp.dot(h, w_ref[...], preferred_element_type=F32)
    u_ref[...] = proj[:, :cw] * _sigmoid(proj[:, cw:2 * cw])
    rc, rs1, rs2 = rc_ref[...], rs1_ref[...], rs2_ref[...]
    half = ROT_DIM // 2

    def rope(t):
        return t * rc + pltpu.roll(t, LANES - half, 1) * rs1 + pltpu.roll(t, half, 1) * rs2

    q0, k0, v0 = 2 * cw, 2 * cw + aw, 2 * cw + 2 * aw
    for g in range(aw // LANES):
        sl = slice(g * LANES, (g + 1) * LANES)
        q_ref[:, sl] = (rope(proj[:, q0 + g * LANES:q0 + (g + 1) * LANES]) * q_scale).astype(BF16)
        k_ref[:, sl] = rope(proj[:, k0 + g * LANES:k0 + (g + 1) * LANES]).astype(BF16)
    v_ref[...] = proj[:, v0:].astype(BF16)


def _mix_in(x, mod9, w, rc, rs1, rs2, *, cw, aw, q_scale, tm):
    s, d = x.shape
    row = lambda i: (i, 0)
    return pl.pallas_call(
        functools.partial(_mix_in_kernel, sub=1, cw=cw, aw=aw, q_scale=q_scale),
        grid=(s // tm,),
        in_specs=[pl.BlockSpec((tm, d), row),
                  _const_spec(mod9.shape),
                  _const_spec(w.shape),
                  pl.BlockSpec((tm, LANES), row),
                  pl.BlockSpec((tm, LANES), row),
                  pl.BlockSpec((tm, LANES), row)],
        out_specs=[pl.BlockSpec((tm, cw), row),
                   pl.BlockSpec((tm, aw), row),
                   pl.BlockSpec((tm, aw), row),
                   pl.BlockSpec((tm, aw), row)],
        out_shape=[jax.ShapeDtypeStruct((s, cw), F32),
                   jax.ShapeDtypeStruct((s, aw), BF16),
                   jax.ShapeDtypeStruct((s, aw), BF16),
                   jax.ShapeDtypeStruct((s, aw), BF16)],
        compiler_params=pltpu.CompilerParams(dimension_semantics=("arbitrary",),
                                             vmem_limit_bytes=VMEM_LIMIT),
        name="mix_in",
    )(x, mod9, w, rc, rs1, rs2)


def _conv_kernel(prev_ref, cur_ref, next_ref, w_ref, cb_ref, g_ref, b_ref, o_ref, ext_ref, y_ref, *, rows):
    i = pl.program_id(0)
    tm, cw = cur_ref.shape
    ext_ref[0:CONV_HALO, :] = jnp.where(i > 0, prev_ref[...], 0.0)
    ext_ref[CONV_HALO:CONV_HALO + tm, :] = cur_ref[...]
    ext_ref[CONV_HALO + tm:, :] = jnp.where(i < pl.num_programs(0) - 1, next_ref[...], 0.0)
    base = CONV_HALO - CONV_PAD
    for lc in range(cw // LANES):
        ls = slice(lc * LANES, (lc + 1) * LANES)
        for rc in range(tm // rows):
            r0 = rc * rows
            acc = jnp.zeros((rows, LANES), F32)
            for t in range(CONV_KERNEL):
                acc = acc + ext_ref[r0 + base + t:r0 + base + t + rows, ls] * w_ref[t:t + 1, ls]
            y_ref[r0:r0 + rows, ls] = acc
    y = _layer_norm(y_ref[...] + cb_ref[...], g_ref[...], b_ref[...])
    o_ref[...] = (y * _sigmoid(y)).astype(BF16)


def _conv(u, w, cb, g, b, *, tm, rows):
    s, cw = u.shape
    nh = tm // CONV_HALO
    last = s // CONV_HALO - 1
    return pl.pallas_call(
        functools.partial(_conv_kernel, rows=rows),
        grid=(s // tm,),
        in_specs=[pl.BlockSpec((CONV_HALO, cw), lambda i: (jnp.maximum(i * nh - 1, 0), 0)),
                  pl.BlockSpec((tm, cw), lambda i: (i, 0)),
                  pl.BlockSpec((CONV_HALO, cw), lambda i: (jnp.minimum((i + 1) * nh, last), 0)),
                  _const_spec(w.shape),
                  _const_spec(cb.shape),
                  _const_spec(g.shape),
                  _const_spec(b.shape)],
        out_specs=pl.BlockSpec((tm, cw), lambda i: (i, 0)),
        out_shape=jax.ShapeDtypeStruct((s, cw), BF16),
        scratch_shapes=[pltpu.VMEM((tm + 2 * CONV_HALO, cw), F32),
                        pltpu.VMEM((tm, cw), F32)],
        compiler_params=pltpu.CompilerParams(dimension_semantics=("arbitrary",),
                                             vmem_limit_bytes=VMEM_LIMIT),
        name="conv",
    )(u, u, u, w, cb, g, b)


def _attn_kernel(qt_ref, k_ref, vt_ref, lq1_ref, lk1_ref, lq2_ref, lk2_ref, g_ref, o_ref,
                 rhs_ref, acc_ref, s_ref, e_ref, *, tk, unroll, lam_init):
    hd2, tq = qt_ref.shape[1], qt_ref.shape[2]
    n = 2 * tq
    n_kv = k_ref.shape[0] // tk
    assert n_kv % 2 == 0 and n_kv >= 4
    qt = qt_ref[0]
    row = lax.broadcasted_iota(jnp.int32, qt.shape, 0)
    zero = jnp.zeros_like(qt)
    rhs_ref[:, :tq] = jnp.where(row < DIFF_HEAD_DIM, qt, zero)
    rhs_ref[:, tq:] = jnp.where(row >= DIFF_HEAD_DIM, qt, zero)
    acc_ref[...] = jnp.zeros_like(acc_ref)

    def scores(j, slot):
        k = k_ref[pl.ds(pl.multiple_of(j * tk, tk), tk), :]
        s = jnp.dot(k, rhs_ref[...], preferred_element_type=F32)
        s_ref[slot] = s
        return jnp.max(s.reshape(tk // 8, 8, n), axis=0)

    def pv_update(j, slot, alpha):
        pv = jnp.dot(vt_ref[0, j], e_ref[slot], preferred_element_type=F32)
        acc_ref[...] = alpha * acc_ref[...] + pv

    def step(j, slot, carry, *, first=False, last=False):
        m, l8, alpha_prev, bm8 = carry
        bm8_next = bm8 if last else scores(j + 1, 1 - slot)
        m_new = jnp.maximum(m, jnp.max(bm8, axis=0, keepdims=True))
        alpha = jnp.exp2(m - m_new)
        e = jnp.exp2(s_ref[slot] - m_new)
        l8 = alpha * l8 + jnp.sum(e.reshape(tk // 8, 8, n), axis=0)
        if not first:
            pv_update(j - 1, 1 - slot, alpha_prev)
        e_ref[slot] = e.astype(BF16)
        return m_new, l8, alpha, bm8_next

    carry = (jnp.full((1, n), -jnp.inf, F32), jnp.zeros((8, n), F32), jnp.ones((1, n), F32), scores(0, 0))
    carry = step(0, 0, carry, first=True)

    def group(i, carry):
        for u in range(unroll):
            carry = step(unroll * i + 1 + u, (1 + u) % 2, carry)
        return carry

    assert unroll % 2 == 0 and (n_kv - 2) % unroll == 0
    carry = lax.fori_loop(0, (n_kv - 2) // unroll, group, carry)
    _, l8, alpha, _ = step(n_kv - 1, 1, carry, last=True)
    pv_update(n_kv - 1, 1, alpha)

    l = jnp.sum(l8, axis=0, keepdims=True)
    o = acc_ref[...] * (1.0 / l)
    lam = (jnp.exp(jnp.sum(lq1_ref[...] * lk1_ref[...])) - jnp.exp(jnp.sum(lq2_ref[...] * lk2_ref[...]))
           + lam_init)
    o = o[:, :tq] - lam * o[:, tq:]
    ms = jnp.mean(o * o, axis=0, keepdims=True)
    o = o * lax.rsqrt(ms + LN_EPS) * g_ref[...] * (1.0 - lam_init)
    o_ref[...] = o.T.astype(BF16)


def _attn(qt, k, vt, lq1, lk1, lq2, lk2, g_col, *, tq, tk, unroll, lam_init):
    n_heads, hd2, s = qt.shape
    lam_spec = _const_spec(lq1.shape)
    return pl.pallas_call(
        functools.partial(_attn_kernel, tk=tk, unroll=unroll, lam_init=lam_init),
        grid=(n_heads, s // tq),
        in_specs=[pl.BlockSpec((1, hd2, tq), lambda h, i: (h, 0, i)),
                  pl.BlockSpec((s, hd2), lambda h, i: (0, h)),
                  pl.BlockSpec((1, s // tk, hd2, tk), lambda h, i: (h, 0, 0, 0)),
                  lam_spec, lam_spec, lam_spec, lam_spec,
                  _const_spec(g_col.shape)],
        out_specs=pl.BlockSpec((tq, hd2), lambda h, i: (i, h)),
        out_shape=jax.ShapeDtypeStruct((s, n_heads * hd2), BF16),
        scratch_shapes=[pltpu.VMEM((hd2, 2 * tq), BF16),
                        pltpu.VMEM((hd2, 2 * tq), F32),
                        pltpu.VMEM((2, tk, 2 * tq), F32),
                        pltpu.VMEM((2, tk, 2 * tq), BF16)],
        compiler_params=pltpu.CompilerParams(dimension_semantics=("arbitrary", "arbitrary"),
                                             vmem_limit_bytes=VMEM_LIMIT),
        name="attn",
    )(qt, k, vt, lq1, lk1, lq2, lk2, g_col)


def _mix_out_kernel(conv_ref, attn_ref, x_ref, mod_ref, w_ref, g_ref, b_ref, o_ref, *, sub):
    cw = conv_ref.shape[1]
    y = (jnp.dot(conv_ref[...], w_ref[:cw, :], preferred_element_type=F32)
         + jnp.dot(attn_ref[...], w_ref[cw:, :], preferred_element_type=F32))
    gate_c = mod_ref[3 * sub + 2:3 * sub + 3, :]
    z = ALPHA * x_ref[...] + (1.0 + gate_c) * y
    o_ref[...] = _layer_norm(z, g_ref[...], b_ref[...])


def _mix_out(conv, attn, x, mod9, w, g, b, *, tm):
    s, d = x.shape
    row = lambda i: (i, 0)
    return pl.pallas_call(
        functools.partial(_mix_out_kernel, sub=1),
        grid=(s // tm,),
        in_specs=[pl.BlockSpec((tm, conv.shape[1]), row),
                  pl.BlockSpec((tm, attn.shape[1]), row),
                  pl.BlockSpec((tm, d), row),
                  _const_spec(mod9.shape),
                  _const_spec(w.shape),
                  _const_spec(g.shape),
                  _const_spec(b.shape)],
        out_specs=pl.BlockSpec((tm, d), row),
        out_shape=jax.ShapeDtypeStruct((s, d), F32),
        compiler_params=pltpu.CompilerParams(dimension_semantics=("arbitrary",),
                                             vmem_limit_bytes=VMEM_LIMIT),
        name="mix_out",
    )(conv, attn, x, mod9, w, g, b)


def _rope_tables(s):
    pos = jnp.arange(s, dtype=F32)
    inv_freq = ROPE_THETA ** (-jnp.arange(0, ROT_DIM, 2, dtype=F32) / ROT_DIM)
    ang = pos[:, None] * inv_freq[None, :]
    cos, sin = jnp.cos(ang), jnp.sin(ang)
    half = ROT_DIM // 2
    pad = DIFF_HEAD_DIM - ROT_DIM
    ones = jnp.ones((s, pad), F32)
    zeros_h = jnp.zeros((s, half), F32)
    zeros_p = jnp.zeros((s, pad), F32)
    reps = LANES // DIFF_HEAD_DIM
    rc = jnp.tile(jnp.concatenate([cos, cos, ones], axis=1), (1, reps))
    rs1 = jnp.tile(jnp.concatenate([-sin, zeros_h, zeros_p], axis=1), (1, reps))
    rs2 = jnp.tile(jnp.concatenate([zeros_h, sin, zeros_p], axis=1), (1, reps))
    return rc, rs1, rs2


def kernel(x, c, w_ada, b_ada, ffn1_w_in, ffn1_w_out, ln1_g, ln1_b, mix_w_in, conv_w, conv_b, conv_ln_g,
           conv_ln_b, lambda_q1, lambda_k1, lambda_q2, lambda_k2, subln_g, mix_w_out, ln2_g, ln2_b,
           ffn2_w_in, ffn2_w_out, ln3_g, ln3_b):
    batch, s, d = x.shape
    assert batch == 1 and w_ada.shape[0] == DEPTH == 1
    cw = conv_w.shape[2]
    aw = (mix_w_in.shape[2] - 2 * cw) // 3
    hd2 = 2 * DIFF_HEAD_DIM
    n_heads = aw // hd2
    lam_init = 0.8 - 0.6 * math.exp(-0.3 * 0)
    q_scale = math.log2(math.e) / math.sqrt(DIFF_HEAD_DIM)
    tq, tk = 256, 512

    mod9 = _ada(c.reshape(d, 1), w_ada[0], b_ada, tn=1152).reshape(9, d)
    x0 = x[0]
    x1 = _ffn(x0, mod9, ffn1_w_in[0].astype(BF16), ffn1_w_out[0].astype(BF16), ln1_g, ln1_b,
              sub=0, weight=0.5, tm=512, tf=256)

    rc, rs1, rs2 = _rope_tables(s)
    u, q, k, v = _mix_in(x1, mod9, mix_w_in[0].astype(BF16), rc, rs1, rs2,
                         cw=cw, aw=aw, q_scale=q_scale, tm=512)
    conv = _conv(u, conv_w[0], conv_b, conv_ln_g, conv_ln_b, tm=256, rows=128)
    qt = q.reshape(s, n_heads, hd2).transpose(1, 2, 0)
    vt = v.reshape(s // tk, tk, n_heads, hd2).transpose(2, 0, 3, 1)
    attn = _attn(qt, k, vt, lambda_q1, lambda_k1, lambda_q2, lambda_k2, subln_g.reshape(hd2, 1),
                 tq=tq, tk=tk, unroll=30, lam_init=lam_init)
    x2 = _mix_out(conv, attn, x1, mod9, mix_w_out[0].astype(BF16), ln2_g, ln2_b, tm=512)

    x3 = _ffn(x2, mod9, ffn2_w_in[0].astype(BF16), ffn2_w_out[0].astype(BF16), ln3_g, ln3_b,
              sub=2, weight=0.5, tm=512, tf=256)
    return x3[None]
```

```python
import functools
import math

import jax
import jax.numpy as jnp
from jax import lax
from jax.experimental import pallas as pl
from jax.experimental.pallas import tpu as pltpu

F32 = jnp.float32
BF16 = jnp.bfloat16

DEPTH = 1
ALPHA = (2.0 * DEPTH) ** 0.25
LN_EPS = 1e-5
DIFF_HEAD_DIM = 64
ROT_DIM = DIFF_HEAD_DIM // 4
ROPE_THETA = 500000.0
CONV_KERNEL = 31
CONV_PAD = (CONV_KERNEL - 1) // 2
CONV_HALO = 16
LANES = 128
SUBLANES = 8
SHIFT_SLACK = 1.0 + 2.0 ** -10
L_FLOOR = 2.0 ** -80
VMEM_LIMIT = 56 * 1024 * 1024


def _sigmoid(x):
    return 1.0 / (1.0 + jnp.exp(-x))


def _layer_norm(z, g, b):
    mu = jnp.mean(z, axis=-1, keepdims=True)
    zc = z - mu
    var = jnp.mean(zc * zc, axis=-1, keepdims=True)
    return zc * lax.rsqrt(var + LN_EPS) * g + b


def _modulate(x, mod_ref, sub):
    shift = mod_ref[3 * sub:3 * sub + 1, :]
    scale = mod_ref[3 * sub + 1:3 * sub + 2, :]
    return x * (1.0 + scale) + shift


def _const_spec(shape):
    return pl.BlockSpec(shape, lambda *_: (0,) * len(shape), pipeline_mode=pl.Buffered(1))


def _ada_kernel(c_ref, w_ref, b_ref, o_ref):
    c = c_ref[...]
    ca = c * _sigmoid(c)
    o_ref[...] = jnp.sum(ca * w_ref[...], axis=0, keepdims=True) + b_ref[...]


def _ada(c_col, w, b_row, tn):
    d, n = w.shape
    return pl.pallas_call(
        _ada_kernel,
        grid=(n // tn,),
        in_specs=[pl.BlockSpec((d, 1), lambda j: (0, 0)),
                  pl.BlockSpec((d, tn), lambda j: (0, j)),
                  pl.BlockSpec((1, tn), lambda j: (0, j))],
        out_specs=pl.BlockSpec((1, tn), lambda j: (0, j)),
        out_shape=jax.ShapeDtypeStruct((1, n), F32),
        compiler_params=pltpu.CompilerParams(dimension_semantics=("arbitrary",),
                                             vmem_limit_bytes=VMEM_LIMIT),
        name="ada",
    )(c_col, w, b_row)


def _ffn_kernel(x_ref, mod_ref, win_ref, wout_ref, g_ref, b_ref, o_ref, act_ref, *, sub, weight, tf):
    x = x_ref[...]
    d_ff = wout_ref.shape[0]
    h = _modulate(x, mod_ref, sub).astype(BF16)
    for c in range(d_ff // tf):
        gate = jnp.dot(h, win_ref[:, c * tf:(c + 1) * tf], preferred_element_type=F32)
        up = jnp.dot(h, win_ref[:, d_ff + c * tf:d_ff + (c + 1) * tf], preferred_element_type=F32)
        act_ref[:, c * tf:(c + 1) * tf] = (gate * _sigmoid(gate) * up).astype(BF16)
    y = jnp.dot(act_ref[...], wout_ref[...], preferred_element_type=F32)
    gate_c = mod_ref[3 * sub + 2:3 * sub + 3, :]
    z = ALPHA * x + weight * (1.0 + gate_c) * y
    o_ref[...] = _layer_norm(z, g_ref[...], b_ref[...])


def _ffn(x, mod9, w_in, w_out, g, b, *, sub, weight, tm, tf):
    s, d = x.shape
    d_ff = w_out.shape[0]
    return pl.pallas_call(
        functools.partial(_ffn_kernel, sub=sub, weight=weight, tf=tf),
        grid=(s // tm,),
        in_specs=[pl.BlockSpec((tm, d), lambda i: (i, 0)),
                  _const_spec(mod9.shape),
                  _const_spec(w_in.shape),
                  _const_spec(w_out.shape),
                  _const_spec(g.shape),
                  _const_spec(b.shape)],
        out_specs=pl.BlockSpec((tm, d), lambda i: (i, 0)),
        out_shape=jax.ShapeDtypeStruct((s, d), F32),
        scratch_shapes=[pltpu.VMEM((tm, d_ff), BF16)],
        compiler_params=pltpu.CompilerParams(dimension_semantics=("arbitrary",),
                                             vmem_limit_bytes=VMEM_LIMIT),
        name=f"ffn{sub}",
    )(x, mod9, w_in, w_out, g, b)


def _mix_in_kernel(x_ref, mod_ref, w_ref, rope_ref, u_ref, qt_ref, k_ref, vt_ref, *, sub, cw, aw, q_scale):
    tm = x_ref.shape[0]
    h = _modulate(x_ref[...], mod_ref, sub).astype(BF16)
    proj = jnp.dot(h, w_ref[...], preferred_element_type=F32)
    u_ref[...] = proj[:, :cw] * _sigmoid(proj[:, cw:2 * cw])
    pos = (pl.program_id(0) * tm + lax.broadcasted_iota(jnp.int32, (tm, LANES), 0)).astype(F32)
    ang = pos * rope_ref[0:1, :]
    rc, sin = jnp.cos(ang), jnp.sin(ang)
    rs1, rs2 = sin * rope_ref[1:2, :], sin * rope_ref[2:3, :]
    half = ROT_DIM // 2

    def rope(t):
        return t * rc + pltpu.roll(t, LANES - half, 1) * rs1 + pltpu.roll(t, half, 1) * rs2

    q0, k0, v0 = 2 * cw, 2 * cw + aw, 2 * cw + 2 * aw
    for g in range(aw // LANES):
        sl = slice(g * LANES, (g + 1) * LANES)
        qt_ref[g] = (rope(proj[:, q0 + g * LANES:q0 + (g + 1) * LANES]) * q_scale).T.astype(BF16)
        k_ref[:, sl] = rope(proj[:, k0 + g * LANES:k0 + (g + 1) * LANES]).astype(BF16)
        vt_ref[g, 0] = proj[:, v0 + g * LANES:v0 + (g + 1) * LANES].T.astype(BF16)


def _mix_in(x, mod9, w, rope_rows, *, cw, aw, q_scale, tm):
    s, d = x.shape
    n_heads = aw // LANES
    row = lambda i: (i, 0)
    return pl.pallas_call(
        functools.partial(_mix_in_kernel, sub=1, cw=cw, aw=aw, q_scale=q_scale),
        grid=(s // tm,),
        in_specs=[pl.BlockSpec((tm, d), row),
                  _const_spec(mod9.shape),
                  _const_spec(w.shape),
                  _const_spec(rope_rows.shape)],
        out_specs=[pl.BlockSpec((tm, cw), row),
                   pl.BlockSpec((n_heads, LANES, tm), lambda i: (0, 0, i)),
                   pl.BlockSpec((tm, aw), row),
                   pl.BlockSpec((n_heads, 1, LANES, tm), lambda i: (0, i, 0, 0))],
        out_shape=[jax.ShapeDtypeStruct((s, cw), F32),
                   jax.ShapeDtypeStruct((n_heads, LANES, s), BF16),
                   jax.ShapeDtypeStruct((s, aw), BF16),
                   jax.ShapeDtypeStruct((n_heads, s // tm, LANES, tm), BF16)],
        compiler_params=pltpu.CompilerParams(dimension_semantics=("arbitrary",),
                                             vmem_limit_bytes=VMEM_LIMIT),
        name="mix_in",
    )(x, mod9, w, rope_rows)


def _conv_kernel(prev_ref, cur_ref, next_ref, w_ref, cb_ref, g_ref, b_ref, o_ref, ext_ref, sh_ref, y_ref, *, rows):
    i = pl.program_id(0)
    tm, cw = cur_ref.shape
    ext_ref[0:CONV_HALO, :] = jnp.where(i > 0, prev_ref[...], 0.0)
    ext_ref[CONV_HALO:CONV_HALO + tm, :] = cur_ref[...]
    ext_ref[CONV_HALO + tm:, :] = jnp.where(i < pl.num_programs(0) - 1, next_ref[...], 0.0)
    span = sh_ref.shape[1]
    for b in range(SUBLANES):
        sh_ref[b] = ext_ref[b:b + span, :]
    base = CONV_HALO - CONV_PAD
    for lc in range(cw // LANES):
        ls = slice(lc * LANES, (lc + 1) * LANES)
        for rc in range(tm // rows):
            r0 = rc * rows
            acc = jnp.zeros((rows, LANES), F32)
            for t in range(CONV_KERNEL):
                off = base + t
                a0 = r0 + SUBLANES * (off // SUBLANES)
                acc = acc + sh_ref[off % SUBLANES, a0:a0 + rows, ls] * w_ref[t:t + 1, ls]
            y_ref[r0:r0 + rows, ls] = acc
    y = _layer_norm(y_ref[...] + cb_ref[...], g_ref[...], b_ref[...])
    o_ref[...] = (y * _sigmoid(y)).astype(BF16)


def _conv(u, w, cb, g, b, *, tm, rows):
    s, cw = u.shape
    nh = tm // CONV_HALO
    last = s // CONV_HALO - 1
    return pl.pallas_call(
        functools.partial(_conv_kernel, rows=rows),
        grid=(s // tm,),
        in_specs=[pl.BlockSpec((CONV_HALO, cw), lambda i: (jnp.maximum(i * nh - 1, 0), 0)),
                  pl.BlockSpec((tm, cw), lambda i: (i, 0)),
                  pl.BlockSpec((CONV_HALO, cw), lambda i: (jnp.minimum((i + 1) * nh, last), 0)),
                  _const_spec(w.shape),
                  _const_spec(cb.shape),
                  _const_spec(g.shape),
                  _const_spec(b.shape)],
        out_specs=pl.BlockSpec((tm, cw), lambda i: (i, 0)),
        out_shape=jax.ShapeDtypeStruct((s, cw), BF16),
        scratch_shapes=[pltpu.VMEM((tm + 2 * CONV_HALO, cw), F32),
                        pltpu.VMEM((SUBLANES, tm + 2 * CONV_HALO - SUBLANES, cw), F32),
                        pltpu.VMEM((tm, cw), F32)],
        compiler_params=pltpu.CompilerParams(dimension_semantics=("arbitrary",),
                                             vmem_limit_bytes=VMEM_LIMIT),
        name="conv",
    )(u, u, u, w, cb, g, b)


def _attn_kernel(qt_ref, k_ref, vt_ref, lq1_ref, lk1_ref, lq2_ref, lk2_ref, g_ref, o_ref,
                 rhs_ref, acc_ref, l_ref, kmax_ref, *, tk, lam_init):
    hd2, tq = qt_ref.shape[1], qt_ref.shape[2]
    n = 2 * tq
    n_kv = k_ref.shape[0] // tk
    qt = qt_ref[0]
    row = lax.broadcasted_iota(jnp.int32, qt.shape, 0)
    zero = jnp.zeros_like(qt)
    rhs_ref[:, :tq] = jnp.where(row < DIFF_HEAD_DIM, qt, zero)
    rhs_ref[:, tq:] = jnp.where(row >= DIFF_HEAD_DIM, qt, zero)

    def k_block(j):
        return k_ref[pl.ds(pl.multiple_of(j * tk, tk), tk), :]

    @pl.when(pl.program_id(1) == 0)
    def _key_norm_bound():
        lane = lax.broadcasted_iota(jnp.int32, (tk, hd2), 1)

        def body(j, carry):
            kf = k_block(j).astype(F32)
            sq = kf * kf
            n0 = jnp.sum(jnp.where(lane < DIFF_HEAD_DIM, sq, 0.0), axis=1, keepdims=True)
            n1 = jnp.sum(jnp.where(lane >= DIFF_HEAD_DIM, sq, 0.0), axis=1, keepdims=True)
            return (jnp.maximum(carry[0], jnp.max(n0, axis=0, keepdims=True)),
                    jnp.maximum(carry[1], jnp.max(n1, axis=0, keepdims=True)))

        init = (jnp.zeros((1, 1), F32), jnp.zeros((1, 1), F32))
        mx0, mx1 = lax.fori_loop(0, n_kv, body, init)
        col = lax.broadcasted_iota(jnp.int32, (1, n), 1)
        kmax_ref[...] = jnp.sqrt(jnp.where(col < tq, mx0, mx1))

    r32 = rhs_ref[...].astype(F32)
    qn = jnp.sqrt(jnp.sum(r32 * r32, axis=0, keepdims=True))
    m = qn * kmax_ref[...] * SHIFT_SLACK
    l8 = jnp.zeros((8, n), F32)
    acc = jnp.zeros((hd2, n), F32)
    e_prev = None
    for j in range(n_kv):
        s = jnp.dot(k_ref[j * tk:(j + 1) * tk, :], rhs_ref[...], preferred_element_type=F32)
        if e_prev is not None:
            acc = acc + jnp.dot(vt_ref[0, j - 1], e_prev, preferred_element_type=F32)
        e = jnp.exp2(s - m)
        l8 = l8 + jnp.sum(e.reshape(tk // 8, 8, n), axis=0)
        e_prev = e.astype(BF16)
    acc = acc + jnp.dot(vt_ref[0, n_kv - 1], e_prev, preferred_element_type=F32)
    l = jnp.sum(l8, axis=0, keepdims=True)
    acc_ref[...] = acc
    l_ref[...] = l

    @pl.when(jnp.logical_not(jnp.min(l) >= L_FLOOR))
    def _running_max_fallback():
        acc_ref[...] = jnp.zeros_like(acc_ref)

        def body(j, carry):
            m_run, l_run = carry
            s = jnp.dot(k_block(j), rhs_ref[...], preferred_element_type=F32)
            m_new = jnp.maximum(m_run, jnp.max(s, axis=0, keepdims=True))
            alpha = jnp.exp2(m_run - m_new)
            e = jnp.exp2(s - m_new)
            pv = jnp.dot(vt_ref[0, j], e.astype(BF16), preferred_element_type=F32)
            acc_ref[...] = alpha * acc_ref[...] + pv
            return m_new, alpha * l_run + jnp.sum(e, axis=0, keepdims=True)

        init = (jnp.full((1, n), -jnp.inf, F32), jnp.zeros((1, n), F32))
        _, l_run = lax.fori_loop(0, n_kv, body, init)
        l_ref[...] = l_run

    o = acc_ref[...] * (1.0 / l_ref[...])
    lam = (jnp.exp(jnp.sum(lq1_ref[...] * lk1_ref[...])) - jnp.exp(jnp.sum(lq2_ref[...] * lk2_ref[...]))
           + lam_init)
    o = o[:, :tq] - lam * o[:, tq:]
    ms = jnp.mean(o * o, axis=0, keepdims=True)
    o = o * lax.rsqrt(ms + LN_EPS) * g_ref[...] * (1.0 - lam_init)
    o_ref[...] = o.T.astype(BF16)


def _attn(qt, k, vt, lq1, lk1, lq2, lk2, g_col, *, tq, tk, lam_init):
    n_heads, hd2, s = qt.shape
    lam_spec = _const_spec(lq1.shape)
    return pl.pallas_call(
        functools.partial(_attn_kernel, tk=tk, lam_init=lam_init),
        grid=(n_heads, s // tq),
        in_specs=[pl.BlockSpec((1, hd2, tq), lambda h, i: (h, 0, i)),
                  pl.BlockSpec((s, hd2), lambda h, i: (0, h)),
                  pl.BlockSpec((1, s // tk, hd2, tk), lambda h, i: (h, 0, 0, 0)),
                  lam_spec, lam_spec, lam_spec, lam_spec,
                  _const_spec(g_col.shape)],
        out_specs=pl.BlockSpec((tq, hd2), lambda h, i: (i, h)),
        out_shape=jax.ShapeDtypeStruct((s, n_heads * hd2), BF16),
        scratch_shapes=[pltpu.VMEM((hd2, 2 * tq), BF16),
                        pltpu.VMEM((hd2, 2 * tq), F32),
                        pltpu.VMEM((1, 2 * tq), F32),
                        pltpu.VMEM((1, 2 * tq), F32)],
        compiler_params=pltpu.CompilerParams(dimension_semantics=("arbitrary", "arbitrary"),
                                             vmem_limit_bytes=VMEM_LIMIT),
        name="attn",
    )(qt, k, vt, lq1, lk1, lq2, lk2, g_col)


def _mix_out_kernel(conv_ref, attn_ref, x_ref, mod_ref, w_ref, g_ref, b_ref, o_ref, *, sub):
    cw = conv_ref.shape[1]
    y = (jnp.dot(conv_ref[...], w_ref[:cw, :], preferred_element_type=F32)
         + jnp.dot(attn_ref[...], w_ref[cw:, :], preferred_element_type=F32))
    gate_c = mod_ref[3 * sub + 2:3 * sub + 3, :]
    z = ALPHA * x_ref[...] + (1.0 + gate_c) * y
    o_ref[...] = _layer_norm(z, g_ref[...], b_ref[...])


def _mix_out(conv, attn, x, mod9, w, g, b, *, tm):
    s, d = x.shape
    row = lambda i: (i, 0)
    return pl.pallas_call(
        functools.partial(_mix_out_kernel, sub=1),
        grid=(s // tm,),
        in_specs=[pl.BlockSpec((tm, conv.shape[1]), row),
                  pl.BlockSpec((tm, attn.shape[1]), row),
                  pl.BlockSpec((tm, d), row),
                  _const_spec(mod9.shape),
                  _const_spec(w.shape),
                  _const_spec(g.shape),
                  _const_spec(b.shape)],
        out_specs=pl.BlockSpec((tm, d), row),
        out_shape=jax.ShapeDtypeStruct((s, d), F32),
        compiler_params=pltpu.CompilerParams(dimension_semantics=("arbitrary",),
                                             vmem_limit_bytes=VMEM_LIMIT),
        name="mix_out",
    )(conv, attn, x, mod9, w, g, b)


def _rope_rows():
    inv_freq = ROPE_THETA ** (-jnp.arange(0, ROT_DIM, 2, dtype=F32) / ROT_DIM)
    half = ROT_DIM // 2
    zeros_h = jnp.zeros((half,), F32)
    zeros_p = jnp.zeros((DIFF_HEAD_DIM - ROT_DIM,), F32)
    ones_h = jnp.ones((half,), F32)
    reps = LANES // DIFF_HEAD_DIM
    freq = jnp.tile(jnp.concatenate([inv_freq, inv_freq, zeros_p]), reps)
    neg_first = jnp.tile(jnp.concatenate([-ones_h, zeros_h, zeros_p]), reps)
    pos_second = jnp.tile(jnp.concatenate([zeros_h, ones_h, zeros_p]), reps)
    return jnp.stack([freq, neg_first, pos_second])


def kernel(x, c, w_ada, b_ada, ffn1_w_in, ffn1_w_out, ln1_g, ln1_b, mix_w_in, conv_w, conv_b, conv_ln_g,
           conv_ln_b, lambda_q1, lambda_k1, lambda_q2, lambda_k2, subln_g, mix_w_out, ln2_g, ln2_b,
           ffn2_w_in, ffn2_w_out, ln3_g, ln3_b):
    batch, s, d = x.shape
    assert batch == 1 and w_ada.shape[0] == DEPTH == 1
    cw = conv_w.shape[2]
    aw = (mix_w_in.shape[2] - 2 * cw) // 3
    hd2 = 2 * DIFF_HEAD_DIM
    n_heads = aw // hd2
    lam_init = 0.8 - 0.6 * math.exp(-0.3 * 0)
    q_scale = math.log2(math.e) / math.sqrt(DIFF_HEAD_DIM)
    tq, tk = 256, 512

    mod9 = _ada(c.reshape(d, 1), w_ada[0], b_ada, tn=1152).reshape(9, d)
    x0 = x[0]
    x1 = _ffn(x0, mod9, ffn1_w_in[0].astype(BF16), ffn1_w_out[0].astype(BF16), ln1_g, ln1_b,
              sub=0, weight=0.5, tm=512, tf=256)

    u, qt, k, vt = _mix_in(x1, mod9, mix_w_in[0].astype(BF16), _rope_rows(),
                           cw=cw, aw=aw, q_scale=q_scale, tm=tk)
    conv = _conv(u, conv_w[0], conv_b, conv_ln_g, conv_ln_b, tm=256, rows=128)
    attn = _attn(qt, k, vt, lambda_q1, lambda_k1, lambda_q2, lambda_k2, subln_g.reshape(hd2, 1),
                 tq=tq, tk=tk, lam_init=lam_init)
    x2 = _mix_out(conv, attn, x1, mod9, mix_w_out[0].astype(BF16), ln2_g, ln2_b, tm=512)

    x3 = _ffn(x2, mod9, ffn2_w_in[0].astype(BF16), ffn2_w_out[0].astype(BF16), ln3_g, ln3_b,
              sub=2, weight=0.5, tm=512, tf=256)
    return x3[None]
```

```python
import functools
import math

import jax
import jax.numpy as jnp
from jax import lax
from jax.experimental import pallas as pl
from jax.experimental.pallas import tpu as pltpu

F32 = jnp.float32
BF16 = jnp.bfloat16

DEPTH = 1
ALPHA = (2.0 * DEPTH) ** 0.25
LN_EPS = 1e-5
DIFF_HEAD_DIM = 64
ROT_DIM = DIFF_HEAD_DIM // 4
ROPE_THETA = 500000.0
CONV_KERNEL = 31
CONV_PAD = (CONV_KERNEL - 1) // 2
CONV_HALO = 16
LANES = 128
SUBLANES = 8
SHIFT_SLACK = 1.0 + 2.0 ** -10
L_FLOOR = 2.0 ** -80
VMEM_LIMIT = 56 * 1024 * 1024


def _sigmoid(x):
    return 1.0 / (1.0 + jnp.exp(-x))


def _layer_norm(z, g, b):
    mu = jnp.mean(z, axis=-1, keepdims=True)
    zc = z - mu
    var = jnp.mean(zc * zc, axis=-1, keepdims=True)
    return zc * lax.rsqrt(var + LN_EPS) * g + b


def _modulate(x, mod_ref, sub):
    shift = mod_ref[3 * sub:3 * sub + 1, :]
    scale = mod_ref[3 * sub + 1:3 * sub + 2, :]
    return x * (1.0 + scale) + shift


def _const_spec(shape):
    return pl.BlockSpec(shape, lambda *_: (0,) * len(shape), pipeline_mode=pl.Buffered(1))


def _ada_kernel(c_ref, w_ref, b_ref, o_ref):
    c = c_ref[...]
    ca = c * _sigmoid(c)
    o_ref[...] = jnp.sum(ca * w_ref[...], axis=0, keepdims=True) + b_ref[...]


def _ada(c_col, w, b_row, tn):
    d, n = w.shape
    return pl.pallas_call(
        _ada_kernel,
        grid=(n // tn,),
        in_specs=[pl.BlockSpec((d, 1), lambda j: (0, 0)),
                  pl.BlockSpec((d, tn), lambda j: (0, j)),
                  pl.BlockSpec((1, tn), lambda j: (0, j))],
        out_specs=pl.BlockSpec((1, tn), lambda j: (0, j)),
        out_shape=jax.ShapeDtypeStruct((1, n), F32),
        compiler_params=pltpu.CompilerParams(dimension_semantics=("arbitrary",),
                                             vmem_limit_bytes=VMEM_LIMIT),
        name="ada",
    )(c_col, w, b_row)


def _ffn_kernel(x_ref, mod_ref, win_ref, wout_ref, g_ref, b_ref, o_ref, act_ref, *, sub, weight, tf):
    x = x_ref[...]
    d_ff = wout_ref.shape[0]
    h = _modulate(x, mod_ref, sub).astype(win_ref.dtype)
    for c in range(d_ff // tf):
        gate = jnp.dot(h, win_ref[:, c * tf:(c + 1) * tf], preferred_element_type=F32)
        up = jnp.dot(h, win_ref[:, d_ff + c * tf:d_ff + (c + 1) * tf], preferred_element_type=F32)
        act_ref[:, c * tf:(c + 1) * tf] = (gate * _sigmoid(gate) * up).astype(act_ref.dtype)
    y = jnp.dot(act_ref[...], wout_ref[...], preferred_element_type=F32)
    gate_c = mod_ref[3 * sub + 2:3 * sub + 3, :]
    z = ALPHA * x + weight * (1.0 + gate_c) * y
    o_ref[...] = _layer_norm(z, g_ref[...], b_ref[...])


def _ffn(x, mod9, w_in, w_out, g, b, *, sub, weight, tm, tf):
    s, d = x.shape
    d_ff = w_out.shape[0]
    return pl.pallas_call(
        functools.partial(_ffn_kernel, sub=sub, weight=weight, tf=tf),
        grid=(s // tm,),
        in_specs=[pl.BlockSpec((tm, d), lambda i: (i, 0)),
                  _const_spec(mod9.shape),
                  _const_spec(w_in.shape),
                  _const_spec(w_out.shape),
                  _const_spec(g.shape),
                  _const_spec(b.shape)],
        out_specs=pl.BlockSpec((tm, d), lambda i: (i, 0)),
        out_shape=jax.ShapeDtypeStruct((s, d), F32),
        scratch_shapes=[pltpu.VMEM((tm, d_ff), w_out.dtype)],
        compiler_params=pltpu.CompilerParams(dimension_semantics=("arbitrary",),
                                             vmem_limit_bytes=VMEM_LIMIT),
        name=f"ffn{sub}",
    )(x, mod9, w_in, w_out, g, b)


def _mix_in_kernel(x_ref, mod_ref, w_ref, rope_ref, u_ref, qt_ref, k_ref, vt_ref, *, sub, cw, aw, q_scale):
    tm = x_ref.shape[0]
    h = _modulate(x_ref[...], mod_ref, sub).astype(w_ref.dtype)
    proj = jnp.dot(h, w_ref[...], preferred_element_type=F32)
    u_ref[...] = proj[:, :cw] * _sigmoid(proj[:, cw:2 * cw])
    pos = (pl.program_id(0) * tm + lax.broadcasted_iota(jnp.int32, (tm, LANES), 0)).astype(F32)
    ang = pos * rope_ref[0:1, :]
    rc, sin = jnp.cos(ang), jnp.sin(ang)
    rs1, rs2 = sin * rope_ref[1:2, :], sin * rope_ref[2:3, :]
    half = ROT_DIM // 2

    def rope(t):
        return t * rc + pltpu.roll(t, LANES - half, 1) * rs1 + pltpu.roll(t, half, 1) * rs2

    q0, k0, v0 = 2 * cw, 2 * cw + aw, 2 * cw + 2 * aw
    for g in range(aw // LANES):
        sl = slice(g * LANES, (g + 1) * LANES)
        qt_ref[g] = (rope(proj[:, q0 + g * LANES:q0 + (g + 1) * LANES]) * q_scale).T.astype(BF16)
        k_ref[:, sl] = rope(proj[:, k0 + g * LANES:k0 + (g + 1) * LANES]).astype(BF16)
        tk = vt_ref.shape[3]
        for c in range(tm // tk):
            vt_ref[g, c] = proj[c * tk:(c + 1) * tk, v0 + g * LANES:v0 + (g + 1) * LANES].T.astype(BF16)


def _mix_in(x, mod9, w, rope_rows, *, cw, aw, q_scale, tm, tk):
    s, d = x.shape
    n_heads = aw // LANES
    row = lambda i: (i, 0)
    return pl.pallas_call(
        functools.partial(_mix_in_kernel, sub=1, cw=cw, aw=aw, q_scale=q_scale),
        grid=(s // tm,),
        in_specs=[pl.BlockSpec((tm, d), row),
                  _const_spec(mod9.shape),
                  _const_spec(w.shape),
                  _const_spec(rope_rows.shape)],
        out_specs=[pl.BlockSpec((tm, cw), row),
                   pl.BlockSpec((n_heads, LANES, tm), lambda i: (0, 0, i)),
                   pl.BlockSpec((tm, aw), row),
                   pl.BlockSpec((n_heads, tm // tk, LANES, tk), lambda i: (0, i, 0, 0))],
        out_shape=[jax.ShapeDtypeStruct((s, cw), F32),
                   jax.ShapeDtypeStruct((n_heads, LANES, s), BF16),
                   jax.ShapeDtypeStruct((s, aw), BF16),
                   jax.ShapeDtypeStruct((n_heads, s // tk, LANES, tk), BF16)],
        compiler_params=pltpu.CompilerParams(dimension_semantics=("arbitrary",),
                                             vmem_limit_bytes=VMEM_LIMIT),
        name="mix_in",
    )(x, mod9, w, rope_rows)


def _conv_tile(i, n_tiles, prev_ref, cur_ref, next_ref, w_ref, cb_ref, g_ref, b_ref, o_ref, ext_ref, sh_ref, y_ref,
               rows):
    tm, cw = cur_ref.shape
    ext_ref[0:CONV_HALO, :] = jnp.where(i > 0, prev_ref[...], 0.0)
    ext_ref[CONV_HALO:CONV_HALO + tm, :] = cur_ref[...]
    ext_ref[CONV_HALO + tm:, :] = jnp.where(i < n_tiles - 1, next_ref[...], 0.0)
    span = sh_ref.shape[1]
    for b in range(SUBLANES):
        sh_ref[b] = ext_ref[b:b + span, :]
    base = CONV_HALO - CONV_PAD
    for lc in range(cw // LANES):
        ls = slice(lc * LANES, (lc + 1) * LANES)
        for rc in range(tm // rows):
            r0 = rc * rows
            acc = jnp.zeros((rows, LANES), F32)
            for t in range(CONV_KERNEL):
                off = base + t
                a0 = r0 + SUBLANES * (off // SUBLANES)
                acc = acc + sh_ref[off % SUBLANES, a0:a0 + rows, ls] * w_ref[t:t + 1, ls]
            y_ref[r0:r0 + rows, ls] = acc
    y = _layer_norm(y_ref[...] + cb_ref[...], g_ref[...], b_ref[...])
    o_ref[...] = (y * _sigmoid(y)).astype(BF16)


def _conv_specs(s, cw, tm, tile_index):
    nh = tm // CONV_HALO
    last = s // CONV_HALO - 1
    return [pl.BlockSpec((CONV_HALO, cw), lambda *g: (jnp.maximum(tile_index(*g) * nh - 1, 0), 0)),
            pl.BlockSpec((tm, cw), lambda *g: (tile_index(*g), 0)),
            pl.BlockSpec((CONV_HALO, cw), lambda *g: (jnp.minimum((tile_index(*g) + 1) * nh, last), 0))]


def _conv_scratch(tm, cw):
    return [pltpu.VMEM((tm + 2 * CONV_HALO, cw), F32),
            pltpu.VMEM((SUBLANES, tm + 2 * CONV_HALO - SUBLANES, cw), F32),
            pltpu.VMEM((tm, cw), F32)]


def _mixer_kernel(qt_ref, k_ref, vt_ref, lq1_ref, lk1_ref, lq2_ref, lk2_ref, g_ref,
                  up_ref, uc_ref, un_ref, cw_ref, cb_ref, cg_ref, cbeta_ref,
                  o_ref, conv_ref,
                  rhs_ref, acc_ref, l_ref, kmax_ref, ext_ref, sh_ref, y_ref, *, tk, conv_rows, lam_init):
    step = pl.program_id(0) * pl.num_programs(1) + pl.program_id(1)
    n_steps = pl.num_programs(0) * pl.num_programs(1)

    hd2, tq = qt_ref.shape[1], qt_ref.shape[2]
    n = 2 * tq
    n_kv = k_ref.shape[0] // tk
    qt = qt_ref[0]
    row = lax.broadcasted_iota(jnp.int32, qt.shape, 0)
    zero = jnp.zeros_like(qt)
    rhs_ref[:, :tq] = jnp.where(row < DIFF_HEAD_DIM, qt, zero)
    rhs_ref[:, tq:] = jnp.where(row >= DIFF_HEAD_DIM, qt, zero)

    def k_block(j):
        return k_ref[pl.ds(pl.multiple_of(j * tk, tk), tk), :]

    @pl.when(pl.program_id(1) == 0)
    def _key_norm_bound():
        lane = lax.broadcasted_iota(jnp.int32, (tk, hd2), 1)

        def body(j, carry):
            kf = k_block(j).astype(F32)
            sq = kf * kf
            n0 = jnp.sum(jnp.where(lane < DIFF_HEAD_DIM, sq, 0.0), axis=1, keepdims=True)
            n1 = jnp.sum(jnp.where(lane >= DIFF_HEAD_DIM, sq, 0.0), axis=1, keepdims=True)
            return (jnp.maximum(carry[0], jnp.max(n0, axis=0, keepdims=True)),
                    jnp.maximum(carry[1], jnp.max(n1, axis=0, keepdims=True)))

        init = (jnp.zeros((1, 1), F32), jnp.zeros((1, 1), F32))
        mx0, mx1 = lax.fori_loop(0, n_kv, body, init)
        col = lax.broadcasted_iota(jnp.int32, (1, n), 1)
        kmax_ref[...] = jnp.sqrt(jnp.where(col < tq, mx0, mx1))

    _conv_tile(step, n_steps, up_ref, uc_ref, un_ref, cw_ref, cb_ref, cg_ref, cbeta_ref, conv_ref,
               ext_ref, sh_ref, y_ref, conv_rows)

    r32 = rhs_ref[...].astype(F32)
    qn = jnp.sqrt(jnp.sum(r32 * r32, axis=0, keepdims=True))
    m = qn * kmax_ref[...] * SHIFT_SLACK
    l8 = jnp.zeros((8, n), F32)
    acc = jnp.zeros((hd2, n), F32)
    e_prev = None
    for j in range(n_kv):
        s = jnp.dot(k_ref[j * tk:(j + 1) * tk, :], rhs_ref[...], preferred_element_type=F32)
        if e_prev is not None:
            acc = acc + jnp.dot(vt_ref[0, j - 1], e_prev, preferred_element_type=F32)
        e = jnp.exp2(s - m)
        l8 = l8 + jnp.sum(e.reshape(tk // 8, 8, n), axis=0)
        e_prev = e.astype(BF16)
    acc = acc + jnp.dot(vt_ref[0, n_kv - 1], e_prev, preferred_element_type=F32)
    l = jnp.sum(l8, axis=0, keepdims=True)
    acc_ref[...] = acc
    l_ref[...] = l

    @pl.when(jnp.logical_not(jnp.min(l) >= L_FLOOR))
    def _running_max_fallback():
        acc_ref[...] = jnp.zeros_like(acc_ref)

        def body(j, carry):
            m_run, l_run = carry
            s = jnp.dot(k_block(j), rhs_ref[...], preferred_element_type=F32)
            m_new = jnp.maximum(m_run, jnp.max(s, axis=0, keepdims=True))
            alpha = jnp.exp2(m_run - m_new)
            e = jnp.exp2(s - m_new)
            pv = jnp.dot(vt_ref[0, j], e.astype(BF16), preferred_element_type=F32)
            acc_ref[...] = alpha * acc_ref[...] + pv
            return m_new, alpha * l_run + jnp.sum(e, axis=0, keepdims=True)

        init = (jnp.full((1, n), -jnp.inf, F32), jnp.zeros((1, n), F32))
        _, l_run = lax.fori_loop(0, n_kv, body, init)
        l_ref[...] = l_run

    o = acc_ref[...] * (1.0 / l_ref[...])
    lam = (jnp.exp(jnp.sum(lq1_ref[...] * lk1_ref[...])) - jnp.exp(jnp.sum(lq2_ref[...] * lk2_ref[...]))
           + lam_init)
    o = o[:, :tq] - lam * o[:, tq:]
    ms = jnp.mean(o * o, axis=0, keepdims=True)
    o = o * lax.rsqrt(ms + LN_EPS) * g_ref[...] * (1.0 - lam_init)
    o_ref[...] = o.T.astype(BF16)


def _mixer(qt, k, vt, lq1, lk1, lq2, lk2, g_col, u, conv_w, conv_b, conv_g, conv_beta, *, tq, tk, conv_rows,
           lam_init):
    n_heads, hd2, s = qt.shape
    cw = u.shape[1]
    nq = s // tq
    tc = s // (n_heads * nq)
    assert tc % conv_rows == 0 and tc % CONV_HALO == 0
    lam_spec = _const_spec(lq1.shape)
    tile = lambda h, i: h * nq + i
    return pl.pallas_call(
        functools.partial(_mixer_kernel, tk=tk, conv_rows=conv_rows, lam_init=lam_init),
        grid=(n_heads, nq),
        in_specs=[pl.BlockSpec((1, hd2, tq), lambda h, i: (h, 0, i)),
                  pl.BlockSpec((s, hd2), lambda h, i: (0, h)),
                  pl.BlockSpec((1, s // tk, hd2, tk), lambda h, i: (h, 0, 0, 0)),
                  lam_spec, lam_spec, lam_spec, lam_spec,
                  _const_spec(g_col.shape),
                  *_conv_specs(s, cw, tc, tile),
                  _const_spec(conv_w.shape),
                  _const_spec(conv_b.shape),
                  _const_spec(conv_g.shape),
                  _const_spec(conv_beta.shape)],
        out_specs=[pl.BlockSpec((tq, hd2), lambda h, i: (i, h)),
                   pl.BlockSpec((tc, cw), lambda h, i: (tile(h, i), 0))],
        out_shape=[jax.ShapeDtypeStruct((s, n_heads * hd2), BF16),
                   jax.ShapeDtypeStruct((s, cw), BF16)],
        scratch_shapes=[pltpu.VMEM((hd2, 2 * tq), BF16),
                        pltpu.VMEM((hd2, 2 * tq), F32),
                        pltpu.VMEM((1, 2 * tq), F32),
                        pltpu.VMEM((1, 2 * tq), F32),
                        *_conv_scratch(tc, cw)],
        compiler_params=pltpu.CompilerParams(dimension_semantics=("arbitrary", "arbitrary"),
                                             vmem_limit_bytes=VMEM_LIMIT),
        name="mixer",
    )(qt, k, vt, lq1, lk1, lq2, lk2, g_col, u, u, u, conv_w, conv_b, conv_g, conv_beta)


def _mix_out_kernel(conv_ref, attn_ref, x_ref, mod_ref, w_ref, g_ref, b_ref, o_ref, *, sub):
    cw = conv_ref.shape[1]
    y = (jnp.dot(conv_ref[...], w_ref[:cw, :], preferred_element_type=F32)
         + jnp.dot(attn_ref[...], w_ref[cw:, :], preferred_element_type=F32))
    gate_c = mod_ref[3 * sub + 2:3 * sub + 3, :]
    z = ALPHA * x_ref[...] + (1.0 + gate_c) * y
    o_ref[...] = _layer_norm(z, g_ref[...], b_ref[...])


def _mix_out(conv, attn, x, mod9, w, g, b, *, tm):
    s, d = x.shape
    row = lambda i: (i, 0)
    return pl.pallas_call(
        functools.partial(_mix_out_kernel, sub=1),
        grid=(s // tm,),
        in_specs=[pl.BlockSpec((tm, conv.shape[1]), row),
                  pl.BlockSpec((tm, attn.shape[1]), row),
                  pl.BlockSpec((tm, d), row),
                  _const_spec(mod9.shape),
                  _const_spec(w.shape),
                  _const_spec(g.shape),
                  _const_spec(b.shape)],
        out_specs=pl.BlockSpec((tm, d), row),
        out_shape=jax.ShapeDtypeStruct((s, d), F32),
        compiler_params=pltpu.CompilerParams(dimension_semantics=("arbitrary",),
                                             vmem_limit_bytes=VMEM_LIMIT),
        name="mix_out",
    )(conv, attn, x, mod9, w, g, b)


def _rope_rows():
    inv_freq = ROPE_THETA ** (-jnp.arange(0, ROT_DIM, 2, dtype=F32) / ROT_DIM)
    half = ROT_DIM // 2
    zeros_h = jnp.zeros((half,), F32)
    zeros_p = jnp.zeros((DIFF_HEAD_DIM - ROT_DIM,), F32)
    ones_h = jnp.ones((half,), F32)
    reps = LANES // DIFF_HEAD_DIM
    freq = jnp.tile(jnp.concatenate([inv_freq, inv_freq, zeros_p]), reps)
    neg_first = jnp.tile(jnp.concatenate([-ones_h, zeros_h, zeros_p]), reps)
    pos_second = jnp.tile(jnp.concatenate([zeros_h, ones_h, zeros_p]), reps)
    return jnp.stack([freq, neg_first, pos_second])


def kernel(x, c, w_ada, b_ada, ffn1_w_in, ffn1_w_out, ln1_g, ln1_b, mix_w_in, conv_w, conv_b, conv_ln_g,
           conv_ln_b, lambda_q1, lambda_k1, lambda_q2, lambda_k2, subln_g, mix_w_out, ln2_g, ln2_b,
           ffn2_w_in, ffn2_w_out, ln3_g, ln3_b):
    batch, s, d = x.shape
    assert batch == 1 and w_ada.shape[0] == DEPTH == 1
    cw = conv_w.shape[2]
    aw = (mix_w_in.shape[2] - 2 * cw) // 3
    hd2 = 2 * DIFF_HEAD_DIM
    n_heads = aw // hd2
    lam_init = 0.8 - 0.6 * math.exp(-0.3 * 0)
    q_scale = math.log2(math.e) / math.sqrt(DIFF_HEAD_DIM)
    tq, tk = 512, 256

    mod9 = _ada(c.reshape(d, 1), w_ada[0], b_ada, tn=1152).reshape(9, d)
    x0 = x[0]
    x1 = _ffn(x0, mod9, ffn1_w_in[0], ffn1_w_out[0], ln1_g, ln1_b,
              sub=0, weight=0.5, tm=512, tf=256)

    u, qt, k, vt = _mix_in(x1, mod9, mix_w_in[0], _rope_rows(),
                           cw=cw, aw=aw, q_scale=q_scale, tm=512, tk=tk)
    attn, conv = _mixer(qt, k, vt, lambda_q1, lambda_k1, lambda_q2, lambda_k2, subln_g.reshape(hd2, 1),
                        u, conv_w[0], conv_b, conv_ln_g, conv_ln_b,
                        tq=tq, tk=tk, conv_rows=128, lam_init=lam_init)
    x2 = _mix_out(conv, attn, x1, mod9, mix_w_out[0].astype(BF16), ln2_g, ln2_b, tm=512)

    x3 = _ffn(x2, mod9, ffn2_w_in[0], ffn2_w_out[0], ln3_g, ln3_b,
              sub=2, weight=0.5, tm=512, tf=256)
    return x3[None]
```

```python
import functools
import math

import jax
import jax.numpy as jnp
from jax import lax
from jax.experimental import pallas as pl
from jax.experimental.pallas import tpu as pltpu

F32 = jnp.float32
BF16 = jnp.bfloat16

DEPTH = 1
ALPHA = (2.0 * DEPTH) ** 0.25
LN_EPS = 1e-5
DIFF_HEAD_DIM = 64
ROT_DIM = DIFF_HEAD_DIM // 4
ROPE_THETA = 500000.0
CONV_KERNEL = 31
CONV_PAD = (CONV_KERNEL - 1) // 2
CONV_HALO = 16
LANES = 128
SUBLANES = 8
SHIFT_SLACK = 1.0 + 2.0 ** -10
L_FLOOR = 2.0 ** -80
CONV_FIRST_BLOCK = 10
VMEM_LIMIT = 56 * 1024 * 1024


def _sigmoid(x):
    return 1.0 / (1.0 + jnp.exp(-x))


def _layer_norm(z, g, b):
    mu = jnp.mean(z, axis=-1, keepdims=True)
    zc = z - mu
    var = jnp.mean(zc * zc, axis=-1, keepdims=True)
    return zc * lax.rsqrt(var + LN_EPS) * g + b


def _modulate(x, mod_ref, sub):
    shift = mod_ref[3 * sub:3 * sub + 1, :]
    scale = mod_ref[3 * sub + 1:3 * sub + 2, :]
    return x * (1.0 + scale) + shift


def _const_spec(shape):
    return pl.BlockSpec(shape, lambda *_: (0,) * len(shape), pipeline_mode=pl.Buffered(1))


def _ada_kernel(c_ref, w_ref, b_ref, o_ref):
    c = c_ref[...]
    ca = c * _sigmoid(c)
    o_ref[...] = jnp.sum(ca * w_ref[...], axis=0, keepdims=True) + b_ref[...]


def _ada(c_col, w, b_row, tn):
    d, n = w.shape
    return pl.pallas_call(
        _ada_kernel,
        grid=(n // tn,),
        in_specs=[pl.BlockSpec((d, 1), lambda j: (0, 0)),
                  pl.BlockSpec((d, tn), lambda j: (0, j)),
                  pl.BlockSpec((1, tn), lambda j: (0, j))],
        out_specs=pl.BlockSpec((1, tn), lambda j: (0, j)),
        out_shape=jax.ShapeDtypeStruct((1, n), F32),
        compiler_params=pltpu.CompilerParams(dimension_semantics=("arbitrary",),
                                             vmem_limit_bytes=VMEM_LIMIT),
        name="ada",
    )(c_col, w, b_row)


def _ffn_kernel(x_ref, mod_ref, win_ref, wout_ref, g_ref, b_ref, o_ref, act_ref, *, sub, weight, tf):
    x = x_ref[...]
    d_ff = wout_ref.shape[0]
    h = _modulate(x, mod_ref, sub).astype(win_ref.dtype)
    for c in range(d_ff // tf):
        gate = jnp.dot(h, win_ref[:, c * tf:(c + 1) * tf], preferred_element_type=F32)
        up = jnp.dot(h, win_ref[:, d_ff + c * tf:d_ff + (c + 1) * tf], preferred_element_type=F32)
        act_ref[:, c * tf:(c + 1) * tf] = (gate * _sigmoid(gate) * up).astype(act_ref.dtype)
    y = jnp.dot(act_ref[...], wout_ref[...], preferred_element_type=F32)
    gate_c = mod_ref[3 * sub + 2:3 * sub + 3, :]
    z = ALPHA * x + weight * (1.0 + gate_c) * y
    o_ref[...] = _layer_norm(z, g_ref[...], b_ref[...])


def _ffn(x, mod9, w_in, w_out, g, b, *, sub, weight, tm, tf):
    s, d = x.shape
    d_ff = w_out.shape[0]
    return pl.pallas_call(
        functools.partial(_ffn_kernel, sub=sub, weight=weight, tf=tf),
        grid=(s // tm,),
        in_specs=[pl.BlockSpec((tm, d), lambda i: (i, 0)),
                  _const_spec(mod9.shape),
                  _const_spec(w_in.shape),
                  _const_spec(w_out.shape),
                  _const_spec(g.shape),
                  _const_spec(b.shape)],
        out_specs=pl.BlockSpec((tm, d), lambda i: (i, 0)),
        out_shape=jax.ShapeDtypeStruct((s, d), F32),
        scratch_shapes=[pltpu.VMEM((tm, d_ff), w_out.dtype)],
        compiler_params=pltpu.CompilerParams(dimension_semantics=("arbitrary",),
                                             vmem_limit_bytes=VMEM_LIMIT),
        name=f"ffn{sub}",
    )(x, mod9, w_in, w_out, g, b)


def _mix_in_kernel(x_ref, mod_ref, w_ref, rope_ref, u_ref, qt_ref, k_ref, vt_ref, *, sub, cw, aw, q_scale):
    tm = x_ref.shape[0]
    h = _modulate(x_ref[...], mod_ref, sub).astype(w_ref.dtype)
    proj = jnp.dot(h, w_ref[...], preferred_element_type=F32)
    u_ref[...] = proj[:, :cw] * _sigmoid(proj[:, cw:2 * cw])
    pos = (pl.program_id(0) * tm + lax.broadcasted_iota(jnp.int32, (tm, LANES), 0)).astype(F32)
    ang = pos * rope_ref[0:1, :]
    rc, sin = jnp.cos(ang), jnp.sin(ang)
    rs1, rs2 = sin * rope_ref[1:2, :], sin * rope_ref[2:3, :]
    half = ROT_DIM // 2

    def rope(t):
        return t * rc + pltpu.roll(t, LANES - half, 1) * rs1 + pltpu.roll(t, half, 1) * rs2

    q0, k0, v0 = 2 * cw, 2 * cw + aw, 2 * cw + 2 * aw
    for g in range(aw // LANES):
        sl = slice(g * LANES, (g + 1) * LANES)
        qt_ref[g] = (rope(proj[:, q0 + g * LANES:q0 + (g + 1) * LANES]) * q_scale).T.astype(BF16)
        k_ref[:, sl] = rope(proj[:, k0 + g * LANES:k0 + (g + 1) * LANES]).astype(BF16)
        tk = vt_ref.shape[3]
        for c in range(tm // tk):
            vt_ref[g, c] = proj[c * tk:(c + 1) * tk, v0 + g * LANES:v0 + (g + 1) * LANES].T.astype(BF16)


def _mix_in(x, mod9, w, rope_rows, *, cw, aw, q_scale, tm, tk):
    s, d = x.shape
    n_heads = aw // LANES
    row = lambda i: (i, 0)
    return pl.pallas_call(
        functools.partial(_mix_in_kernel, sub=1, cw=cw, aw=aw, q_scale=q_scale),
        grid=(s // tm,),
        in_specs=[pl.BlockSpec((tm, d), row),
                  _const_spec(mod9.shape),
                  _const_spec(w.shape),
                  _const_spec(rope_rows.shape)],
        out_specs=[pl.BlockSpec((tm, cw), row),
                   pl.BlockSpec((n_heads, LANES, tm), lambda i: (0, 0, i)),
                   pl.BlockSpec((tm, aw), row),
                   pl.BlockSpec((n_heads, tm // tk, LANES, tk), lambda i: (0, i, 0, 0))],
        out_shape=[jax.ShapeDtypeStruct((s, cw), F32),
                   jax.ShapeDtypeStruct((n_heads, LANES, s), BF16),
                   jax.ShapeDtypeStruct((s, aw), BF16),
                   jax.ShapeDtypeStruct((n_heads, s // tk, LANES, tk), BF16)],
        compiler_params=pltpu.CompilerParams(dimension_semantics=("arbitrary",),
                                             vmem_limit_bytes=VMEM_LIMIT),
        name="mix_in",
    )(x, mod9, w, rope_rows)


def _dependent_zero(v):
    r, c = v.shape
    folded = jnp.sum(v.reshape(r // SUBLANES, SUBLANES, c), axis=0)
    folded = sum(folded[:, g * LANES:(g + 1) * LANES] for g in range(c // LANES))
    bits = lax.bitcast_convert_type(folded[0:1, :], jnp.uint32)
    return lax.bitcast_convert_type((bits >> 16) >> 16, F32)


def _conv_tile(i, n_tiles, prev_ref, cur_ref, next_ref, w_ref, cb_ref, g_ref, b_ref, o_ref, ext_ref, sh_ref, y_ref,
               rows):
    tm, cw = cur_ref.shape
    ext_ref[0:CONV_HALO, :] = jnp.where(i > 0, prev_ref[...], 0.0)
    ext_ref[CONV_HALO:CONV_HALO + tm, :] = cur_ref[...]
    ext_ref[CONV_HALO + tm:, :] = jnp.where(i < n_tiles - 1, next_ref[...], 0.0)
    span = sh_ref.shape[1]
    for b in range(SUBLANES):
        sh_ref[b] = ext_ref[b:b + span, :]
    base = CONV_HALO - CONV_PAD
    done = []
    for lc in range(cw // LANES):
        ls = slice(lc * LANES, (lc + 1) * LANES)
        for rc in range(tm // rows):
            r0 = rc * rows
            acc = jnp.zeros((rows, LANES), F32)
            for t in range(CONV_KERNEL):
                off = base + t
                a0 = r0 + SUBLANES * (off // SUBLANES)
                acc = acc + sh_ref[off % SUBLANES, a0:a0 + rows, ls] * w_ref[t:t + 1, ls]
            y_ref[r0:r0 + rows, ls] = acc
            done.append(_dependent_zero(acc))
    y = _layer_norm(y_ref[...] + cb_ref[...], g_ref[...], b_ref[...])
    y = y * _sigmoid(y)
    o_ref[...] = y.astype(BF16)
    done.append(_dependent_zero(y))
    return done


def _conv_specs(s, cw, tm, tile_index):
    nh = tm // CONV_HALO
    last = s // CONV_HALO - 1
    return [pl.BlockSpec((CONV_HALO, cw), lambda *g: (jnp.maximum(tile_index(*g) * nh - 1, 0), 0)),
            pl.BlockSpec((tm, cw), lambda *g: (tile_index(*g), 0)),
            pl.BlockSpec((CONV_HALO, cw), lambda *g: (jnp.minimum((tile_index(*g) + 1) * nh, last), 0))]


def _conv_scratch(tm, cw):
    return [pltpu.VMEM((tm + 2 * CONV_HALO, cw), F32),
            pltpu.VMEM((SUBLANES, tm + 2 * CONV_HALO - SUBLANES, cw), F32),
            pltpu.VMEM((tm, cw), F32)]


def _mixer_kernel(qt_ref, k_ref, vt_ref, lq1_ref, lk1_ref, lq2_ref, lk2_ref, g_ref,
                  up_ref, uc_ref, un_ref, cw_ref, cb_ref, cg_ref, cbeta_ref,
                  o_ref, conv_ref,
                  rhs_ref, acc_ref, l_ref, kmax_ref, ext_ref, sh_ref, y_ref, *, tk, conv_rows, lam_init):
    step = pl.program_id(0) * pl.num_programs(1) + pl.program_id(1)
    n_steps = pl.num_programs(0) * pl.num_programs(1)

    hd2, tq = qt_ref.shape[1], qt_ref.shape[2]
    n = 2 * tq
    n_kv = k_ref.shape[0] // tk
    qt = qt_ref[0]
    row = lax.broadcasted_iota(jnp.int32, qt.shape, 0)
    zero = jnp.zeros_like(qt)
    rhs_ref[:, :tq] = jnp.where(row < DIFF_HEAD_DIM, qt, zero)
    rhs_ref[:, tq:] = jnp.where(row >= DIFF_HEAD_DIM, qt, zero)

    def k_block(j):
        return k_ref[pl.ds(pl.multiple_of(j * tk, tk), tk), :]

    @pl.when(pl.program_id(1) == 0)
    def _key_norm_bound():
        lane = lax.broadcasted_iota(jnp.int32, (tk, hd2), 1)

        def body(j, carry):
            kf = k_block(j).astype(F32)
            sq = kf * kf
            n0 = jnp.sum(jnp.where(lane < DIFF_HEAD_DIM, sq, 0.0), axis=1, keepdims=True)
            n1 = jnp.sum(jnp.where(lane >= DIFF_HEAD_DIM, sq, 0.0), axis=1, keepdims=True)
            return (jnp.maximum(carry[0], jnp.max(n0, axis=0, keepdims=True)),
                    jnp.maximum(carry[1], jnp.max(n1, axis=0, keepdims=True)))

        init = (jnp.zeros((1, 1), F32), jnp.zeros((1, 1), F32))
        mx0, mx1 = lax.fori_loop(0, n_kv, body, init)
        col = lax.broadcasted_iota(jnp.int32, (1, n), 1)
        kmax_ref[...] = jnp.sqrt(jnp.where(col < tq, mx0, mx1))

    conv_done = _conv_tile(step, n_steps, up_ref, uc_ref, un_ref, cw_ref, cb_ref, cg_ref, cbeta_ref, conv_ref,
                           ext_ref, sh_ref, y_ref, conv_rows)
    conv_stride = (n_kv - 1 - CONV_FIRST_BLOCK) // len(conv_done)
    assert conv_stride >= 1

    r32 = rhs_ref[...].astype(F32)
    qn = jnp.sqrt(jnp.sum(r32 * r32, axis=0, keepdims=True))
    m = qn * kmax_ref[...] * SHIFT_SLACK
    l8 = jnp.zeros((8, n), F32)
    acc = jnp.zeros((hd2, n), F32)
    e_prev = None
    for j in range(n_kv):
        s = jnp.dot(k_ref[j * tk:(j + 1) * tk, :], rhs_ref[...], preferred_element_type=F32)
        if e_prev is not None:
            acc = acc + jnp.dot(vt_ref[0, j - 1], e_prev, preferred_element_type=F32)
        if j >= CONV_FIRST_BLOCK and (j - CONV_FIRST_BLOCK) % conv_stride == 0 and conv_done:
            m = m + jnp.concatenate([conv_done.pop(0)] * (n // LANES), axis=1)
        e = jnp.exp2(s - m)
        l8 = l8 + jnp.sum(e.reshape(tk // 8, 8, n), axis=0)
        e_prev = e.astype(BF16)
    acc = acc + jnp.dot(vt_ref[0, n_kv - 1], e_prev, preferred_element_type=F32)
    l = jnp.sum(l8, axis=0, keepdims=True)
    acc_ref[...] = acc
    l_ref[...] = l

    @pl.when(jnp.logical_not(jnp.min(l) >= L_FLOOR))
    def _running_max_fallback():
        acc_ref[...] = jnp.zeros_like(acc_ref)

        def body(j, carry):
            m_run, l_run = carry
            s = jnp.dot(k_block(j), rhs_ref[...], preferred_element_type=F32)
            m_new = jnp.maximum(m_run, jnp.max(s, axis=0, keepdims=True))
            alpha = jnp.exp2(m_run - m_new)
            e = jnp.exp2(s - m_new)
            pv = jnp.dot(vt_ref[0, j], e.astype(BF16), preferred_element_type=F32)
            acc_ref[...] = alpha * acc_ref[...] + pv
            return m_new, alpha * l_run + jnp.sum(e, axis=0, keepdims=True)

        init = (jnp.full((1, n), -jnp.inf, F32), jnp.zeros((1, n), F32))
        _, l_run = lax.fori_loop(0, n_kv, body, init)
        l_ref[...] = l_run

    o = acc_ref[...] * (1.0 / l_ref[...])
    lam = (jnp.exp(jnp.sum(lq1_ref[...] * lk1_ref[...])) - jnp.exp(jnp.sum(lq2_ref[...] * lk2_ref[...]))
           + lam_init)
    o = o[:, :tq] - lam * o[:, tq:]
    ms = jnp.mean(o * o, axis=0, keepdims=True)
    o = o * lax.rsqrt(ms + LN_EPS) * g_ref[...] * (1.0 - lam_init)
    o_ref[...] = o.T.astype(BF16)


def _mixer(qt, k, vt, lq1, lk1, lq2, lk2, g_col, u, conv_w, conv_b, conv_g, conv_beta, *, tq, tk, conv_rows,
           lam_init):
    n_heads, hd2, s = qt.shape
    cw = u.shape[1]
    nq = s // tq
    tc = s // (n_heads * nq)
    assert tc % conv_rows == 0 and tc % CONV_HALO == 0
    lam_spec = _const_spec(lq1.shape)
    tile = lambda h, i: h * nq + i
    return pl.pallas_call(
        functools.partial(_mixer_kernel, tk=tk, conv_rows=conv_rows, lam_init=lam_init),
        grid=(n_heads, nq),
        in_specs=[pl.BlockSpec((1, hd2, tq), lambda h, i: (h, 0, i)),
                  pl.BlockSpec((s, hd2), lambda h, i: (0, h)),
                  pl.BlockSpec((1, s // tk, hd2, tk), lambda h, i: (h, 0, 0, 0)),
                  lam_spec, lam_spec, lam_spec, lam_spec,
                  _const_spec(g_col.shape),
                  *_conv_specs(s, cw, tc, tile),
                  _const_spec(conv_w.shape),
                  _const_spec(conv_b.shape),
                  _const_spec(conv_g.shape),
                  _const_spec(conv_beta.shape)],
        out_specs=[pl.BlockSpec((tq, hd2), lambda h, i: (i, h)),
                   pl.BlockSpec((tc, cw), lambda h, i: (tile(h, i), 0))],
        out_shape=[jax.ShapeDtypeStruct((s, n_heads * hd2), BF16),
                   jax.ShapeDtypeStruct((s, cw), BF16)],
        scratch_shapes=[pltpu.VMEM((hd2, 2 * tq), BF16),
                        pltpu.VMEM((hd2, 2 * tq), F32),
                        pltpu.VMEM((1, 2 * tq), F32),
                        pltpu.VMEM((1, 2 * tq), F32),
                        *_conv_scratch(tc, cw)],
        compiler_params=pltpu.CompilerParams(dimension_semantics=("arbitrary", "arbitrary"),
                                             vmem_limit_bytes=VMEM_LIMIT),
        name="mixer",
    )(qt, k, vt, lq1, lk1, lq2, lk2, g_col, u, u, u, conv_w, conv_b, conv_g, conv_beta)


def _mix_out_kernel(conv_ref, attn_ref, x_ref, mod_ref, w_ref, g_ref, b_ref, o_ref, *, sub):
    cw = conv_ref.shape[1]
    y = (jnp.dot(conv_ref[...], w_ref[:cw, :], preferred_element_type=F32)
         + jnp.dot(attn_ref[...], w_ref[cw:, :], preferred_element_type=F32))
    gate_c = mod_ref[3 * sub + 2:3 * sub + 3, :]
    z = ALPHA * x_ref[...] + (1.0 + gate_c) * y
    o_ref[...] = _layer_norm(z, g_ref[...], b_ref[...])


def _mix_out(conv, attn, x, mod9, w, g, b, *, tm):
    s, d = x.shape
    row = lambda i: (i, 0)
    return pl.pallas_call(
        functools.partial(_mix_out_kernel, sub=1),
        grid=(s // tm,),
        in_specs=[pl.BlockSpec((tm, conv.shape[1]), row),
                  pl.BlockSpec((tm, attn.shape[1]), row),
                  pl.BlockSpec((tm, d), row),
                  _const_spec(mod9.shape),
                  _const_spec(w.shape),
                  _const_spec(g.shape),
                  _const_spec(b.shape)],
        out_specs=pl.BlockSpec((tm, d), row),
        out_shape=jax.ShapeDtypeStruct((s, d), F32),
        compiler_params=pltpu.CompilerParams(dimension_semantics=("arbitrary",),
                                             vmem_limit_bytes=VMEM_LIMIT),
        name="mix_out",
    )(conv, attn, x, mod9, w, g, b)


def _rope_rows():
    inv_freq = ROPE_THETA ** (-jnp.arange(0, ROT_DIM, 2, dtype=F32) / ROT_DIM)
    half = ROT_DIM // 2
    zeros_h = jnp.zeros((half,), F32)
    zeros_p = jnp.zeros((DIFF_HEAD_DIM - ROT_DIM,), F32)
    ones_h = jnp.ones((half,), F32)
    reps = LANES // DIFF_HEAD_DIM
    freq = jnp.tile(jnp.concatenate([inv_freq, inv_freq, zeros_p]), reps)
    neg_first = jnp.tile(jnp.concatenate([-ones_h, zeros_h, zeros_p]), reps)
    pos_second = jnp.tile(jnp.concatenate([zeros_h, ones_h, zeros_p]), reps)
    return jnp.stack([freq, neg_first, pos_second])


def kernel(x, c, w_ada, b_ada, ffn1_w_in, ffn1_w_out, ln1_g, ln1_b, mix_w_in, conv_w, conv_b, conv_ln_g,
           conv_ln_b, lambda_q1, lambda_k1, lambda_q2, lambda_k2, subln_g, mix_w_out, ln2_g, ln2_b,
           ffn2_w_in, ffn2_w_out, ln3_g, ln3_b):
    batch, s, d = x.shape
    assert batch == 1 and w_ada.shape[0] == DEPTH == 1
    cw = conv_w.shape[2]
    aw = (mix_w_in.shape[2] - 2 * cw) // 3
    hd2 = 2 * DIFF_HEAD_DIM
    n_heads = aw // hd2
    lam_init = 0.8 - 0.6 * math.exp(-0.3 * 0)
    q_scale = math.log2(math.e) / math.sqrt(DIFF_HEAD_DIM)
    tq, tk = 512, 256

    mod9 = _ada(c.reshape(d, 1), w_ada[0], b_ada, tn=1152).reshape(9, d)
    x0 = x[0]
    x1 = _ffn(x0, mod9, ffn1_w_in[0], ffn1_w_out[0], ln1_g, ln1_b,
              sub=0, weight=0.5, tm=512, tf=256)

    u, qt, k, vt = _mix_in(x1, mod9, mix_w_in[0], _rope_rows(),
                           cw=cw, aw=aw, q_scale=q_scale, tm=512, tk=tk)
    attn, conv = _mixer(qt, k, vt, lambda_q1, lambda_k1, lambda_q2, lambda_k2, subln_g.reshape(hd2, 1),
                        u, conv_w[0], conv_b, conv_ln_g, conv_ln_b,
                        tq=tq, tk=tk, conv_rows=32, lam_init=lam_init)
    x2 = _mix_out(conv, attn, x1, mod9, mix_w_out[0].astype(BF16), ln2_g, ln2_b, tm=512)

    x3 = _ffn(x2, mod9, ffn2_w_in[0], ffn2_w_out[0], ln3_g, ln3_b,
              sub=2, weight=0.5, tm=512, tf=256)
    return x3[None]
```

```python
import functools
import math

import jax
import jax.numpy as jnp
from jax import lax
from jax.experimental import pallas as pl
from jax.experimental.pallas import tpu as pltpu

F32 = jnp.float32
BF16 = jnp.bfloat16

DEPTH = 1
ALPHA = (2.0 * DEPTH) ** 0.25
LN_EPS = 1e-5
DIFF_HEAD_DIM = 64
ROT_DIM = DIFF_HEAD_DIM // 4
ROPE_THETA = 500000.0
CONV_KERNEL = 31
CONV_PAD = (CONV_KERNEL - 1) // 2
CONV_HALO = 16
LANES = 128
SUBLANES = 8
SHIFT_SLACK = 1.0 + 2.0 ** -6
L_FLOOR = 2.0 ** -80
CONV_FIRST_BLOCK = 10
VMEM_LIMIT = 56 * 1024 * 1024


def _sigmoid(x):
    return 1.0 / (1.0 + jnp.exp(-x))


def _layer_norm(z, g, b):
    mu = jnp.mean(z, axis=-1, keepdims=True)
    zc = z - mu
    var = jnp.mean(zc * zc, axis=-1, keepdims=True)
    return zc * lax.rsqrt(var + LN_EPS) * g + b


def _modulate(x, mod_ref, sub):
    shift = mod_ref[3 * sub:3 * sub + 1, :]
    scale = mod_ref[3 * sub + 1:3 * sub + 2, :]
    return x * (1.0 + scale) + shift


def _const_spec(shape):
    return pl.BlockSpec(shape, lambda *_: (0,) * len(shape), pipeline_mode=pl.Buffered(1))


def _ada_kernel(c_ref, w_ref, b_ref, o_ref):
    c = c_ref[...]
    ca = c * _sigmoid(c)
    o_ref[...] = jnp.sum(ca * w_ref[...], axis=0, keepdims=True) + b_ref[...]


def _ada(c_col, w, b_row, tn):
    d, n = w.shape
    return pl.pallas_call(
        _ada_kernel,
        grid=(n // tn,),
        in_specs=[pl.BlockSpec((d, 1), lambda j: (0, 0)),
                  pl.BlockSpec((d, tn), lambda j: (0, j)),
                  pl.BlockSpec((1, tn), lambda j: (0, j))],
        out_specs=pl.BlockSpec((1, tn), lambda j: (0, j)),
        out_shape=jax.ShapeDtypeStruct((1, n), F32),
        compiler_params=pltpu.CompilerParams(dimension_semantics=("arbitrary",),
                                             vmem_limit_bytes=VMEM_LIMIT),
        name="ada",
    )(c_col, w, b_row)


def _ffn_kernel(x_ref, mod_ref, win_ref, wout_ref, g_ref, b_ref, o_ref, act_ref, *, sub, weight, tf):
    x = x_ref[...]
    d_ff = wout_ref.shape[0]
    h = _modulate(x, mod_ref, sub).astype(win_ref.dtype)
    for c in range(d_ff // tf):
        gate = jnp.dot(h, win_ref[:, c * tf:(c + 1) * tf], preferred_element_type=F32)
        up = jnp.dot(h, win_ref[:, d_ff + c * tf:d_ff + (c + 1) * tf], preferred_element_type=F32)
        act_ref[:, c * tf:(c + 1) * tf] = (gate * _sigmoid(gate) * up).astype(act_ref.dtype)
    y = jnp.dot(act_ref[...], wout_ref[...], preferred_element_type=F32)
    gate_c = mod_ref[3 * sub + 2:3 * sub + 3, :]
    z = ALPHA * x + weight * (1.0 + gate_c) * y
    o_ref[...] = _layer_norm(z, g_ref[...], b_ref[...])


def _ffn(x, mod9, w_in, w_out, g, b, *, sub, weight, tm, tf):
    s, d = x.shape
    d_ff = w_out.shape[0]
    return pl.pallas_call(
        functools.partial(_ffn_kernel, sub=sub, weight=weight, tf=tf),
        grid=(s // tm,),
        in_specs=[pl.BlockSpec((tm, d), lambda i: (i, 0)),
                  _const_spec(mod9.shape),
                  _const_spec(w_in.shape),
                  _const_spec(w_out.shape),
                  _const_spec(g.shape),
                  _const_spec(b.shape)],
        out_specs=pl.BlockSpec((tm, d), lambda i: (i, 0)),
        out_shape=jax.ShapeDtypeStruct((s, d), F32),
        scratch_shapes=[pltpu.VMEM((tm, d_ff), w_out.dtype)],
        compiler_params=pltpu.CompilerParams(dimension_semantics=("arbitrary",),
                                             vmem_limit_bytes=VMEM_LIMIT),
        name=f"ffn{sub}",
    )(x, mod9, w_in, w_out, g, b)


def _mix_in_kernel(x_ref, mod_ref, w_ref, rope_ref, u_ref, qt_ref, k_ref, vt_ref, cos_ref, sin_ref, *,
                   sub, cw, aw, q_scale):
    tm = x_ref.shape[0]
    freq = rope_ref[0:1, :]

    @pl.when(pl.program_id(0) == 0)
    def _in_tile_angles():
        ang = lax.broadcasted_iota(jnp.int32, (tm, LANES), 0).astype(F32) * freq
        cos_ref[...] = jnp.cos(ang)
        sin_ref[...] = jnp.sin(ang)

    h = _modulate(x_ref[...], mod_ref, sub).astype(w_ref.dtype)
    proj = jnp.dot(h, w_ref[...], preferred_element_type=F32)
    u_ref[...] = proj[:, :cw] * _sigmoid(proj[:, cw:2 * cw])
    ang0 = (pl.program_id(0) * tm).astype(F32) * freq
    c0, s0 = jnp.cos(ang0), jnp.sin(ang0)
    rc = c0 * cos_ref[...] - s0 * sin_ref[...]
    sin = s0 * cos_ref[...] + c0 * sin_ref[...]
    rs1, rs2 = sin * rope_ref[1:2, :], sin * rope_ref[2:3, :]
    half = ROT_DIM // 2

    def rope(t):
        return t * rc + pltpu.roll(t, LANES - half, 1) * rs1 + pltpu.roll(t, half, 1) * rs2

    q0, k0, v0 = 2 * cw, 2 * cw + aw, 2 * cw + 2 * aw
    for g in range(aw // LANES):
        sl = slice(g * LANES, (g + 1) * LANES)
        qt_ref[g] = (rope(proj[:, q0 + g * LANES:q0 + (g + 1) * LANES]) * q_scale).T.astype(BF16)
        k_ref[:, sl] = rope(proj[:, k0 + g * LANES:k0 + (g + 1) * LANES]).astype(BF16)
        tk = vt_ref.shape[3]
        for c in range(tm // tk):
            vt_ref[g, c] = proj[c * tk:(c + 1) * tk, v0 + g * LANES:v0 + (g + 1) * LANES].T.astype(BF16)


def _mix_in(x, mod9, w, rope_rows, *, cw, aw, q_scale, tm, tk):
    s, d = x.shape
    n_heads = aw // LANES
    row = lambda i: (i, 0)
    return pl.pallas_call(
        functools.partial(_mix_in_kernel, sub=1, cw=cw, aw=aw, q_scale=q_scale),
        grid=(s // tm,),
        in_specs=[pl.BlockSpec((tm, d), row),
                  _const_spec(mod9.shape),
                  _const_spec(w.shape),
                  _const_spec(rope_rows.shape)],
        out_specs=[pl.BlockSpec((tm, cw), row),
                   pl.BlockSpec((n_heads, LANES, tm), lambda i: (0, 0, i)),
                   pl.BlockSpec((tm, aw), row),
                   pl.BlockSpec((n_heads, tm // tk, LANES, tk), lambda i: (0, i, 0, 0))],
        out_shape=[jax.ShapeDtypeStruct((s, cw), F32),
                   jax.ShapeDtypeStruct((n_heads, LANES, s), BF16),
                   jax.ShapeDtypeStruct((s, aw), BF16),
                   jax.ShapeDtypeStruct((n_heads, s // tk, LANES, tk), BF16)],
        scratch_shapes=[pltpu.VMEM((tm, LANES), F32),
                        pltpu.VMEM((tm, LANES), F32)],
        compiler_params=pltpu.CompilerParams(dimension_semantics=("arbitrary",),
                                             vmem_limit_bytes=VMEM_LIMIT),
        name="mix_in",
    )(x, mod9, w, rope_rows)


def _dependent_zero(v):
    r, c = v.shape
    folded = jnp.sum(v.reshape(r // SUBLANES, SUBLANES, c), axis=0)
    folded = sum(folded[:, g * LANES:(g + 1) * LANES] for g in range(c // LANES))
    bits = lax.bitcast_convert_type(folded[0:1, :], jnp.uint32)
    return lax.bitcast_convert_type((bits >> 16) >> 16, F32)


def _conv_tile(i, n_tiles, prev_ref, cur_ref, next_ref, w_ref, cb_ref, g_ref, b_ref, o_ref, ext_ref, sh_ref, y_ref,
               rows):
    tm, cw = cur_ref.shape
    ext_ref[0:CONV_HALO, :] = jnp.where(i > 0, prev_ref[...], 0.0)
    ext_ref[CONV_HALO:CONV_HALO + tm, :] = cur_ref[...]
    ext_ref[CONV_HALO + tm:, :] = jnp.where(i < n_tiles - 1, next_ref[...], 0.0)
    span = sh_ref.shape[1]
    for b in range(SUBLANES):
        sh_ref[b] = ext_ref[b:b + span, :]
    base = CONV_HALO - CONV_PAD
    done = []
    for lc in range(cw // LANES):
        ls = slice(lc * LANES, (lc + 1) * LANES)
        for rc in range(tm // rows):
            r0 = rc * rows
            acc = jnp.zeros((rows, LANES), F32)
            for t in range(CONV_KERNEL):
                off = base + t
                a0 = r0 + SUBLANES * (off // SUBLANES)
                acc = acc + sh_ref[off % SUBLANES, a0:a0 + rows, ls] * w_ref[t:t + 1, ls]
            y_ref[r0:r0 + rows, ls] = acc
            done.append(_dependent_zero(acc))
    y = _layer_norm(y_ref[...] + cb_ref[...], g_ref[...], b_ref[...])
    y = y * _sigmoid(y)
    o_ref[...] = y.astype(BF16)
    done.append(_dependent_zero(y))
    return done


def _conv_specs(s, cw, tm, tile_index):
    nh = tm // CONV_HALO
    last = s // CONV_HALO - 1
    return [pl.BlockSpec((CONV_HALO, cw), lambda *g: (jnp.maximum(tile_index(*g) * nh - 1, 0), 0)),
            pl.BlockSpec((tm, cw), lambda *g: (tile_index(*g), 0)),
            pl.BlockSpec((CONV_HALO, cw), lambda *g: (jnp.minimum((tile_index(*g) + 1) * nh, last), 0))]


def _conv_scratch(tm, cw):
    return [pltpu.VMEM((tm + 2 * CONV_HALO, cw), F32),
            pltpu.VMEM((SUBLANES, tm + 2 * CONV_HALO - SUBLANES, cw), F32),
            pltpu.VMEM((tm, cw), F32)]


def _mixer_kernel(qt_ref, k_ref, vt_ref, lq1_ref, lk1_ref, lq2_ref, lk2_ref, g_ref,
                  up_ref, uc_ref, un_ref, cw_ref, cb_ref, cg_ref, cbeta_ref,
                  o_ref, conv_ref,
                  rhs_ref, acc_ref, l_ref, kmax_ref, ext_ref, sh_ref, y_ref, *, tk, conv_rows, lam_init):
    step = pl.program_id(0) * pl.num_programs(1) + pl.program_id(1)
    n_steps = pl.num_programs(0) * pl.num_programs(1)

    hd2, tq = qt_ref.shape[1], qt_ref.shape[2]
    n = 2 * tq
    n_kv = k_ref.shape[0] // tk
    qt = qt_ref[0]
    row = lax.broadcasted_iota(jnp.int32, qt.shape, 0)
    zero = jnp.zeros_like(qt)
    rhs_ref[:, :tq] = jnp.where(row < DIFF_HEAD_DIM, qt, zero)
    rhs_ref[:, tq:] = jnp.where(row >= DIFF_HEAD_DIM, qt, zero)

    def k_block(j):
        return k_ref[pl.ds(pl.multiple_of(j * tk, tk), tk), :]

    @pl.when(pl.program_id(1) == 0)
    def _key_norm_bound():
        comp = lax.broadcasted_iota(jnp.int32, (SUBLANES, hd2), 0)
        lane = lax.broadcasted_iota(jnp.int32, (SUBLANES, hd2), 1)
        ind = jnp.where(comp == lane // DIFF_HEAD_DIM, 1.0, 0.0).astype(BF16)

        def body(j, mx):
            kf = k_block(j).astype(F32)
            sq = (kf * kf).astype(BF16)
            nrm = lax.dot_general(ind, sq, (((1,), (1,)), ((), ())), preferred_element_type=F32)
            return jnp.maximum(mx, nrm)

        mx = lax.fori_loop(0, n_kv, body, jnp.zeros((SUBLANES, tk), F32), unroll=4)
        mx = jnp.max(mx, axis=1, keepdims=True)
        col = lax.broadcasted_iota(jnp.int32, (1, n), 1)
        kmax_ref[...] = jnp.sqrt(jnp.where(col < tq, mx[0:1, :], mx[1:2, :]))

    conv_done = _conv_tile(step, n_steps, up_ref, uc_ref, un_ref, cw_ref, cb_ref, cg_ref, cbeta_ref, conv_ref,
                           ext_ref, sh_ref, y_ref, conv_rows)
    conv_stride = (n_kv - 1 - CONV_FIRST_BLOCK) // len(conv_done)
    assert conv_stride >= 1

    r32 = rhs_ref[...].astype(F32)
    qn = jnp.sqrt(jnp.sum(r32 * r32, axis=0, keepdims=True))
    m = qn * kmax_ref[...] * SHIFT_SLACK
    l8 = jnp.zeros((8, n), F32)
    acc = jnp.zeros((hd2, n), F32)
    e_prev = None
    for j in range(n_kv):
        s = jnp.dot(k_ref[j * tk:(j + 1) * tk, :], rhs_ref[...], preferred_element_type=F32)
        if e_prev is not None:
            acc = acc + jnp.dot(vt_ref[0, j - 1], e_prev, preferred_element_type=F32)
        if j >= CONV_FIRST_BLOCK and (j - CONV_FIRST_BLOCK) % conv_stride == 0 and conv_done:
            m = m + jnp.concatenate([conv_done.pop(0)] * (n // LANES), axis=1)
        e = jnp.exp2(s - m)
        l8 = l8 + jnp.sum(e.reshape(tk // 8, 8, n), axis=0)
        e_prev = e.astype(BF16)
    acc = acc + jnp.dot(vt_ref[0, n_kv - 1], e_prev, preferred_element_type=F32)
    l = jnp.sum(l8, axis=0, keepdims=True)
    acc_ref[...] = acc
    l_ref[...] = l

    @pl.when(jnp.logical_not(jnp.min(l) >= L_FLOOR))
    def _running_max_fallback():
        acc_ref[...] = jnp.zeros_like(acc_ref)

        def body(j, carry):
            m_run, l_run = carry
            s = jnp.dot(k_block(j), rhs_ref[...], preferred_element_type=F32)
            m_new = jnp.maximum(m_run, jnp.max(s, axis=0, keepdims=True))
            alpha = jnp.exp2(m_run - m_new)
            e = jnp.exp2(s - m_new)
            pv = jnp.dot(vt_ref[0, j], e.astype(BF16), preferred_element_type=F32)
            acc_ref[...] = alpha * acc_ref[...] + pv
            return m_new, alpha * l_run + jnp.sum(e, axis=0, keepdims=True)

        init = (jnp.full((1, n), -jnp.inf, F32), jnp.zeros((1, n), F32))
        _, l_run = lax.fori_loop(0, n_kv, body, init)
        l_ref[...] = l_run

    o = acc_ref[...] * (1.0 / l_ref[...])
    lam = (jnp.exp(jnp.sum(lq1_ref[...] * lk1_ref[...])) - jnp.exp(jnp.sum(lq2_ref[...] * lk2_ref[...]))
           + lam_init)
    o = o[:, :tq] - lam * o[:, tq:]
    ms = jnp.mean(o * o, axis=0, keepdims=True)
    o = o * lax.rsqrt(ms + LN_EPS) * g_ref[...] * (1.0 - lam_init)
    o_ref[...] = o.T.astype(BF16)


def _mixer(qt, k, vt, lq1, lk1, lq2, lk2, g_col, u, conv_w, conv_b, conv_g, conv_beta, *, tq, tk, conv_rows,
           lam_init):
    n_heads, hd2, s = qt.shape
    cw = u.shape[1]
    nq = s // tq
    tc = s // (n_heads * nq)
    assert tc % conv_rows == 0 and tc % CONV_HALO == 0
    lam_spec = _const_spec(lq1.shape)
    tile = lambda h, i: h * nq + i
    return pl.pallas_call(
        functools.partial(_mixer_kernel, tk=tk, conv_rows=conv_rows, lam_init=lam_init),
        grid=(n_heads, nq),
        in_specs=[pl.BlockSpec((1, hd2, tq), lambda h, i: (h, 0, i)),
                  pl.BlockSpec((s, hd2), lambda h, i: (0, h)),
                  pl.BlockSpec((1, s // tk, hd2, tk), lambda h, i: (h, 0, 0, 0)),
                  lam_spec, lam_spec, lam_spec, lam_spec,
                  _const_spec(g_col.shape),
                  *_conv_specs(s, cw, tc, tile),
                  _const_spec(conv_w.shape),
                  _const_spec(conv_b.shape),
                  _const_spec(conv_g.shape),
                  _const_spec(conv_beta.shape)],
        out_specs=[pl.BlockSpec((tq, hd2), lambda h, i: (i, h)),
                   pl.BlockSpec((tc, cw), lambda h, i: (tile(h, i), 0))],
        out_shape=[jax.ShapeDtypeStruct((s, n_heads * hd2), BF16),
                   jax.ShapeDtypeStruct((s, cw), BF16)],
        scratch_shapes=[pltpu.VMEM((hd2, 2 * tq), BF16),
                        pltpu.VMEM((hd2, 2 * tq), F32),
                        pltpu.VMEM((1, 2 * tq), F32),
                        pltpu.VMEM((1, 2 * tq), F32),
                        *_conv_scratch(tc, cw)],
        compiler_params=pltpu.CompilerParams(dimension_semantics=("arbitrary", "arbitrary"),
                                             vmem_limit_bytes=VMEM_LIMIT),
        name="mixer",
    )(qt, k, vt, lq1, lk1, lq2, lk2, g_col, u, u, u, conv_w, conv_b, conv_g, conv_beta)


def _mix_out_kernel(conv_ref, attn_ref, x_ref, mod_ref, w_ref, g_ref, b_ref, o_ref, *, sub):
    cw = conv_ref.shape[1]
    y = (jnp.dot(conv_ref[...], w_ref[:cw, :], preferred_element_type=F32)
         + jnp.dot(attn_ref[...], w_ref[cw:, :], preferred_element_type=F32))
    gate_c = mod_ref[3 * sub + 2:3 * sub + 3, :]
    z = ALPHA * x_ref[...] + (1.0 + gate_c) * y
    o_ref[...] = _layer_norm(z, g_ref[...], b_ref[...])


def _mix_out(conv, attn, x, mod9, w, g, b, *, tm):
    s, d = x.shape
    row = lambda i: (i, 0)
    return pl.pallas_call(
        functools.partial(_mix_out_kernel, sub=1),
        grid=(s // tm,),
        in_specs=[pl.BlockSpec((tm, conv.shape[1]), row),
                  pl.BlockSpec((tm, attn.shape[1]), row),
                  pl.BlockSpec((tm, d), row),
                  _const_spec(mod9.shape),
                  _const_spec(w.shape),
                  _const_spec(g.shape),
                  _const_spec(b.shape)],
        out_specs=pl.BlockSpec((tm, d), row),
        out_shape=jax.ShapeDtypeStruct((s, d), F32),
        compiler_params=pltpu.CompilerParams(dimension_semantics=("arbitrary",),
                                             vmem_limit_bytes=VMEM_LIMIT),
        name="mix_out",
    )(conv, attn, x, mod9, w, g, b)


def _rope_rows():
    inv_freq = ROPE_THETA ** (-jnp.arange(0, ROT_DIM, 2, dtype=F32) / ROT_DIM)
    half = ROT_DIM // 2
    zeros_h = jnp.zeros((half,), F32)
    zeros_p = jnp.zeros((DIFF_HEAD_DIM - ROT_DIM,), F32)
    ones_h = jnp.ones((half,), F32)
    reps = LANES // DIFF_HEAD_DIM
    freq = jnp.tile(jnp.concatenate([inv_freq, inv_freq, zeros_p]), reps)
    neg_first = jnp.tile(jnp.concatenate([-ones_h, zeros_h, zeros_p]), reps)
    pos_second = jnp.tile(jnp.concatenate([zeros_h, ones_h, zeros_p]), reps)
    return jnp.stack([freq, neg_first, pos_second])


def kernel(x, c, w_ada, b_ada, ffn1_w_in, ffn1_w_out, ln1_g, ln1_b, mix_w_in, conv_w, conv_b, conv_ln_g,
           conv_ln_b, lambda_q1, lambda_k1, lambda_q2, lambda_k2, subln_g, mix_w_out, ln2_g, ln2_b,
           ffn2_w_in, ffn2_w_out, ln3_g, ln3_b):
    batch, s, d = x.shape
    assert batch == 1 and w_ada.shape[0] == DEPTH == 1
    cw = conv_w.shape[2]
    aw = (mix_w_in.shape[2] - 2 * cw) // 3
    hd2 = 2 * DIFF_HEAD_DIM
    n_heads = aw // hd2
    lam_init = 0.8 - 0.6 * math.exp(-0.3 * 0)
    q_scale = math.log2(math.e) / math.sqrt(DIFF_HEAD_DIM)
    tq, tk = 512, 256

    mod9 = _ada(c.reshape(d, 1), w_ada[0], b_ada, tn=1152).reshape(9, d)
    x0 = x[0]
    x1 = _ffn(x0, mod9, ffn1_w_in[0], ffn1_w_out[0], ln1_g, ln1_b,
              sub=0, weight=0.5, tm=512, tf=256)

    u, qt, k, vt = _mix_in(x1, mod9, mix_w_in[0], _rope_rows(),
                           cw=cw, aw=aw, q_scale=q_scale, tm=512, tk=tk)
    attn, conv = _mixer(qt, k, vt, lambda_q1, lambda_k1, lambda_q2, lambda_k2, subln_g.reshape(hd2, 1),
                        u, conv_w[0], conv_b, conv_ln_g, conv_ln_b,
                        tq=tq, tk=tk, conv_rows=32, lam_init=lam_init)
    x2 = _mix_out(conv, attn, x1, mod9, mix_w_out[0].astype(BF16), ln2_g, ln2_b, tm=512)

    x3 = _ffn(x2, mod9, ffn2_w_in[0], ffn2_w_out[0], ln3_g, ln3_b,
              sub=2, weight=0.5, tm=512, tf=256)
    return x3[None]
```

```python
import functools
import math

import jax
import jax.numpy as jnp
from jax import lax
from jax.experimental import pallas as pl
from jax.experimental.pallas import tpu as pltpu

F32 = jnp.float32
BF16 = jnp.bfloat16

DEPTH = 1
ALPHA = (2.0 * DEPTH) ** 0.25
LN_EPS = 1e-5
DIFF_HEAD_DIM = 64
ROT_DIM = DIFF_HEAD_DIM // 4
ROPE_THETA = 500000.0
CONV_KERNEL = 31
CONV_PAD = (CONV_KERNEL - 1) // 2
CONV_HALO = 16
LANES = 128
SUBLANES = 8
SHIFT_SLACK = 1.0 + 2.0 ** -10
L_FLOOR = 2.0 ** -80
CONV_FIRST_BLOCK = 10
VMEM_LIMIT = 56 * 1024 * 1024


def _sigmoid(x):
    return 1.0 / (1.0 + jnp.exp(-x))


def _layer_norm(z, g, b):
    mu = jnp.mean(z, axis=-1, keepdims=True)
    zc = z - mu
    var = jnp.mean(zc * zc, axis=-1, keepdims=True)
    return zc * lax.rsqrt(var + LN_EPS) * g + b


def _modulate(x, mod_ref, sub):
    shift = mod_ref[3 * sub:3 * sub + 1, :]
    scale = mod_ref[3 * sub + 1:3 * sub + 2, :]
    return x * (1.0 + scale) + shift


def _const_spec(shape):
    return pl.BlockSpec(shape, lambda *_: (0,) * len(shape), pipeline_mode=pl.Buffered(1))


def _ada_kernel(c_ref, w_ref, b_ref, o_ref):
    c = c_ref[...]
    ca = c * _sigmoid(c)
    o_ref[...] = jnp.sum(ca * w_ref[...], axis=0, keepdims=True) + b_ref[...]


def _ada(c_col, w, b_row, tn):
    d, n = w.shape
    return pl.pallas_call(
        _ada_kernel,
        grid=(n // tn,),
        in_specs=[pl.BlockSpec((d, 1), lambda j: (0, 0)),
                  pl.BlockSpec((d, tn), lambda j: (0, j)),
                  pl.BlockSpec((1, tn), lambda j: (0, j))],
        out_specs=pl.BlockSpec((1, tn), lambda j: (0, j)),
        out_shape=jax.ShapeDtypeStruct((1, n), F32),
        compiler_params=pltpu.CompilerParams(dimension_semantics=("arbitrary",),
                                             vmem_limit_bytes=VMEM_LIMIT),
        name="ada",
    )(c_col, w, b_row)


def _ffn_kernel(x_ref, mod_ref, win_ref, wout_ref, g_ref, b_ref, o_ref, act_ref, *, sub, weight, tf):
    x = x_ref[...]
    d_ff = wout_ref.shape[0]
    h = _modulate(x, mod_ref, sub).astype(win_ref.dtype)
    for c in range(d_ff // tf):
        gate = jnp.dot(h, win_ref[:, c * tf:(c + 1) * tf], preferred_element_type=F32)
        up = jnp.dot(h, win_ref[:, d_ff + c * tf:d_ff + (c + 1) * tf], preferred_element_type=F32)
        act_ref[:, c * tf:(c + 1) * tf] = (gate * _sigmoid(gate) * up).astype(act_ref.dtype)
    y = jnp.dot(act_ref[...], wout_ref[...], preferred_element_type=F32)
    gate_c = mod_ref[3 * sub + 2:3 * sub + 3, :]
    z = ALPHA * x + weight * (1.0 + gate_c) * y
    o_ref[...] = _layer_norm(z, g_ref[...], b_ref[...])


def _ffn(x, mod9, w_in, w_out, g, b, *, sub, weight, tm, tf):
    s, d = x.shape
    d_ff = w_out.shape[0]
    return pl.pallas_call(
        functools.partial(_ffn_kernel, sub=sub, weight=weight, tf=tf),
        grid=(s // tm,),
        in_specs=[pl.BlockSpec((tm, d), lambda i: (i, 0)),
                  _const_spec(mod9.shape),
                  _const_spec(w_in.shape),
                  _const_spec(w_out.shape),
                  _const_spec(g.shape),
                  _const_spec(b.shape)],
        out_specs=pl.BlockSpec((tm, d), lambda i: (i, 0)),
        out_shape=jax.ShapeDtypeStruct((s, d), F32),
        scratch_shapes=[pltpu.VMEM((tm, d_ff), w_out.dtype)],
        compiler_params=pltpu.CompilerParams(dimension_semantics=("arbitrary",),
                                             vmem_limit_bytes=VMEM_LIMIT),
        name=f"ffn{sub}",
    )(x, mod9, w_in, w_out, g, b)


def _mix_in_kernel(x_ref, mod_ref, w_ref, rope_ref, u_ref, qt_ref, k_ref, vt_ref, cos_ref, sin_ref, *,
                   sub, cw, aw, q_scale):
    tm = x_ref.shape[0]
    freq = rope_ref[0:1, :]

    @pl.when(pl.program_id(0) == 0)
    def _in_tile_angles():
        ang = lax.broadcasted_iota(jnp.int32, (tm, LANES), 0).astype(F32) * freq
        cos_ref[...] = jnp.cos(ang)
        sin_ref[...] = jnp.sin(ang)

    h = _modulate(x_ref[...], mod_ref, sub).astype(w_ref.dtype)
    proj = jnp.dot(h, w_ref[...], preferred_element_type=F32)
    u_ref[...] = proj[:, :cw] * _sigmoid(proj[:, cw:2 * cw])
    ang0 = (pl.program_id(0) * tm).astype(F32) * freq
    c0, s0 = jnp.cos(ang0), jnp.sin(ang0)
    rc = c0 * cos_ref[...] - s0 * sin_ref[...]
    sin = s0 * cos_ref[...] + c0 * sin_ref[...]
    rs1, rs2 = sin * rope_ref[1:2, :], sin * rope_ref[2:3, :]
    half = ROT_DIM // 2

    def rope(t):
        return t * rc + pltpu.roll(t, LANES - half, 1) * rs1 + pltpu.roll(t, half, 1) * rs2

    q0, k0, v0 = 2 * cw, 2 * cw + aw, 2 * cw + 2 * aw
    for g in range(aw // LANES):
        sl = slice(g * LANES, (g + 1) * LANES)
        qt_ref[g] = (rope(proj[:, q0 + g * LANES:q0 + (g + 1) * LANES]) * q_scale).T.astype(BF16)
        k_ref[:, sl] = rope(proj[:, k0 + g * LANES:k0 + (g + 1) * LANES]).astype(BF16)
        tk = vt_ref.shape[3]
        for c in range(tm // tk):
            vt_ref[g, c] = proj[c * tk:(c + 1) * tk, v0 + g * LANES:v0 + (g + 1) * LANES].T.astype(BF16)


def _mix_in(x, mod9, w, rope_rows, *, cw, aw, q_scale, tm, tk):
    s, d = x.shape
    n_heads = aw // LANES
    row = lambda i: (i, 0)
    return pl.pallas_call(
        functools.partial(_mix_in_kernel, sub=1, cw=cw, aw=aw, q_scale=q_scale),
        grid=(s // tm,),
        in_specs=[pl.BlockSpec((tm, d), row),
                  _const_spec(mod9.shape),
                  _const_spec(w.shape),
                  _const_spec(rope_rows.shape)],
        out_specs=[pl.BlockSpec((tm, cw), row),
                   pl.BlockSpec((n_heads, LANES, tm), lambda i: (0, 0, i)),
                   pl.BlockSpec((tm, aw), row),
                   pl.BlockSpec((n_heads, tm // tk, LANES, tk), lambda i: (0, i, 0, 0))],
        out_shape=[jax.ShapeDtypeStruct((s, cw), F32),
                   jax.ShapeDtypeStruct((n_heads, LANES, s), BF16),
                   jax.ShapeDtypeStruct((s, aw), BF16),
                   jax.ShapeDtypeStruct((n_heads, s // tk, LANES, tk), BF16)],
        scratch_shapes=[pltpu.VMEM((tm, LANES), F32),
                        pltpu.VMEM((tm, LANES), F32)],
        compiler_params=pltpu.CompilerParams(dimension_semantics=("arbitrary",),
                                             vmem_limit_bytes=VMEM_LIMIT),
        name="mix_in",
    )(x, mod9, w, rope_rows)


def _dependent_zero(v):
    r, c = v.shape
    folded = jnp.sum(v.reshape(r // SUBLANES, SUBLANES, c), axis=0)
    folded = sum(folded[:, g * LANES:(g + 1) * LANES] for g in range(c // LANES))
    bits = lax.bitcast_convert_type(folded[0:1, :], jnp.uint32)
    return lax.bitcast_convert_type((bits >> 16) >> 16, F32)


def _conv_tile(i, n_tiles, prev_ref, cur_ref, next_ref, w_ref, cb_ref, g_ref, b_ref, o_ref, ext_ref, sh_ref, y_ref,
               rows):
    tm, cw = cur_ref.shape
    ext_ref[0:CONV_HALO, :] = jnp.where(i > 0, prev_ref[...], 0.0)
    ext_ref[CONV_HALO:CONV_HALO + tm, :] = cur_ref[...]
    ext_ref[CONV_HALO + tm:, :] = jnp.where(i < n_tiles - 1, next_ref[...], 0.0)
    span = sh_ref.shape[1]
    for b in range(SUBLANES):
        sh_ref[b] = ext_ref[b:b + span, :]
    base = CONV_HALO - CONV_PAD
    done = []
    for lc in range(cw // LANES):
        ls = slice(lc * LANES, (lc + 1) * LANES)
        for rc in range(tm // rows):
            r0 = rc * rows
            acc = jnp.zeros((rows, LANES), F32)
            for t in range(CONV_KERNEL):
                off = base + t
                a0 = r0 + SUBLANES * (off // SUBLANES)
                acc = acc + sh_ref[off % SUBLANES, a0:a0 + rows, ls] * w_ref[t:t + 1, ls]
            y_ref[r0:r0 + rows, ls] = acc
            done.append(_dependent_zero(acc))
    y = _layer_norm(y_ref[...] + cb_ref[...], g_ref[...], b_ref[...])
    y = y * _sigmoid(y)
    o_ref[...] = y.astype(BF16)
    done.append(_dependent_zero(y))
    return done


def _conv_specs(s, cw, tm, tile_index):
    nh = tm // CONV_HALO
    last = s // CONV_HALO - 1
    return [pl.BlockSpec((CONV_HALO, cw), lambda *g: (jnp.maximum(tile_index(*g) * nh - 1, 0), 0)),
            pl.BlockSpec((tm, cw), lambda *g: (tile_index(*g), 0)),
            pl.BlockSpec((CONV_HALO, cw), lambda *g: (jnp.minimum((tile_index(*g) + 1) * nh, last), 0))]


def _conv_scratch(tm, cw):
    return [pltpu.VMEM((tm + 2 * CONV_HALO, cw), F32),
            pltpu.VMEM((SUBLANES, tm + 2 * CONV_HALO - SUBLANES, cw), F32),
            pltpu.VMEM((tm, cw), F32)]


def _mixer_kernel(qt_ref, k_ref, vt_ref, lq1_ref, lk1_ref, lq2_ref, lk2_ref, g_ref,
                  up_ref, uc_ref, un_ref, cw_ref, cb_ref, cg_ref, cbeta_ref,
                  o_ref, conv_ref,
                  rhs_ref, acc_ref, l_ref, kmax_ref, ext_ref, sh_ref, y_ref, *, tk, conv_rows, lam_init):
    step = pl.program_id(0) * pl.num_programs(1) + pl.program_id(1)
    n_steps = pl.num_programs(0) * pl.num_programs(1)

    hd2, tq = qt_ref.shape[1], qt_ref.shape[2]
    n = 2 * tq
    n_kv = k_ref.shape[0] // tk
    qt = qt_ref[0]
    row = lax.broadcasted_iota(jnp.int32, qt.shape, 0)
    zero = jnp.zeros_like(qt)
    rhs_ref[:, :tq] = jnp.where(row < DIFF_HEAD_DIM, qt, zero)
    rhs_ref[:, tq:] = jnp.where(row >= DIFF_HEAD_DIM, qt, zero)

    def k_block(j):
        return k_ref[pl.ds(pl.multiple_of(j * tk, tk), tk), :]

    @pl.when(pl.program_id(1) == 0)
    def _key_norm_bound():
        lane = lax.broadcasted_iota(jnp.int32, (tk, hd2), 1)

        def body(j, carry):
            kf = k_block(j).astype(F32)
            sq = kf * kf
            n0 = jnp.sum(jnp.where(lane < DIFF_HEAD_DIM, sq, 0.0), axis=1, keepdims=True)
            n1 = jnp.sum(jnp.where(lane >= DIFF_HEAD_DIM, sq, 0.0), axis=1, keepdims=True)
            return (jnp.maximum(carry[0], jnp.max(n0, axis=0, keepdims=True)),
                    jnp.maximum(carry[1], jnp.max(n1, axis=0, keepdims=True)))

        init = (jnp.zeros((1, 1), F32), jnp.zeros((1, 1), F32))
        mx0, mx1 = lax.fori_loop(0, n_kv, body, init)
        col = lax.broadcasted_iota(jnp.int32, (1, n), 1)
        kmax_ref[...] = jnp.sqrt(jnp.where(col < tq, mx0, mx1))

    conv_done = _conv_tile(step, n_steps, up_ref, uc_ref, un_ref, cw_ref, cb_ref, cg_ref, cbeta_ref, conv_ref,
                           ext_ref, sh_ref, y_ref, conv_rows)
    conv_stride = (n_kv - 1 - CONV_FIRST_BLOCK) // len(conv_done)
    assert conv_stride >= 1

    r32 = rhs_ref[...].astype(F32)
    qn = jnp.sqrt(jnp.sum(r32 * r32, axis=0, keepdims=True))
    m = qn * kmax_ref[...] * SHIFT_SLACK
    l8 = jnp.zeros((8, n), F32)
    acc = jnp.zeros((hd2, n), F32)
    e_prev = None
    for j in range(n_kv):
        s = jnp.dot(k_ref[j * tk:(j + 1) * tk, :], rhs_ref[...], preferred_element_type=F32)
        if e_prev is not None:
            acc = acc + jnp.dot(vt_ref[0, j - 1], e_prev, preferred_element_type=F32)
        if j >= CONV_FIRST_BLOCK and (j - CONV_FIRST_BLOCK) % conv_stride == 0 and conv_done:
            m = m + jnp.concatenate([conv_done.pop(0)] * (n // LANES), axis=1)
        e = jnp.exp2(s - m)
        l8 = l8 + jnp.sum(e.reshape(tk // 8, 8, n), axis=0)
        e_prev = e.astype(BF16)
    acc = acc + jnp.dot(vt_ref[0, n_kv - 1], e_prev, preferred_element_type=F32)
    l = jnp.sum(l8, axis=0, keepdims=True)
    acc_ref[...] = acc
    l_ref[...] = l

    @pl.when(jnp.logical_not(jnp.min(l) >= L_FLOOR))
    def _running_max_fallback():
        acc_ref[...] = jnp.zeros_like(acc_ref)

        def body(j, carry):
            m_run, l_run = carry
            s = jnp.dot(k_block(j), rhs_ref[...], preferred_element_type=F32)
            m_new = jnp.maximum(m_run, jnp.max(s, axis=0, keepdims=True))
            alpha = jnp.exp2(m_run - m_new)
            e = jnp.exp2(s - m_new)
            pv = jnp.dot(vt_ref[0, j], e.astype(BF16), preferred_element_type=F32)
            acc_ref[...] = alpha * acc_ref[...] + pv
            return m_new, alpha * l_run + jnp.sum(e, axis=0, keepdims=True)

        init = (jnp.full((1, n), -jnp.inf, F32), jnp.zeros((1, n), F32))
        _, l_run = lax.fori_loop(0, n_kv, body, init)
        l_ref[...] = l_run

    o = acc_ref[...] * (1.0 / l_ref[...])
    lam = (jnp.exp(jnp.sum(lq1_ref[...] * lk1_ref[...])) - jnp.exp(jnp.sum(lq2_ref[...] * lk2_ref[...]))
           + lam_init)
    o = o[:, :tq] - lam * o[:, tq:]
    ms = jnp.mean(o * o, axis=0, keepdims=True)
    o = o * lax.rsqrt(ms + LN_EPS) * g_ref[...] * (1.0 - lam_init)
    o_ref[...] = o.T.astype(BF16)


def _mixer(qt, k, vt, lq1, lk1, lq2, lk2, g_col, u, conv_w, conv_b, conv_g, conv_beta, *, tq, tk, conv_rows,
           lam_init):
    n_heads, hd2, s = qt.shape
    cw = u.shape[1]
    nq = s // tq
    tc = s // (n_heads * nq)
    assert tc % conv_rows == 0 and tc % CONV_HALO == 0
    lam_spec = _const_spec(lq1.shape)
    tile = lambda h, i: h * nq + i
    return pl.pallas_call(
        functools.partial(_mixer_kernel, tk=tk, conv_rows=conv_rows, lam_init=lam_init),
        grid=(n_heads, nq),
        in_specs=[pl.BlockSpec((1, hd2, tq), lambda h, i: (h, 0, i)),
                  pl.BlockSpec((s, hd2), lambda h, i: (0, h)),
                  pl.BlockSpec((1, s // tk, hd2, tk), lambda h, i: (h, 0, 0, 0)),
                  lam_spec, lam_spec, lam_spec, lam_spec,
                  _const_spec(g_col.shape),
                  *_conv_specs(s, cw, tc, tile),
                  _const_spec(conv_w.shape),
                  _const_spec(conv_b.shape),
                  _const_spec(conv_g.shape),
                  _const_spec(conv_beta.shape)],
        out_specs=[pl.BlockSpec((tq, hd2), lambda h, i: (i, h)),
                   pl.BlockSpec((tc, cw), lambda h, i: (tile(h, i), 0))],
        out_shape=[jax.ShapeDtypeStruct((s, n_heads * hd2), BF16),
                   jax.ShapeDtypeStruct((s, cw), BF16)],
        scratch_shapes=[pltpu.VMEM((hd2, 2 * tq), BF16),
                        pltpu.VMEM((hd2, 2 * tq), F32),
                        pltpu.VMEM((1, 2 * tq), F32),
                        pltpu.VMEM((1, 2 * tq), F32),
                        *_conv_scratch(tc, cw)],
        compiler_params=pltpu.CompilerParams(dimension_semantics=("arbitrary", "arbitrary"),
                                             vmem_limit_bytes=VMEM_LIMIT),
        name="mixer",
    )(qt, k, vt, lq1, lk1, lq2, lk2, g_col, u, u, u, conv_w, conv_b, conv_g, conv_beta)


def _mix_out_kernel(conv_ref, attn_ref, x_ref, mod_ref, w_ref, g_ref, b_ref, o_ref, *, sub):
    cw = conv_ref.shape[1]
    y = (jnp.dot(conv_ref[...], w_ref[:cw, :], preferred_element_type=F32)
         + jnp.dot(attn_ref[...], w_ref[cw:, :], preferred_element_type=F32))
    gate_c = mod_ref[3 * sub + 2:3 * sub + 3, :]
    z = ALPHA * x_ref[...] + (1.0 + gate_c) * y
    o_ref[...] = _layer_norm(z, g_ref[...], b_ref[...])


def _mix_out(conv, attn, x, mod9, w, g, b, *, tm):
    s, d = x.shape
    row = lambda i: (i, 0)
    return pl.pallas_call(
        functools.partial(_mix_out_kernel, sub=1),
        grid=(s // tm,),
        in_specs=[pl.BlockSpec((tm, conv.shape[1]), row),
                  pl.BlockSpec((tm, attn.shape[1]), row),
                  pl.BlockSpec((tm, d), row),
                  _const_spec(mod9.shape),
                  _const_spec(w.shape),
                  _const_spec(g.shape),
                  _const_spec(b.shape)],
        out_specs=pl.BlockSpec((tm, d), row),
        out_shape=jax.ShapeDtypeStruct((s, d), F32),
        compiler_params=pltpu.CompilerParams(dimension_semantics=("arbitrary",),
                                             vmem_limit_bytes=VMEM_LIMIT),
        name="mix_out",
    )(conv, attn, x, mod9, w, g, b)


def _rope_rows():
    inv_freq = ROPE_THETA ** (-jnp.arange(0, ROT_DIM, 2, dtype=F32) / ROT_DIM)
    half = ROT_DIM // 2
    zeros_h = jnp.zeros((half,), F32)
    zeros_p = jnp.zeros((DIFF_HEAD_DIM - ROT_DIM,), F32)
    ones_h = jnp.ones((half,), F32)
    reps = LANES // DIFF_HEAD_DIM
    freq = jnp.tile(jnp.concatenate([inv_freq, inv_freq, zeros_p]), reps)
    neg_first = jnp.tile(jnp.concatenate([-ones_h, zeros_h, zeros_p]), reps)
    pos_second = jnp.tile(jnp.concatenate([zeros_h, ones_h, zeros_p]), reps)
    return jnp.stack([freq, neg_first, pos_second])


def kernel(x, c, w_ada, b_ada, ffn1_w_in, ffn1_w_out, ln1_g, ln1_b, mix_w_in, conv_w, conv_b, conv_ln_g,
           conv_ln_b, lambda_q1, lambda_k1, lambda_q2, lambda_k2, subln_g, mix_w_out, ln2_g, ln2_b,
           ffn2_w_in, ffn2_w_out, ln3_g, ln3_b):
    batch, s, d = x.shape
    assert batch == 1 and w_ada.shape[0] == DEPTH == 1
    cw = conv_w.shape[2]
    aw = (mix_w_in.shape[2] - 2 * cw) // 3
    hd2 = 2 * DIFF_HEAD_DIM
    n_heads = aw // hd2
    lam_init = 0.8 - 0.6 * math.exp(-0.3 * 0)
    q_scale = math.log2(math.e) / math.sqrt(DIFF_HEAD_DIM)
    tq, tk = 512, 256

    mod9 = _ada(c.reshape(d, 1), w_ada[0], b_ada, tn=1152).reshape(9, d)
    x0 = x[0]
    x1 = _ffn(x0, mod9, ffn1_w_in[0], ffn1_w_out[0], ln1_g, ln1_b,
              sub=0, weight=0.5, tm=512, tf=256)

    u, qt, k, vt = _mix_in(x1, mod9, mix_w_in[0], _rope_rows(),
                           cw=cw, aw=aw, q_scale=q_scale, tm=512, tk=tk)
    attn, conv = _mixer(qt, k, vt, lambda_q1, lambda_k1, lambda_q2, lambda_k2, subln_g.reshape(hd2, 1),
                        u, conv_w[0], conv_b, conv_ln_g, conv_ln_b,
                        tq=tq, tk=tk, conv_rows=32, lam_init=lam_init)
    x2 = _mix_out(conv, attn, x1, mod9, mix_w_out[0].astype(BF16), ln2_g, ln2_b, tm=512)

    x3 = _ffn(x2, mod9, ffn2_w_in[0], ffn2_w_out[0], ln3_g, ln3_b,
              sub=2, weight=0.5, tm=512, tf=256)
    return x3[None]
```

```python
import functools
import math

import jax
import jax.numpy as jnp
from jax import lax
from jax.experimental import pallas as pl
from jax.experimental.pallas import tpu as pltpu

F32 = jnp.float32
BF16 = jnp.bfloat16

DEPTH = 1
ALPHA = (2.0 * DEPTH) ** 0.25
LN_EPS = 1e-5
DIFF_HEAD_DIM = 64
ROT_DIM = DIFF_HEAD_DIM // 4
ROPE_THETA = 500000.0
CONV_KERNEL = 31
CONV_PAD = (CONV_KERNEL - 1) // 2
CONV_HALO = 16
LANES = 128
SUBLANES = 8
SHIFT_SLACK = 1.0 + 2.0 ** -10
L_FLOOR = 2.0 ** -80
CONV_FIRST_BLOCK = 10
VMEM_LIMIT = 56 * 1024 * 1024


def _sigmoid(x):
    return 1.0 / (1.0 + jnp.exp(-x))


def _layer_norm(z, g, b):
    mu = jnp.mean(z, axis=-1, keepdims=True)
    zc = z - mu
    var = jnp.mean(zc * zc, axis=-1, keepdims=True)
    return zc * lax.rsqrt(var + LN_EPS) * g + b


def _modulate(x, mod_ref, sub):
    shift = mod_ref[3 * sub:3 * sub + 1, :]
    scale = mod_ref[3 * sub + 1:3 * sub + 2, :]
    return x * (1.0 + scale) + shift


def _const_spec(shape):
    return pl.BlockSpec(shape, lambda *_: (0,) * len(shape), pipeline_mode=pl.Buffered(1))


def _ada_kernel(c_ref, w_ref, b_ref, o_ref):
    c = c_ref[...]
    ca = c * _sigmoid(c)
    o_ref[...] = jnp.sum(ca * w_ref[...], axis=0, keepdims=True) + b_ref[...]


def _ada(c_col, w, b_row, tn):
    d, n = w.shape
    return pl.pallas_call(
        _ada_kernel,
        grid=(n // tn,),
        in_specs=[pl.BlockSpec((d, 1), lambda j: (0, 0)),
                  pl.BlockSpec((d, tn), lambda j: (0, j)),
                  pl.BlockSpec((1, tn), lambda j: (0, j))],
        out_specs=pl.BlockSpec((1, tn), lambda j: (0, j)),
        out_shape=jax.ShapeDtypeStruct((1, n), F32),
        compiler_params=pltpu.CompilerParams(dimension_semantics=("arbitrary",),
                                             vmem_limit_bytes=VMEM_LIMIT),
        name="ada",
    )(c_col, w, b_row)


def _ffn_kernel(x_ref, mod_ref, win_ref, wout_ref, g_ref, b_ref, o_ref, act_ref, *, sub, weight, tf):
    x = x_ref[...]
    d_ff = wout_ref.shape[0]
    h = _modulate(x, mod_ref, sub).astype(win_ref.dtype)
    for c in range(d_ff // tf):
        gate = jnp.dot(h, win_ref[:, c * tf:(c + 1) * tf], preferred_element_type=F32)
        up = jnp.dot(h, win_ref[:, d_ff + c * tf:d_ff + (c + 1) * tf], preferred_element_type=F32)
        act_ref[:, c * tf:(c + 1) * tf] = (gate * _sigmoid(gate) * up).astype(act_ref.dtype)
    y = jnp.dot(act_ref[...], wout_ref[...], preferred_element_type=F32)
    gate_c = mod_ref[3 * sub + 2:3 * sub + 3, :]
    z = ALPHA * x + weight * (1.0 + gate_c) * y
    o_ref[...] = _layer_norm(z, g_ref[...], b_ref[...])


def _ffn(x, mod9, w_in, w_out, g, b, *, sub, weight, tm, tf):
    s, d = x.shape
    d_ff = w_out.shape[0]
    return pl.pallas_call(
        functools.partial(_ffn_kernel, sub=sub, weight=weight, tf=tf),
        grid=(s // tm,),
        in_specs=[pl.BlockSpec((tm, d), lambda i: (i, 0)),
                  _const_spec(mod9.shape),
                  _const_spec(w_in.shape),
                  _const_spec(w_out.shape),
                  _const_spec(g.shape),
                  _const_spec(b.shape)],
        out_specs=pl.BlockSpec((tm, d), lambda i: (i, 0)),
        out_shape=jax.ShapeDtypeStruct((s, d), F32),
        scratch_shapes=[pltpu.VMEM((tm, d_ff), w_out.dtype)],
        compiler_params=pltpu.CompilerParams(dimension_semantics=("arbitrary",),
                                             vmem_limit_bytes=VMEM_LIMIT),
        name=f"ffn{sub}",
    )(x, mod9, w_in, w_out, g, b)


def _mix_in_kernel(x_ref, mod_ref, w_ref, rope_ref, u_ref, qt_ref, k_ref, vt_ref, cos_ref, sin_ref, *,
                   sub, cw, aw, q_scale, n_part):
    tm = x_ref.shape[0]
    tp = tm // n_part
    tk = vt_ref.shape[3]
    freq = rope_ref[0:1, :]

    @pl.when(pl.program_id(0) == 0)
    def _in_tile_angles():
        ang = lax.broadcasted_iota(jnp.int32, (tm, LANES), 0).astype(F32) * freq
        cos_ref[...] = jnp.cos(ang)
        sin_ref[...] = jnp.sin(ang)

    projs = []
    for p in range(n_part):
        h = _modulate(x_ref[p * tp:(p + 1) * tp, :], mod_ref, sub).astype(w_ref.dtype)
        projs.append(jnp.dot(h, w_ref[...], preferred_element_type=F32))

    ang0 = (pl.program_id(0) * tm).astype(F32) * freq
    c0, s0 = jnp.cos(ang0), jnp.sin(ang0)
    half = ROT_DIM // 2
    q0, k0, v0 = 2 * cw, 2 * cw + aw, 2 * cw + 2 * aw
    for p, proj in enumerate(projs):
        rows = slice(p * tp, (p + 1) * tp)
        u_ref[rows, :] = proj[:, :cw] * _sigmoid(proj[:, cw:2 * cw])
        rc = c0 * cos_ref[rows, :] - s0 * sin_ref[rows, :]
        sin = s0 * cos_ref[rows, :] + c0 * sin_ref[rows, :]
        rs1, rs2 = sin * rope_ref[1:2, :], sin * rope_ref[2:3, :]

        def rope(t):
            return t * rc + pltpu.roll(t, LANES - half, 1) * rs1 + pltpu.roll(t, half, 1) * rs2

        for g in range(aw // LANES):
            sl = slice(g * LANES, (g + 1) * LANES)
            qt_ref[g, :, rows] = (rope(proj[:, q0 + g * LANES:q0 + (g + 1) * LANES]) * q_scale).T.astype(BF16)
            k_ref[rows, sl] = rope(proj[:, k0 + g * LANES:k0 + (g + 1) * LANES]).astype(BF16)
            for c in range(tp // tk):
                vt_ref[g, p * (tp // tk) + c] = (
                    proj[c * tk:(c + 1) * tk, v0 + g * LANES:v0 + (g + 1) * LANES].T.astype(BF16))


def _mix_in(x, mod9, w, rope_rows, *, cw, aw, q_scale, tm, tk, n_part):
    s, d = x.shape
    n_heads = aw // LANES
    assert (tm // n_part) % tk == 0
    row = lambda i: (i, 0)
    return pl.pallas_call(
        functools.partial(_mix_in_kernel, sub=1, cw=cw, aw=aw, q_scale=q_scale, n_part=n_part),
        grid=(s // tm,),
        in_specs=[pl.BlockSpec((tm, d), row),
                  _const_spec(mod9.shape),
                  _const_spec(w.shape),
                  _const_spec(rope_rows.shape)],
        out_specs=[pl.BlockSpec((tm, cw), row),
                   pl.BlockSpec((n_heads, LANES, tm), lambda i: (0, 0, i)),
                   pl.BlockSpec((tm, aw), row),
                   pl.BlockSpec((n_heads, tm // tk, LANES, tk), lambda i: (0, i, 0, 0))],
        out_shape=[jax.ShapeDtypeStruct((s, cw), F32),
                   jax.ShapeDtypeStruct((n_heads, LANES, s), BF16),
                   jax.ShapeDtypeStruct((s, aw), BF16),
                   jax.ShapeDtypeStruct((n_heads, s // tk, LANES, tk), BF16)],
        scratch_shapes=[pltpu.VMEM((tm, LANES), F32),
                        pltpu.VMEM((tm, LANES), F32)],
        compiler_params=pltpu.CompilerParams(dimension_semantics=("arbitrary",),
                                             vmem_limit_bytes=VMEM_LIMIT),
        name="mix_in",
    )(x, mod9, w, rope_rows)


def _dependent_zero(v):
    r, c = v.shape
    folded = jnp.sum(v.reshape(r // SUBLANES, SUBLANES, c), axis=0)
    folded = sum(folded[:, g * LANES:(g + 1) * LANES] for g in range(c // LANES))
    bits = lax.bitcast_convert_type(folded[0:1, :], jnp.uint32)
    return lax.bitcast_convert_type((bits >> 16) >> 16, F32)


def _conv_tile(i, n_tiles, prev_ref, cur_ref, next_ref, w_ref, cb_ref, g_ref, b_ref, o_ref, ext_ref, sh_ref, y_ref,
               rows):
    tm, cw = cur_ref.shape
    ext_ref[0:CONV_HALO, :] = jnp.where(i > 0, prev_ref[...], 0.0)
    ext_ref[CONV_HALO:CONV_HALO + tm, :] = cur_ref[...]
    ext_ref[CONV_HALO + tm:, :] = jnp.where(i < n_tiles - 1, next_ref[...], 0.0)
    span = sh_ref.shape[1]
    for b in range(SUBLANES):
        sh_ref[b] = ext_ref[b:b + span, :]
    base = CONV_HALO - CONV_PAD
    done = []
    for lc in range(cw // LANES):
        ls = slice(lc * LANES, (lc + 1) * LANES)
        for rc in range(tm // rows):
            r0 = rc * rows
            acc = jnp.zeros((rows, LANES), F32)
            for t in range(CONV_KERNEL):
                off = base + t
                a0 = r0 + SUBLANES * (off // SUBLANES)
                acc = acc + sh_ref[off % SUBLANES, a0:a0 + rows, ls] * w_ref[t:t + 1, ls]
            y_ref[r0:r0 + rows, ls] = acc
            done.append(_dependent_zero(acc))
    y = _layer_norm(y_ref[...] + cb_ref[...], g_ref[...], b_ref[...])
    y = y * _sigmoid(y)
    o_ref[...] = y.astype(BF16)
    done.append(_dependent_zero(y))
    return done


def _conv_specs(s, cw, tm, tile_index):
    nh = tm // CONV_HALO
    last = s // CONV_HALO - 1
    return [pl.BlockSpec((CONV_HALO, cw), lambda *g: (jnp.maximum(tile_index(*g) * nh - 1, 0), 0)),
            pl.BlockSpec((tm, cw), lambda *g: (tile_index(*g), 0)),
            pl.BlockSpec((CONV_HALO, cw), lambda *g: (jnp.minimum((tile_index(*g) + 1) * nh, last), 0))]


def _conv_scratch(tm, cw):
    return [pltpu.VMEM((tm + 2 * CONV_HALO, cw), F32),
            pltpu.VMEM((SUBLANES, tm + 2 * CONV_HALO - SUBLANES, cw), F32),
            pltpu.VMEM((tm, cw), F32)]


def _mixer_kernel(qt_ref, k_ref, vt_ref, lq1_ref, lk1_ref, lq2_ref, lk2_ref, g_ref,
                  up_ref, uc_ref, un_ref, cw_ref, cb_ref, cg_ref, cbeta_ref,
                  o_ref, conv_ref,
                  rhs_ref, acc_ref, l_ref, kmax_ref, ext_ref, sh_ref, y_ref, *, tk, conv_rows, lam_init):
    step = pl.program_id(0) * pl.num_programs(1) + pl.program_id(1)
    n_steps = pl.num_programs(0) * pl.num_programs(1)

    hd2, tq = qt_ref.shape[1], qt_ref.shape[2]
    n = 2 * tq
    n_kv = k_ref.shape[0] // tk
    qt = qt_ref[0]
    row = lax.broadcasted_iota(jnp.int32, qt.shape, 0)
    zero = jnp.zeros_like(qt)
    rhs_ref[:, :tq] = jnp.where(row < DIFF_HEAD_DIM, qt, zero)
    rhs_ref[:, tq:] = jnp.where(row >= DIFF_HEAD_DIM, qt, zero)

    def k_block(j):
        return k_ref[pl.ds(pl.multiple_of(j * tk, tk), tk), :]

    @pl.when(pl.program_id(1) == 0)
    def _key_norm_bound():
        lane = lax.broadcasted_iota(jnp.int32, (tk, hd2), 1)

        def body(j, carry):
            kf = k_block(j).astype(F32)
            sq = kf * kf
            n0 = jnp.sum(jnp.where(lane < DIFF_HEAD_DIM, sq, 0.0), axis=1, keepdims=True)
            n1 = jnp.sum(jnp.where(lane >= DIFF_HEAD_DIM, sq, 0.0), axis=1, keepdims=True)
            return (jnp.maximum(carry[0], jnp.max(n0, axis=0, keepdims=True)),
                    jnp.maximum(carry[1], jnp.max(n1, axis=0, keepdims=True)))

        init = (jnp.zeros((1, 1), F32), jnp.zeros((1, 1), F32))
        mx0, mx1 = lax.fori_loop(0, n_kv, body, init)
        col = lax.broadcasted_iota(jnp.int32, (1, n), 1)
        kmax_ref[...] = jnp.sqrt(jnp.where(col < tq, mx0, mx1))

    conv_done = _conv_tile(step, n_steps, up_ref, uc_ref, un_ref, cw_ref, cb_ref, cg_ref, cbeta_ref, conv_ref,
                           ext_ref, sh_ref, y_ref, conv_rows)
    conv_stride = (n_kv - 1 - CONV_FIRST_BLOCK) // len(conv_done)
    assert conv_stride >= 1

    r32 = rhs_ref[...].astype(F32)
    qn = jnp.sqrt(jnp.sum(r32 * r32, axis=0, keepdims=True))
    m = qn * kmax_ref[...] * SHIFT_SLACK
    l8 = jnp.zeros((8, n), F32)
    acc = jnp.zeros((hd2, n), F32)
    e_prev = None
    for j in range(n_kv):
        s = jnp.dot(k_ref[j * tk:(j + 1) * tk, :], rhs_ref[...], preferred_element_type=F32)
        if e_prev is not None:
            acc = acc + jnp.dot(vt_ref[0, j - 1], e_prev, preferred_element_type=F32)
        if j >= CONV_FIRST_BLOCK and (j - CONV_FIRST_BLOCK) % conv_stride == 0 and conv_done:
            m = m + jnp.concatenate([conv_done.pop(0)] * (n // LANES), axis=1)
        e = jnp.exp2(s - m)
        l8 = l8 + jnp.sum(e.reshape(tk // 8, 8, n), axis=0)
        e_prev = e.astype(BF16)
    acc = acc + jnp.dot(vt_ref[0, n_kv - 1], e_prev, preferred_element_type=F32)
    l = jnp.sum(l8, axis=0, keepdims=True)
    acc_ref[...] = acc
    l_ref[...] = l

    @pl.when(jnp.logical_not(jnp.min(l) >= L_FLOOR))
    def _running_max_fallback():
        acc_ref[...] = jnp.zeros_like(acc_ref)

        def body(j, carry):
            m_run, l_run = carry
            s = jnp.dot(k_block(j), rhs_ref[...], preferred_element_type=F32)
            m_new = jnp.maximum(m_run, jnp.max(s, axis=0, keepdims=True))
            alpha = jnp.exp2(m_run - m_new)
            e = jnp.exp2(s - m_new)
            pv = jnp.dot(vt_ref[0, j], e.astype(BF16), preferred_element_type=F32)
            acc_ref[...] = alpha * acc_ref[...] + pv
            return m_new, alpha * l_run + jnp.sum(e, axis=0, keepdims=True)

        init = (jnp.full((1, n), -jnp.inf, F32), jnp.zeros((1, n), F32))
        _, l_run = lax.fori_loop(0, n_kv, body, init)
        l_ref[...] = l_run

    o = acc_ref[...] * (1.0 / l_ref[...])
    lam = (jnp.exp(jnp.sum(lq1_ref[...] * lk1_ref[...])) - jnp.exp(jnp.sum(lq2_ref[...] * lk2_ref[...]))
           + lam_init)
    o = o[:, :tq] - lam * o[:, tq:]
    ms = jnp.mean(o * o, axis=0, keepdims=True)
    o = o * lax.rsqrt(ms + LN_EPS) * g_ref[...] * (1.0 - lam_init)
    o_ref[...] = o.T.astype(BF16)


def _mixer(qt, k, vt, lq1, lk1, lq2, lk2, g_col, u, conv_w, conv_b, conv_g, conv_beta, *, tq, tk, conv_rows,
           lam_init):
    n_heads, hd2, s = qt.shape
    cw = u.shape[1]
    nq = s // tq
    tc = s // (n_heads * nq)
    assert tc % conv_rows == 0 and tc % CONV_HALO == 0
    lam_spec = _const_spec(lq1.shape)
    tile = lambda h, i: h * nq + i
    return pl.pallas_call(
        functools.partial(_mixer_kernel, tk=tk, conv_rows=conv_rows, lam_init=lam_init),
        grid=(n_heads, nq),
        in_specs=[pl.BlockSpec((1, hd2, tq), lambda h, i: (h, 0, i)),
                  pl.BlockSpec((s, hd2), lambda h, i: (0, h)),
                  pl.BlockSpec((1, s // tk, hd2, tk), lambda h, i: (h, 0, 0, 0)),
                  lam_spec, lam_spec, lam_spec, lam_spec,
                  _const_spec(g_col.shape),
                  *_conv_specs(s, cw, tc, tile),
                  _const_spec(conv_w.shape),
                  _const_spec(conv_b.shape),
                  _const_spec(conv_g.shape),
                  _const_spec(conv_beta.shape)],
        out_specs=[pl.BlockSpec((tq, hd2), lambda h, i: (i, h)),
                   pl.BlockSpec((tc, cw), lambda h, i: (tile(h, i), 0))],
        out_shape=[jax.ShapeDtypeStruct((s, n_heads * hd2), BF16),
                   jax.ShapeDtypeStruct((s, cw), BF16)],
        scratch_shapes=[pltpu.VMEM((hd2, 2 * tq), BF16),
                        pltpu.VMEM((hd2, 2 * tq), F32),
                        pltpu.VMEM((1, 2 * tq), F32),
                        pltpu.VMEM((1, 2 * tq), F32),
                        *_conv_scratch(tc, cw)],
        compiler_params=pltpu.CompilerParams(dimension_semantics=("arbitrary", "arbitrary"),
                                             vmem_limit_bytes=VMEM_LIMIT),
        name="mixer",
    )(qt, k, vt, lq1, lk1, lq2, lk2, g_col, u, u, u, conv_w, conv_b, conv_g, conv_beta)


def _mix_out_kernel(conv_ref, attn_ref, x_ref, mod_ref, w_ref, g_ref, b_ref, o_ref, *, sub, n_part):
    cw = conv_ref.shape[1]
    tp = x_ref.shape[0] // n_part
    gate_c = mod_ref[3 * sub + 2:3 * sub + 3, :]
    ys = []
    for p in range(n_part):
        rows = slice(p * tp, (p + 1) * tp)
        ys.append(jnp.dot(conv_ref[rows, :], w_ref[:cw, :], preferred_element_type=F32)
                  + jnp.dot(attn_ref[rows, :], w_ref[cw:, :], preferred_element_type=F32))
    for p in range(n_part):
        rows = slice(p * tp, (p + 1) * tp)
        z = ALPHA * x_ref[rows, :] + (1.0 + gate_c) * ys[p]
        o_ref[rows, :] = _layer_norm(z, g_ref[...], b_ref[...])


def _mix_out(conv, attn, x, mod9, w, g, b, *, tm, n_part):
    s, d = x.shape
    row = lambda i: (i, 0)
    return pl.pallas_call(
        functools.partial(_mix_out_kernel, sub=1, n_part=n_part),
        grid=(s // tm,),
        in_specs=[pl.BlockSpec((tm, conv.shape[1]), row),
                  pl.BlockSpec((tm, attn.shape[1]), row),
                  pl.BlockSpec((tm, d), row),
                  _const_spec(mod9.shape),
                  _const_spec(w.shape),
                  _const_spec(g.shape),
                  _const_spec(b.shape)],
        out_specs=pl.BlockSpec((tm, d), row),
        out_shape=jax.ShapeDtypeStruct((s, d), F32),
        compiler_params=pltpu.CompilerParams(dimension_semantics=("arbitrary",),
                                             vmem_limit_bytes=VMEM_LIMIT),
        name="mix_out",
    )(conv, attn, x, mod9, w, g, b)


def _rope_rows():
    inv_freq = ROPE_THETA ** (-jnp.arange(0, ROT_DIM, 2, dtype=F32) / ROT_DIM)
    half = ROT_DIM // 2
    zeros_h = jnp.zeros((half,), F32)
    zeros_p = jnp.zeros((DIFF_HEAD_DIM - ROT_DIM,), F32)
    ones_h = jnp.ones((half,), F32)
    reps = LANES // DIFF_HEAD_DIM
    freq = jnp.tile(jnp.concatenate([inv_freq, inv_freq, zeros_p]), reps)
    neg_first = jnp.tile(jnp.concatenate([-ones_h, zeros_h, zeros_p]), reps)
    pos_second = jnp.tile(jnp.concatenate([zeros_h, ones_h, zeros_p]), reps)
    return jnp.stack([freq, neg_first, pos_second])


def kernel(x, c, w_ada, b_ada, ffn1_w_in, ffn1_w_out, ln1_g, ln1_b, mix_w_in, conv_w, conv_b, conv_ln_g,
           conv_ln_b, lambda_q1, lambda_k1, lambda_q2, lambda_k2, subln_g, mix_w_out, ln2_g, ln2_b,
           ffn2_w_in, ffn2_w_out, ln3_g, ln3_b):
    batch, s, d = x.shape
    assert batch == 1 and w_ada.shape[0] == DEPTH == 1
    cw = conv_w.shape[2]
    aw = (mix_w_in.shape[2] - 2 * cw) // 3
    hd2 = 2 * DIFF_HEAD_DIM
    n_heads = aw // hd2
    lam_init = 0.8 - 0.6 * math.exp(-0.3 * 0)
    q_scale = math.log2(math.e) / math.sqrt(DIFF_HEAD_DIM)
    tq, tk = 512, 256

    mod9 = _ada(c.reshape(d, 1), w_ada[0], b_ada, tn=1152).reshape(9, d)
    x0 = x[0]
    x1 = _ffn(x0, mod9, ffn1_w_in[0], ffn1_w_out[0], ln1_g, ln1_b,
              sub=0, weight=0.5, tm=512, tf=256)

    u, qt, k, vt = _mix_in(x1, mod9, mix_w_in[0], _rope_rows(),
                           cw=cw, aw=aw, q_scale=q_scale, tm=1024, tk=tk, n_part=4)
    attn, conv = _mixer(qt, k, vt, lambda_q1, lambda_k1, lambda_q2, lambda_k2, subln_g.reshape(hd2, 1),
                        u, conv_w[0], conv_b, conv_ln_g, conv_ln_b,
                        tq=tq, tk=tk, conv_rows=32, lam_init=lam_init)
    x2 = _mix_out(conv, attn, x1, mod9, mix_w_out[0].astype(BF16), ln2_g, ln2_b, tm=1024, n_part=4)

    x3 = _ffn(x2, mod9, ffn2_w_in[0], ffn2_w_out[0], ln3_g, ln3_b,
              sub=2, weight=0.5, tm=512, tf=256)
    return x3[None]
```

```python
import functools
import math

import jax
import jax.numpy as jnp
from jax import lax
from jax.experimental import pallas as pl
from jax.experimental.pallas import tpu as pltpu

F32 = jnp.float32
BF16 = jnp.bfloat16

DEPTH = 1
ALPHA = (2.0 * DEPTH) ** 0.25
LN_EPS = 1e-5
DIFF_HEAD_DIM = 64
ROT_DIM = DIFF_HEAD_DIM // 4
ROPE_THETA = 500000.0
CONV_KERNEL = 31
CONV_PAD = (CONV_KERNEL - 1) // 2
CONV_HALO = 16
LANES = 128
SUBLANES = 8
SHIFT_SLACK = 1.0 + 2.0 ** -10
L_FLOOR = 2.0 ** -80
CONV_FIRST_BLOCK = 10
VMEM_LIMIT = 56 * 1024 * 1024


def _sigmoid(x):
    return 1.0 / (1.0 + jnp.exp(-x))


def _layer_norm(z, g, b):
    mu = jnp.mean(z, axis=-1, keepdims=True)
    zc = z - mu
    var = jnp.mean(zc * zc, axis=-1, keepdims=True)
    return zc * lax.rsqrt(var + LN_EPS) * g + b


def _modulate(x, mod_ref, sub):
    shift = mod_ref[3 * sub:3 * sub + 1, :]
    scale = mod_ref[3 * sub + 1:3 * sub + 2, :]
    return x * (1.0 + scale) + shift


def _const_spec(shape):
    return pl.BlockSpec(shape, lambda *_: (0,) * len(shape), pipeline_mode=pl.Buffered(1))


def _ada_kernel(c_ref, w_ref, b_ref, o_ref):
    c = c_ref[...]
    ca = c * _sigmoid(c)
    o_ref[...] = jnp.sum(ca * w_ref[...], axis=0, keepdims=True) + b_ref[...]


def _ada(c_col, w, b_row, tn):
    d, n = w.shape
    return pl.pallas_call(
        _ada_kernel,
        grid=(n // tn,),
        in_specs=[pl.BlockSpec((d, 1), lambda j: (0, 0)),
                  pl.BlockSpec((d, tn), lambda j: (0, j)),
                  pl.BlockSpec((1, tn), lambda j: (0, j))],
        out_specs=pl.BlockSpec((1, tn), lambda j: (0, j)),
        out_shape=jax.ShapeDtypeStruct((1, n), F32),
        compiler_params=pltpu.CompilerParams(dimension_semantics=("arbitrary",),
                                             vmem_limit_bytes=VMEM_LIMIT),
        name="ada",
    )(c_col, w, b_row)


def _ffn_kernel(x_ref, mod_ref, win_ref, wout_ref, g_ref, b_ref, o_ref, act_ref, *, sub, weight, tf):
    x = x_ref[...]
    d_ff = wout_ref.shape[0]
    h = _modulate(x, mod_ref, sub).astype(win_ref.dtype)
    for c in range(d_ff // tf):
        gate = jnp.dot(h, win_ref[:, c * tf:(c + 1) * tf], preferred_element_type=F32)
        up = jnp.dot(h, win_ref[:, d_ff + c * tf:d_ff + (c + 1) * tf], preferred_element_type=F32)
        act_ref[:, c * tf:(c + 1) * tf] = (gate * _sigmoid(gate) * up).astype(act_ref.dtype)
    y = jnp.dot(act_ref[...], wout_ref[...], preferred_element_type=F32)
    gate_c = mod_ref[3 * sub + 2:3 * sub + 3, :]
    z = ALPHA * x + weight * (1.0 + gate_c) * y
    o_ref[...] = _layer_norm(z, g_ref[...], b_ref[...])


def _ffn(x, mod9, w_in, w_out, g, b, *, sub, weight, tm, tf):
    s, d = x.shape
    d_ff = w_out.shape[0]
    return pl.pallas_call(
        functools.partial(_ffn_kernel, sub=sub, weight=weight, tf=tf),
        grid=(s // tm,),
        in_specs=[pl.BlockSpec((tm, d), lambda i: (i, 0)),
                  _const_spec(mod9.shape),
                  _const_spec(w_in.shape),
                  _const_spec(w_out.shape),
                  _const_spec(g.shape),
                  _const_spec(b.shape)],
        out_specs=pl.BlockSpec((tm, d), lambda i: (i, 0)),
        out_shape=jax.ShapeDtypeStruct((s, d), F32),
        scratch_shapes=[pltpu.VMEM((tm, d_ff), w_out.dtype)],
        compiler_params=pltpu.CompilerParams(dimension_semantics=("arbitrary",),
                                             vmem_limit_bytes=VMEM_LIMIT),
        name=f"ffn{sub}",
    )(x, mod9, w_in, w_out, g, b)


def _mix_in_kernel(x_ref, mod_ref, w_ref, rope_ref, u_ref, qt_ref, k_ref, vt_ref, kn_ref, cos_ref, sin_ref, *,
                   sub, cw, aw, q_scale, n_part):
    tm = x_ref.shape[0]
    tp = tm // n_part
    tk = vt_ref.shape[3]
    freq = rope_ref[0:1, :]

    @pl.when(pl.program_id(0) == 0)
    def _in_tile_angles():
        ang = lax.broadcasted_iota(jnp.int32, (tm, LANES), 0).astype(F32) * freq
        cos_ref[...] = jnp.cos(ang)
        sin_ref[...] = jnp.sin(ang)
        kn_ref[...] = jnp.zeros_like(kn_ref)

    comp0 = lax.broadcasted_iota(jnp.int32, (tp, LANES), 1) < DIFF_HEAD_DIM
    projs = []
    for p in range(n_part):
        h = _modulate(x_ref[p * tp:(p + 1) * tp, :], mod_ref, sub).astype(w_ref.dtype)
        projs.append(jnp.dot(h, w_ref[...], preferred_element_type=F32))

    ang0 = (pl.program_id(0) * tm).astype(F32) * freq
    c0, s0 = jnp.cos(ang0), jnp.sin(ang0)
    half = ROT_DIM // 2
    q0, k0, v0 = 2 * cw, 2 * cw + aw, 2 * cw + 2 * aw
    for p, proj in enumerate(projs):
        rows = slice(p * tp, (p + 1) * tp)
        u_ref[rows, :] = proj[:, :cw] * _sigmoid(proj[:, cw:2 * cw])
        rc = c0 * cos_ref[rows, :] - s0 * sin_ref[rows, :]
        sin = s0 * cos_ref[rows, :] + c0 * sin_ref[rows, :]
        rs1, rs2 = sin * rope_ref[1:2, :], sin * rope_ref[2:3, :]

        def rope(t):
            return t * rc + pltpu.roll(t, LANES - half, 1) * rs1 + pltpu.roll(t, half, 1) * rs2

        for g in range(aw // LANES):
            sl = slice(g * LANES, (g + 1) * LANES)
            qt_ref[g, :, rows] = (rope(proj[:, q0 + g * LANES:q0 + (g + 1) * LANES]) * q_scale).T.astype(BF16)
            kb = rope(proj[:, k0 + g * LANES:k0 + (g + 1) * LANES]).astype(BF16)
            k_ref[rows, sl] = kb
            kf = kb.astype(F32)
            sq = kf * kf
            n0 = jnp.max(jnp.sum(jnp.where(comp0, sq, 0.0), axis=1, keepdims=True), axis=0, keepdims=True)
            n1 = jnp.max(jnp.sum(jnp.where(comp0, 0.0, sq), axis=1, keepdims=True), axis=0, keepdims=True)
            kn_ref[g:g + 1, :] = jnp.maximum(kn_ref[g:g + 1, :], jnp.where(comp0[0:1, :], n0, n1))
            for c in range(tp // tk):
                vt_ref[g, p * (tp // tk) + c] = (
                    proj[c * tk:(c + 1) * tk, v0 + g * LANES:v0 + (g + 1) * LANES].T.astype(BF16))


def _mix_in(x, mod9, w, rope_rows, *, cw, aw, q_scale, tm, tk, n_part):
    s, d = x.shape
    n_heads = aw // LANES
    assert (tm // n_part) % tk == 0 and n_heads <= SUBLANES
    row = lambda i: (i, 0)
    return pl.pallas_call(
        functools.partial(_mix_in_kernel, sub=1, cw=cw, aw=aw, q_scale=q_scale, n_part=n_part),
        grid=(s // tm,),
        in_specs=[pl.BlockSpec((tm, d), row),
                  _const_spec(mod9.shape),
                  _const_spec(w.shape),
                  _const_spec(rope_rows.shape)],
        out_specs=[pl.BlockSpec((tm, cw), row),
                   pl.BlockSpec((n_heads, LANES, tm), lambda i: (0, 0, i)),
                   pl.BlockSpec((tm, aw), row),
                   pl.BlockSpec((n_heads, tm // tk, LANES, tk), lambda i: (0, i, 0, 0)),
                   pl.BlockSpec((SUBLANES, LANES), lambda i: (0, 0))],
        out_shape=[jax.ShapeDtypeStruct((s, cw), F32),
                   jax.ShapeDtypeStruct((n_heads, LANES, s), BF16),
                   jax.ShapeDtypeStruct((s, aw), BF16),
                   jax.ShapeDtypeStruct((n_heads, s // tk, LANES, tk), BF16),
                   jax.ShapeDtypeStruct((SUBLANES, LANES), F32)],
        scratch_shapes=[pltpu.VMEM((tm, LANES), F32),
                        pltpu.VMEM((tm, LANES), F32)],
        compiler_params=pltpu.CompilerParams(dimension_semantics=("arbitrary",),
                                             vmem_limit_bytes=VMEM_LIMIT),
        name="mix_in",
    )(x, mod9, w, rope_rows)


def _dependent_zero(v):
    r, c = v.shape
    folded = jnp.sum(v.reshape(r // SUBLANES, SUBLANES, c), axis=0)
    folded = sum(folded[:, g * LANES:(g + 1) * LANES] for g in range(c // LANES))
    bits = lax.bitcast_convert_type(folded[0:1, :], jnp.uint32)
    return lax.bitcast_convert_type((bits >> 16) >> 16, F32)


def _conv_tile(i, n_tiles, prev_ref, cur_ref, next_ref, w_ref, cb_ref, g_ref, b_ref, o_ref, ext_ref, sh_ref, y_ref,
               rows):
    tm, cw = cur_ref.shape
    ext_ref[0:CONV_HALO, :] = jnp.where(i > 0, prev_ref[...], 0.0)
    ext_ref[CONV_HALO:CONV_HALO + tm, :] = cur_ref[...]
    ext_ref[CONV_HALO + tm:, :] = jnp.where(i < n_tiles - 1, next_ref[...], 0.0)
    span = sh_ref.shape[1]
    for b in range(SUBLANES):
        sh_ref[b] = ext_ref[b:b + span, :]
    base = CONV_HALO - CONV_PAD
    done = []
    for lc in range(cw // LANES):
        ls = slice(lc * LANES, (lc + 1) * LANES)
        for rc in range(tm // rows):
            r0 = rc * rows
            acc = jnp.zeros((rows, LANES), F32)
            for t in range(CONV_KERNEL):
                off = base + t
                a0 = r0 + SUBLANES * (off // SUBLANES)
                acc = acc + sh_ref[off % SUBLANES, a0:a0 + rows, ls] * w_ref[t:t + 1, ls]
            y_ref[r0:r0 + rows, ls] = acc
            done.append(_dependent_zero(acc))
    y = _layer_norm(y_ref[...] + cb_ref[...], g_ref[...], b_ref[...])
    y = y * _sigmoid(y)
    o_ref[...] = y.astype(BF16)
    done.append(_dependent_zero(y))
    return done


def _conv_specs(s, cw, tm, tile_index):
    nh = tm // CONV_HALO
    last = s // CONV_HALO - 1
    return [pl.BlockSpec((CONV_HALO, cw), lambda *g: (jnp.maximum(tile_index(*g) * nh - 1, 0), 0)),
            pl.BlockSpec((tm, cw), lambda *g: (tile_index(*g), 0)),
            pl.BlockSpec((CONV_HALO, cw), lambda *g: (jnp.minimum((tile_index(*g) + 1) * nh, last), 0))]


def _conv_scratch(tm, cw):
    return [pltpu.VMEM((tm + 2 * CONV_HALO, cw), F32),
            pltpu.VMEM((SUBLANES, tm + 2 * CONV_HALO - SUBLANES, cw), F32),
            pltpu.VMEM((tm, cw), F32)]


def _mixer_kernel(qt_ref, k_ref, vt_ref, kn_ref, lq1_ref, lk1_ref, lq2_ref, lk2_ref, g_ref,
                  up_ref, uc_ref, un_ref, cw_ref, cb_ref, cg_ref, cbeta_ref,
                  o_ref, conv_ref,
                  rhs_ref, acc_ref, l_ref, kmax_ref, ext_ref, sh_ref, y_ref, *, tk, conv_rows, lam_init):
    step = pl.program_id(0) * pl.num_programs(1) + pl.program_id(1)
    n_steps = pl.num_programs(0) * pl.num_programs(1)

    hd2, tq = qt_ref.shape[1], qt_ref.shape[2]
    n = 2 * tq
    n_kv = k_ref.shape[0] // tk
    qt = qt_ref[0]
    row = lax.broadcasted_iota(jnp.int32, qt.shape, 0)
    zero = jnp.zeros_like(qt)
    rhs_ref[:, :tq] = jnp.where(row < DIFF_HEAD_DIM, qt, zero)
    rhs_ref[:, tq:] = jnp.where(row >= DIFF_HEAD_DIM, qt, zero)

    def k_block(j):
        return k_ref[pl.ds(pl.multiple_of(j * tk, tk), tk), :]

    @pl.when(pl.program_id(1) == 0)
    def _key_norm_bound():
        kn = kn_ref[pl.ds(pl.program_id(0), 1), :]
        col = lax.broadcasted_iota(jnp.int32, (1, n), 1)
        kmax_ref[...] = jnp.sqrt(jnp.where(col < tq, kn[:, 0:1], kn[:, DIFF_HEAD_DIM:DIFF_HEAD_DIM + 1]))

    conv_done = _conv_tile(step, n_steps, up_ref, uc_ref, un_ref, cw_ref, cb_ref, cg_ref, cbeta_ref, conv_ref,
                           ext_ref, sh_ref, y_ref, conv_rows)
    conv_stride = (n_kv - 1 - CONV_FIRST_BLOCK) // len(conv_done)
    assert conv_stride >= 1

    r32 = rhs_ref[...].astype(F32)
    qn = jnp.sqrt(jnp.sum(r32 * r32, axis=0, keepdims=True))
    m = qn * kmax_ref[...] * SHIFT_SLACK
    l8 = jnp.zeros((8, n), F32)
    acc = jnp.zeros((hd2, n), F32)
    e_prev = None
    for j in range(n_kv):
        s = jnp.dot(k_ref[j * tk:(j + 1) * tk, :], rhs_ref[...], preferred_element_type=F32)
        if e_prev is not None:
            acc = acc + jnp.dot(vt_ref[0, j - 1], e_prev, preferred_element_type=F32)
        if j >= CONV_FIRST_BLOCK and (j - CONV_FIRST_BLOCK) % conv_stride == 0 and conv_done:
            m = m + jnp.concatenate([conv_done.pop(0)] * (n // LANES), axis=1)
        e = jnp.exp2(s - m)
        l8 = l8 + jnp.sum(e.reshape(tk // 8, 8, n), axis=0)
        e_prev = e.astype(BF16)
    acc = acc + jnp.dot(vt_ref[0, n_kv - 1], e_prev, preferred_element_type=F32)
    l = jnp.sum(l8, axis=0, keepdims=True)
    acc_ref[...] = acc
    l_ref[...] = l

    @pl.when(jnp.logical_not(jnp.min(l) >= L_FLOOR))
    def _running_max_fallback():
        acc_ref[...] = jnp.zeros_like(acc_ref)

        def body(j, carry):
            m_run, l_run = carry
            s = jnp.dot(k_block(j), rhs_ref[...], preferred_element_type=F32)
            m_new = jnp.maximum(m_run, jnp.max(s, axis=0, keepdims=True))
            alpha = jnp.exp2(m_run - m_new)
            e = jnp.exp2(s - m_new)
            pv = jnp.dot(vt_ref[0, j], e.astype(BF16), preferred_element_type=F32)
            acc_ref[...] = alpha * acc_ref[...] + pv
            return m_new, alpha * l_run + jnp.sum(e, axis=0, keepdims=True)

        init = (jnp.full((1, n), -jnp.inf, F32), jnp.zeros((1, n), F32))
        _, l_run = lax.fori_loop(0, n_kv, body, init)
        l_ref[...] = l_run

    o = acc_ref[...] * (1.0 / l_ref[...])
    lam = (jnp.exp(jnp.sum(lq1_ref[...] * lk1_ref[...])) - jnp.exp(jnp.sum(lq2_ref[...] * lk2_ref[...]))
           + lam_init)
    o = o[:, :tq] - lam * o[:, tq:]
    ms = jnp.mean(o * o, axis=0, keepdims=True)
    o = o * lax.rsqrt(ms + LN_EPS) * g_ref[...] * (1.0 - lam_init)
    o_ref[...] = o.T.astype(BF16)


def _mixer(qt, k, vt, kn, lq1, lk1, lq2, lk2, g_col, u, conv_w, conv_b, conv_g, conv_beta, *, tq, tk, conv_rows,
           lam_init):
    n_heads, hd2, s = qt.shape
    cw = u.shape[1]
    nq = s // tq
    tc = s // (n_heads * nq)
    assert tc % conv_rows == 0 and tc % CONV_HALO == 0
    lam_spec = _const_spec(lq1.shape)
    tile = lambda h, i: h * nq + i
    return pl.pallas_call(
        functools.partial(_mixer_kernel, tk=tk, conv_rows=conv_rows, lam_init=lam_init),
        grid=(n_heads, nq),
        in_specs=[pl.BlockSpec((1, hd2, tq), lambda h, i: (h, 0, i)),
                  pl.BlockSpec((s, hd2), lambda h, i: (0, h)),
                  pl.BlockSpec((1, s // tk, hd2, tk), lambda h, i: (h, 0, 0, 0)),
                  _const_spec(kn.shape),
                  lam_spec, lam_spec, lam_spec, lam_spec,
                  _const_spec(g_col.shape),
                  *_conv_specs(s, cw, tc, tile),
                  _const_spec(conv_w.shape),
                  _const_spec(conv_b.shape),
                  _const_spec(conv_g.shape),
                  _const_spec(conv_beta.shape)],
        out_specs=[pl.BlockSpec((tq, hd2), lambda h, i: (i, h)),
                   pl.BlockSpec((tc, cw), lambda h, i: (tile(h, i), 0))],
        out_shape=[jax.ShapeDtypeStruct((s, n_heads * hd2), BF16),
                   jax.ShapeDtypeStruct((s, cw), BF16)],
        scratch_shapes=[pltpu.VMEM((hd2, 2 * tq), BF16),
                        pltpu.VMEM((hd2, 2 * tq), F32),
                        pltpu.VMEM((1, 2 * tq), F32),
                        pltpu.VMEM((1, 2 * tq), F32),
                        *_conv_scratch(tc, cw)],
        compiler_params=pltpu.CompilerParams(dimension_semantics=("arbitrary", "arbitrary"),
                                             vmem_limit_bytes=VMEM_LIMIT),
        name="mixer",
    )(qt, k, vt, kn, lq1, lk1, lq2, lk2, g_col, u, u, u, conv_w, conv_b, conv_g, conv_beta)


def _mix_out_kernel(conv_ref, attn_ref, x_ref, mod_ref, w_ref, g_ref, b_ref, o_ref, *, sub, n_part):
    cw = conv_ref.shape[1]
    tp = x_ref.shape[0] // n_part
    gate_c = mod_ref[3 * sub + 2:3 * sub + 3, :]
    ys = []
    for p in range(n_part):
        rows = slice(p * tp, (p + 1) * tp)
        ys.append(jnp.dot(conv_ref[rows, :], w_ref[:cw, :], preferred_element_type=F32)
                  + jnp.dot(attn_ref[rows, :], w_ref[cw:, :], preferred_element_type=F32))
    for p in range(n_part):
        rows = slice(p * tp, (p + 1) * tp)
        z = ALPHA * x_ref[rows, :] + (1.0 + gate_c) * ys[p]
        o_ref[rows, :] = _layer_norm(z, g_ref[...], b_ref[...])


def _mix_out(conv, attn, x, mod9, w, g, b, *, tm, n_part):
    s, d = x.shape
    row = lambda i: (i, 0)
    return pl.pallas_call(
        functools.partial(_mix_out_kernel, sub=1, n_part=n_part),
        grid=(s // tm,),
        in_specs=[pl.BlockSpec((tm, conv.shape[1]), row),
                  pl.BlockSpec((tm, attn.shape[1]), row),
                  pl.BlockSpec((tm, d), row),
                  _const_spec(mod9.shape),
                  _const_spec(w.shape),
                  _const_spec(g.shape),
                  _const_spec(b.shape)],
        out_specs=pl.BlockSpec((tm, d), row),
        out_shape=jax.ShapeDtypeStruct((s, d), F32),
        compiler_params=pltpu.CompilerParams(dimension_semantics=("arbitrary",),
                                             vmem_limit_bytes=VMEM_LIMIT),
        name="mix_out",
    )(conv, attn, x, mod9, w, g, b)


def _rope_rows():
    inv_freq = ROPE_THETA ** (-jnp.arange(0, ROT_DIM, 2, dtype=F32) / ROT_DIM)
    half = ROT_DIM // 2
    zeros_h = jnp.zeros((half,), F32)
    zeros_p = jnp.zeros((DIFF_HEAD_DIM - ROT_DIM,), F32)
    ones_h = jnp.ones((half,), F32)
    reps = LANES // DIFF_HEAD_DIM
    freq = jnp.tile(jnp.concatenate([inv_freq, inv_freq, zeros_p]), reps)
    neg_first = jnp.tile(jnp.concatenate([-ones_h, zeros_h, zeros_p]), reps)
    pos_second = jnp.tile(jnp.concatenate([zeros_h, ones_h, zeros_p]), reps)
    return jnp.stack([freq, neg_first, pos_second])


def kernel(x, c, w_ada, b_ada, ffn1_w_in, ffn1_w_out, ln1_g, ln1_b, mix_w_in, conv_w, conv_b, conv_ln_g,
           conv_ln_b, lambda_q1, lambda_k1, lambda_q2, lambda_k2, subln_g, mix_w_out, ln2_g, ln2_b,
           ffn2_w_in, ffn2_w_out, ln3_g, ln3_b):
    batch, s, d = x.shape
    assert batch == 1 and w_ada.shape[0] == DEPTH == 1
    cw = conv_w.shape[2]
    aw = (mix_w_in.shape[2] - 2 * cw) // 3
    hd2 = 2 * DIFF_HEAD_DIM
    n_heads = aw // hd2
    lam_init = 0.8 - 0.6 * math.exp(-0.3 * 0)
    q_scale = math.log2(math.e) / math.sqrt(DIFF_HEAD_DIM)
    tq, tk = 512, 256

    mod9 = _ada(c.reshape(d, 1), w_ada[0], b_ada, tn=1152).reshape(9, d)
    x0 = x[0]
    x1 = _ffn(x0, mod9, ffn1_w_in[0], ffn1_w_out[0], ln1_g, ln1_b,
              sub=0, weight=0.5, tm=512, tf=256)

    u, qt, k, vt, kn = _mix_in(x1, mod9, mix_w_in[0], _rope_rows(),
                               cw=cw, aw=aw, q_scale=q_scale, tm=1024, tk=tk, n_part=4)
    attn, conv = _mixer(qt, k, vt, kn, lambda_q1, lambda_k1, lambda_q2, lambda_k2, subln_g.reshape(hd2, 1),
                        u, conv_w[0], conv_b, conv_ln_g, conv_ln_b,
                        tq=tq, tk=tk, conv_rows=32, lam_init=lam_init)
    x2 = _mix_out(conv, attn, x1, mod9, mix_w_out[0].astype(BF16), ln2_g, ln2_b, tm=1024, n_part=4)

    x3 = _ffn(x2, mod9, ffn2_w_in[0], ffn2_w_out[0], ln3_g, ln3_b,
              sub=2, weight=0.5, tm=512, tf=256)
    return x3[None]
```

```python
import functools
import math

import jax
import jax.numpy as jnp
from jax import lax
from jax.experimental import pallas as pl
from jax.experimental.pallas import tpu as pltpu

F32 = jnp.float32
BF16 = jnp.bfloat16

DEPTH = 1
ALPHA = (2.0 * DEPTH) ** 0.25
LN_EPS = 1e-5
DIFF_HEAD_DIM = 64
ROT_DIM = DIFF_HEAD_DIM // 4
ROPE_THETA = 500000.0
CONV_KERNEL = 31
CONV_PAD = (CONV_KERNEL - 1) // 2
CONV_HALO = 16
LANES = 128
SUBLANES = 8
SHIFT_SLACK = 1.0 + 2.0 ** -10
L_FLOOR = 2.0 ** -80
CONV_FIRST_BLOCK = 10
VMEM_LIMIT = 56 * 1024 * 1024


def _sigmoid(x):
    return 1.0 / (1.0 + jnp.exp(-x))


def _layer_norm(z, g, b):
    mu = jnp.mean(z, axis=-1, keepdims=True)
    zc = z - mu
    var = jnp.mean(zc * zc, axis=-1, keepdims=True)
    return zc * lax.rsqrt(var + LN_EPS) * g + b


def _modulate(x, mod_ref, sub):
    shift = mod_ref[3 * sub:3 * sub + 1, :]
    scale = mod_ref[3 * sub + 1:3 * sub + 2, :]
    return x * (1.0 + scale) + shift


def _const_spec(shape):
    return pl.BlockSpec(shape, lambda *_: (0,) * len(shape), pipeline_mode=pl.Buffered(1))


def _ada_kernel(c_ref, w_ref, b_ref, o_ref):
    c = c_ref[...]
    ca = c * _sigmoid(c)
    o_ref[...] = jnp.sum(ca * w_ref[...], axis=0, keepdims=True) + b_ref[...]


def _ada(c_col, w, b_row, tn):
    d, n = w.shape
    return pl.pallas_call(
        _ada_kernel,
        grid=(n // tn,),
        in_specs=[pl.BlockSpec((d, 1), lambda j: (0, 0)),
                  pl.BlockSpec((d, tn), lambda j: (0, j)),
                  pl.BlockSpec((1, tn), lambda j: (0, j))],
        out_specs=pl.BlockSpec((1, tn), lambda j: (0, j)),
        out_shape=jax.ShapeDtypeStruct((1, n), F32),
        compiler_params=pltpu.CompilerParams(dimension_semantics=("arbitrary",),
                                             vmem_limit_bytes=VMEM_LIMIT),
        name="ada",
    )(c_col, w, b_row)


def _ffn_kernel(x_ref, mod_ref, win_ref, wout_ref, g_ref, b_ref, o_ref, act_ref, *, sub, weight, tf):
    x = x_ref[...]
    d_ff = wout_ref.shape[0]
    h = _modulate(x, mod_ref, sub).astype(win_ref.dtype)
    for c in range(d_ff // tf):
        gate = jnp.dot(h, win_ref[:, c * tf:(c + 1) * tf], preferred_element_type=F32)
        up = jnp.dot(h, win_ref[:, d_ff + c * tf:d_ff + (c + 1) * tf], preferred_element_type=F32)
        act_ref[:, c * tf:(c + 1) * tf] = (gate * _sigmoid(gate) * up).astype(act_ref.dtype)
    y = jnp.dot(act_ref[...], wout_ref[...], preferred_element_type=F32)
    gate_c = mod_ref[3 * sub + 2:3 * sub + 3, :]
    z = ALPHA * x + weight * (1.0 + gate_c) * y
    o_ref[...] = _layer_norm(z, g_ref[...], b_ref[...])


def _ffn(x, mod9, w_in, w_out, g, b, *, sub, weight, tm, tf):
    s, d = x.shape
    d_ff = w_out.shape[0]
    return pl.pallas_call(
        functools.partial(_ffn_kernel, sub=sub, weight=weight, tf=tf),
        grid=(s // tm,),
        in_specs=[pl.BlockSpec((tm, d), lambda i: (i, 0)),
                  _const_spec(mod9.shape),
                  _const_spec(w_in.shape),
                  _const_spec(w_out.shape),
                  _const_spec(g.shape),
                  _const_spec(b.shape)],
        out_specs=pl.BlockSpec((tm, d), lambda i: (i, 0)),
        out_shape=jax.ShapeDtypeStruct((s, d), F32),
        scratch_shapes=[pltpu.VMEM((tm, d_ff), w_out.dtype)],
        compiler_params=pltpu.CompilerParams(dimension_semantics=("arbitrary",),
                                             vmem_limit_bytes=VMEM_LIMIT),
        name=f"ffn{sub}",
    )(x, mod9, w_in, w_out, g, b)


def _mix_in_kernel(x_ref, mod_ref, w_ref, rope_ref, u_ref, qt_ref, k_ref, vt_ref, kn_ref, cos_ref, sin_ref, *,
                   sub, cw, aw, q_scale, n_part):
    tm = x_ref.shape[0]
    tp = tm // n_part
    tk = vt_ref.shape[3]
    freq = rope_ref[0:1, :]

    @pl.when(pl.program_id(0) == 0)
    def _in_tile_angles():
        ang = lax.broadcasted_iota(jnp.int32, (tm, LANES), 0).astype(F32) * freq
        cos_ref[...] = jnp.cos(ang)
        sin_ref[...] = jnp.sin(ang)
        kn_ref[...] = jnp.zeros_like(kn_ref)

    comp0 = lax.broadcasted_iota(jnp.int32, (tp, LANES), 1) < DIFF_HEAD_DIM
    projs = []
    for p in range(n_part):
        h = _modulate(x_ref[p * tp:(p + 1) * tp, :], mod_ref, sub).astype(w_ref.dtype)
        projs.append(jnp.dot(h, w_ref[...], preferred_element_type=F32))

    ang0 = (pl.program_id(0) * tm).astype(F32) * freq
    c0, s0 = jnp.cos(ang0), jnp.sin(ang0)
    half = ROT_DIM // 2
    q0, k0, v0 = 2 * cw, 2 * cw + aw, 2 * cw + 2 * aw
    for p, proj in enumerate(projs):
        rows = slice(p * tp, (p + 1) * tp)
        u_ref[rows, :] = proj[:, :cw] * _sigmoid(proj[:, cw:2 * cw])
        rc = c0 * cos_ref[rows, :] - s0 * sin_ref[rows, :]
        sin = s0 * cos_ref[rows, :] + c0 * sin_ref[rows, :]
        rs1, rs2 = sin * rope_ref[1:2, :], sin * rope_ref[2:3, :]

        def rope(t):
            return t * rc + pltpu.roll(t, LANES - half, 1) * rs1 + pltpu.roll(t, half, 1) * rs2

        for g in range(aw // LANES):
            sl = slice(g * LANES, (g + 1) * LANES)
            qt_ref[g, :, rows] = (rope(proj[:, q0 + g * LANES:q0 + (g + 1) * LANES]) * q_scale).T.astype(BF16)
            kb = rope(proj[:, k0 + g * LANES:k0 + (g + 1) * LANES]).astype(BF16)
            k_ref[rows, sl] = kb
            kf = kb.astype(F32)
            sq = kf * kf
            n0 = jnp.max(jnp.sum(jnp.where(comp0, sq, 0.0), axis=1, keepdims=True), axis=0, keepdims=True)
            n1 = jnp.max(jnp.sum(jnp.where(comp0, 0.0, sq), axis=1, keepdims=True), axis=0, keepdims=True)
            kn_ref[g:g + 1, :] = jnp.maximum(kn_ref[g:g + 1, :], jnp.where(comp0[0:1, :], n0, n1))
            for c in range(tp // tk):
                vt_ref[g, p * (tp // tk) + c] = (
                    proj[c * tk:(c + 1) * tk, v0 + g * LANES:v0 + (g + 1) * LANES].T.astype(BF16))


def _mix_in(x, mod9, w, rope_rows, *, cw, aw, q_scale, tm, tk, n_part):
    s, d = x.shape
    n_heads = aw // LANES
    assert (tm // n_part) % tk == 0 and n_heads <= SUBLANES
    row = lambda i: (i, 0)
    return pl.pallas_call(
        functools.partial(_mix_in_kernel, sub=1, cw=cw, aw=aw, q_scale=q_scale, n_part=n_part),
        grid=(s // tm,),
        in_specs=[pl.BlockSpec((tm, d), row),
                  _const_spec(mod9.shape),
                  _const_spec(w.shape),
                  _const_spec(rope_rows.shape)],
        out_specs=[pl.BlockSpec((tm, cw), row),
                   pl.BlockSpec((n_heads, LANES, tm), lambda i: (0, 0, i)),
                   pl.BlockSpec((tm, aw), row),
                   pl.BlockSpec((n_heads, tm // tk, LANES, tk), lambda i: (0, i, 0, 0)),
                   pl.BlockSpec((SUBLANES, LANES), lambda i: (0, 0))],
        out_shape=[jax.ShapeDtypeStruct((s, cw), F32),
                   jax.ShapeDtypeStruct((n_heads, LANES, s), BF16),
                   jax.ShapeDtypeStruct((s, aw), BF16),
                   jax.ShapeDtypeStruct((n_heads, s // tk, LANES, tk), BF16),
                   jax.ShapeDtypeStruct((SUBLANES, LANES), F32)],
        scratch_shapes=[pltpu.VMEM((tm, LANES), F32),
                        pltpu.VMEM((tm, LANES), F32)],
        compiler_params=pltpu.CompilerParams(dimension_semantics=("arbitrary",),
                                             vmem_limit_bytes=VMEM_LIMIT),
        name="mix_in",
    )(x, mod9, w, rope_rows)


def _dependent_zero(v):
    r, c = v.shape
    folded = jnp.sum(v.reshape(r // SUBLANES, SUBLANES, c), axis=0)
    folded = sum(folded[:, g * LANES:(g + 1) * LANES] for g in range(c // LANES))
    bits = lax.bitcast_convert_type(folded[0:1, :], jnp.uint32)
    return lax.bitcast_convert_type((bits >> 16) >> 16, F32)


def _conv_tile(i, n_tiles, prev_ref, cur_ref, next_ref, w_ref, cb_ref, g_ref, b_ref, o_ref, ext_ref, sh_ref, y_ref,
               rows):
    tm, cw = cur_ref.shape
    ext_ref[0:CONV_HALO, :] = jnp.where(i > 0, prev_ref[...], 0.0)
    ext_ref[CONV_HALO:CONV_HALO + tm, :] = cur_ref[...]
    ext_ref[CONV_HALO + tm:, :] = jnp.where(i < n_tiles - 1, next_ref[...], 0.0)
    span = sh_ref.shape[1]
    for b in range(SUBLANES):
        sh_ref[b] = ext_ref[b:b + span, :]
    base = CONV_HALO - CONV_PAD
    done = []
    for lc in range(cw // LANES):
        ls = slice(lc * LANES, (lc + 1) * LANES)
        for rc in range(tm // rows):
            r0 = rc * rows
            acc = jnp.zeros((rows, LANES), F32)
            for t in range(CONV_KERNEL):
                off = base + t
                a0 = r0 + SUBLANES * (off // SUBLANES)
                acc = acc + sh_ref[off % SUBLANES, a0:a0 + rows, ls] * w_ref[t:t + 1, ls]
            y_ref[r0:r0 + rows, ls] = acc
            done.append(_dependent_zero(acc))
    y = _layer_norm(y_ref[...] + cb_ref[...], g_ref[...], b_ref[...])
    y = y * _sigmoid(y)
    o_ref[...] = y.astype(BF16)
    done.append(_dependent_zero(y))
    return done


def _conv_specs(s, cw, tm, tile_index):
    nh = tm // CONV_HALO
    last = s // CONV_HALO - 1
    return [pl.BlockSpec((CONV_HALO, cw), lambda *g: (jnp.maximum(tile_index(*g) * nh - 1, 0), 0)),
            pl.BlockSpec((tm, cw), lambda *g: (tile_index(*g), 0)),
            pl.BlockSpec((CONV_HALO, cw), lambda *g: (jnp.minimum((tile_index(*g) + 1) * nh, last), 0))]


def _conv_scratch(tm, cw):
    return [pltpu.VMEM((tm + 2 * CONV_HALO, cw), F32),
            pltpu.VMEM((SUBLANES, tm + 2 * CONV_HALO - SUBLANES, cw), F32),
            pltpu.VMEM((tm, cw), F32)]


def _mixer_kernel(qt_ref, k_ref, vt_ref, kn_ref, lq1_ref, lk1_ref, lq2_ref, lk2_ref, g_ref,
                  up_ref, uc_ref, un_ref, cw_ref, cb_ref, cg_ref, cbeta_ref,
                  o_ref, conv_ref,
                  rhs_ref, acc_ref, l_ref, kmax_ref, ext_ref, sh_ref, y_ref, *, tk, conv_rows, lam_init):
    step = pl.program_id(0) * pl.num_programs(1) + pl.program_id(1)
    n_steps = pl.num_programs(0) * pl.num_programs(1)

    hd2, tq = qt_ref.shape[1], qt_ref.shape[2]
    n = 2 * tq
    n_kv = k_ref.shape[0] // tk
    qt = qt_ref[0]
    row = lax.broadcasted_iota(jnp.int32, qt.shape, 0)
    zero = jnp.zeros_like(qt)
    rhs_ref[:, :tq] = jnp.where(row < DIFF_HEAD_DIM, qt, zero)
    rhs_ref[:, tq:] = jnp.where(row >= DIFF_HEAD_DIM, qt, zero)

    def k_block(j):
        return k_ref[pl.ds(pl.multiple_of(j * tk, tk), tk), :]

    @pl.when(pl.program_id(1) == 0)
    def _key_norm_bound():
        kn = kn_ref[pl.ds(pl.program_id(0), 1), :]
        col = lax.broadcasted_iota(jnp.int32, (1, n), 1)
        kmax_ref[...] = jnp.sqrt(jnp.where(col < tq, kn[:, 0:1], kn[:, DIFF_HEAD_DIM:DIFF_HEAD_DIM + 1]))

    conv_done = _conv_tile(step, n_steps, up_ref, uc_ref, un_ref, cw_ref, cb_ref, cg_ref, cbeta_ref, conv_ref,
                           ext_ref, sh_ref, y_ref, conv_rows)
    conv_stride = (n_kv - 1 - CONV_FIRST_BLOCK) // len(conv_done)
    assert conv_stride >= 1

    r32 = rhs_ref[...].astype(F32)
    qn = jnp.sqrt(jnp.sum(r32 * r32, axis=0, keepdims=True))
    m = qn * kmax_ref[...] * SHIFT_SLACK
    l8 = [jnp.zeros((8, tq), F32)] * 2
    acc = [jnp.zeros((hd2, tq), F32)] * 2
    e_prev = [None, None]
    for j in range(n_kv):
        if j >= CONV_FIRST_BLOCK and (j - CONV_FIRST_BLOCK) % conv_stride == 0 and conv_done:
            m = m + jnp.concatenate([conv_done.pop(0)] * (n // LANES), axis=1)
        for c in range(2):
            cols = slice(c * tq, (c + 1) * tq)
            s = jnp.dot(k_ref[j * tk:(j + 1) * tk, :], rhs_ref[:, cols], preferred_element_type=F32)
            if e_prev[c] is not None:
                acc[c] = acc[c] + jnp.dot(vt_ref[0, j - 1], e_prev[c], preferred_element_type=F32)
            e = jnp.exp2(s - m[:, cols])
            l8[c] = l8[c] + jnp.sum(e.reshape(tk // 8, 8, tq), axis=0)
            e_prev[c] = e.astype(BF16)
    for c in range(2):
        cols = slice(c * tq, (c + 1) * tq)
        acc_c = acc[c] + jnp.dot(vt_ref[0, n_kv - 1], e_prev[c], preferred_element_type=F32)
        acc_ref[:, cols] = acc_c
        l_ref[:, cols] = jnp.sum(l8[c], axis=0, keepdims=True)
    l = l_ref[...]

    @pl.when(jnp.logical_not(jnp.min(l) >= L_FLOOR))
    def _running_max_fallback():
        acc_ref[...] = jnp.zeros_like(acc_ref)

        def body(j, carry):
            m_run, l_run = carry
            s = jnp.dot(k_block(j), rhs_ref[...], preferred_element_type=F32)
            m_new = jnp.maximum(m_run, jnp.max(s, axis=0, keepdims=True))
            alpha = jnp.exp2(m_run - m_new)
            e = jnp.exp2(s - m_new)
            pv = jnp.dot(vt_ref[0, j], e.astype(BF16), preferred_element_type=F32)
            acc_ref[...] = alpha * acc_ref[...] + pv
            return m_new, alpha * l_run + jnp.sum(e, axis=0, keepdims=True)

        init = (jnp.full((1, n), -jnp.inf, F32), jnp.zeros((1, n), F32))
        _, l_run = lax.fori_loop(0, n_kv, body, init)
        l_ref[...] = l_run

    o = acc_ref[...] * (1.0 / l_ref[...])
    lam = (jnp.exp(jnp.sum(lq1_ref[...] * lk1_ref[...])) - jnp.exp(jnp.sum(lq2_ref[...] * lk2_ref[...]))
           + lam_init)
    o = o[:, :tq] - lam * o[:, tq:]
    ms = jnp.mean(o * o, axis=0, keepdims=True)
    o = o * lax.rsqrt(ms + LN_EPS) * g_ref[...] * (1.0 - lam_init)
    o_ref[...] = o.T.astype(BF16)


def _mixer(qt, k, vt, kn, lq1, lk1, lq2, lk2, g_col, u, conv_w, conv_b, conv_g, conv_beta, *, tq, tk, conv_rows,
           lam_init):
    n_heads, hd2, s = qt.shape
    cw = u.shape[1]
    nq = s // tq
    tc = s // (n_heads * nq)
    assert tc % conv_rows == 0 and tc % CONV_HALO == 0
    lam_spec = _const_spec(lq1.shape)
    tile = lambda h, i: h * nq + i
    return pl.pallas_call(
        functools.partial(_mixer_kernel, tk=tk, conv_rows=conv_rows, lam_init=lam_init),
        grid=(n_heads, nq),
        in_specs=[pl.BlockSpec((1, hd2, tq), lambda h, i: (h, 0, i)),
                  pl.BlockSpec((s, hd2), lambda h, i: (0, h)),
                  pl.BlockSpec((1, s // tk, hd2, tk), lambda h, i: (h, 0, 0, 0)),
                  _const_spec(kn.shape),
                  lam_spec, lam_spec, lam_spec, lam_spec,
                  _const_spec(g_col.shape),
                  *_conv_specs(s, cw, tc, tile),
                  _const_spec(conv_w.shape),
                  _const_spec(conv_b.shape),
                  _const_spec(conv_g.shape),
                  _const_spec(conv_beta.shape)],
        out_specs=[pl.BlockSpec((tq, hd2), lambda h, i: (i, h)),
                   pl.BlockSpec((tc, cw), lambda h, i: (tile(h, i), 0))],
        out_shape=[jax.ShapeDtypeStruct((s, n_heads * hd2), BF16),
                   jax.ShapeDtypeStruct((s, cw), BF16)],
        scratch_shapes=[pltpu.VMEM((hd2, 2 * tq), BF16),
                        pltpu.VMEM((hd2, 2 * tq), F32),
                        pltpu.VMEM((1, 2 * tq), F32),
                        pltpu.VMEM((1, 2 * tq), F32),
                        *_conv_scratch(tc, cw)],
        compiler_params=pltpu.CompilerParams(dimension_semantics=("arbitrary", "arbitrary"),
                                             vmem_limit_bytes=VMEM_LIMIT),
        name="mixer",
    )(qt, k, vt, kn, lq1, lk1, lq2, lk2, g_col, u, u, u, conv_w, conv_b, conv_g, conv_beta)


def _mix_out_kernel(conv_ref, attn_ref, x_ref, mod_ref, w_ref, g_ref, b_ref, o_ref, *, sub, n_part):
    cw = conv_ref.shape[1]
    tp = x_ref.shape[0] // n_part
    gate_c = mod_ref[3 * sub + 2:3 * sub + 3, :]
    ys = []
    for p in range(n_part):
        rows = slice(p * tp, (p + 1) * tp)
        ys.append(jnp.dot(conv_ref[rows, :], w_ref[:cw, :], preferred_element_type=F32)
                  + jnp.dot(attn_ref[rows, :], w_ref[cw:, :], preferred_element_type=F32))
    for p in range(n_part):
        rows = slice(p * tp, (p + 1) * tp)
        z = ALPHA * x_ref[rows, :] + (1.0 + gate_c) * ys[p]
        o_ref[rows, :] = _layer_norm(z, g_ref[...], b_ref[...])


def _mix_out(conv, attn, x, mod9, w, g, b, *, tm, n_part):
    s, d = x.shape
    row = lambda i: (i, 0)
    return pl.pallas_call(
        functools.partial(_mix_out_kernel, sub=1, n_part=n_part),
        grid=(s // tm,),
        in_specs=[pl.BlockSpec((tm, conv.shape[1]), row),
                  pl.BlockSpec((tm, attn.shape[1]), row),
                  pl.BlockSpec((tm, d), row),
                  _const_spec(mod9.shape),
                  _const_spec(w.shape),
                  _const_spec(g.shape),
                  _const_spec(b.shape)],
        out_specs=pl.BlockSpec((tm, d), row),
        out_shape=jax.ShapeDtypeStruct((s, d), F32),
        compiler_params=pltpu.CompilerParams(dimension_semantics=("arbitrary",),
                                             vmem_limit_bytes=VMEM_LIMIT),
        name="mix_out",
    )(conv, attn, x, mod9, w, g, b)


def _rope_rows():
    inv_freq = ROPE_THETA ** (-jnp.arange(0, ROT_DIM, 2, dtype=F32) / ROT_DIM)
    half = ROT_DIM // 2
    zeros_h = jnp.zeros((half,), F32)
    zeros_p = jnp.zeros((DIFF_HEAD_DIM - ROT_DIM,), F32)
    ones_h = jnp.ones((half,), F32)
    reps = LANES // DIFF_HEAD_DIM
    freq = jnp.tile(jnp.concatenate([inv_freq, inv_freq, zeros_p]), reps)
    neg_first = jnp.tile(jnp.concatenate([-ones_h, zeros_h, zeros_p]), reps)
    pos_second = jnp.tile(jnp.concatenate([zeros_h, ones_h, zeros_p]), reps)
    return jnp.stack([freq, neg_first, pos_second])


def kernel(x, c, w_ada, b_ada, ffn1_w_in, ffn1_w_out, ln1_g, ln1_b, mix_w_in, conv_w, conv_b, conv_ln_g,
           conv_ln_b, lambda_q1, lambda_k1, lambda_q2, lambda_k2, subln_g, mix_w_out, ln2_g, ln2_b,
           ffn2_w_in, ffn2_w_out, ln3_g, ln3_b):
    batch, s, d = x.shape
    assert batch == 1 and w_ada.shape[0] == DEPTH == 1
    cw = conv_w.shape[2]
    aw = (mix_w_in.shape[2] - 2 * cw) // 3
    hd2 = 2 * DIFF_HEAD_DIM
    n_heads = aw // hd2
    lam_init = 0.8 - 0.6 * math.exp(-0.3 * 0)
    q_scale = math.log2(math.e) / math.sqrt(DIFF_HEAD_DIM)
    tq, tk = 512, 256

    mod9 = _ada(c.reshape(d, 1), w_ada[0], b_ada, tn=1152).reshape(9, d)
    x0 = x[0]
    x1 = _ffn(x0, mod9, ffn1_w_in[0], ffn1_w_out[0], ln1_g, ln1_b,
              sub=0, weight=0.5, tm=512, tf=256)

    u, qt, k, vt, kn = _mix_in(x1, mod9, mix_w_in[0], _rope_rows(),
                               cw=cw, aw=aw, q_scale=q_scale, tm=1024, tk=tk, n_part=4)
    attn, conv = _mixer(qt, k, vt, kn, lambda_q1, lambda_k1, lambda_q2, lambda_k2, subln_g.reshape(hd2, 1),
                        u, conv_w[0], conv_b, conv_ln_g, conv_ln_b,
                        tq=tq, tk=tk, conv_rows=32, lam_init=lam_init)
    x2 = _mix_out(conv, attn, x1, mod9, mix_w_out[0].astype(BF16), ln2_g, ln2_b, tm=1024, n_part=4)

    x3 = _ffn(x2, mod9, ffn2_w_in[0], ffn2_w_out[0], ln3_g, ln3_b,
              sub=2, weight=0.5, tm=512, tf=256)
    return x3[None]
```

```python
import functools
import math

import jax
import jax.numpy as jnp
from jax import lax
from jax.experimental import pallas as pl
from jax.experimental.pallas import tpu as pltpu

F32 = jnp.float32
BF16 = jnp.bfloat16

DEPTH = 1
ALPHA = (2.0 * DEPTH) ** 0.25
LN_EPS = 1e-5
DIFF_HEAD_DIM = 64
ROT_DIM = DIFF_HEAD_DIM // 4
ROPE_THETA = 500000.0
CONV_KERNEL = 31
CONV_PAD = (CONV_KERNEL - 1) // 2
CONV_HALO = 16
LANES = 128
SUBLANES = 8
SHIFT_SLACK = 1.0 + 2.0 ** -10
L_FLOOR = 2.0 ** -80
CONV_FIRST_BLOCK = 10
VMEM_LIMIT = 56 * 1024 * 1024


def _sigmoid(x):
    return 1.0 / (1.0 + jnp.exp(-x))


def _layer_norm(z, g, b):
    mu = jnp.mean(z, axis=-1, keepdims=True)
    zc = z - mu
    var = jnp.mean(zc * zc, axis=-1, keepdims=True)
    return zc * lax.rsqrt(var + LN_EPS) * g + b


def _modulate(x, mod_ref, sub):
    shift = mod_ref[3 * sub:3 * sub + 1, :]
    scale = mod_ref[3 * sub + 1:3 * sub + 2, :]
    return x * (1.0 + scale) + shift


def _const_spec(shape):
    return pl.BlockSpec(shape, lambda *_: (0,) * len(shape), pipeline_mode=pl.Buffered(1))


def _ada_kernel(c_ref, w_ref, b_ref, o_ref):
    c = c_ref[...]
    ca = c * _sigmoid(c)
    o_ref[...] = jnp.sum(ca * w_ref[...], axis=0, keepdims=True) + b_ref[...]


def _ada(c_col, w, b_row, tn):
    d, n = w.shape
    return pl.pallas_call(
        _ada_kernel,
        grid=(n // tn,),
        in_specs=[pl.BlockSpec((d, 1), lambda j: (0, 0)),
                  pl.BlockSpec((d, tn), lambda j: (0, j)),
                  pl.BlockSpec((1, tn), lambda j: (0, j))],
        out_specs=pl.BlockSpec((1, tn), lambda j: (0, j)),
        out_shape=jax.ShapeDtypeStruct((1, n), F32),
        compiler_params=pltpu.CompilerParams(dimension_semantics=("arbitrary",),
                                             vmem_limit_bytes=VMEM_LIMIT),
        name="ada",
    )(c_col, w, b_row)


def _ffn_kernel(x_ref, mod_ref, win_ref, wout_ref, g_ref, b_ref, o_ref, act_ref, *, sub, weight, tf):
    x = x_ref[...]
    d_ff = wout_ref.shape[0]
    h = _modulate(x, mod_ref, sub).astype(win_ref.dtype)
    for c in range(d_ff // tf):
        gate = jnp.dot(h, win_ref[:, c * tf:(c + 1) * tf], preferred_element_type=F32)
        up = jnp.dot(h, win_ref[:, d_ff + c * tf:d_ff + (c + 1) * tf], preferred_element_type=F32)
        act_ref[:, c * tf:(c + 1) * tf] = (gate * _sigmoid(gate) * up).astype(act_ref.dtype)
    y = jnp.dot(act_ref[...], wout_ref[...], preferred_element_type=F32)
    gate_c = mod_ref[3 * sub + 2:3 * sub + 3, :]
    z = ALPHA * x + weight * (1.0 + gate_c) * y
    o_ref[...] = _layer_norm(z, g_ref[...], b_ref[...])


def _ffn(x, mod9, w_in, w_out, g, b, *, sub, weight, tm, tf):
    s, d = x.shape
    d_ff = w_out.shape[0]
    return pl.pallas_call(
        functools.partial(_ffn_kernel, sub=sub, weight=weight, tf=tf),
        grid=(s // tm,),
        in_specs=[pl.BlockSpec((tm, d), lambda i: (i, 0)),
                  _const_spec(mod9.shape),
                  _const_spec(w_in.shape),
                  _const_spec(w_out.shape),
                  _const_spec(g.shape),
                  _const_spec(b.shape)],
        out_specs=pl.BlockSpec((tm, d), lambda i: (i, 0)),
        out_shape=jax.ShapeDtypeStruct((s, d), F32),
        scratch_shapes=[pltpu.VMEM((tm, d_ff), w_out.dtype)],
        compiler_params=pltpu.CompilerParams(dimension_semantics=("arbitrary",),
                                             vmem_limit_bytes=VMEM_LIMIT),
        name=f"ffn{sub}",
    )(x, mod9, w_in, w_out, g, b)


def _mix_in_kernel(x_ref, mod_ref, w_ref, rope_ref, u_ref, qt_ref, k_ref, vt_ref, kn_ref, cos_ref, sin_ref, *,
                   sub, cw, aw, q_scale, n_part):
    tm = x_ref.shape[0]
    tp = tm // n_part
    tk = vt_ref.shape[3]
    freq = rope_ref[0:1, :]

    @pl.when(pl.program_id(0) == 0)
    def _in_tile_angles():
        ang = lax.broadcasted_iota(jnp.int32, (tm, LANES), 0).astype(F32) * freq
        cos_ref[...] = jnp.cos(ang)
        sin_ref[...] = jnp.sin(ang)
        kn_ref[...] = jnp.zeros_like(kn_ref)

    comp0 = lax.broadcasted_iota(jnp.int32, (tp, LANES), 1) < DIFF_HEAD_DIM
    projs = []
    for p in range(n_part):
        h = _modulate(x_ref[p * tp:(p + 1) * tp, :], mod_ref, sub).astype(w_ref.dtype)
        projs.append(jnp.dot(h, w_ref[...], preferred_element_type=F32))

    ang0 = (pl.program_id(0) * tm).astype(F32) * freq
    c0, s0 = jnp.cos(ang0), jnp.sin(ang0)
    half = ROT_DIM // 2
    q0, k0, v0 = 2 * cw, 2 * cw + aw, 2 * cw + 2 * aw
    for p, proj in enumerate(projs):
        rows = slice(p * tp, (p + 1) * tp)
        u_ref[rows, :] = proj[:, :cw] * _sigmoid(proj[:, cw:2 * cw])
        rc = c0 * cos_ref[rows, :] - s0 * sin_ref[rows, :]
        sin = s0 * cos_ref[rows, :] + c0 * sin_ref[rows, :]
        rs1, rs2 = sin * rope_ref[1:2, :], sin * rope_ref[2:3, :]

        def rope(t):
            return t * rc + pltpu.roll(t, LANES - half, 1) * rs1 + pltpu.roll(t, half, 1) * rs2

        for g in range(aw // LANES):
            sl = slice(g * LANES, (g + 1) * LANES)
            qt_ref[g, :, rows] = (rope(proj[:, q0 + g * LANES:q0 + (g + 1) * LANES]) * q_scale).T.astype(BF16)
            kb = rope(proj[:, k0 + g * LANES:k0 + (g + 1) * LANES]).astype(BF16)
            k_ref[rows, sl] = kb
            kf = kb.astype(F32)
            sq = kf * kf
            n0 = jnp.max(jnp.sum(jnp.where(comp0, sq, 0.0), axis=1, keepdims=True), axis=0, keepdims=True)
            n1 = jnp.max(jnp.sum(jnp.where(comp0, 0.0, sq), axis=1, keepdims=True), axis=0, keepdims=True)
            kn_ref[g:g + 1, :] = jnp.maximum(kn_ref[g:g + 1, :], jnp.where(comp0[0:1, :], n0, n1))
            for c in range(tp // tk):
                vt_ref[g, p * (tp // tk) + c] = (
                    proj[c * tk:(c + 1) * tk, v0 + g * LANES:v0 + (g + 1) * LANES].T.astype(BF16))


def _mix_in(x, mod9, w, rope_rows, *, cw, aw, q_scale, tm, tk, n_part):
    s, d = x.shape
    n_heads = aw // LANES
    assert (tm // n_part) % tk == 0 and n_heads <= SUBLANES
    row = lambda i: (i, 0)
    return pl.pallas_call(
        functools.partial(_mix_in_kernel, sub=1, cw=cw, aw=aw, q_scale=q_scale, n_part=n_part),
        grid=(s // tm,),
        in_specs=[pl.BlockSpec((tm, d), row),
                  _const_spec(mod9.shape),
                  _const_spec(w.shape),
                  _const_spec(rope_rows.shape)],
        out_specs=[pl.BlockSpec((tm, cw), row),
                   pl.BlockSpec((n_heads, LANES, tm), lambda i: (0, 0, i)),
                   pl.BlockSpec((tm, aw), row),
                   pl.BlockSpec((n_heads, tm // tk, LANES, tk), lambda i: (0, i, 0, 0)),
                   pl.BlockSpec((SUBLANES, LANES), lambda i: (0, 0))],
        out_shape=[jax.ShapeDtypeStruct((s, cw), F32),
                   jax.ShapeDtypeStruct((n_heads, LANES, s), BF16),
                   jax.ShapeDtypeStruct((s, aw), BF16),
                   jax.ShapeDtypeStruct((n_heads, s // tk, LANES, tk), BF16),
                   jax.ShapeDtypeStruct((SUBLANES, LANES), F32)],
        scratch_shapes=[pltpu.VMEM((tm, LANES), F32),
                        pltpu.VMEM((tm, LANES), F32)],
        compiler_params=pltpu.CompilerParams(dimension_semantics=("arbitrary",),
                                             vmem_limit_bytes=VMEM_LIMIT),
        name="mix_in",
    )(x, mod9, w, rope_rows)


def _dependent_zero(v):
    r, c = v.shape
    folded = jnp.sum(v.reshape(r // SUBLANES, SUBLANES, c), axis=0)
    folded = sum(folded[:, g * LANES:(g + 1) * LANES] for g in range(c // LANES))
    bits = lax.bitcast_convert_type(folded[0:1, :], jnp.uint32)
    return lax.bitcast_convert_type((bits >> 16) >> 16, F32)


def _conv_tile(i, n_tiles, prev_ref, cur_ref, next_ref, w_ref, cb_ref, g_ref, b_ref, o_ref, ext_ref, sh_ref, y_ref,
               rows):
    tm, cw = cur_ref.shape
    ext_ref[0:CONV_HALO, :] = jnp.where(i > 0, prev_ref[...], 0.0)
    ext_ref[CONV_HALO:CONV_HALO + tm, :] = cur_ref[...]
    ext_ref[CONV_HALO + tm:, :] = jnp.where(i < n_tiles - 1, next_ref[...], 0.0)
    span = sh_ref.shape[1]
    for b in range(SUBLANES):
        sh_ref[b] = ext_ref[b:b + span, :]
    base = CONV_HALO - CONV_PAD
    done = []
    for lc in range(cw // LANES):
        ls = slice(lc * LANES, (lc + 1) * LANES)
        for rc in range(tm // rows):
            r0 = rc * rows
            acc = jnp.zeros((rows, LANES), F32)
            for t in range(CONV_KERNEL):
                off = base + t
                a0 = r0 + SUBLANES * (off // SUBLANES)
                acc = acc + sh_ref[off % SUBLANES, a0:a0 + rows, ls] * w_ref[t:t + 1, ls]
            y_ref[r0:r0 + rows, ls] = acc
            done.append(_dependent_zero(acc))
    y = _layer_norm(y_ref[...] + cb_ref[...], g_ref[...], b_ref[...])
    y = y * _sigmoid(y)
    o_ref[...] = y.astype(BF16)
    done.append(_dependent_zero(y))
    return done


def _conv_specs(s, cw, tm, tile_index):
    nh = tm // CONV_HALO
    last = s // CONV_HALO - 1
    return [pl.BlockSpec((CONV_HALO, cw), lambda *g: (jnp.maximum(tile_index(*g) * nh - 1, 0), 0)),
            pl.BlockSpec((tm, cw), lambda *g: (tile_index(*g), 0)),
            pl.BlockSpec((CONV_HALO, cw), lambda *g: (jnp.minimum((tile_index(*g) + 1) * nh, last), 0))]


def _conv_scratch(tm, cw):
    return [pltpu.VMEM((tm + 2 * CONV_HALO, cw), F32),
            pltpu.VMEM((SUBLANES, tm + 2 * CONV_HALO - SUBLANES, cw), F32),
            pltpu.VMEM((tm, cw), F32)]


def _mixer_kernel(qt_ref, k_ref, vt_ref, kn_ref, lq1_ref, lk1_ref, lq2_ref, lk2_ref, g_ref,
                  up_ref, uc_ref, un_ref, cw_ref, cb_ref, cg_ref, cbeta_ref,
                  o_ref, conv_ref,
                  rhs_ref, acc_ref, l_ref, kmax_ref, ext_ref, sh_ref, y_ref, *, tk, conv_rows, lam_init):
    step = pl.program_id(0) * pl.num_programs(1) + pl.program_id(1)
    n_steps = pl.num_programs(0) * pl.num_programs(1)

    hd2, tq = qt_ref.shape[1], qt_ref.shape[2]
    n = 2 * tq
    n_kv = k_ref.shape[0] // tk
    qt = qt_ref[0]
    row = lax.broadcasted_iota(jnp.int32, qt.shape, 0)
    zero = jnp.zeros_like(qt)
    rhs_ref[:, :tq] = jnp.where(row < DIFF_HEAD_DIM, qt, zero)
    rhs_ref[:, tq:] = jnp.where(row >= DIFF_HEAD_DIM, qt, zero)

    def k_block(j):
        return k_ref[pl.ds(pl.multiple_of(j * tk, tk), tk), :]

    @pl.when(pl.program_id(1) == 0)
    def _key_norm_bound():
        kn = kn_ref[pl.ds(pl.program_id(0), 1), :]
        col = lax.broadcasted_iota(jnp.int32, (1, n), 1)
        kmax_ref[...] = jnp.sqrt(jnp.where(col < tq, kn[:, 0:1], kn[:, DIFF_HEAD_DIM:DIFF_HEAD_DIM + 1]))

    conv_done = _conv_tile(step, n_steps, up_ref, uc_ref, un_ref, cw_ref, cb_ref, cg_ref, cbeta_ref, conv_ref,
                           ext_ref, sh_ref, y_ref, conv_rows)
    conv_stride = (n_kv - 1 - CONV_FIRST_BLOCK) // len(conv_done)
    assert conv_stride >= 1

    r32 = rhs_ref[...].astype(F32)
    qn = jnp.sqrt(jnp.sum(r32 * r32, axis=0, keepdims=True))
    m = qn * kmax_ref[...] * SHIFT_SLACK
    l8 = jnp.zeros((8, n), F32)
    acc = jnp.zeros((hd2, n), F32)
    e_prev = None
    for j in range(n_kv):
        s = jnp.dot(k_ref[j * tk:(j + 1) * tk, :], rhs_ref[...], preferred_element_type=F32)
        if e_prev is not None:
            acc = acc + jnp.dot(vt_ref[0, j - 1], e_prev, preferred_element_type=F32)
        if j >= CONV_FIRST_BLOCK and (j - CONV_FIRST_BLOCK) % conv_stride == 0 and conv_done:
            m = m + jnp.concatenate([conv_done.pop(0)] * (n // LANES), axis=1)
        e = jnp.exp2(s - m)
        l8 = l8 + jnp.sum(e.reshape(tk // 8, 8, n), axis=0)
        e_prev = e.astype(BF16)
    acc = acc + jnp.dot(vt_ref[0, n_kv - 1], e_prev, preferred_element_type=F32)
    l = jnp.sum(l8, axis=0, keepdims=True)
    acc_ref[...] = acc
    l_ref[...] = l

    @pl.when(jnp.logical_not(jnp.min(l) >= L_FLOOR))
    def _running_max_fallback():
        acc_ref[...] = jnp.zeros_like(acc_ref)

        def body(j, carry):
            m_run, l_run = carry
            s = jnp.dot(k_block(j), rhs_ref[...], preferred_element_type=F32)
            m_new = jnp.maximum(m_run, jnp.max(s, axis=0, keepdims=True))
            alpha = jnp.exp2(m_run - m_new)
            e = jnp.exp2(s - m_new)
            pv = jnp.dot(vt_ref[0, j], e.astype(BF16), preferred_element_type=F32)
            acc_ref[...] = alpha * acc_ref[...] + pv
            return m_new, alpha * l_run + jnp.sum(e, axis=0, keepdims=True)

        init = (jnp.full((1, n), -jnp.inf, F32), jnp.zeros((1, n), F32))
        _, l_run = lax.fori_loop(0, n_kv, body, init)
        l_ref[...] = l_run

    o = acc_ref[...] * (1.0 / l_ref[...])
    lam = (jnp.exp(jnp.sum(lq1_ref[...] * lk1_ref[...])) - jnp.exp(jnp.sum(lq2_ref[...] * lk2_ref[...]))
           + lam_init)
    o = o[:, :tq] - lam * o[:, tq:]
    ms = jnp.mean(o * o, axis=0, keepdims=True)
    o = o * lax.rsqrt(ms + LN_EPS) * g_ref[...] * (1.0 - lam_init)
    o_ref[...] = o.T.astype(BF16)


def _mixer(qt, k, vt, kn, lq1, lk1, lq2, lk2, g_col, u, conv_w, conv_b, conv_g, conv_beta, *, tq, tk, conv_rows,
           lam_init):
    n_heads, hd2, s = qt.shape
    cw = u.shape[1]
    nq = s // tq
    tc = s // (n_heads * nq)
    assert tc % conv_rows == 0 and tc % CONV_HALO == 0
    lam_spec = _const_spec(lq1.shape)
    tile = lambda h, i: h * nq + i
    return pl.pallas_call(
        functools.partial(_mixer_kernel, tk=tk, conv_rows=conv_rows, lam_init=lam_init),
        grid=(n_heads, nq),
        in_specs=[pl.BlockSpec((1, hd2, tq), lambda h, i: (h, 0, i)),
                  pl.BlockSpec((s, hd2), lambda h, i: (0, h)),
                  pl.BlockSpec((1, s // tk, hd2, tk), lambda h, i: (h, 0, 0, 0)),
                  _const_spec(kn.shape),
                  lam_spec, lam_spec, lam_spec, lam_spec,
                  _const_spec(g_col.shape),
                  *_conv_specs(s, cw, tc, tile),
                  _const_spec(conv_w.shape),
                  _const_spec(conv_b.shape),
                  _const_spec(conv_g.shape),
                  _const_spec(conv_beta.shape)],
        out_specs=[pl.BlockSpec((tq, hd2), lambda h, i: (i, h)),
                   pl.BlockSpec((tc, cw), lambda h, i: (tile(h, i), 0))],
        out_shape=[jax.ShapeDtypeStruct((s, n_heads * hd2), BF16),
                   jax.ShapeDtypeStruct((s, cw), BF16)],
        scratch_shapes=[pltpu.VMEM((hd2, 2 * tq), BF16),
                        pltpu.VMEM((hd2, 2 * tq), F32),
                        pltpu.VMEM((1, 2 * tq), F32),
                        pltpu.VMEM((1, 2 * tq), F32),
                        *_conv_scratch(tc, cw)],
        compiler_params=pltpu.CompilerParams(dimension_semantics=("arbitrary", "arbitrary"),
                                             vmem_limit_bytes=VMEM_LIMIT),
        name="mixer",
    )(qt, k, vt, kn, lq1, lk1, lq2, lk2, g_col, u, u, u, conv_w, conv_b, conv_g, conv_beta)


def _mix_out_kernel(conv_ref, attn_ref, x_ref, mod_ref, w_ref, g_ref, b_ref, o_ref, *, sub, n_part):
    cw = conv_ref.shape[1]
    tp = x_ref.shape[0] // n_part
    gate_c = mod_ref[3 * sub + 2:3 * sub + 3, :]
    ys = []
    for p in range(n_part):
        rows = slice(p * tp, (p + 1) * tp)
        ys.append(jnp.dot(conv_ref[rows, :], w_ref[:cw, :], preferred_element_type=F32)
                  + jnp.dot(attn_ref[rows, :], w_ref[cw:, :], preferred_element_type=F32))
    for p in range(n_part):
        rows = slice(p * tp, (p + 1) * tp)
        z = ALPHA * x_ref[rows, :] + (1.0 + gate_c) * ys[p]
        o_ref[rows, :] = _layer_norm(z, g_ref[...], b_ref[...])


def _mix_out(conv, attn, x, mod9, w, g, b, *, tm, n_part):
    s, d = x.shape
    row = lambda i: (i, 0)
    return pl.pallas_call(
        functools.partial(_mix_out_kernel, sub=1, n_part=n_part),
        grid=(s // tm,),
        in_specs=[pl.BlockSpec((tm, conv.shape[1]), row),
                  pl.BlockSpec((tm, attn.shape[1]), row),
                  pl.BlockSpec((tm, d), row),
                  _const_spec(mod9.shape),
                  _const_spec(w.shape),
                  _const_spec(g.shape),
                  _const_spec(b.shape)],
        out_specs=pl.BlockSpec((tm, d), row),
        out_shape=jax.ShapeDtypeStruct((s, d), F32),
        compiler_params=pltpu.CompilerParams(dimension_semantics=("arbitrary",),
                                             vmem_limit_bytes=VMEM_LIMIT),
        name="mix_out",
    )(conv, attn, x, mod9, w, g, b)


def _rope_rows():
    inv_freq = ROPE_THETA ** (-jnp.arange(0, ROT_DIM, 2, dtype=F32) / ROT_DIM)
    half = ROT_DIM // 2
    zeros_h = jnp.zeros((half,), F32)
    zeros_p = jnp.zeros((DIFF_HEAD_DIM - ROT_DIM,), F32)
    ones_h = jnp.ones((half,), F32)
    reps = LANES // DIFF_HEAD_DIM
    freq = jnp.tile(jnp.concatenate([inv_freq, inv_freq, zeros_p]), reps)
    neg_first = jnp.tile(jnp.concatenate([-ones_h, zeros_h, zeros_p]), reps)
    pos_second = jnp.tile(jnp.concatenate([zeros_h, ones_h, zeros_p]), reps)
    return jnp.stack([freq, neg_first, pos_second])


def kernel(x, c, w_ada, b_ada, ffn1_w_in, ffn1_w_out, ln1_g, ln1_b, mix_w_in, conv_w, conv_b, conv_ln_g,
           conv_ln_b, lambda_q1, lambda_k1, lambda_q2, lambda_k2, subln_g, mix_w_out, ln2_g, ln2_b,
           ffn2_w_in, ffn2_w_out, ln3_g, ln3_b):
    batch, s, d = x.shape
    assert batch == 1 and w_ada.shape[0] == DEPTH == 1
    cw = conv_w.shape[2]
    aw = (mix_w_in.shape[2] - 2 * cw) // 3
    hd2 = 2 * DIFF_HEAD_DIM
    n_heads = aw // hd2
    lam_init = 0.8 - 0.6 * math.exp(-0.3 * 0)
    q_scale = math.log2(math.e) / math.sqrt(DIFF_HEAD_DIM)
    tq, tk = 512, 256

    mod9 = _ada(c.reshape(d, 1), w_ada[0], b_ada, tn=1152).reshape(9, d)
    x0 = x[0]
    x1 = _ffn(x0, mod9, ffn1_w_in[0], ffn1_w_out[0], ln1_g, ln1_b,
              sub=0, weight=0.5, tm=512, tf=256)

    u, qt, k, vt, kn = _mix_in(x1, mod9, mix_w_in[0], _rope_rows(),
                               cw=cw, aw=aw, q_scale=q_scale, tm=1024, tk=tk, n_part=4)
    attn, conv = _mixer(qt, k, vt, kn, lambda_q1, lambda_k1, lambda_q2, lambda_k2, subln_g.reshape(hd2, 1),
                        u, conv_w[0], conv_b, conv_ln_g, conv_ln_b,
                        tq=tq, tk=tk, conv_rows=64, lam_init=lam_init)
    x2 = _mix_out(conv, attn, x1, mod9, mix_w_out[0].astype(BF16), ln2_g, ln2_b, tm=1024, n_part=4)

    x3 = _ffn(x2, mod9, ffn2_w_in[0], ffn2_w_out[0], ln3_g, ln3_b,
              sub=2, weight=0.5, tm=512, tf=256)
    return x3[None]
```

```python
import functools
import math
from typing import NamedTuple

import jax
import jax.numpy as jnp
from jax import lax
from jax.experimental import pallas as pl
from jax.experimental.pallas import tpu as pltpu

F32 = jnp.float32
BF16 = jnp.bfloat16

DEPTH = 1
ALPHA = (2.0 * DEPTH) ** 0.25
LN_EPS = 1e-5
DIFF_HEAD_DIM = 64
ROT_DIM = DIFF_HEAD_DIM // 4
ROPE_THETA = 500000.0
CONV_KERNEL = 31
CONV_PAD = (CONV_KERNEL - 1) // 2
CONV_HALO = 16
LANES = 128
SUBLANES = 8
SHIFT_SLACK = 1.0 + 2.0 ** -10
L_FLOOR = 2.0 ** -80
CONV_FIRST_BLOCK = 10
VMEM_LIMIT = 56 * 1024 * 1024


class _Tiles(NamedTuple):
    ada_cols: int = 1152
    ffn_rows: int = 512
    ffn_cols: int = 256
    proj_rows: int = 1024
    proj_parts: int = 4
    attn_q: int = 512
    attn_kv: int = 256
    conv_rows: int = 32


TILES = _Tiles()


def _sigmoid(x):
    return 1.0 / (1.0 + jnp.exp(-x))


def _layer_norm(z, g, b):
    mu = jnp.mean(z, axis=-1, keepdims=True)
    zc = z - mu
    var = jnp.mean(zc * zc, axis=-1, keepdims=True)
    return zc * lax.rsqrt(var + LN_EPS) * g + b


def _modulate(x, mod_ref, sub):
    shift = mod_ref[3 * sub:3 * sub + 1, :]
    scale = mod_ref[3 * sub + 1:3 * sub + 2, :]
    return x * (1.0 + scale) + shift


def _const_spec(shape):
    return pl.BlockSpec(shape, lambda *_: (0,) * len(shape), pipeline_mode=pl.Buffered(1))


def _ada_kernel(c_ref, w_ref, b_ref, o_ref):
    c = c_ref[...]
    ca = c * _sigmoid(c)
    o_ref[...] = jnp.sum(ca * w_ref[...], axis=0, keepdims=True) + b_ref[...]


def _ada(c_col, w, b_row, tn):
    d, n = w.shape
    assert n % tn == 0 and tn % LANES == 0
    return pl.pallas_call(
        _ada_kernel,
        grid=(n // tn,),
        in_specs=[pl.BlockSpec((d, 1), lambda j: (0, 0)),
                  pl.BlockSpec((d, tn), lambda j: (0, j)),
                  pl.BlockSpec((1, tn), lambda j: (0, j))],
        out_specs=pl.BlockSpec((1, tn), lambda j: (0, j)),
        out_shape=jax.ShapeDtypeStruct((1, n), F32),
        compiler_params=pltpu.CompilerParams(dimension_semantics=("arbitrary",),
                                             vmem_limit_bytes=VMEM_LIMIT),
        name="ada",
    )(c_col, w, b_row)


def _ffn_kernel(x_ref, mod_ref, win_ref, wout_ref, g_ref, b_ref, o_ref, act_ref, *, sub, weight, tf):
    x = x_ref[...]
    d_ff = wout_ref.shape[0]
    h = _modulate(x, mod_ref, sub).astype(win_ref.dtype)
    for c in range(d_ff // tf):
        gate = jnp.dot(h, win_ref[:, c * tf:(c + 1) * tf], preferred_element_type=F32)
        up = jnp.dot(h, win_ref[:, d_ff + c * tf:d_ff + (c + 1) * tf], preferred_element_type=F32)
        act_ref[:, c * tf:(c + 1) * tf] = (gate * _sigmoid(gate) * up).astype(act_ref.dtype)
    y = jnp.dot(act_ref[...], wout_ref[...], preferred_element_type=F32)
    gate_c = mod_ref[3 * sub + 2:3 * sub + 3, :]
    z = ALPHA * x + weight * (1.0 + gate_c) * y
    o_ref[...] = _layer_norm(z, g_ref[...], b_ref[...])


def _ffn(x, mod9, w_in, w_out, g, b, *, sub, weight, tm, tf):
    s, d = x.shape
    d_ff = w_out.shape[0]
    assert s % tm == 0 and d_ff % tf == 0 and w_in.shape == (d, 2 * d_ff)
    return pl.pallas_call(
        functools.partial(_ffn_kernel, sub=sub, weight=weight, tf=tf),
        grid=(s // tm,),
        in_specs=[pl.BlockSpec((tm, d), lambda i: (i, 0)),
                  _const_spec(mod9.shape),
                  _const_spec(w_in.shape),
                  _const_spec(w_out.shape),
                  _const_spec(g.shape),
                  _const_spec(b.shape)],
        out_specs=pl.BlockSpec((tm, d), lambda i: (i, 0)),
        out_shape=jax.ShapeDtypeStruct((s, d), F32),
        scratch_shapes=[pltpu.VMEM((tm, d_ff), w_out.dtype)],
        compiler_params=pltpu.CompilerParams(dimension_semantics=("arbitrary",),
                                             vmem_limit_bytes=VMEM_LIMIT),
        name=f"ffn{sub}",
    )(x, mod9, w_in, w_out, g, b)


def _mix_in_kernel(x_ref, mod_ref, w_ref, rope_ref, u_ref, qt_ref, k_ref, vt_ref, kn_ref, cos_ref, sin_ref, *,
                   sub, cw, aw, q_scale, n_part):
    tm = x_ref.shape[0]
    tp = tm // n_part
    tk = vt_ref.shape[3]
    freq = rope_ref[0:1, :]

    @pl.when(pl.program_id(0) == 0)
    def _in_tile_angles():
        ang = lax.broadcasted_iota(jnp.int32, (tm, LANES), 0).astype(F32) * freq
        cos_ref[...] = jnp.cos(ang)
        sin_ref[...] = jnp.sin(ang)
        kn_ref[...] = jnp.zeros_like(kn_ref)

    comp0 = lax.broadcasted_iota(jnp.int32, (tp, LANES), 1) < DIFF_HEAD_DIM
    projs = []
    for p in range(n_part):
        h = _modulate(x_ref[p * tp:(p + 1) * tp, :], mod_ref, sub).astype(w_ref.dtype)
        projs.append(jnp.dot(h, w_ref[...], preferred_element_type=F32))

    ang0 = (pl.program_id(0) * tm).astype(F32) * freq
    c0, s0 = jnp.cos(ang0), jnp.sin(ang0)
    half = ROT_DIM // 2
    q0, k0, v0 = 2 * cw, 2 * cw + aw, 2 * cw + 2 * aw
    for p, proj in enumerate(projs):
        rows = slice(p * tp, (p + 1) * tp)
        u_ref[rows, :] = proj[:, :cw] * _sigmoid(proj[:, cw:2 * cw])
        rc = c0 * cos_ref[rows, :] - s0 * sin_ref[rows, :]
        sin = s0 * cos_ref[rows, :] + c0 * sin_ref[rows, :]
        rs1, rs2 = sin * rope_ref[1:2, :], sin * rope_ref[2:3, :]

        def rope(t):
            return t * rc + pltpu.roll(t, LANES - half, 1) * rs1 + pltpu.roll(t, half, 1) * rs2

        for g in range(aw // LANES):
            sl = slice(g * LANES, (g + 1) * LANES)
            qt_ref[g, :, rows] = (rope(proj[:, q0 + g * LANES:q0 + (g + 1) * LANES]) * q_scale).T.astype(BF16)
            kb = rope(proj[:, k0 + g * LANES:k0 + (g + 1) * LANES]).astype(BF16)
            k_ref[rows, sl] = kb
            kf = kb.astype(F32)
            sq = kf * kf
            n0 = jnp.max(jnp.sum(jnp.where(comp0, sq, 0.0), axis=1, keepdims=True), axis=0, keepdims=True)
            n1 = jnp.max(jnp.sum(jnp.where(comp0, 0.0, sq), axis=1, keepdims=True), axis=0, keepdims=True)
            kn_ref[g:g + 1, :] = jnp.maximum(kn_ref[g:g + 1, :], jnp.where(comp0[0:1, :], n0, n1))
            for c in range(tp // tk):
                vt_ref[g, p * (tp // tk) + c] = (
                    proj[c * tk:(c + 1) * tk, v0 + g * LANES:v0 + (g + 1) * LANES].T.astype(BF16))


def _mix_in(x, mod9, w, rope_rows, *, cw, aw, q_scale, tm, tk, n_part):
    s, d = x.shape
    n_heads = aw // LANES
    assert (tm // n_part) % tk == 0 and n_heads <= SUBLANES
    row = lambda i: (i, 0)
    return pl.pallas_call(
        functools.partial(_mix_in_kernel, sub=1, cw=cw, aw=aw, q_scale=q_scale, n_part=n_part),
        grid=(s // tm,),
        in_specs=[pl.BlockSpec((tm, d), row),
                  _const_spec(mod9.shape),
                  _const_spec(w.shape),
                  _const_spec(rope_rows.shape)],
        out_specs=[pl.BlockSpec((tm, cw), row),
                   pl.BlockSpec((n_heads, LANES, tm), lambda i: (0, 0, i)),
                   pl.BlockSpec((tm, aw), row),
                   pl.BlockSpec((n_heads, tm // tk, LANES, tk), lambda i: (0, i, 0, 0)),
                   pl.BlockSpec((SUBLANES, LANES), lambda i: (0, 0))],
        out_shape=[jax.ShapeDtypeStruct((s, cw), F32),
                   jax.ShapeDtypeStruct((n_heads, LANES, s), BF16),
                   jax.ShapeDtypeStruct((s, aw), BF16),
                   jax.ShapeDtypeStruct((n_heads, s // tk, LANES, tk), BF16),
                   jax.ShapeDtypeStruct((SUBLANES, LANES), F32)],
        scratch_shapes=[pltpu.VMEM((tm, LANES), F32),
                        pltpu.VMEM((tm, LANES), F32)],
        compiler_params=pltpu.CompilerParams(dimension_semantics=("arbitrary",),
                                             vmem_limit_bytes=VMEM_LIMIT),
        name="mix_in",
    )(x, mod9, w, rope_rows)


def _dependent_zero(v):
    r, c = v.shape
    folded = jnp.sum(v.reshape(r // SUBLANES, SUBLANES, c), axis=0)
    folded = sum(folded[:, g * LANES:(g + 1) * LANES] for g in range(c // LANES))
    bits = lax.bitcast_convert_type(folded[0:1, :], jnp.uint32)
    return lax.bitcast_convert_type((bits >> 16) >> 16, F32)


def _conv_tile(i, n_tiles, prev_ref, cur_ref, next_ref, w_ref, cb_ref, g_ref, b_ref, o_ref, ext_ref, sh_ref, y_ref,
               rows):
    tm, cw = cur_ref.shape
    ext_ref[0:CONV_HALO, :] = jnp.where(i > 0, prev_ref[...], 0.0)
    ext_ref[CONV_HALO:CONV_HALO + tm, :] = cur_ref[...]
    ext_ref[CONV_HALO + tm:, :] = jnp.where(i < n_tiles - 1, next_ref[...], 0.0)
    span = sh_ref.shape[1]
    for b in range(SUBLANES):
        sh_ref[b] = ext_ref[b:b + span, :]
    base = CONV_HALO - CONV_PAD
    done = []
    for lc in range(cw // LANES):
        ls = slice(lc * LANES, (lc + 1) * LANES)
        for rc in range(tm // rows):
            r0 = rc * rows
            acc = jnp.zeros((rows, LANES), F32)
            for t in range(CONV_KERNEL):
                off = base + t
                a0 = r0 + SUBLANES * (off // SUBLANES)
                acc = acc + sh_ref[off % SUBLANES, a0:a0 + rows, ls] * w_ref[t:t + 1, ls]
            y_ref[r0:r0 + rows, ls] = acc
            done.append(_dependent_zero(acc))
    y = _layer_norm(y_ref[...] + cb_ref[...], g_ref[...], b_ref[...])
    y = y * _sigmoid(y)
    o_ref[...] = y.astype(BF16)
    done.append(_dependent_zero(y))
    return done


def _conv_specs(s, cw, tm, tile_index):
    nh = tm // CONV_HALO
    last = s // CONV_HALO - 1
    return [pl.BlockSpec((CONV_HALO, cw), lambda *g: (jnp.maximum(tile_index(*g) * nh - 1, 0), 0)),
            pl.BlockSpec((tm, cw), lambda *g: (tile_index(*g), 0)),
            pl.BlockSpec((CONV_HALO, cw), lambda *g: (jnp.minimum((tile_index(*g) + 1) * nh, last), 0))]


def _conv_scratch(tm, cw):
    return [pltpu.VMEM((tm + 2 * CONV_HALO, cw), F32),
            pltpu.VMEM((SUBLANES, tm + 2 * CONV_HALO - SUBLANES, cw), F32),
            pltpu.VMEM((tm, cw), F32)]


def _mixer_kernel(qt_ref, k_ref, vt_ref, kn_ref, lq1_ref, lk1_ref, lq2_ref, lk2_ref, g_ref,
                  up_ref, uc_ref, un_ref, cw_ref, cb_ref, cg_ref, cbeta_ref,
                  o_ref, conv_ref,
                  rhs_ref, acc_ref, l_ref, kmax_ref, ext_ref, sh_ref, y_ref, *, tk, conv_rows, lam_init):
    step = pl.program_id(0) * pl.num_programs(1) + pl.program_id(1)
    n_steps = pl.num_programs(0) * pl.num_programs(1)

    hd2, tq = qt_ref.shape[1], qt_ref.shape[2]
    n = 2 * tq
    n_kv = k_ref.shape[0] // tk
    qt = qt_ref[0]
    row = lax.broadcasted_iota(jnp.int32, qt.shape, 0)
    zero = jnp.zeros_like(qt)
    rhs_ref[:, :tq] = jnp.where(row < DIFF_HEAD_DIM, qt, zero)
    rhs_ref[:, tq:] = jnp.where(row >= DIFF_HEAD_DIM, qt, zero)

    def k_block(j):
        return k_ref[pl.ds(pl.multiple_of(j * tk, tk), tk), :]

    @pl.when(pl.program_id(1) == 0)
    def _key_norm_bound():
        kn = kn_ref[pl.ds(pl.program_id(0), 1), :]
        col = lax.broadcasted_iota(jnp.int32, (1, n), 1)
        kmax_ref[...] = jnp.sqrt(jnp.where(col < tq, kn[:, 0:1], kn[:, DIFF_HEAD_DIM:DIFF_HEAD_DIM + 1]))

    conv_done = _conv_tile(step, n_steps, up_ref, uc_ref, un_ref, cw_ref, cb_ref, cg_ref, cbeta_ref, conv_ref,
                           ext_ref, sh_ref, y_ref, conv_rows)
    conv_stride = (n_kv - 1 - CONV_FIRST_BLOCK) // len(conv_done)
    assert conv_stride >= 1

    r32 = rhs_ref[...].astype(F32)
    qn = jnp.sqrt(jnp.sum(r32 * r32, axis=0, keepdims=True))
    m = qn * kmax_ref[...] * SHIFT_SLACK
    l8 = jnp.zeros((8, n), F32)
    acc = jnp.zeros((hd2, n), F32)
    e_prev = None
    for j in range(n_kv):
        s = jnp.dot(k_ref[j * tk:(j + 1) * tk, :], rhs_ref[...], preferred_element_type=F32)
        if e_prev is not None:
            acc = acc + jnp.dot(vt_ref[0, j - 1], e_prev, preferred_element_type=F32)
        if j >= CONV_FIRST_BLOCK and (j - CONV_FIRST_BLOCK) % conv_stride == 0 and conv_done:
            m = m + jnp.concatenate([conv_done.pop(0)] * (n // LANES), axis=1)
        e = jnp.exp2(s - m)
        l8 = l8 + jnp.sum(e.reshape(tk // 8, 8, n), axis=0)
        e_prev = e.astype(BF16)
    acc = acc + jnp.dot(vt_ref[0, n_kv - 1], e_prev, preferred_element_type=F32)
    l = jnp.sum(l8, axis=0, keepdims=True)
    acc_ref[...] = acc
    l_ref[...] = l

    @pl.when(jnp.logical_not(jnp.min(l) >= L_FLOOR))
    def _running_max_fallback():
        acc_ref[...] = jnp.zeros_like(acc_ref)

        def body(j, carry):
            m_run, l_run = carry
            s = jnp.dot(k_block(j), rhs_ref[...], preferred_element_type=F32)
            m_new = jnp.maximum(m_run, jnp.max(s, axis=0, keepdims=True))
            alpha = jnp.exp2(m_run - m_new)
            e = jnp.exp2(s - m_new)
            pv = jnp.dot(vt_ref[0, j], e.astype(BF16), preferred_element_type=F32)
            acc_ref[...] = alpha * acc_ref[...] + pv
            return m_new, alpha * l_run + jnp.sum(e, axis=0, keepdims=True)

        init = (jnp.full((1, n), -jnp.inf, F32), jnp.zeros((1, n), F32))
        _, l_run = lax.fori_loop(0, n_kv, body, init)
        l_ref[...] = l_run

    o = acc_ref[...] * (1.0 / l_ref[...])
    lam = (jnp.exp(jnp.sum(lq1_ref[...] * lk1_ref[...])) - jnp.exp(jnp.sum(lq2_ref[...] * lk2_ref[...]))
           + lam_init)
    o = o[:, :tq] - lam * o[:, tq:]
    ms = jnp.mean(o * o, axis=0, keepdims=True)
    o = o * lax.rsqrt(ms + LN_EPS) * g_ref[...] * (1.0 - lam_init)
    o_ref[...] = o.T.astype(BF16)


def _mixer(qt, k, vt, kn, lq1, lk1, lq2, lk2, g_col, u, conv_w, conv_b, conv_g, conv_beta, *, tq, tk, conv_rows,
           lam_init):
    n_heads, hd2, s = qt.shape
    cw = u.shape[1]
    nq = s // tq
    tc = s // (n_heads * nq)
    assert tc % conv_rows == 0 and tc % CONV_HALO == 0
    lam_spec = _const_spec(lq1.shape)
    tile = lambda h, i: h * nq + i
    return pl.pallas_call(
        functools.partial(_mixer_kernel, tk=tk, conv_rows=conv_rows, lam_init=lam_init),
        grid=(n_heads, nq),
        in_specs=[pl.BlockSpec((1, hd2, tq), lambda h, i: (h, 0, i)),
                  pl.BlockSpec((s, hd2), lambda h, i: (0, h)),
                  pl.BlockSpec((1, s // tk, hd2, tk), lambda h, i: (h, 0, 0, 0)),
                  _const_spec(kn.shape),
                  lam_spec, lam_spec, lam_spec, lam_spec,
                  _const_spec(g_col.shape),
                  *_conv_specs(s, cw, tc, tile),
                  _const_spec(conv_w.shape),
                  _const_spec(conv_b.shape),
                  _const_spec(conv_g.shape),
                  _const_spec(conv_beta.shape)],
        out_specs=[pl.BlockSpec((tq, hd2), lambda h, i: (i, h)),
                   pl.BlockSpec((tc, cw), lambda h, i: (tile(h, i), 0))],
        out_shape=[jax.ShapeDtypeStruct((s, n_heads * hd2), BF16),
                   jax.ShapeDtypeStruct((s, cw), BF16)],
        scratch_shapes=[pltpu.VMEM((hd2, 2 * tq), BF16),
                        pltpu.VMEM((hd2, 2 * tq), F32),
                        pltpu.VMEM((1, 2 * tq), F32),
                        pltpu.VMEM((1, 2 * tq), F32),
                        *_conv_scratch(tc, cw)],
        compiler_params=pltpu.CompilerParams(dimension_semantics=("arbitrary", "arbitrary"),
                                             vmem_limit_bytes=VMEM_LIMIT),
        name="mixer",
    )(qt, k, vt, kn, lq1, lk1, lq2, lk2, g_col, u, u, u, conv_w, conv_b, conv_g, conv_beta)


def _mix_out_kernel(conv_ref, attn_ref, x_ref, mod_ref, w_ref, g_ref, b_ref, o_ref, *, sub, n_part):
    cw = conv_ref.shape[1]
    tp = x_ref.shape[0] // n_part
    gate_c = mod_ref[3 * sub + 2:3 * sub + 3, :]
    ys = []
    for p in range(n_part):
        rows = slice(p * tp, (p + 1) * tp)
        ys.append(jnp.dot(conv_ref[rows, :], w_ref[:cw, :], preferred_element_type=F32)
                  + jnp.dot(attn_ref[rows, :], w_ref[cw:, :], preferred_element_type=F32))
    for p in range(n_part):
        rows = slice(p * tp, (p + 1) * tp)
        z = ALPHA * x_ref[rows, :] + (1.0 + gate_c) * ys[p]
        o_ref[rows, :] = _layer_norm(z, g_ref[...], b_ref[...])


def _mix_out(conv, attn, x, mod9, w, g, b, *, tm, n_part):
    s, d = x.shape
    row = lambda i: (i, 0)
    return pl.pallas_call(
        functools.partial(_mix_out_kernel, sub=1, n_part=n_part),
        grid=(s // tm,),
        in_specs=[pl.BlockSpec((tm, conv.shape[1]), row),
                  pl.BlockSpec((tm, attn.shape[1]), row),
                  pl.BlockSpec((tm, d), row),
                  _const_spec(mod9.shape),
                  _const_spec(w.shape),
                  _const_spec(g.shape),
                  _const_spec(b.shape)],
        out_specs=pl.BlockSpec((tm, d), row),
        out_shape=jax.ShapeDtypeStruct((s, d), F32),
        compiler_params=pltpu.CompilerParams(dimension_semantics=("arbitrary",),
                                             vmem_limit_bytes=VMEM_LIMIT),
        name="mix_out",
    )(conv, attn, x, mod9, w, g, b)


def _rope_rows():
    inv_freq = ROPE_THETA ** (-jnp.arange(0, ROT_DIM, 2, dtype=F32) / ROT_DIM)
    half = ROT_DIM // 2
    zeros_h = jnp.zeros((half,), F32)
    zeros_p = jnp.zeros((DIFF_HEAD_DIM - ROT_DIM,), F32)
    ones_h = jnp.ones((half,), F32)
    reps = LANES // DIFF_HEAD_DIM
    freq = jnp.tile(jnp.concatenate([inv_freq, inv_freq, zeros_p]), reps)
    neg_first = jnp.tile(jnp.concatenate([-ones_h, zeros_h, zeros_p]), reps)
    pos_second = jnp.tile(jnp.concatenate([zeros_h, ones_h, zeros_p]), reps)
    return jnp.stack([freq, neg_first, pos_second])


def kernel(x, c, w_ada, b_ada, ffn1_w_in, ffn1_w_out, ln1_g, ln1_b, mix_w_in, conv_w, conv_b, conv_ln_g,
           conv_ln_b, lambda_q1, lambda_k1, lambda_q2, lambda_k2, subln_g, mix_w_out, ln2_g, ln2_b,
           ffn2_w_in, ffn2_w_out, ln3_g, ln3_b):
    batch, s, d = x.shape
    assert batch == 1 and w_ada.shape[0] == DEPTH == 1
    cw = conv_w.shape[2]
    aw = (mix_w_in.shape[2] - 2 * cw) // 3
    hd2 = 2 * DIFF_HEAD_DIM
    n_heads = aw // hd2
    lam_init = 0.8 - 0.6 * math.exp(-0.3 * 0)
    q_scale = math.log2(math.e) / math.sqrt(DIFF_HEAD_DIM)
    t = TILES
    for rows in (t.ffn_rows, t.proj_rows, t.attn_q, t.attn_kv):
        assert s % rows == 0

    mod9 = _ada(c.reshape(d, 1), w_ada[0], b_ada, tn=t.ada_cols).reshape(9, d)
    x0 = x[0]
    x1 = _ffn(x0, mod9, ffn1_w_in[0], ffn1_w_out[0], ln1_g, ln1_b,
              sub=0, weight=0.5, tm=t.ffn_rows, tf=t.ffn_cols)

    u, qt, k, vt, kn = _mix_in(x1, mod9, mix_w_in[0], _rope_rows(),
                               cw=cw, aw=aw, q_scale=q_scale, tm=t.proj_rows, tk=t.attn_kv, n_part=t.proj_parts)
    attn, conv = _mixer(qt, k, vt, kn, lambda_q1, lambda_k1, lambda_q2, lambda_k2, subln_g.reshape(hd2, 1),
                        u, conv_w[0], conv_b, conv_ln_g, conv_ln_b,
                        tq=t.attn_q, tk=t.attn_kv, conv_rows=t.conv_rows, lam_init=lam_init)
    x2 = _mix_out(conv, attn, x1, mod9, mix_w_out[0].astype(BF16), ln2_g, ln2_b,
                  tm=t.proj_rows, n_part=t.proj_parts)

    x3 = _ffn(x2, mod9, ffn2_w_in[0], ffn2_w_out[0], ln3_g, ln3_b,
              sub=2, weight=0.5, tm=t.ffn_rows, tf=t.ffn_cols)
    return x3[None]
```

```python
import functools
import math
from typing import NamedTuple

import jax
import jax.numpy as jnp
from jax import lax
from jax.experimental import pallas as pl
from jax.experimental.pallas import tpu as pltpu

F32 = jnp.float32
BF16 = jnp.bfloat16

DEPTH = 1
ALPHA = (2.0 * DEPTH) ** 0.25
LN_EPS = 1e-5
DIFF_HEAD_DIM = 64
ROT_DIM = DIFF_HEAD_DIM // 4
ROPE_THETA = 500000.0
CONV_KERNEL = 31
CONV_PAD = (CONV_KERNEL - 1) // 2
CONV_HALO = 16
LANES = 128
SUBLANES = 8
SHIFT_SLACK = 1.0 + 2.0 ** -10
L_FLOOR = 2.0 ** -80
CONV_FIRST_BLOCK = 6
VMEM_LIMIT = 56 * 1024 * 1024


class _Tiles(NamedTuple):
    ada_cols: int = 1152
    ffn_rows: int = 512
    ffn_cols: int = 256
    proj_rows: int = 1024
    proj_parts: int = 4
    attn_q: int = 512
    attn_kv: int = 256
    conv_rows: int = 32


TILES = _Tiles()


def _sigmoid(x):
    return 1.0 / (1.0 + jnp.exp(-x))


def _layer_norm(z, g, b):
    mu = jnp.mean(z, axis=-1, keepdims=True)
    zc = z - mu
    var = jnp.mean(zc * zc, axis=-1, keepdims=True)
    return zc * lax.rsqrt(var + LN_EPS) * g + b


def _modulate(x, mod_ref, sub):
    shift = mod_ref[3 * sub:3 * sub + 1, :]
    scale = mod_ref[3 * sub + 1:3 * sub + 2, :]
    return x * (1.0 + scale) + shift


def _const_spec(shape):
    return pl.BlockSpec(shape, lambda *_: (0,) * len(shape), pipeline_mode=pl.Buffered(1))


def _ada_kernel(c_ref, w_ref, b_ref, o_ref):
    c = c_ref[...]
    ca = c * _sigmoid(c)
    o_ref[...] = jnp.sum(ca * w_ref[...], axis=0, keepdims=True) + b_ref[...]


def _ada(c_col, w, b_row, tn):
    d, n = w.shape
    assert n % tn == 0 and tn % LANES == 0
    return pl.pallas_call(
        _ada_kernel,
        grid=(n // tn,),
        in_specs=[pl.BlockSpec((d, 1), lambda j: (0, 0)),
                  pl.BlockSpec((d, tn), lambda j: (0, j)),
                  pl.BlockSpec((1, tn), lambda j: (0, j))],
        out_specs=pl.BlockSpec((1, tn), lambda j: (0, j)),
        out_shape=jax.ShapeDtypeStruct((1, n), F32),
        compiler_params=pltpu.CompilerParams(dimension_semantics=("arbitrary",),
                                             vmem_limit_bytes=VMEM_LIMIT),
        name="ada",
    )(c_col, w, b_row)


def _ffn_kernel(x_ref, mod_ref, win_ref, wout_ref, g_ref, b_ref, o_ref, act_ref, *, sub, weight, tf):
    x = x_ref[...]
    d_ff = wout_ref.shape[0]
    h = _modulate(x, mod_ref, sub).astype(win_ref.dtype)
    for c in range(d_ff // tf):
        gate = jnp.dot(h, win_ref[:, c * tf:(c + 1) * tf], preferred_element_type=F32)
        up = jnp.dot(h, win_ref[:, d_ff + c * tf:d_ff + (c + 1) * tf], preferred_element_type=F32)
        act_ref[:, c * tf:(c + 1) * tf] = (gate * _sigmoid(gate) * up).astype(act_ref.dtype)
    y = jnp.dot(act_ref[...], wout_ref[...], preferred_element_type=F32)
    gate_c = mod_ref[3 * sub + 2:3 * sub + 3, :]
    z = ALPHA * x + weight * (1.0 + gate_c) * y
    o_ref[...] = _layer_norm(z, g_ref[...], b_ref[...])


def _ffn(x, mod9, w_in, w_out, g, b, *, sub, weight, tm, tf):
    s, d = x.shape
    d_ff = w_out.shape[0]
    assert s % tm == 0 and d_ff % tf == 0 and w_in.shape == (d, 2 * d_ff)
    return pl.pallas_call(
        functools.partial(_ffn_kernel, sub=sub, weight=weight, tf=tf),
        grid=(s // tm,),
        in_specs=[pl.BlockSpec((tm, d), lambda i: (i, 0)),
                  _const_spec(mod9.shape),
                  _const_spec(w_in.shape),
                  _const_spec(w_out.shape),
                  _const_spec(g.shape),
                  _const_spec(b.shape)],
        out_specs=pl.BlockSpec((tm, d), lambda i: (i, 0)),
        out_shape=jax.ShapeDtypeStruct((s, d), F32),
        scratch_shapes=[pltpu.VMEM((tm, d_ff), w_out.dtype)],
        compiler_params=pltpu.CompilerParams(dimension_semantics=("arbitrary",),
                                             vmem_limit_bytes=VMEM_LIMIT),
        name=f"ffn{sub}",
    )(x, mod9, w_in, w_out, g, b)


def _mix_in_kernel(x_ref, mod_ref, w_ref, rope_ref, u_ref, qt_ref, k_ref, vt_ref, kn_ref, cos_ref, sin_ref, *,
                   sub, cw, aw, q_scale, n_part):
    tm = x_ref.shape[0]
    tp = tm // n_part
    tk = vt_ref.shape[3]
    freq = rope_ref[0:1, :]

    @pl.when(pl.program_id(0) == 0)
    def _in_tile_angles():
        ang = lax.broadcasted_iota(jnp.int32, (tm, LANES), 0).astype(F32) * freq
        cos_ref[...] = jnp.cos(ang)
        sin_ref[...] = jnp.sin(ang)
        kn_ref[...] = jnp.zeros_like(kn_ref)

    comp0 = lax.broadcasted_iota(jnp.int32, (tp, LANES), 1) < DIFF_HEAD_DIM
    projs = []
    for p in range(n_part):
        h = _modulate(x_ref[p * tp:(p + 1) * tp, :], mod_ref, sub).astype(w_ref.dtype)
        projs.append(jnp.dot(h, w_ref[...], preferred_element_type=F32))

    ang0 = (pl.program_id(0) * tm).astype(F32) * freq
    c0, s0 = jnp.cos(ang0), jnp.sin(ang0)
    half = ROT_DIM // 2
    q0, k0, v0 = 2 * cw, 2 * cw + aw, 2 * cw + 2 * aw
    for p, proj in enumerate(projs):
        rows = slice(p * tp, (p + 1) * tp)
        u_ref[rows, :] = proj[:, :cw] * _sigmoid(proj[:, cw:2 * cw])
        rc = c0 * cos_ref[rows, :] - s0 * sin_ref[rows, :]
        sin = s0 * cos_ref[rows, :] + c0 * sin_ref[rows, :]
        rs1, rs2 = sin * rope_ref[1:2, :], sin * rope_ref[2:3, :]

        def rope(t):
            return t * rc + pltpu.roll(t, LANES - half, 1) * rs1 + pltpu.roll(t, half, 1) * rs2

        for g in range(aw // LANES):
            sl = slice(g * LANES, (g + 1) * LANES)
            qt_ref[g, :, rows] = (rope(proj[:, q0 + g * LANES:q0 + (g + 1) * LANES]) * q_scale).T.astype(BF16)
            kb = rope(proj[:, k0 + g * LANES:k0 + (g + 1) * LANES]).astype(BF16)
            k_ref[rows, sl] = kb
            kf = kb.astype(F32)
            sq = kf * kf
            n0 = jnp.max(jnp.sum(jnp.where(comp0, sq, 0.0), axis=1, keepdims=True), axis=0, keepdims=True)
            n1 = jnp.max(jnp.sum(jnp.where(comp0, 0.0, sq), axis=1, keepdims=True), axis=0, keepdims=True)
            kn_ref[g:g + 1, :] = jnp.maximum(kn_ref[g:g + 1, :], jnp.where(comp0[0:1, :], n0, n1))
            for c in range(tp // tk):
                vt_ref[g, p * (tp // tk) + c] = (
                    proj[c * tk:(c + 1) * tk, v0 + g * LANES:v0 + (g + 1) * LANES].T.astype(BF16))


def _mix_in(x, mod9, w, rope_rows, *, cw, aw, q_scale, tm, tk, n_part):
    s, d = x.shape
    n_heads = aw // LANES
    assert (tm // n_part) % tk == 0 and n_heads <= SUBLANES
    row = lambda i: (i, 0)
    return pl.pallas_call(
        functools.partial(_mix_in_kernel, sub=1, cw=cw, aw=aw, q_scale=q_scale, n_part=n_part),
        grid=(s // tm,),
        in_specs=[pl.BlockSpec((tm, d), row),
                  _const_spec(mod9.shape),
                  _const_spec(w.shape),
                  _const_spec(rope_rows.shape)],
        out_specs=[pl.BlockSpec((tm, cw), row),
                   pl.BlockSpec((n_heads, LANES, tm), lambda i: (0, 0, i)),
                   pl.BlockSpec((tm, aw), row),
                   pl.BlockSpec((n_heads, tm // tk, LANES, tk), lambda i: (0, i, 0, 0)),
                   pl.BlockSpec((SUBLANES, LANES), lambda i: (0, 0))],
        out_shape=[jax.ShapeDtypeStruct((s, cw), F32),
                   jax.ShapeDtypeStruct((n_heads, LANES, s), BF16),
                   jax.ShapeDtypeStruct((s, aw), BF16),
                   jax.ShapeDtypeStruct((n_heads, s // tk, LANES, tk), BF16),
                   jax.ShapeDtypeStruct((SUBLANES, LANES), F32)],
        scratch_shapes=[pltpu.VMEM((tm, LANES), F32),
                        pltpu.VMEM((tm, LANES), F32)],
        compiler_params=pltpu.CompilerParams(dimension_semantics=("arbitrary",),
                                             vmem_limit_bytes=VMEM_LIMIT),
        name="mix_in",
    )(x, mod9, w, rope_rows)


def _dependent_zero(v):
    r, c = v.shape
    folded = jnp.sum(v.reshape(r // SUBLANES, SUBLANES, c), axis=0)
    folded = sum(folded[:, g * LANES:(g + 1) * LANES] for g in range(c // LANES))
    bits = lax.bitcast_convert_type(folded[0:1, :], jnp.uint32)
    return lax.bitcast_convert_type((bits >> 16) >> 16, F32)


def _conv_tile(i, n_tiles, prev_ref, cur_ref, next_ref, w_ref, cb_ref, g_ref, b_ref, o_ref, ext_ref, sh_ref, y_ref,
               rows):
    tm, cw = cur_ref.shape
    ext_ref[0:CONV_HALO, :] = jnp.where(i > 0, prev_ref[...], 0.0)
    ext_ref[CONV_HALO:CONV_HALO + tm, :] = cur_ref[...]
    ext_ref[CONV_HALO + tm:, :] = jnp.where(i < n_tiles - 1, next_ref[...], 0.0)
    span = sh_ref.shape[1]
    for b in range(SUBLANES):
        sh_ref[b] = ext_ref[b:b + span, :]
    base = CONV_HALO - CONV_PAD
    done = []
    for lc in range(cw // LANES):
        ls = slice(lc * LANES, (lc + 1) * LANES)
        for rc in range(tm // rows):
            r0 = rc * rows
            acc = jnp.zeros((rows, LANES), F32)
            for t in range(CONV_KERNEL):
                off = base + t
                a0 = r0 + SUBLANES * (off // SUBLANES)
                acc = acc + sh_ref[off % SUBLANES, a0:a0 + rows, ls] * w_ref[t:t + 1, ls]
            y_ref[r0:r0 + rows, ls] = acc
            done.append(_dependent_zero(acc))
    y = _layer_norm(y_ref[...] + cb_ref[...], g_ref[...], b_ref[...])
    y = y * _sigmoid(y)
    o_ref[...] = y.astype(BF16)
    done.append(_dependent_zero(y))
    return done


def _conv_specs(s, cw, tm, tile_index):
    nh = tm // CONV_HALO
    last = s // CONV_HALO - 1
    return [pl.BlockSpec((CONV_HALO, cw), lambda *g: (jnp.maximum(tile_index(*g) * nh - 1, 0), 0)),
            pl.BlockSpec((tm, cw), lambda *g: (tile_index(*g), 0)),
            pl.BlockSpec((CONV_HALO, cw), lambda *g: (jnp.minimum((tile_index(*g) + 1) * nh, last), 0))]


def _conv_scratch(tm, cw):
    return [pltpu.VMEM((tm + 2 * CONV_HALO, cw), F32),
            pltpu.VMEM((SUBLANES, tm + 2 * CONV_HALO - SUBLANES, cw), F32),
            pltpu.VMEM((tm, cw), F32)]


def _mixer_kernel(qt_ref, k_ref, vt_ref, kn_ref, lq1_ref, lk1_ref, lq2_ref, lk2_ref, g_ref,
                  up_ref, uc_ref, un_ref, cw_ref, cb_ref, cg_ref, cbeta_ref,
                  o_ref, conv_ref,
                  rhs_ref, acc_ref, l_ref, kmax_ref, ext_ref, sh_ref, y_ref, *, tk, conv_rows, lam_init):
    step = pl.program_id(0) * pl.num_programs(1) + pl.program_id(1)
    n_steps = pl.num_programs(0) * pl.num_programs(1)

    hd2, tq = qt_ref.shape[1], qt_ref.shape[2]
    n = 2 * tq
    n_kv = k_ref.shape[0] // tk
    qt = qt_ref[0]
    row = lax.broadcasted_iota(jnp.int32, qt.shape, 0)
    zero = jnp.zeros_like(qt)
    rhs_ref[:, :tq] = jnp.where(row < DIFF_HEAD_DIM, qt, zero)
    rhs_ref[:, tq:] = jnp.where(row >= DIFF_HEAD_DIM, qt, zero)

    def k_block(j):
        return k_ref[pl.ds(pl.multiple_of(j * tk, tk), tk), :]

    @pl.when(pl.program_id(1) == 0)
    def _key_norm_bound():
        kn = kn_ref[pl.ds(pl.program_id(0), 1), :]
        col = lax.broadcasted_iota(jnp.int32, (1, n), 1)
        kmax_ref[...] = jnp.sqrt(jnp.where(col < tq, kn[:, 0:1], kn[:, DIFF_HEAD_DIM:DIFF_HEAD_DIM + 1]))

    conv_done = _conv_tile(step, n_steps, up_ref, uc_ref, un_ref, cw_ref, cb_ref, cg_ref, cbeta_ref, conv_ref,
                           ext_ref, sh_ref, y_ref, conv_rows)
    conv_stride = (n_kv - 1 - CONV_FIRST_BLOCK) // len(conv_done)
    assert conv_stride >= 1

    r32 = rhs_ref[...].astype(F32)
    qn = jnp.sqrt(jnp.sum(r32 * r32, axis=0, keepdims=True))
    m = qn * kmax_ref[...] * SHIFT_SLACK
    l8 = jnp.zeros((8, n), F32)
    acc = jnp.zeros((hd2, n), F32)
    e_prev = None
    for j in range(n_kv):
        s = jnp.dot(k_ref[j * tk:(j + 1) * tk, :], rhs_ref[...], preferred_element_type=F32)
        if e_prev is not None:
            acc = acc + jnp.dot(vt_ref[0, j - 1], e_prev, preferred_element_type=F32)
        if j >= CONV_FIRST_BLOCK and (j - CONV_FIRST_BLOCK) % conv_stride == 0 and conv_done:
            m = m + jnp.concatenate([conv_done.pop(0)] * (n // LANES), axis=1)
        e = jnp.exp2(s - m)
        l8 = l8 + jnp.sum(e.reshape(tk // 8, 8, n), axis=0)
        e_prev = e.astype(BF16)
    acc = acc + jnp.dot(vt_ref[0, n_kv - 1], e_prev, preferred_element_type=F32)
    l = jnp.sum(l8, axis=0, keepdims=True)
    acc_ref[...] = acc
    l_ref[...] = l

    @pl.when(jnp.logical_not(jnp.min(l) >= L_FLOOR))
    def _running_max_fallback():
        acc_ref[...] = jnp.zeros_like(acc_ref)

        def body(j, carry):
            m_run, l_run = carry
            s = jnp.dot(k_block(j), rhs_ref[...], preferred_element_type=F32)
            m_new = jnp.maximum(m_run, jnp.max(s, axis=0, keepdims=True))
            alpha = jnp.exp2(m_run - m_new)
            e = jnp.exp2(s - m_new)
            pv = jnp.dot(vt_ref[0, j], e.astype(BF16), preferred_element_type=F32)
            acc_ref[...] = alpha * acc_ref[...] + pv
            return m_new, alpha * l_run + jnp.sum(e, axis=0, keepdims=True)

        init = (jnp.full((1, n), -jnp.inf, F32), jnp.zeros((1, n), F32))
        _, l_run = lax.fori_loop(0, n_kv, body, init)
        l_ref[...] = l_run

    o = acc_ref[...] * (1.0 / l_ref[...])
    lam = (jnp.exp(jnp.sum(lq1_ref[...] * lk1_ref[...])) - jnp.exp(jnp.sum(lq2_ref[...] * lk2_ref[...]))
           + lam_init)
    o = o[:, :tq] - lam * o[:, tq:]
    ms = jnp.mean(o * o, axis=0, keepdims=True)
    o = o * lax.rsqrt(ms + LN_EPS) * g_ref[...] * (1.0 - lam_init)
    o_ref[...] = o.T.astype(BF16)


def _mixer(qt, k, vt, kn, lq1, lk1, lq2, lk2, g_col, u, conv_w, conv_b, conv_g, conv_beta, *, tq, tk, conv_rows,
           lam_init):
    n_heads, hd2, s = qt.shape
    cw = u.shape[1]
    nq = s // tq
    tc = s // (n_heads * nq)
    assert tc % conv_rows == 0 and tc % CONV_HALO == 0
    lam_spec = _const_spec(lq1.shape)
    tile = lambda h, i: h * nq + i
    return pl.pallas_call(
        functools.partial(_mixer_kernel, tk=tk, conv_rows=conv_rows, lam_init=lam_init),
        grid=(n_heads, nq),
        in_specs=[pl.BlockSpec((1, hd2, tq), lambda h, i: (h, 0, i)),
                  pl.BlockSpec((s, hd2), lambda h, i: (0, h)),
                  pl.BlockSpec((1, s // tk, hd2, tk), lambda h, i: (h, 0, 0, 0)),
                  _const_spec(kn.shape),
                  lam_spec, lam_spec, lam_spec, lam_spec,
                  _const_spec(g_col.shape),
                  *_conv_specs(s, cw, tc, tile),
                  _const_spec(conv_w.shape),
                  _const_spec(conv_b.shape),
                  _const_spec(conv_g.shape),
                  _const_spec(conv_beta.shape)],
        out_specs=[pl.BlockSpec((tq, hd2), lambda h, i: (i, h)),
                   pl.BlockSpec((tc, cw), lambda h, i: (tile(h, i), 0))],
        out_shape=[jax.ShapeDtypeStruct((s, n_heads * hd2), BF16),
                   jax.ShapeDtypeStruct((s, cw), BF16)],
        scratch_shapes=[pltpu.VMEM((hd2, 2 * tq), BF16),
                        pltpu.VMEM((hd2, 2 * tq), F32),
                        pltpu.VMEM((1, 2 * tq), F32),
                        pltpu.VMEM((1, 2 * tq), F32),
                        *_conv_scratch(tc, cw)],
        compiler_params=pltpu.CompilerParams(dimension_semantics=("arbitrary", "arbitrary"),
                                             vmem_limit_bytes=VMEM_LIMIT),
        name="mixer",
    )(qt, k, vt, kn, lq1, lk1, lq2, lk2, g_col, u, u, u, conv_w, conv_b, conv_g, conv_beta)


def _mix_out_kernel(conv_ref, attn_ref, x_ref, mod_ref, w_ref, g_ref, b_ref, o_ref, *, sub, n_part):
    cw = conv_ref.shape[1]
    tp = x_ref.shape[0] // n_part
    gate_c = mod_ref[3 * sub + 2:3 * sub + 3, :]
    ys = []
    for p in range(n_part):
        rows = slice(p * tp, (p + 1) * tp)
        ys.append(jnp.dot(conv_ref[rows, :], w_ref[:cw, :], preferred_element_type=F32)
                  + jnp.dot(attn_ref[rows, :], w_ref[cw:, :], preferred_element_type=F32))
    for p in range(n_part):
        rows = slice(p * tp, (p + 1) * tp)
        z = ALPHA * x_ref[rows, :] + (1.0 + gate_c) * ys[p]
        o_ref[rows, :] = _layer_norm(z, g_ref[...], b_ref[...])


def _mix_out(conv, attn, x, mod9, w, g, b, *, tm, n_part):
    s, d = x.shape
    row = lambda i: (i, 0)
    return pl.pallas_call(
        functools.partial(_mix_out_kernel, sub=1, n_part=n_part),
        grid=(s // tm,),
        in_specs=[pl.BlockSpec((tm, conv.shape[1]), row),
                  pl.BlockSpec((tm, attn.shape[1]), row),
                  pl.BlockSpec((tm, d), row),
                  _const_spec(mod9.shape),
                  _const_spec(w.shape),
                  _const_spec(g.shape),
                  _const_spec(b.shape)],
        out_specs=pl.BlockSpec((tm, d), row),
        out_shape=jax.ShapeDtypeStruct((s, d), F32),
        compiler_params=pltpu.CompilerParams(dimension_semantics=("arbitrary",),
                                             vmem_limit_bytes=VMEM_LIMIT),
        name="mix_out",
    )(conv, attn, x, mod9, w, g, b)


def _rope_rows():
    inv_freq = ROPE_THETA ** (-jnp.arange(0, ROT_DIM, 2, dtype=F32) / ROT_DIM)
    half = ROT_DIM // 2
    zeros_h = jnp.zeros((half,), F32)
    zeros_p = jnp.zeros((DIFF_HEAD_DIM - ROT_DIM,), F32)
    ones_h = jnp.ones((half,), F32)
    reps = LANES // DIFF_HEAD_DIM
    freq = jnp.tile(jnp.concatenate([inv_freq, inv_freq, zeros_p]), reps)
    neg_first = jnp.tile(jnp.concatenate([-ones_h, zeros_h, zeros_p]), reps)
    pos_second = jnp.tile(jnp.concatenate([zeros_h, ones_h, zeros_p]), reps)
    return jnp.stack([freq, neg_first, pos_second])


def kernel(x, c, w_ada, b_ada, ffn1_w_in, ffn1_w_out, ln1_g, ln1_b, mix_w_in, conv_w, conv_b, conv_ln_g,
           conv_ln_b, lambda_q1, lambda_k1, lambda_q2, lambda_k2, subln_g, mix_w_out, ln2_g, ln2_b,
           ffn2_w_in, ffn2_w_out, ln3_g, ln3_b):
    batch, s, d = x.shape
    assert batch == 1 and w_ada.shape[0] == DEPTH == 1
    cw = conv_w.shape[2]
    aw = (mix_w_in.shape[2] - 2 * cw) // 3
    hd2 = 2 * DIFF_HEAD_DIM
    n_heads = aw // hd2
    lam_init = 0.8 - 0.6 * math.exp(-0.3 * 0)
    q_scale = math.log2(math.e) / math.sqrt(DIFF_HEAD_DIM)
    t = TILES
    for rows in (t.ffn_rows, t.proj_rows, t.attn_q, t.attn_kv):
        assert s % rows == 0

    mod9 = _ada(c.reshape(d, 1), w_ada[0], b_ada, tn=t.ada_cols).reshape(9, d)
    x0 = x[0]
    x1 = _ffn(x0, mod9, ffn1_w_in[0], ffn1_w_out[0], ln1_g, ln1_b,
              sub=0, weight=0.5, tm=t.ffn_rows, tf=t.ffn_cols)

    u, qt, k, vt, kn = _mix_in(x1, mod9, mix_w_in[0], _rope_rows(),
                               cw=cw, aw=aw, q_scale=q_scale, tm=t.proj_rows, tk=t.attn_kv, n_part=t.proj_parts)
    attn, conv = _mixer(qt, k, vt, kn, lambda_q1, lambda_k1, lambda_q2, lambda_k2, subln_g.reshape(hd2, 1),
                        u, conv_w[0], conv_b, conv_ln_g, conv_ln_b,
                        tq=t.attn_q, tk=t.attn_kv, conv_rows=t.conv_rows, lam_init=lam_init)
    x2 = _mix_out(conv, attn, x1, mod9, mix_w_out[0].astype(BF16), ln2_g, ln2_b,
                  tm=t.proj_rows, n_part=t.proj_parts)

    x3 = _ffn(x2, mod9, ffn2_w_in[0], ffn2_w_out[0], ln3_g, ln3_b,
              sub=2, weight=0.5, tm=t.ffn_rows, tf=t.ffn_cols)
    return x3[None]
```

```python
import functools
import math
from typing import NamedTuple

import jax
import jax.numpy as jnp
from jax import lax
from jax.experimental import pallas as pl
from jax.experimental.pallas import tpu as pltpu

F32 = jnp.float32
BF16 = jnp.bfloat16

DEPTH = 1
ALPHA = (2.0 * DEPTH) ** 0.25
LN_EPS = 1e-5
DIFF_HEAD_DIM = 64
ROT_DIM = DIFF_HEAD_DIM // 4
ROPE_THETA = 500000.0
CONV_KERNEL = 31
CONV_PAD = (CONV_KERNEL - 1) // 2
CONV_HALO = 16
LANES = 128
SUBLANES = 8
SHIFT_SLACK = 1.0 + 2.0 ** -10
L_FLOOR = 2.0 ** -80
CONV_FIRST_BLOCK = 24
VMEM_LIMIT = 56 * 1024 * 1024


class _Tiles(NamedTuple):
    ada_cols: int = 1152
    ffn_rows: int = 512
    ffn_cols: int = 256
    proj_rows: int = 1024
    proj_parts: int = 4
    attn_q: int = 512
    attn_kv: int = 256
    conv_rows: int = 32


TILES = _Tiles()


def _sigmoid(x):
    return 1.0 / (1.0 + jnp.exp(-x))


def _layer_norm(z, g, b):
    mu = jnp.mean(z, axis=-1, keepdims=True)
    zc = z - mu
    var = jnp.mean(zc * zc, axis=-1, keepdims=True)
    return zc * lax.rsqrt(var + LN_EPS) * g + b


def _modulate(x, mod_ref, sub):
    shift = mod_ref[3 * sub:3 * sub + 1, :]
    scale = mod_ref[3 * sub + 1:3 * sub + 2, :]
    return x * (1.0 + scale) + shift


def _const_spec(shape):
    return pl.BlockSpec(shape, lambda *_: (0,) * len(shape), pipeline_mode=pl.Buffered(1))


def _ada_kernel(c_ref, w_ref, b_ref, o_ref):
    c = c_ref[...]
    ca = c * _sigmoid(c)
    o_ref[...] = jnp.sum(ca * w_ref[...], axis=0, keepdims=True) + b_ref[...]


def _ada(c_col, w, b_row, tn):
    d, n = w.shape
    assert n % tn == 0 and tn % LANES == 0
    return pl.pallas_call(
        _ada_kernel,
        grid=(n // tn,),
        in_specs=[pl.BlockSpec((d, 1), lambda j: (0, 0)),
                  pl.BlockSpec((d, tn), lambda j: (0, j)),
                  pl.BlockSpec((1, tn), lambda j: (0, j))],
        out_specs=pl.BlockSpec((1, tn), lambda j: (0, j)),
        out_shape=jax.ShapeDtypeStruct((1, n), F32),
        compiler_params=pltpu.CompilerParams(dimension_semantics=("arbitrary",),
                                             vmem_limit_bytes=VMEM_LIMIT),
        name="ada",
    )(c_col, w, b_row)


def _ffn_kernel(x_ref, mod_ref, win_ref, wout_ref, g_ref, b_ref, o_ref, act_ref, *, sub, weight, tf):
    x = x_ref[...]
    d_ff = wout_ref.shape[0]
    h = _modulate(x, mod_ref, sub).astype(win_ref.dtype)
    for c in range(d_ff // tf):
        gate = jnp.dot(h, win_ref[:, c * tf:(c + 1) * tf], preferred_element_type=F32)
        up = jnp.dot(h, win_ref[:, d_ff + c * tf:d_ff + (c + 1) * tf], preferred_element_type=F32)
        act_ref[:, c * tf:(c + 1) * tf] = (gate * _sigmoid(gate) * up).astype(act_ref.dtype)
    y = jnp.dot(act_ref[...], wout_ref[...], preferred_element_type=F32)
    gate_c = mod_ref[3 * sub + 2:3 * sub + 3, :]
    z = ALPHA * x + weight * (1.0 + gate_c) * y
    o_ref[...] = _layer_norm(z, g_ref[...], b_ref[...])


def _ffn(x, mod9, w_in, w_out, g, b, *, sub, weight, tm, tf):
    s, d = x.shape
    d_ff = w_out.shape[0]
    assert s % tm == 0 and d_ff % tf == 0 and w_in.shape == (d, 2 * d_ff)
    return pl.pallas_call(
        functools.partial(_ffn_kernel, sub=sub, weight=weight, tf=tf),
        grid=(s // tm,),
        in_specs=[pl.BlockSpec((tm, d), lambda i: (i, 0)),
                  _const_spec(mod9.shape),
                  _const_spec(w_in.shape),
                  _const_spec(w_out.shape),
                  _const_spec(g.shape),
                  _const_spec(b.shape)],
        out_specs=pl.BlockSpec((tm, d), lambda i: (i, 0)),
        out_shape=jax.ShapeDtypeStruct((s, d), F32),
        scratch_shapes=[pltpu.VMEM((tm, d_ff), w_out.dtype)],
        compiler_params=pltpu.CompilerParams(dimension_semantics=("arbitrary",),
                                             vmem_limit_bytes=VMEM_LIMIT),
        name=f"ffn{sub}",
    )(x, mod9, w_in, w_out, g, b)


def _mix_in_kernel(x_ref, mod_ref, w_ref, rope_ref, u_ref, qt_ref, k_ref, vt_ref, kn_ref, cos_ref, sin_ref, *,
                   sub, cw, aw, q_scale, n_part):
    tm = x_ref.shape[0]
    tp = tm // n_part
    tk = vt_ref.shape[3]
    freq = rope_ref[0:1, :]

    @pl.when(pl.program_id(0) == 0)
    def _in_tile_angles():
        ang = lax.broadcasted_iota(jnp.int32, (tm, LANES), 0).astype(F32) * freq
        cos_ref[...] = jnp.cos(ang)
        sin_ref[...] = jnp.sin(ang)
        kn_ref[...] = jnp.zeros_like(kn_ref)

    comp0 = lax.broadcasted_iota(jnp.int32, (tp, LANES), 1) < DIFF_HEAD_DIM
    projs = []
    for p in range(n_part):
        h = _modulate(x_ref[p * tp:(p + 1) * tp, :], mod_ref, sub).astype(w_ref.dtype)
        projs.append(jnp.dot(h, w_ref[...], preferred_element_type=F32))

    ang0 = (pl.program_id(0) * tm).astype(F32) * freq
    c0, s0 = jnp.cos(ang0), jnp.sin(ang0)
    half = ROT_DIM // 2
    q0, k0, v0 = 2 * cw, 2 * cw + aw, 2 * cw + 2 * aw
    for p, proj in enumerate(projs):
        rows = slice(p * tp, (p + 1) * tp)
        u_ref[rows, :] = proj[:, :cw] * _sigmoid(proj[:, cw:2 * cw])
        rc = c0 * cos_ref[rows, :] - s0 * sin_ref[rows, :]
        sin = s0 * cos_ref[rows, :] + c0 * sin_ref[rows, :]
        rs1, rs2 = sin * rope_ref[1:2, :], sin * rope_ref[2:3, :]

        def rope(t):
            return t * rc + pltpu.roll(t, LANES - half, 1) * rs1 + pltpu.roll(t, half, 1) * rs2

        for g in range(aw // LANES):
            sl = slice(g * LANES, (g + 1) * LANES)
            qt_ref[g, :, rows] = (rope(proj[:, q0 + g * LANES:q0 + (g + 1) * LANES]) * q_scale).T.astype(BF16)
            kb = rope(proj[:, k0 + g * LANES:k0 + (g + 1) * LANES]).astype(BF16)
            k_ref[rows, sl] = kb
            kf = kb.astype(F32)
            sq = kf * kf
            n0 = jnp.max(jnp.sum(jnp.where(comp0, sq, 0.0), axis=1, keepdims=True), axis=0, keepdims=True)
            n1 = jnp.max(jnp.sum(jnp.where(comp0, 0.0, sq), axis=1, keepdims=True), axis=0, keepdims=True)
            kn_ref[g:g + 1, :] = jnp.maximum(kn_ref[g:g + 1, :], jnp.where(comp0[0:1, :], n0, n1))
            for c in range(tp // tk):
                vt_ref[g, p * (tp // tk) + c] = (
                    proj[c * tk:(c + 1) * tk, v0 + g * LANES:v0 + (g + 1) * LANES].T.astype(BF16))


def _mix_in(x, mod9, w, rope_rows, *, cw, aw, q_scale, tm, tk, n_part):
    s, d = x.shape
    n_heads = aw // LANES
    assert (tm // n_part) % tk == 0 and n_heads <= SUBLANES
    row = lambda i: (i, 0)
    return pl.pallas_call(
        functools.partial(_mix_in_kernel, sub=1, cw=cw, aw=aw, q_scale=q_scale, n_part=n_part),
        grid=(s // tm,),
        in_specs=[pl.BlockSpec((tm, d), row),
                  _const_spec(mod9.shape),
                  _const_spec(w.shape),
                  _const_spec(rope_rows.shape)],
        out_specs=[pl.BlockSpec((tm, cw), row),
                   pl.BlockSpec((n_heads, LANES, tm), lambda i: (0, 0, i)),
                   pl.BlockSpec((tm, aw), row),
                   pl.BlockSpec((n_heads, tm // tk, LANES, tk), lambda i: (0, i, 0, 0)),
                   pl.BlockSpec((SUBLANES, LANES), lambda i: (0, 0))],
        out_shape=[jax.ShapeDtypeStruct((s, cw), F32),
                   jax.ShapeDtypeStruct((n_heads, LANES, s), BF16),
                   jax.ShapeDtypeStruct((s, aw), BF16),
                   jax.ShapeDtypeStruct((n_heads, s // tk, LANES, tk), BF16),
                   jax.ShapeDtypeStruct((SUBLANES, LANES), F32)],
        scratch_shapes=[pltpu.VMEM((tm, LANES), F32),
                        pltpu.VMEM((tm, LANES), F32)],
        compiler_params=pltpu.CompilerParams(dimension_semantics=("arbitrary",),
                                             vmem_limit_bytes=VMEM_LIMIT),
        name="mix_in",
    )(x, mod9, w, rope_rows)


def _dependent_zero(v):
    r, c = v.shape
    folded = jnp.sum(v.reshape(r // SUBLANES, SUBLANES, c), axis=0)
    folded = sum(folded[:, g * LANES:(g + 1) * LANES] for g in range(c // LANES))
    bits = lax.bitcast_convert_type(folded[0:1, :], jnp.uint32)
    return lax.bitcast_convert_type((bits >> 16) >> 16, F32)


def _conv_tile(i, n_tiles, prev_ref, cur_ref, next_ref, w_ref, cb_ref, g_ref, b_ref, o_ref, ext_ref, sh_ref, y_ref,
               rows):
    tm, cw = cur_ref.shape
    ext_ref[0:CONV_HALO, :] = jnp.where(i > 0, prev_ref[...], 0.0)
    ext_ref[CONV_HALO:CONV_HALO + tm, :] = cur_ref[...]
    ext_ref[CONV_HALO + tm:, :] = jnp.where(i < n_tiles - 1, next_ref[...], 0.0)
    span = sh_ref.shape[1]
    for b in range(SUBLANES):
        sh_ref[b] = ext_ref[b:b + span, :]
    base = CONV_HALO - CONV_PAD
    done = []
    for lc in range(cw // LANES):
        ls = slice(lc * LANES, (lc + 1) * LANES)
        for rc in range(tm // rows):
            r0 = rc * rows
            acc = jnp.zeros((rows, LANES), F32)
            for t in range(CONV_KERNEL):
                off = base + t
                a0 = r0 + SUBLANES * (off // SUBLANES)
                acc = acc + sh_ref[off % SUBLANES, a0:a0 + rows, ls] * w_ref[t:t + 1, ls]
            y_ref[r0:r0 + rows, ls] = acc
            done.append(_dependent_zero(acc))
    y = _layer_norm(y_ref[...] + cb_ref[...], g_ref[...], b_ref[...])
    y = y * _sigmoid(y)
    o_ref[...] = y.astype(BF16)
    done.append(_dependent_zero(y))
    return done


def _conv_specs(s, cw, tm, tile_index):
    nh = tm // CONV_HALO
    last = s // CONV_HALO - 1
    return [pl.BlockSpec((CONV_HALO, cw), lambda *g: (jnp.maximum(tile_index(*g) * nh - 1, 0), 0)),
            pl.BlockSpec((tm, cw), lambda *g: (tile_index(*g), 0)),
            pl.BlockSpec((CONV_HALO, cw), lambda *g: (jnp.minimum((tile_index(*g) + 1) * nh, last), 0))]


def _conv_scratch(tm, cw):
    return [pltpu.VMEM((tm + 2 * CONV_HALO, cw), F32),
            pltpu.VMEM((SUBLANES, tm + 2 * CONV_HALO - SUBLANES, cw), F32),
            pltpu.VMEM((tm, cw), F32)]


def _mixer_kernel(qt_ref, k_ref, vt_ref, kn_ref, lq1_ref, lk1_ref, lq2_ref, lk2_ref, g_ref,
                  up_ref, uc_ref, un_ref, cw_ref, cb_ref, cg_ref, cbeta_ref,
                  o_ref, conv_ref,
                  rhs_ref, acc_ref, l_ref, kmax_ref, ext_ref, sh_ref, y_ref, *, tk, conv_rows, lam_init):
    step = pl.program_id(0) * pl.num_programs(1) + pl.program_id(1)
    n_steps = pl.num_programs(0) * pl.num_programs(1)

    hd2, tq = qt_ref.shape[1], qt_ref.shape[2]
    n = 2 * tq
    n_kv = k_ref.shape[0] // tk
    qt = qt_ref[0]
    row = lax.broadcasted_iota(jnp.int32, qt.shape, 0)
    zero = jnp.zeros_like(qt)
    rhs_ref[:, :tq] = jnp.where(row < DIFF_HEAD_DIM, qt, zero)
    rhs_ref[:, tq:] = jnp.where(row >= DIFF_HEAD_DIM, qt, zero)

    def k_block(j):
        return k_ref[pl.ds(pl.multiple_of(j * tk, tk), tk), :]

    @pl.when(pl.program_id(1) == 0)
    def _key_norm_bound():
        kn = kn_ref[pl.ds(pl.program_id(0), 1), :]
        col = lax.broadcasted_iota(jnp.int32, (1, n), 1)
        kmax_ref[...] = jnp.sqrt(jnp.where(col < tq, kn[:, 0:1], kn[:, DIFF_HEAD_DIM:DIFF_HEAD_DIM + 1]))

    conv_done = _conv_tile(step, n_steps, up_ref, uc_ref, un_ref, cw_ref, cb_ref, cg_ref, cbeta_ref, conv_ref,
                           ext_ref, sh_ref, y_ref, conv_rows)
    conv_stride = (n_kv - 1 - CONV_FIRST_BLOCK) // len(conv_done)
    assert conv_stride >= 1

    r32 = rhs_ref[...].astype(F32)
    qn = jnp.sqrt(jnp.sum(r32 * r32, axis=0, keepdims=True))
    m = qn * kmax_ref[...] * SHIFT_SLACK
    l8 = jnp.zeros((8, n), F32)
    acc = jnp.zeros((hd2, n), F32)
    e_prev = None
    for j in range(n_kv):
        s = jnp.dot(k_ref[j * tk:(j + 1) * tk, :], rhs_ref[...], preferred_element_type=F32)
        if e_prev is not None:
            acc = acc + jnp.dot(vt_ref[0, j - 1], e_prev, preferred_element_type=F32)
        if j >= CONV_FIRST_BLOCK and (j - CONV_FIRST_BLOCK) % conv_stride == 0 and conv_done:
            m = m + jnp.concatenate([conv_done.pop(0)] * (n // LANES), axis=1)
        e = jnp.exp2(s - m)
        l8 = l8 + jnp.sum(e.reshape(tk // 8, 8, n), axis=0)
        e_prev = e.astype(BF16)
    acc = acc + jnp.dot(vt_ref[0, n_kv - 1], e_prev, preferred_element_type=F32)
    l = jnp.sum(l8, axis=0, keepdims=True)
    acc_ref[...] = acc
    l_ref[...] = l

    @pl.when(jnp.logical_not(jnp.min(l) >= L_FLOOR))
    def _running_max_fallback():
        acc_ref[...] = jnp.zeros_like(acc_ref)

        def body(j, carry):
            m_run, l_run = carry
            s = jnp.dot(k_block(j), rhs_ref[...], preferred_element_type=F32)
            m_new = jnp.maximum(m_run, jnp.max(s, axis=0, keepdims=True))
            alpha = jnp.exp2(m_run - m_new)
            e = jnp.exp2(s - m_new)
            pv = jnp.dot(vt_ref[0, j], e.astype(BF16), preferred_element_type=F32)
            acc_ref[...] = alpha * acc_ref[...] + pv
            return m_new, alpha * l_run + jnp.sum(e, axis=0, keepdims=True)

        init = (jnp.full((1, n), -jnp.inf, F32), jnp.zeros((1, n), F32))
        _, l_run = lax.fori_loop(0, n_kv, body, init)
        l_ref[...] = l_run

    o = acc_ref[...] * (1.0 / l_ref[...])
    lam = (jnp.exp(jnp.sum(lq1_ref[...] * lk1_ref[...])) - jnp.exp(jnp.sum(lq2_ref[...] * lk2_ref[...]))
           + lam_init)
    o = o[:, :tq] - lam * o[:, tq:]
    ms = jnp.mean(o * o, axis=0, keepdims=True)
    o = o * lax.rsqrt(ms + LN_EPS) * g_ref[...] * (1.0 - lam_init)
    o_ref[...] = o.T.astype(BF16)


def _mixer(qt, k, vt, kn, lq1, lk1, lq2, lk2, g_col, u, conv_w, conv_b, conv_g, conv_beta, *, tq, tk, conv_rows,
           lam_init):
    n_heads, hd2, s = qt.shape
    cw = u.shape[1]
    nq = s // tq
    tc = s // (n_heads * nq)
    assert tc % conv_rows == 0 and tc % CONV_HALO == 0
    lam_spec = _const_spec(lq1.shape)
    tile = lambda h, i: h * nq + i
    return pl.pallas_call(
        functools.partial(_mixer_kernel, tk=tk, conv_rows=conv_rows, lam_init=lam_init),
        grid=(n_heads, nq),
        in_specs=[pl.BlockSpec((1, hd2, tq), lambda h, i: (h, 0, i)),
                  pl.BlockSpec((s, hd2), lambda h, i: (0, h)),
                  pl.BlockSpec((1, s // tk, hd2, tk), lambda h, i: (h, 0, 0, 0)),
                  _const_spec(kn.shape),
                  lam_spec, lam_spec, lam_spec, lam_spec,
                  _const_spec(g_col.shape),
                  *_conv_specs(s, cw, tc, tile),
                  _const_spec(conv_w.shape),
                  _const_spec(conv_b.shape),
                  _const_spec(conv_g.shape),
                  _const_spec(conv_beta.shape)],
        out_specs=[pl.BlockSpec((tq, hd2), lambda h, i: (i, h)),
                   pl.BlockSpec((tc, cw), lambda h, i: (tile(h, i), 0))],
        out_shape=[jax.ShapeDtypeStruct((s, n_heads * hd2), BF16),
                   jax.ShapeDtypeStruct((s, cw), BF16)],
        scratch_shapes=[pltpu.VMEM((hd2, 2 * tq), BF16),
                        pltpu.VMEM((hd2, 2 * tq), F32),
                        pltpu.VMEM((1, 2 * tq), F32),
                        pltpu.VMEM((1, 2 * tq), F32),
                        *_conv_scratch(tc, cw)],
        compiler_params=pltpu.CompilerParams(dimension_semantics=("arbitrary", "arbitrary"),
                                             vmem_limit_bytes=VMEM_LIMIT),
        name="mixer",
    )(qt, k, vt, kn, lq1, lk1, lq2, lk2, g_col, u, u, u, conv_w, conv_b, conv_g, conv_beta)


def _mix_out_kernel(conv_ref, attn_ref, x_ref, mod_ref, w_ref, g_ref, b_ref, o_ref, *, sub, n_part):
    cw = conv_ref.shape[1]
    tp = x_ref.shape[0] // n_part
    gate_c = mod_ref[3 * sub + 2:3 * sub + 3, :]
    ys = []
    for p in range(n_part):
        rows = slice(p * tp, (p + 1) * tp)
        ys.append(jnp.dot(conv_ref[rows, :], w_ref[:cw, :], preferred_element_type=F32)
                  + jnp.dot(attn_ref[rows, :], w_ref[cw:, :], preferred_element_type=F32))
    for p in range(n_part):
        rows = slice(p * tp, (p + 1) * tp)
        z = ALPHA * x_ref[rows, :] + (1.0 + gate_c) * ys[p]
        o_ref[rows, :] = _layer_norm(z, g_ref[...], b_ref[...])


def _mix_out(conv, attn, x, mod9, w, g, b, *, tm, n_part):
    s, d = x.shape
    row = lambda i: (i, 0)
    return pl.pallas_call(
        functools.partial(_mix_out_kernel, sub=1, n_part=n_part),
        grid=(s // tm,),
        in_specs=[pl.BlockSpec((tm, conv.shape[1]), row),
                  pl.BlockSpec((tm, attn.shape[1]), row),
                  pl.BlockSpec((tm, d), row),
                  _const_spec(mod9.shape),
                  _const_spec(w.shape),
                  _const_spec(g.shape),
                  _const_spec(b.shape)],
        out_specs=pl.BlockSpec((tm, d), row),
        out_shape=jax.ShapeDtypeStruct((s, d), F32),
        compiler_params=pltpu.CompilerParams(dimension_semantics=("arbitrary",),
                                             vmem_limit_bytes=VMEM_LIMIT),
        name="mix_out",
    )(conv, attn, x, mod9, w, g, b)


def _rope_rows():
    inv_freq = ROPE_THETA ** (-jnp.arange(0, ROT_DIM, 2, dtype=F32) / ROT_DIM)
    half = ROT_DIM // 2
    zeros_h = jnp.zeros((half,), F32)
    zeros_p = jnp.zeros((DIFF_HEAD_DIM - ROT_DIM,), F32)
    ones_h = jnp.ones((half,), F32)
    reps = LANES // DIFF_HEAD_DIM
    freq = jnp.tile(jnp.concatenate([inv_freq, inv_freq, zeros_p]), reps)
    neg_first = jnp.tile(jnp.concatenate([-ones_h, zeros_h, zeros_p]), reps)
    pos_second = jnp.tile(jnp.concatenate([zeros_h, ones_h, zeros_p]), reps)
    return jnp.stack([freq, neg_first, pos_second])


def kernel(x, c, w_ada, b_ada, ffn1_w_in, ffn1_w_out, ln1_g, ln1_b, mix_w_in, conv_w, conv_b, conv_ln_g,
           conv_ln_b, lambda_q1, lambda_k1, lambda_q2, lambda_k2, subln_g, mix_w_out, ln2_g, ln2_b,
           ffn2_w_in, ffn2_w_out, ln3_g, ln3_b):
    batch, s, d = x.shape
    assert batch == 1 and w_ada.shape[0] == DEPTH == 1
    cw = conv_w.shape[2]
    aw = (mix_w_in.shape[2] - 2 * cw) // 3
    hd2 = 2 * DIFF_HEAD_DIM
    n_heads = aw // hd2
    lam_init = 0.8 - 0.6 * math.exp(-0.3 * 0)
    q_scale = math.log2(math.e) / math.sqrt(DIFF_HEAD_DIM)
    t = TILES
    for rows in (t.ffn_rows, t.proj_rows, t.attn_q, t.attn_kv):
        assert s % rows == 0

    mod9 = _ada(c.reshape(d, 1), w_ada[0], b_ada, tn=t.ada_cols).reshape(9, d)
    x0 = x[0]
    x1 = _ffn(x0, mod9, ffn1_w_in[0], ffn1_w_out[0], ln1_g, ln1_b,
              sub=0, weight=0.5, tm=t.ffn_rows, tf=t.ffn_cols)

    u, qt, k, vt, kn = _mix_in(x1, mod9, mix_w_in[0], _rope_rows(),
                               cw=cw, aw=aw, q_scale=q_scale, tm=t.proj_rows, tk=t.attn_kv, n_part=t.proj_parts)
    attn, conv = _mixer(qt, k, vt, kn, lambda_q1, lambda_k1, lambda_q2, lambda_k2, subln_g.reshape(hd2, 1),
                        u, conv_w[0], conv_b, conv_ln_g, conv_ln_b,
                        tq=t.attn_q, tk=t.attn_kv, conv_rows=t.conv_rows, lam_init=lam_init)
    x2 = _mix_out(conv, attn, x1, mod9, mix_w_out[0].astype(BF16), ln2_g, ln2_b,
                  tm=t.proj_rows, n_part=t.proj_parts)

    x3 = _ffn(x2, mod9, ffn2_w_in[0], ffn2_w_out[0], ln3_g, ln3_b,
              sub=2, weight=0.5, tm=t.ffn_rows, tf=t.ffn_cols)
    return x3[None]
```

```python
import functools
import math
from typing import NamedTuple

import jax
import jax.numpy as jnp
from jax import lax
from jax.experimental import pallas as pl
from jax.experimental.pallas import tpu as pltpu

F32 = jnp.float32
BF16 = jnp.bfloat16

DEPTH = 1
ALPHA = (2.0 * DEPTH) ** 0.25
LN_EPS = 1e-5
DIFF_HEAD_DIM = 64
ROT_DIM = DIFF_HEAD_DIM // 4
ROPE_THETA = 500000.0
CONV_KERNEL = 31
CONV_PAD = (CONV_KERNEL - 1) // 2
CONV_HALO = 16
LANES = 128
SUBLANES = 8
SHIFT_SLACK = 1.0 + 2.0 ** -10
L_FLOOR = 2.0 ** -80
CONV_FIRST_BLOCK = 40
VMEM_LIMIT = 56 * 1024 * 1024


class _Tiles(NamedTuple):
    ada_cols: int = 1152
    ffn_rows: int = 512
    ffn_cols: int = 256
    proj_rows: int = 1024
    proj_parts: int = 4
    attn_q: int = 512
    attn_kv: int = 256
    conv_rows: int = 32


TILES = _Tiles()


def _sigmoid(x):
    return 1.0 / (1.0 + jnp.exp(-x))


def _layer_norm(z, g, b):
    mu = jnp.mean(z, axis=-1, keepdims=True)
    zc = z - mu
    var = jnp.mean(zc * zc, axis=-1, keepdims=True)
    return zc * lax.rsqrt(var + LN_EPS) * g + b


def _modulate(x, mod_ref, sub):
    shift = mod_ref[3 * sub:3 * sub + 1, :]
    scale = mod_ref[3 * sub + 1:3 * sub + 2, :]
    return x * (1.0 + scale) + shift


def _const_spec(shape):
    return pl.BlockSpec(shape, lambda *_: (0,) * len(shape), pipeline_mode=pl.Buffered(1))


def _ada_kernel(c_ref, w_ref, b_ref, o_ref):
    c = c_ref[...]
    ca = c * _sigmoid(c)
    o_ref[...] = jnp.sum(ca * w_ref[...], axis=0, keepdims=True) + b_ref[...]


def _ada(c_col, w, b_row, tn):
    d, n = w.shape
    assert n % tn == 0 and tn % LANES == 0
    return pl.pallas_call(
        _ada_kernel,
        grid=(n // tn,),
        in_specs=[pl.BlockSpec((d, 1), lambda j: (0, 0)),
                  pl.BlockSpec((d, tn), lambda j: (0, j)),
                  pl.BlockSpec((1, tn), lambda j: (0, j))],
        out_specs=pl.BlockSpec((1, tn), lambda j: (0, j)),
        out_shape=jax.ShapeDtypeStruct((1, n), F32),
        compiler_params=pltpu.CompilerParams(dimension_semantics=("arbitrary",),
                                             vmem_limit_bytes=VMEM_LIMIT),
        name="ada",
    )(c_col, w, b_row)


def _ffn_kernel(x_ref, mod_ref, win_ref, wout_ref, g_ref, b_ref, o_ref, act_ref, *, sub, weight, tf):
    x = x_ref[...]
    d_ff = wout_ref.shape[0]
    h = _modulate(x, mod_ref, sub).astype(win_ref.dtype)
    for c in range(d_ff // tf):
        gate = jnp.dot(h, win_ref[:, c * tf:(c + 1) * tf], preferred_element_type=F32)
        up = jnp.dot(h, win_ref[:, d_ff + c * tf:d_ff + (c + 1) * tf], preferred_element_type=F32)
        act_ref[:, c * tf:(c + 1) * tf] = (gate * _sigmoid(gate) * up).astype(act_ref.dtype)
    y = jnp.dot(act_ref[...], wout_ref[...], preferred_element_type=F32)
    gate_c = mod_ref[3 * sub + 2:3 * sub + 3, :]
    z = ALPHA * x + weight * (1.0 + gate_c) * y
    o_ref[...] = _layer_norm(z, g_ref[...], b_ref[...])


def _ffn(x, mod9, w_in, w_out, g, b, *, sub, weight, tm, tf):
    s, d = x.shape
    d_ff = w_out.shape[0]
    assert s % tm == 0 and d_ff % tf == 0 and w_in.shape == (d, 2 * d_ff)
    return pl.pallas_call(
        functools.partial(_ffn_kernel, sub=sub, weight=weight, tf=tf),
        grid=(s // tm,),
        in_specs=[pl.BlockSpec((tm, d), lambda i: (i, 0)),
                  _const_spec(mod9.shape),
                  _const_spec(w_in.shape),
                  _const_spec(w_out.shape),
                  _const_spec(g.shape),
                  _const_spec(b.shape)],
        out_specs=pl.BlockSpec((tm, d), lambda i: (i, 0)),
        out_shape=jax.ShapeDtypeStruct((s, d), F32),
        scratch_shapes=[pltpu.VMEM((tm, d_ff), w_out.dtype)],
        compiler_params=pltpu.CompilerParams(dimension_semantics=("arbitrary",),
                                             vmem_limit_bytes=VMEM_LIMIT),
        name=f"ffn{sub}",
    )(x, mod9, w_in, w_out, g, b)


def _mix_in_kernel(x_ref, mod_ref, w_ref, rope_ref, u_ref, qt_ref, k_ref, vt_ref, kn_ref, cos_ref, sin_ref, *,
                   sub, cw, aw, q_scale, n_part):
    tm = x_ref.shape[0]
    tp = tm // n_part
    tk = vt_ref.shape[3]
    freq = rope_ref[0:1, :]

    @pl.when(pl.program_id(0) == 0)
    def _in_tile_angles():
        ang = lax.broadcasted_iota(jnp.int32, (tm, LANES), 0).astype(F32) * freq
        cos_ref[...] = jnp.cos(ang)
        sin_ref[...] = jnp.sin(ang)
        kn_ref[...] = jnp.zeros_like(kn_ref)

    comp0 = lax.broadcasted_iota(jnp.int32, (tp, LANES), 1) < DIFF_HEAD_DIM
    projs = []
    for p in range(n_part):
        h = _modulate(x_ref[p * tp:(p + 1) * tp, :], mod_ref, sub).astype(w_ref.dtype)
        projs.append(jnp.dot(h, w_ref[...], preferred_element_type=F32))

    ang0 = (pl.program_id(0) * tm).astype(F32) * freq
    c0, s0 = jnp.cos(ang0), jnp.sin(ang0)
    half = ROT_DIM // 2
    q0, k0, v0 = 2 * cw, 2 * cw + aw, 2 * cw + 2 * aw
    for p, proj in enumerate(projs):
        rows = slice(p * tp, (p + 1) * tp)
        u_ref[rows, :] = proj[:, :cw] * _sigmoid(proj[:, cw:2 * cw])
        rc = c0 * cos_ref[rows, :] - s0 * sin_ref[rows, :]
        sin = s0 * cos_ref[rows, :] + c0 * sin_ref[rows, :]
        rs1, rs2 = sin * rope_ref[1:2, :], sin * rope_ref[2:3, :]

        def rope(t):
            return t * rc + pltpu.roll(t, LANES - half, 1) * rs1 + pltpu.roll(t, half, 1) * rs2

        for g in range(aw // LANES):
            sl = slice(g * LANES, (g + 1) * LANES)
            qt_ref[g, :, rows] = (rope(proj[:, q0 + g * LANES:q0 + (g + 1) * LANES]) * q_scale).T.astype(BF16)
            kb = rope(proj[:, k0 + g * LANES:k0 + (g + 1) * LANES]).astype(BF16)
            k_ref[rows, sl] = kb
            kf = kb.astype(F32)
            sq = kf * kf
            n0 = jnp.max(jnp.sum(jnp.where(comp0, sq, 0.0), axis=1, keepdims=True), axis=0, keepdims=True)
            n1 = jnp.max(jnp.sum(jnp.where(comp0, 0.0, sq), axis=1, keepdims=True), axis=0, keepdims=True)
            kn_ref[g:g + 1, :] = jnp.maximum(kn_ref[g:g + 1, :], jnp.where(comp0[0:1, :], n0, n1))
            for c in range(tp // tk):
                vt_ref[g, p * (tp // tk) + c] = (
                    proj[c * tk:(c + 1) * tk, v0 + g * LANES:v0 + (g + 1) * LANES].T.astype(BF16))


def _mix_in(x, mod9, w, rope_rows, *, cw, aw, q_scale, tm, tk, n_part):
    s, d = x.shape
    n_heads = aw // LANES
    assert (tm // n_part) % tk == 0 and n_heads <= SUBLANES
    row = lambda i: (i, 0)
    return pl.pallas_call(
        functools.partial(_mix_in_kernel, sub=1, cw=cw, aw=aw, q_scale=q_scale, n_part=n_part),
        grid=(s // tm,),
        in_specs=[pl.BlockSpec((tm, d), row),
                  _const_spec(mod9.shape),
                  _const_spec(w.shape),
                  _const_spec(rope_rows.shape)],
        out_specs=[pl.BlockSpec((tm, cw), row),
                   pl.BlockSpec((n_heads, LANES, tm), lambda i: (0, 0, i)),
                   pl.BlockSpec((tm, aw), row),
                   pl.BlockSpec((n_heads, tm // tk, LANES, tk), lambda i: (0, i, 0, 0)),
                   pl.BlockSpec((SUBLANES, LANES), lambda i: (0, 0))],
        out_shape=[jax.ShapeDtypeStruct((s, cw), F32),
                   jax.ShapeDtypeStruct((n_heads, LANES, s), BF16),
                   jax.ShapeDtypeStruct((s, aw), BF16),
                   jax.ShapeDtypeStruct((n_heads, s // tk, LANES, tk), BF16),
                   jax.ShapeDtypeStruct((SUBLANES, LANES), F32)],
        scratch_shapes=[pltpu.VMEM((tm, LANES), F32),
                        pltpu.VMEM((tm, LANES), F32)],
        compiler_params=pltpu.CompilerParams(dimension_semantics=("arbitrary",),
                                             vmem_limit_bytes=VMEM_LIMIT),
        name="mix_in",
    )(x, mod9, w, rope_rows)


def _dependent_zero(v):
    r, c = v.shape
    folded = jnp.sum(v.reshape(r // SUBLANES, SUBLANES, c), axis=0)
    folded = sum(folded[:, g * LANES:(g + 1) * LANES] for g in range(c // LANES))
    bits = lax.bitcast_convert_type(folded[0:1, :], jnp.uint32)
    return lax.bitcast_convert_type((bits >> 16) >> 16, F32)


def _conv_tile(i, n_tiles, prev_ref, cur_ref, next_ref, w_ref, cb_ref, g_ref, b_ref, o_ref, ext_ref, sh_ref, y_ref,
               rows):
    tm, cw = cur_ref.shape
    ext_ref[0:CONV_HALO, :] = jnp.where(i > 0, prev_ref[...], 0.0)
    ext_ref[CONV_HALO:CONV_HALO + tm, :] = cur_ref[...]
    ext_ref[CONV_HALO + tm:, :] = jnp.where(i < n_tiles - 1, next_ref[...], 0.0)
    span = sh_ref.shape[1]
    for b in range(SUBLANES):
        sh_ref[b] = ext_ref[b:b + span, :]
    base = CONV_HALO - CONV_PAD
    done = []
    for lc in range(cw // LANES):
        ls = slice(lc * LANES, (lc + 1) * LANES)
        for rc in range(tm // rows):
            r0 = rc * rows
            acc = jnp.zeros((rows, LANES), F32)
            for t in range(CONV_KERNEL):
                off = base + t
                a0 = r0 + SUBLANES * (off // SUBLANES)
                acc = acc + sh_ref[off % SUBLANES, a0:a0 + rows, ls] * w_ref[t:t + 1, ls]
            y_ref[r0:r0 + rows, ls] = acc
            done.append(_dependent_zero(acc))
    y = _layer_norm(y_ref[...] + cb_ref[...], g_ref[...], b_ref[...])
    y = y * _sigmoid(y)
    o_ref[...] = y.astype(BF16)
    done.append(_dependent_zero(y))
    return done


def _conv_specs(s, cw, tm, tile_index):
    nh = tm // CONV_HALO
    last = s // CONV_HALO - 1
    return [pl.BlockSpec((CONV_HALO, cw), lambda *g: (jnp.maximum(tile_index(*g) * nh - 1, 0), 0)),
            pl.BlockSpec((tm, cw), lambda *g: (tile_index(*g), 0)),
            pl.BlockSpec((CONV_HALO, cw), lambda *g: (jnp.minimum((tile_index(*g) + 1) * nh, last), 0))]


def _conv_scratch(tm, cw):
    return [pltpu.VMEM((tm + 2 * CONV_HALO, cw), F32),
            pltpu.VMEM((SUBLANES, tm + 2 * CONV_HALO - SUBLANES, cw), F32),
            pltpu.VMEM((tm, cw), F32)]


def _mixer_kernel(qt_ref, k_ref, vt_ref, kn_ref, lq1_ref, lk1_ref, lq2_ref, lk2_ref, g_ref,
                  up_ref, uc_ref, un_ref, cw_ref, cb_ref, cg_ref, cbeta_ref,
                  o_ref, conv_ref,
                  rhs_ref, acc_ref, l_ref, kmax_ref, ext_ref, sh_ref, y_ref, *, tk, conv_rows, lam_init):
    step = pl.program_id(0) * pl.num_programs(1) + pl.program_id(1)
    n_steps = pl.num_programs(0) * pl.num_programs(1)

    hd2, tq = qt_ref.shape[1], qt_ref.shape[2]
    n = 2 * tq
    n_kv = k_ref.shape[0] // tk
    qt = qt_ref[0]
    row = lax.broadcasted_iota(jnp.int32, qt.shape, 0)
    zero = jnp.zeros_like(qt)
    rhs_ref[:, :tq] = jnp.where(row < DIFF_HEAD_DIM, qt, zero)
    rhs_ref[:, tq:] = jnp.where(row >= DIFF_HEAD_DIM, qt, zero)

    def k_block(j):
        return k_ref[pl.ds(pl.multiple_of(j * tk, tk), tk), :]

    @pl.when(pl.program_id(1) == 0)
    def _key_norm_bound():
        kn = kn_ref[pl.ds(pl.program_id(0), 1), :]
        col = lax.broadcasted_iota(jnp.int32, (1, n), 1)
        kmax_ref[...] = jnp.sqrt(jnp.where(col < tq, kn[:, 0:1], kn[:, DIFF_HEAD_DIM:DIFF_HEAD_DIM + 1]))

    conv_done = _conv_tile(step, n_steps, up_ref, uc_ref, un_ref, cw_ref, cb_ref, cg_ref, cbeta_ref, conv_ref,
                           ext_ref, sh_ref, y_ref, conv_rows)
    conv_stride = (n_kv - 1 - CONV_FIRST_BLOCK) // len(conv_done)
    assert conv_stride >= 1

    r32 = rhs_ref[...].astype(F32)
    qn = jnp.sqrt(jnp.sum(r32 * r32, axis=0, keepdims=True))
    m = qn * kmax_ref[...] * SHIFT_SLACK
    l8 = jnp.zeros((8, n), F32)
    acc = jnp.zeros((hd2, n), F32)
    e_prev = None
    for j in range(n_kv):
        s = jnp.dot(k_ref[j * tk:(j + 1) * tk, :], rhs_ref[...], preferred_element_type=F32)
        if e_prev is not None:
            acc = acc + jnp.dot(vt_ref[0, j - 1], e_prev, preferred_element_type=F32)
        if j >= CONV_FIRST_BLOCK and (j - CONV_FIRST_BLOCK) % conv_stride == 0 and conv_done:
            m = m + jnp.concatenate([conv_done.pop(0)] * (n // LANES), axis=1)
        e = jnp.exp2(s - m)
        l8 = l8 + jnp.sum(e.reshape(tk // 8, 8, n), axis=0)
        e_prev = e.astype(BF16)
    acc = acc + jnp.dot(vt_ref[0, n_kv - 1], e_prev, preferred_element_type=F32)
    l = jnp.sum(l8, axis=0, keepdims=True)
    acc_ref[...] = acc
    l_ref[...] = l

    @pl.when(jnp.logical_not(jnp.min(l) >= L_FLOOR))
    def _running_max_fallback():
        acc_ref[...] = jnp.zeros_like(acc_ref)

        def body(j, carry):
            m_run, l_run = carry
            s = jnp.dot(k_block(j), rhs_ref[...], preferred_element_type=F32)
            m_new = jnp.maximum(m_run, jnp.max(s, axis=0, keepdims=True))
            alpha = jnp.exp2(m_run - m_new)
            e = jnp.exp2(s - m_new)
            pv = jnp.dot(vt_ref[0, j], e.astype(BF16), preferred_element_type=F32)
            acc_ref[...] = alpha * acc_ref[...] + pv
            return m_new, alpha * l_run + jnp.sum(e, axis=0, keepdims=True)

        init = (jnp.full((1, n), -jnp.inf, F32), jnp.zeros((1, n), F32))
        _, l_run = lax.fori_loop(0, n_kv, body, init)
        l_ref[...] = l_run

    o = acc_ref[...] * (1.0 / l_ref[...])
    lam = (jnp.exp(jnp.sum(lq1_ref[...] * lk1_ref[...])) - jnp.exp(jnp.sum(lq2_ref[...] * lk2_ref[...]))
           + lam_init)
    o = o[:, :tq] - lam * o[:, tq:]
    ms = jnp.mean(o * o, axis=0, keepdims=True)
    o = o * lax.rsqrt(ms + LN_EPS) * g_ref[...] * (1.0 - lam_init)
    o_ref[...] = o.T.astype(BF16)


def _mixer(qt, k, vt, kn, lq1, lk1, lq2, lk2, g_col, u, conv_w, conv_b, conv_g, conv_beta, *, tq, tk, conv_rows,
           lam_init):
    n_heads, hd2, s = qt.shape
    cw = u.shape[1]
    nq = s // tq
    tc = s // (n_heads * nq)
    assert tc % conv_rows == 0 and tc % CONV_HALO == 0
    lam_spec = _const_spec(lq1.shape)
    tile = lambda h, i: h * nq + i
    return pl.pallas_call(
        functools.partial(_mixer_kernel, tk=tk, conv_rows=conv_rows, lam_init=lam_init),
        grid=(n_heads, nq),
        in_specs=[pl.BlockSpec((1, hd2, tq), lambda h, i: (h, 0, i)),
                  pl.BlockSpec((s, hd2), lambda h, i: (0, h)),
                  pl.BlockSpec((1, s // tk, hd2, tk), lambda h, i: (h, 0, 0, 0)),
                  _const_spec(kn.shape),
                  lam_spec, lam_spec, lam_spec, lam_spec,
                  _const_spec(g_col.shape),
                  *_conv_specs(s, cw, tc, tile),
                  _const_spec(conv_w.shape),
                  _const_spec(conv_b.shape),
                  _const_spec(conv_g.shape),
                  _const_spec(conv_beta.shape)],
        out_specs=[pl.BlockSpec((tq, hd2), lambda h, i: (i, h)),
                   pl.BlockSpec((tc, cw), lambda h, i: (tile(h, i), 0))],
        out_shape=[jax.ShapeDtypeStruct((s, n_heads * hd2), BF16),
                   jax.ShapeDtypeStruct((s, cw), BF16)],
        scratch_shapes=[pltpu.VMEM((hd2, 2 * tq), BF16),
                        pltpu.VMEM((hd2, 2 * tq), F32),
                        pltpu.VMEM((1, 2 * tq), F32),
                        pltpu.VMEM((1, 2 * tq), F32),
                        *_conv_scratch(tc, cw)],
        compiler_params=pltpu.CompilerParams(dimension_semantics=("arbitrary", "arbitrary"),
                                             vmem_limit_bytes=VMEM_LIMIT),
        name="mixer",
    )(qt, k, vt, kn, lq1, lk1, lq2, lk2, g_col, u, u, u, conv_w, conv_b, conv_g, conv_beta)


def _mix_out_kernel(conv_ref, attn_ref, x_ref, mod_ref, w_ref, g_ref, b_ref, o_ref, *, sub, n_part):
    cw = conv_ref.shape[1]
    tp = x_ref.shape[0] // n_part
    gate_c = mod_ref[3 * sub + 2:3 * sub + 3, :]
    ys = []
    for p in range(n_part):
        rows = slice(p * tp, (p + 1) * tp)
        ys.append(jnp.dot(conv_ref[rows, :], w_ref[:cw, :], preferred_element_type=F32)
                  + jnp.dot(attn_ref[rows, :], w_ref[cw:, :], preferred_element_type=F32))
    for p in range(n_part):
        rows = slice(p * tp, (p + 1) * tp)
        z = ALPHA * x_ref[rows, :] + (1.0 + gate_c) * ys[p]
        o_ref[rows, :] = _layer_norm(z, g_ref[...], b_ref[...])


def _mix_out(conv, attn, x, mod9, w, g, b, *, tm, n_part):
    s, d = x.shape
    row = lambda i: (i, 0)
    return pl.pallas_call(
        functools.partial(_mix_out_kernel, sub=1, n_part=n_part),
        grid=(s // tm,),
        in_specs=[pl.BlockSpec((tm, conv.shape[1]), row),
                  pl.BlockSpec((tm, attn.shape[1]), row),
                  pl.BlockSpec((tm, d), row),
                  _const_spec(mod9.shape),
                  _const_spec(w.shape),
                  _const_spec(g.shape),
                  _const_spec(b.shape)],
        out_specs=pl.BlockSpec((tm, d), row),
        out_shape=jax.ShapeDtypeStruct((s, d), F32),
        compiler_params=pltpu.CompilerParams(dimension_semantics=("arbitrary",),
                                             vmem_limit_bytes=VMEM_LIMIT),
        name="mix_out",
    )(conv, attn, x, mod9, w, g, b)


def _rope_rows():
    inv_freq = ROPE_THETA ** (-jnp.arange(0, ROT_DIM, 2, dtype=F32) / ROT_DIM)
    half = ROT_DIM // 2
    zeros_h = jnp.zeros((half,), F32)
    zeros_p = jnp.zeros((DIFF_HEAD_DIM - ROT_DIM,), F32)
    ones_h = jnp.ones((half,), F32)
    reps = LANES // DIFF_HEAD_DIM
    freq = jnp.tile(jnp.concatenate([inv_freq, inv_freq, zeros_p]), reps)
    neg_first = jnp.tile(jnp.concatenate([-ones_h, zeros_h, zeros_p]), reps)
    pos_second = jnp.tile(jnp.concatenate([zeros_h, ones_h, zeros_p]), reps)
    return jnp.stack([freq, neg_first, pos_second])


def kernel(x, c, w_ada, b_ada, ffn1_w_in, ffn1_w_out, ln1_g, ln1_b, mix_w_in, conv_w, conv_b, conv_ln_g,
           conv_ln_b, lambda_q1, lambda_k1, lambda_q2, lambda_k2, subln_g, mix_w_out, ln2_g, ln2_b,
           ffn2_w_in, ffn2_w_out, ln3_g, ln3_b):
    batch, s, d = x.shape
    assert batch == 1 and w_ada.shape[0] == DEPTH == 1
    cw = conv_w.shape[2]
    aw = (mix_w_in.shape[2] - 2 * cw) // 3
    hd2 = 2 * DIFF_HEAD_DIM
    n_heads = aw // hd2
    lam_init = 0.8 - 0.6 * math.exp(-0.3 * 0)
    q_scale = math.log2(math.e) / math.sqrt(DIFF_HEAD_DIM)
    t = TILES
    for rows in (t.ffn_rows, t.proj_rows, t.attn_q, t.attn_kv):
        assert s % rows == 0

    mod9 = _ada(c.reshape(d, 1), w_ada[0], b_ada, tn=t.ada_cols).reshape(9, d)
    x0 = x[0]
    x1 = _ffn(x0, mod9, ffn1_w_in[0], ffn1_w_out[0], ln1_g, ln1_b,
              sub=0, weight=0.5, tm=t.ffn_rows, tf=t.ffn_cols)

    u, qt, k, vt, kn = _mix_in(x1, mod9, mix_w_in[0], _rope_rows(),
                               cw=cw, aw=aw, q_scale=q_scale, tm=t.proj_rows, tk=t.attn_kv, n_part=t.proj_parts)
    attn, conv = _mixer(qt, k, vt, kn, lambda_q1, lambda_k1, lambda_q2, lambda_k2, subln_g.reshape(hd2, 1),
                        u, conv_w[0], conv_b, conv_ln_g, conv_ln_b,
                        tq=t.attn_q, tk=t.attn_kv, conv_rows=t.conv_rows, lam_init=lam_init)
    x2 = _mix_out(conv, attn, x1, mod9, mix_w_out[0].astype(BF16), ln2_g, ln2_b,
                  tm=t.proj_rows, n_part=t.proj_parts)

    x3 = _ffn(x2, mod9, ffn2_w_in[0], ffn2_w_out[0], ln3_g, ln3_b,
              sub=2, weight=0.5, tm=t.ffn_rows, tf=t.ffn_cols)
    return x3[None]
```

```python
import functools
import math
from typing import NamedTuple

import jax
import jax.numpy as jnp
from jax import lax
from jax.experimental import pallas as pl
from jax.experimental.pallas import tpu as pltpu

F32 = jnp.float32
BF16 = jnp.bfloat16

DEPTH = 1
ALPHA = (2.0 * DEPTH) ** 0.25
LN_EPS = 1e-5
DIFF_HEAD_DIM = 64
ROT_DIM = DIFF_HEAD_DIM // 4
ROPE_THETA = 500000.0
CONV_KERNEL = 31
CONV_PAD = (CONV_KERNEL - 1) // 2
CONV_HALO = 16
LANES = 128
SUBLANES = 8
SHIFT_SLACK = 1.0 + 2.0 ** -10
L_FLOOR = 2.0 ** -80
CONV_FIRST_BLOCK = 46
VMEM_LIMIT = 56 * 1024 * 1024


class _Tiles(NamedTuple):
    ada_cols: int = 1152
    ffn_rows: int = 512
    ffn_cols: int = 256
    proj_rows: int = 1024
    proj_parts: int = 4
    attn_q: int = 512
    attn_kv: int = 256
    conv_rows: int = 32


TILES = _Tiles()


def _sigmoid(x):
    return 1.0 / (1.0 + jnp.exp(-x))


def _layer_norm(z, g, b):
    mu = jnp.mean(z, axis=-1, keepdims=True)
    zc = z - mu
    var = jnp.mean(zc * zc, axis=-1, keepdims=True)
    return zc * lax.rsqrt(var + LN_EPS) * g + b


def _modulate(x, mod_ref, sub):
    shift = mod_ref[3 * sub:3 * sub + 1, :]
    scale = mod_ref[3 * sub + 1:3 * sub + 2, :]
    return x * (1.0 + scale) + shift


def _const_spec(shape):
    return pl.BlockSpec(shape, lambda *_: (0,) * len(shape), pipeline_mode=pl.Buffered(1))


def _ada_kernel(c_ref, w_ref, b_ref, o_ref):
    c = c_ref[...]
    ca = c * _sigmoid(c)
    o_ref[...] = jnp.sum(ca * w_ref[...], axis=0, keepdims=True) + b_ref[...]


def _ada(c_col, w, b_row, tn):
    d, n = w.shape
    assert n % tn == 0 and tn % LANES == 0
    return pl.pallas_call(
        _ada_kernel,
        grid=(n // tn,),
        in_specs=[pl.BlockSpec((d, 1), lambda j: (0, 0)),
                  pl.BlockSpec((d, tn), lambda j: (0, j)),
                  pl.BlockSpec((1, tn), lambda j: (0, j))],
        out_specs=pl.BlockSpec((1, tn), lambda j: (0, j)),
        out_shape=jax.ShapeDtypeStruct((1, n), F32),
        compiler_params=pltpu.CompilerParams(dimension_semantics=("arbitrary",),
                                             vmem_limit_bytes=VMEM_LIMIT),
        name="ada",
    )(c_col, w, b_row)


def _ffn_kernel(x_ref, mod_ref, win_ref, wout_ref, g_ref, b_ref, o_ref, act_ref, *, sub, weight, tf):
    x = x_ref[...]
    d_ff = wout_ref.shape[0]
    h = _modulate(x, mod_ref, sub).astype(win_ref.dtype)
    for c in range(d_ff // tf):
        gate = jnp.dot(h, win_ref[:, c * tf:(c + 1) * tf], preferred_element_type=F32)
        up = jnp.dot(h, win_ref[:, d_ff + c * tf:d_ff + (c + 1) * tf], preferred_element_type=F32)
        act_ref[:, c * tf:(c + 1) * tf] = (gate * _sigmoid(gate) * up).astype(act_ref.dtype)
    y = jnp.dot(act_ref[...], wout_ref[...], preferred_element_type=F32)
    gate_c = mod_ref[3 * sub + 2:3 * sub + 3, :]
    z = ALPHA * x + weight * (1.0 + gate_c) * y
    o_ref[...] = _layer_norm(z, g_ref[...], b_ref[...])


def _ffn(x, mod9, w_in, w_out, g, b, *, sub, weight, tm, tf):
    s, d = x.shape
    d_ff = w_out.shape[0]
    assert s % tm == 0 and d_ff % tf == 0 and w_in.shape == (d, 2 * d_ff)
    return pl.pallas_call(
        functools.partial(_ffn_kernel, sub=sub, weight=weight, tf=tf),
        grid=(s // tm,),
        in_specs=[pl.BlockSpec((tm, d), lambda i: (i, 0)),
                  _const_spec(mod9.shape),
                  _const_spec(w_in.shape),
                  _const_spec(w_out.shape),
                  _const_spec(g.shape),
                  _const_spec(b.shape)],
        out_specs=pl.BlockSpec((tm, d), lambda i: (i, 0)),
        out_shape=jax.ShapeDtypeStruct((s, d), F32),
        scratch_shapes=[pltpu.VMEM((tm, d_ff), w_out.dtype)],
        compiler_params=pltpu.CompilerParams(dimension_semantics=("arbitrary",),
                                             vmem_limit_bytes=VMEM_LIMIT),
        name=f"ffn{sub}",
    )(x, mod9, w_in, w_out, g, b)


def _mix_in_kernel(x_ref, mod_ref, w_ref, rope_ref, u_ref, qt_ref, k_ref, vt_ref, kn_ref, cos_ref, sin_ref, *,
                   sub, cw, aw, q_scale, n_part):
    tm = x_ref.shape[0]
    tp = tm // n_part
    tk = vt_ref.shape[3]
    freq = rope_ref[0:1, :]

    @pl.when(pl.program_id(0) == 0)
    def _in_tile_angles():
        ang = lax.broadcasted_iota(jnp.int32, (tm, LANES), 0).astype(F32) * freq
        cos_ref[...] = jnp.cos(ang)
        sin_ref[...] = jnp.sin(ang)
        kn_ref[...] = jnp.zeros_like(kn_ref)

    comp0 = lax.broadcasted_iota(jnp.int32, (tp, LANES), 1) < DIFF_HEAD_DIM
    projs = []
    for p in range(n_part):
        h = _modulate(x_ref[p * tp:(p + 1) * tp, :], mod_ref, sub).astype(w_ref.dtype)
        projs.append(jnp.dot(h, w_ref[...], preferred_element_type=F32))

    ang0 = (pl.program_id(0) * tm).astype(F32) * freq
    c0, s0 = jnp.cos(ang0), jnp.sin(ang0)
    half = ROT_DIM // 2
    q0, k0, v0 = 2 * cw, 2 * cw + aw, 2 * cw + 2 * aw
    for p, proj in enumerate(projs):
        rows = slice(p * tp, (p + 1) * tp)
        u_ref[rows, :] = proj[:, :cw] * _sigmoid(proj[:, cw:2 * cw])
        rc = c0 * cos_ref[rows, :] - s0 * sin_ref[rows, :]
        sin = s0 * cos_ref[rows, :] + c0 * sin_ref[rows, :]
        rs1, rs2 = sin * rope_ref[1:2, :], sin * rope_ref[2:3, :]

        def rope(t):
            return t * rc + pltpu.roll(t, LANES - half, 1) * rs1 + pltpu.roll(t, half, 1) * rs2

        for g in range(aw // LANES):
            sl = slice(g * LANES, (g + 1) * LANES)
            qt_ref[g, :, rows] = (rope(proj[:, q0 + g * LANES:q0 + (g + 1) * LANES]) * q_scale).T.astype(BF16)
            kb = rope(proj[:, k0 + g * LANES:k0 + (g + 1) * LANES]).astype(BF16)
            k_ref[rows, sl] = kb
            kf = kb.astype(F32)
            sq = kf * kf
            n0 = jnp.max(jnp.sum(jnp.where(comp0, sq, 0.0), axis=1, keepdims=True), axis=0, keepdims=True)
            n1 = jnp.max(jnp.sum(jnp.where(comp0, 0.0, sq), axis=1, keepdims=True), axis=0, keepdims=True)
            kn_ref[g:g + 1, :] = jnp.maximum(kn_ref[g:g + 1, :], jnp.where(comp0[0:1, :], n0, n1))
            for c in range(tp // tk):
                vt_ref[g, p * (tp // tk) + c] = (
                    proj[c * tk:(c + 1) * tk, v0 + g * LANES:v0 + (g + 1) * LANES].T.astype(BF16))


def _mix_in(x, mod9, w, rope_rows, *, cw, aw, q_scale, tm, tk, n_part):
    s, d = x.shape
    n_heads = aw // LANES
    assert (tm // n_part) % tk == 0 and n_heads <= SUBLANES
    row = lambda i: (i, 0)
    return pl.pallas_call(
        functools.partial(_mix_in_kernel, sub=1, cw=cw, aw=aw, q_scale=q_scale, n_part=n_part),
        grid=(s // tm,),
        in_specs=[pl.BlockSpec((tm, d), row),
                  _const_spec(mod9.shape),
                  _const_spec(w.shape),
                  _const_spec(rope_rows.shape)],
        out_specs=[pl.BlockSpec((tm, cw), row),
                   pl.BlockSpec((n_heads, LANES, tm), lambda i: (0, 0, i)),
                   pl.BlockSpec((tm, aw), row),
                   pl.BlockSpec((n_heads, tm // tk, LANES, tk), lambda i: (0, i, 0, 0)),
                   pl.BlockSpec((SUBLANES, LANES), lambda i: (0, 0))],
        out_shape=[jax.ShapeDtypeStruct((s, cw), F32),
                   jax.ShapeDtypeStruct((n_heads, LANES, s), BF16),
                   jax.ShapeDtypeStruct((s, aw), BF16),
                   jax.ShapeDtypeStruct((n_heads, s // tk, LANES, tk), BF16),
                   jax.ShapeDtypeStruct((SUBLANES, LANES), F32)],
        scratch_shapes=[pltpu.VMEM((tm, LANES), F32),
                        pltpu.VMEM((tm, LANES), F32)],
        compiler_params=pltpu.CompilerParams(dimension_semantics=("arbitrary",),
                                             vmem_limit_bytes=VMEM_LIMIT),
        name="mix_in",
    )(x, mod9, w, rope_rows)


def _dependent_zero(v):
    r, c = v.shape
    folded = jnp.sum(v.reshape(r // SUBLANES, SUBLANES, c), axis=0)
    folded = sum(folded[:, g * LANES:(g + 1) * LANES] for g in range(c // LANES))
    bits = lax.bitcast_convert_type(folded[0:1, :], jnp.uint32)
    return lax.bitcast_convert_type((bits >> 16) >> 16, F32)


def _conv_tile(i, n_tiles, prev_ref, cur_ref, next_ref, w_ref, cb_ref, g_ref, b_ref, o_ref, ext_ref, sh_ref, y_ref,
               rows):
    tm, cw = cur_ref.shape
    ext_ref[0:CONV_HALO, :] = jnp.where(i > 0, prev_ref[...], 0.0)
    ext_ref[CONV_HALO:CONV_HALO + tm, :] = cur_ref[...]
    ext_ref[CONV_HALO + tm:, :] = jnp.where(i < n_tiles - 1, next_ref[...], 0.0)
    span = sh_ref.shape[1]
    for b in range(SUBLANES):
        sh_ref[b] = ext_ref[b:b + span, :]
    base = CONV_HALO - CONV_PAD
    done = []
    for lc in range(cw // LANES):
        ls = slice(lc * LANES, (lc + 1) * LANES)
        for rc in range(tm // rows):
            r0 = rc * rows
            acc = jnp.zeros((rows, LANES), F32)
            for t in range(CONV_KERNEL):
                off = base + t
                a0 = r0 + SUBLANES * (off // SUBLANES)
                acc = acc + sh_ref[off % SUBLANES, a0:a0 + rows, ls] * w_ref[t:t + 1, ls]
            y_ref[r0:r0 + rows, ls] = acc
            done.append(_dependent_zero(acc))
    y = _layer_norm(y_ref[...] + cb_ref[...], g_ref[...], b_ref[...])
    y = y * _sigmoid(y)
    o_ref[...] = y.astype(BF16)
    done.append(_dependent_zero(y))
    return done


def _conv_specs(s, cw, tm, tile_index):
    nh = tm // CONV_HALO
    last = s // CONV_HALO - 1
    return [pl.BlockSpec((CONV_HALO, cw), lambda *g: (jnp.maximum(tile_index(*g) * nh - 1, 0), 0)),
            pl.BlockSpec((tm, cw), lambda *g: (tile_index(*g), 0)),
            pl.BlockSpec((CONV_HALO, cw), lambda *g: (jnp.minimum((tile_index(*g) + 1) * nh, last), 0))]


def _conv_scratch(tm, cw):
    return [pltpu.VMEM((tm + 2 * CONV_HALO, cw), F32),
            pltpu.VMEM((SUBLANES, tm + 2 * CONV_HALO - SUBLANES, cw), F32),
            pltpu.VMEM((tm, cw), F32)]


def _mixer_kernel(qt_ref, k_ref, vt_ref, kn_ref, lq1_ref, lk1_ref, lq2_ref, lk2_ref, g_ref,
                  up_ref, uc_ref, un_ref, cw_ref, cb_ref, cg_ref, cbeta_ref,
                  o_ref, conv_ref,
                  rhs_ref, acc_ref, l_ref, kmax_ref, ext_ref, sh_ref, y_ref, *, tk, conv_rows, lam_init):
    step = pl.program_id(0) * pl.num_programs(1) + pl.program_id(1)
    n_steps = pl.num_programs(0) * pl.num_programs(1)

    hd2, tq = qt_ref.shape[1], qt_ref.shape[2]
    n = 2 * tq
    n_kv = k_ref.shape[0] // tk
    qt = qt_ref[0]
    row = lax.broadcasted_iota(jnp.int32, qt.shape, 0)
    zero = jnp.zeros_like(qt)
    rhs_ref[:, :tq] = jnp.where(row < DIFF_HEAD_DIM, qt, zero)
    rhs_ref[:, tq:] = jnp.where(row >= DIFF_HEAD_DIM, qt, zero)

    def k_block(j):
        return k_ref[pl.ds(pl.multiple_of(j * tk, tk), tk), :]

    @pl.when(pl.program_id(1) == 0)
    def _key_norm_bound():
        kn = kn_ref[pl.ds(pl.program_id(0), 1), :]
        col = lax.broadcasted_iota(jnp.int32, (1, n), 1)
        kmax_ref[...] = jnp.sqrt(jnp.where(col < tq, kn[:, 0:1], kn[:, DIFF_HEAD_DIM:DIFF_HEAD_DIM + 1]))

    conv_done = _conv_tile(step, n_steps, up_ref, uc_ref, un_ref, cw_ref, cb_ref, cg_ref, cbeta_ref, conv_ref,
                           ext_ref, sh_ref, y_ref, conv_rows)
    conv_stride = (n_kv - 1 - CONV_FIRST_BLOCK) // len(conv_done)
    assert conv_stride >= 1

    r32 = rhs_ref[...].astype(F32)
    qn = jnp.sqrt(jnp.sum(r32 * r32, axis=0, keepdims=True))
    m = qn * kmax_ref[...] * SHIFT_SLACK
    l8 = jnp.zeros((8, n), F32)
    acc = jnp.zeros((hd2, n), F32)
    e_prev = None
    for j in range(n_kv):
        s = jnp.dot(k_ref[j * tk:(j + 1) * tk, :], rhs_ref[...], preferred_element_type=F32)
        if e_prev is not None:
            acc = acc + jnp.dot(vt_ref[0, j - 1], e_prev, preferred_element_type=F32)
        if j >= CONV_FIRST_BLOCK and (j - CONV_FIRST_BLOCK) % conv_stride == 0 and conv_done:
            m = m + jnp.concatenate([conv_done.pop(0)] * (n // LANES), axis=1)
        e = jnp.exp2(s - m)
        l8 = l8 + jnp.sum(e.reshape(tk // 8, 8, n), axis=0)
        e_prev = e.astype(BF16)
    acc = acc + jnp.dot(vt_ref[0, n_kv - 1], e_prev, preferred_element_type=F32)
    l = jnp.sum(l8, axis=0, keepdims=True)
    acc_ref[...] = acc
    l_ref[...] = l

    @pl.when(jnp.logical_not(jnp.min(l) >= L_FLOOR))
    def _running_max_fallback():
        acc_ref[...] = jnp.zeros_like(acc_ref)

        def body(j, carry):
            m_run, l_run = carry
            s = jnp.dot(k_block(j), rhs_ref[...], preferred_element_type=F32)
            m_new = jnp.maximum(m_run, jnp.max(s, axis=0, keepdims=True))
            alpha = jnp.exp2(m_run - m_new)
            e = jnp.exp2(s - m_new)
            pv = jnp.dot(vt_ref[0, j], e.astype(BF16), preferred_element_type=F32)
            acc_ref[...] = alpha * acc_ref[...] + pv
            return m_new, alpha * l_run + jnp.sum(e, axis=0, keepdims=True)

        init = (jnp.full((1, n), -jnp.inf, F32), jnp.zeros((1, n), F32))
        _, l_run = lax.fori_loop(0, n_kv, body, init)
        l_ref[...] = l_run

    o = acc_ref[...] * (1.0 / l_ref[...])
    lam = (jnp.exp(jnp.sum(lq1_ref[...] * lk1_ref[...])) - jnp.exp(jnp.sum(lq2_ref[...] * lk2_ref[...]))
           + lam_init)
    o = o[:, :tq] - lam * o[:, tq:]
    ms = jnp.mean(o * o, axis=0, keepdims=True)
    o = o * lax.rsqrt(ms + LN_EPS) * g_ref[...] * (1.0 - lam_init)
    o_ref[...] = o.T.astype(BF16)


def _mixer(qt, k, vt, kn, lq1, lk1, lq2, lk2, g_col, u, conv_w, conv_b, conv_g, conv_beta, *, tq, tk, conv_rows,
           lam_init):
    n_heads, hd2, s = qt.shape
    cw = u.shape[1]
    nq = s // tq
    tc = s // (n_heads * nq)
    assert tc % conv_rows == 0 and tc % CONV_HALO == 0
    lam_spec = _const_spec(lq1.shape)
    tile = lambda h, i: h * nq + i
    return pl.pallas_call(
        functools.partial(_mixer_kernel, tk=tk, conv_rows=conv_rows, lam_init=lam_init),
        grid=(n_heads, nq),
        in_specs=[pl.BlockSpec((1, hd2, tq), lambda h, i: (h, 0, i)),
                  pl.BlockSpec((s, hd2), lambda h, i: (0, h)),
                  pl.BlockSpec((1, s // tk, hd2, tk), lambda h, i: (h, 0, 0, 0)),
                  _const_spec(kn.shape),
                  lam_spec, lam_spec, lam_spec, lam_spec,
                  _const_spec(g_col.shape),
                  *_conv_specs(s, cw, tc, tile),
                  _const_spec(conv_w.shape),
                  _const_spec(conv_b.shape),
                  _const_spec(conv_g.shape),
                  _const_spec(conv_beta.shape)],
        out_specs=[pl.BlockSpec((tq, hd2), lambda h, i: (i, h)),
                   pl.BlockSpec((tc, cw), lambda h, i: (tile(h, i), 0))],
        out_shape=[jax.ShapeDtypeStruct((s, n_heads * hd2), BF16),
                   jax.ShapeDtypeStruct((s, cw), BF16)],
        scratch_shapes=[pltpu.VMEM((hd2, 2 * tq), BF16),
                        pltpu.VMEM((hd2, 2 * tq), F32),
                        pltpu.VMEM((1, 2 * tq), F32),
                        pltpu.VMEM((1, 2 * tq), F32),
                        *_conv_scratch(tc, cw)],
        compiler_params=pltpu.CompilerParams(dimension_semantics=("arbitrary", "arbitrary"),
                                             vmem_limit_bytes=VMEM_LIMIT),
        name="mixer",
    )(qt, k, vt, kn, lq1, lk1, lq2, lk2, g_col, u, u, u, conv_w, conv_b, conv_g, conv_beta)


def _mix_out_kernel(conv_ref, attn_ref, x_ref, mod_ref, w_ref, g_ref, b_ref, o_ref, *, sub, n_part):
    cw = conv_ref.shape[1]
    tp = x_ref.shape[0] // n_part
    gate_c = mod_ref[3 * sub + 2:3 * sub + 3, :]
    ys = []
    for p in range(n_part):
        rows = slice(p * tp, (p + 1) * tp)
        ys.append(jnp.dot(conv_ref[rows, :], w_ref[:cw, :], preferred_element_type=F32)
                  + jnp.dot(attn_ref[rows, :], w_ref[cw:, :], preferred_element_type=F32))
    for p in range(n_part):
        rows = slice(p * tp, (p + 1) * tp)
        z = ALPHA * x_ref[rows, :] + (1.0 + gate_c) * ys[p]
        o_ref[rows, :] = _layer_norm(z, g_ref[...], b_ref[...])


def _mix_out(conv, attn, x, mod9, w, g, b, *, tm, n_part):
    s, d = x.shape
    row = lambda i: (i, 0)
    return pl.pallas_call(
        functools.partial(_mix_out_kernel, sub=1, n_part=n_part),
        grid=(s // tm,),
        in_specs=[pl.BlockSpec((tm, conv.shape[1]), row),
                  pl.BlockSpec((tm, attn.shape[1]), row),
                  pl.BlockSpec((tm, d), row),
                  _const_spec(mod9.shape),
                  _const_spec(w.shape),
                  _const_spec(g.shape),
                  _const_spec(b.shape)],
        out_specs=pl.BlockSpec((tm, d), row),
        out_shape=jax.ShapeDtypeStruct((s, d), F32),
        compiler_params=pltpu.CompilerParams(dimension_semantics=("arbitrary",),
                                             vmem_limit_bytes=VMEM_LIMIT),
        name="mix_out",
    )(conv, attn, x, mod9, w, g, b)


def _rope_rows():
    inv_freq = ROPE_THETA ** (-jnp.arange(0, ROT_DIM, 2, dtype=F32) / ROT_DIM)
    half = ROT_DIM // 2
    zeros_h = jnp.zeros((half,), F32)
    zeros_p = jnp.zeros((DIFF_HEAD_DIM - ROT_DIM,), F32)
    ones_h = jnp.ones((half,), F32)
    reps = LANES // DIFF_HEAD_DIM
    freq = jnp.tile(jnp.concatenate([inv_freq, inv_freq, zeros_p]), reps)
    neg_first = jnp.tile(jnp.concatenate([-ones_h, zeros_h, zeros_p]), reps)
    pos_second = jnp.tile(jnp.concatenate([zeros_h, ones_h, zeros_p]), reps)
    return jnp.stack([freq, neg_first, pos_second])


def kernel(x, c, w_ada, b_ada, ffn1_w_in, ffn1_w_out, ln1_g, ln1_b, mix_w_in, conv_w, conv_b, conv_ln_g,
           conv_ln_b, lambda_q1, lambda_k1, lambda_q2, lambda_k2, subln_g, mix_w_out, ln2_g, ln2_b,
           ffn2_w_in, ffn2_w_out, ln3_g, ln3_b):
    batch, s, d = x.shape
    assert batch == 1 and w_ada.shape[0] == DEPTH == 1
    cw = conv_w.shape[2]
    aw = (mix_w_in.shape[2] - 2 * cw) // 3
    hd2 = 2 * DIFF_HEAD_DIM
    n_heads = aw // hd2
    lam_init = 0.8 - 0.6 * math.exp(-0.3 * 0)
    q_scale = math.log2(math.e) / math.sqrt(DIFF_HEAD_DIM)
    t = TILES
    for rows in (t.ffn_rows, t.proj_rows, t.attn_q, t.attn_kv):
        assert s % rows == 0

    mod9 = _ada(c.reshape(d, 1), w_ada[0], b_ada, tn=t.ada_cols).reshape(9, d)
    x0 = x[0]
    x1 = _ffn(x0, mod9, ffn1_w_in[0], ffn1_w_out[0], ln1_g, ln1_b,
              sub=0, weight=0.5, tm=t.ffn_rows, tf=t.ffn_cols)

    u, qt, k, vt, kn = _mix_in(x1, mod9, mix_w_in[0], _rope_rows(),
                               cw=cw, aw=aw, q_scale=q_scale, tm=t.proj_rows, tk=t.attn_kv, n_part=t.proj_parts)
    attn, conv = _mixer(qt, k, vt, kn, lambda_q1, lambda_k1, lambda_q2, lambda_k2, subln_g.reshape(hd2, 1),
                        u, conv_w[0], conv_b, conv_ln_g, conv_ln_b,
                        tq=t.attn_q, tk=t.attn_kv, conv_rows=t.conv_rows, lam_init=lam_init)
    x2 = _mix_out(conv, attn, x1, mod9, mix_w_out[0].astype(BF16), ln2_g, ln2_b,
                  tm=t.proj_rows, n_part=t.proj_parts)

    x3 = _ffn(x2, mod9, ffn2_w_in[0], ffn2_w_out[0], ln3_g, ln3_b,
              sub=2, weight=0.5, tm=t.ffn_rows, tf=t.ffn_cols)
    return x3[None]
```

```python
import functools
import math
from typing import NamedTuple

import jax
import jax.numpy as jnp
from jax import lax
from jax.experimental import pallas as pl
from jax.experimental.pallas import tpu as pltpu

F32 = jnp.float32
BF16 = jnp.bfloat16

DEPTH = 1
ALPHA = (2.0 * DEPTH) ** 0.25
LN_EPS = 1e-5
DIFF_HEAD_DIM = 64
ROT_DIM = DIFF_HEAD_DIM // 4
ROPE_THETA = 500000.0
CONV_KERNEL = 31
CONV_PAD = (CONV_KERNEL - 1) // 2
CONV_HALO = 16
LANES = 128
SUBLANES = 8
SHIFT_SLACK = 1.0 + 2.0 ** -10
L_FLOOR = 2.0 ** -80
CONV_FIRST_BLOCK = 55
CONV_PIECES_PER_BLOCK = 2
VMEM_LIMIT = 56 * 1024 * 1024


class _Tiles(NamedTuple):
    ada_cols: int = 1152
    ffn_rows: int = 512
    ffn_cols: int = 256
    proj_rows: int = 1024
    proj_parts: int = 4
    attn_q: int = 512
    attn_kv: int = 256
    conv_rows: int = 32


TILES = _Tiles()


def _sigmoid(x):
    return 1.0 / (1.0 + jnp.exp(-x))


def _layer_norm(z, g, b):
    mu = jnp.mean(z, axis=-1, keepdims=True)
    zc = z - mu
    var = jnp.mean(zc * zc, axis=-1, keepdims=True)
    return zc * lax.rsqrt(var + LN_EPS) * g + b


def _modulate(x, mod_ref, sub):
    shift = mod_ref[3 * sub:3 * sub + 1, :]
    scale = mod_ref[3 * sub + 1:3 * sub + 2, :]
    return x * (1.0 + scale) + shift


def _const_spec(shape):
    return pl.BlockSpec(shape, lambda *_: (0,) * len(shape), pipeline_mode=pl.Buffered(1))


def _ada_kernel(c_ref, w_ref, b_ref, o_ref):
    c = c_ref[...]
    ca = c * _sigmoid(c)
    o_ref[...] = jnp.sum(ca * w_ref[...], axis=0, keepdims=True) + b_ref[...]


def _ada(c_col, w, b_row, tn):
    d, n = w.shape
    assert n % tn == 0 and tn % LANES == 0
    return pl.pallas_call(
        _ada_kernel,
        grid=(n // tn,),
        in_specs=[pl.BlockSpec((d, 1), lambda j: (0, 0)),
                  pl.BlockSpec((d, tn), lambda j: (0, j)),
                  pl.BlockSpec((1, tn), lambda j: (0, j))],
        out_specs=pl.BlockSpec((1, tn), lambda j: (0, j)),
        out_shape=jax.ShapeDtypeStruct((1, n), F32),
        compiler_params=pltpu.CompilerParams(dimension_semantics=("arbitrary",),
                                             vmem_limit_bytes=VMEM_LIMIT),
        name="ada",
    )(c_col, w, b_row)


def _ffn_kernel(x_ref, mod_ref, win_ref, wout_ref, g_ref, b_ref, o_ref, act_ref, *, sub, weight, tf):
    x = x_ref[...]
    d_ff = wout_ref.shape[0]
    h = _modulate(x, mod_ref, sub).astype(win_ref.dtype)
    for c in range(d_ff // tf):
        gate = jnp.dot(h, win_ref[:, c * tf:(c + 1) * tf], preferred_element_type=F32)
        up = jnp.dot(h, win_ref[:, d_ff + c * tf:d_ff + (c + 1) * tf], preferred_element_type=F32)
        act_ref[:, c * tf:(c + 1) * tf] = (gate * _sigmoid(gate) * up).astype(act_ref.dtype)
    y = jnp.dot(act_ref[...], wout_ref[...], preferred_element_type=F32)
    gate_c = mod_ref[3 * sub + 2:3 * sub + 3, :]
    z = ALPHA * x + weight * (1.0 + gate_c) * y
    o_ref[...] = _layer_norm(z, g_ref[...], b_ref[...])


def _ffn(x, mod9, w_in, w_out, g, b, *, sub, weight, tm, tf):
    s, d = x.shape
    d_ff = w_out.shape[0]
    assert s % tm == 0 and d_ff % tf == 0 and w_in.shape == (d, 2 * d_ff)
    return pl.pallas_call(
        functools.partial(_ffn_kernel, sub=sub, weight=weight, tf=tf),
        grid=(s // tm,),
        in_specs=[pl.BlockSpec((tm, d), lambda i: (i, 0)),
                  _const_spec(mod9.shape),
                  _const_spec(w_in.shape),
                  _const_spec(w_out.shape),
                  _const_spec(g.shape),
                  _const_spec(b.shape)],
        out_specs=pl.BlockSpec((tm, d), lambda i: (i, 0)),
        out_shape=jax.ShapeDtypeStruct((s, d), F32),
        scratch_shapes=[pltpu.VMEM((tm, d_ff), w_out.dtype)],
        compiler_params=pltpu.CompilerParams(dimension_semantics=("arbitrary",),
                                             vmem_limit_bytes=VMEM_LIMIT),
        name=f"ffn{sub}",
    )(x, mod9, w_in, w_out, g, b)


def _mix_in_kernel(x_ref, mod_ref, w_ref, rope_ref, u_ref, qt_ref, k_ref, vt_ref, kn_ref, cos_ref, sin_ref, *,
                   sub, cw, aw, q_scale, n_part):
    tm = x_ref.shape[0]
    tp = tm // n_part
    tk = vt_ref.shape[3]
    freq = rope_ref[0:1, :]

    @pl.when(pl.program_id(0) == 0)
    def _in_tile_angles():
        ang = lax.broadcasted_iota(jnp.int32, (tm, LANES), 0).astype(F32) * freq
        cos_ref[...] = jnp.cos(ang)
        sin_ref[...] = jnp.sin(ang)
        kn_ref[...] = jnp.zeros_like(kn_ref)

    comp0 = lax.broadcasted_iota(jnp.int32, (tp, LANES), 1) < DIFF_HEAD_DIM
    projs = []
    for p in range(n_part):
        h = _modulate(x_ref[p * tp:(p + 1) * tp, :], mod_ref, sub).astype(w_ref.dtype)
        projs.append(jnp.dot(h, w_ref[...], preferred_element_type=F32))

    ang0 = (pl.program_id(0) * tm).astype(F32) * freq
    c0, s0 = jnp.cos(ang0), jnp.sin(ang0)
    half = ROT_DIM // 2
    q0, k0, v0 = 2 * cw, 2 * cw + aw, 2 * cw + 2 * aw
    for p, proj in enumerate(projs):
        rows = slice(p * tp, (p + 1) * tp)
        u_ref[rows, :] = proj[:, :cw] * _sigmoid(proj[:, cw:2 * cw])
        rc = c0 * cos_ref[rows, :] - s0 * sin_ref[rows, :]
        sin = s0 * cos_ref[rows, :] + c0 * sin_ref[rows, :]
        rs1, rs2 = sin * rope_ref[1:2, :], sin * rope_ref[2:3, :]

        def rope(t):
            return t * rc + pltpu.roll(t, LANES - half, 1) * rs1 + pltpu.roll(t, half, 1) * rs2

        for g in range(aw // LANES):
            sl = slice(g * LANES, (g + 1) * LANES)
            qt_ref[g, :, rows] = (rope(proj[:, q0 + g * LANES:q0 + (g + 1) * LANES]) * q_scale).T.astype(BF16)
            kb = rope(proj[:, k0 + g * LANES:k0 + (g + 1) * LANES]).astype(BF16)
            k_ref[rows, sl] = kb
            kf = kb.astype(F32)
            sq = kf * kf
            n0 = jnp.max(jnp.sum(jnp.where(comp0, sq, 0.0), axis=1, keepdims=True), axis=0, keepdims=True)
            n1 = jnp.max(jnp.sum(jnp.where(comp0, 0.0, sq), axis=1, keepdims=True), axis=0, keepdims=True)
            kn_ref[g:g + 1, :] = jnp.maximum(kn_ref[g:g + 1, :], jnp.where(comp0[0:1, :], n0, n1))
            for c in range(tp // tk):
                vt_ref[g, p * (tp // tk) + c] = (
                    proj[c * tk:(c + 1) * tk, v0 + g * LANES:v0 + (g + 1) * LANES].T.astype(BF16))


def _mix_in(x, mod9, w, rope_rows, *, cw, aw, q_scale, tm, tk, n_part):
    s, d = x.shape
    n_heads = aw // LANES
    assert (tm // n_part) % tk == 0 and n_heads <= SUBLANES
    row = lambda i: (i, 0)
    return pl.pallas_call(
        functools.partial(_mix_in_kernel, sub=1, cw=cw, aw=aw, q_scale=q_scale, n_part=n_part),
        grid=(s // tm,),
        in_specs=[pl.BlockSpec((tm, d), row),
                  _const_spec(mod9.shape),
                  _const_spec(w.shape),
                  _const_spec(rope_rows.shape)],
        out_specs=[pl.BlockSpec((tm, cw), row),
                   pl.BlockSpec((n_heads, LANES, tm), lambda i: (0, 0, i)),
                   pl.BlockSpec((tm, aw), row),
                   pl.BlockSpec((n_heads, tm // tk, LANES, tk), lambda i: (0, i, 0, 0)),
                   pl.BlockSpec((SUBLANES, LANES), lambda i: (0, 0))],
        out_shape=[jax.ShapeDtypeStruct((s, cw), F32),
                   jax.ShapeDtypeStruct((n_heads, LANES, s), BF16),
                   jax.ShapeDtypeStruct((s, aw), BF16),
                   jax.ShapeDtypeStruct((n_heads, s // tk, LANES, tk), BF16),
                   jax.ShapeDtypeStruct((SUBLANES, LANES), F32)],
        scratch_shapes=[pltpu.VMEM((tm, LANES), F32),
                        pltpu.VMEM((tm, LANES), F32)],
        compiler_params=pltpu.CompilerParams(dimension_semantics=("arbitrary",),
                                             vmem_limit_bytes=VMEM_LIMIT),
        name="mix_in",
    )(x, mod9, w, rope_rows)


def _dependent_zero(v):
    r, c = v.shape
    folded = jnp.sum(v.reshape(r // SUBLANES, SUBLANES, c), axis=0)
    folded = sum(folded[:, g * LANES:(g + 1) * LANES] for g in range(c // LANES))
    bits = lax.bitcast_convert_type(folded[0:1, :], jnp.uint32)
    return lax.bitcast_convert_type((bits >> 16) >> 16, F32)


def _conv_tile(i, n_tiles, prev_ref, cur_ref, next_ref, w_ref, cb_ref, g_ref, b_ref, o_ref, ext_ref, sh_ref, y_ref,
               rows):
    tm, cw = cur_ref.shape
    ext_ref[0:CONV_HALO, :] = jnp.where(i > 0, prev_ref[...], 0.0)
    ext_ref[CONV_HALO:CONV_HALO + tm, :] = cur_ref[...]
    ext_ref[CONV_HALO + tm:, :] = jnp.where(i < n_tiles - 1, next_ref[...], 0.0)
    span = sh_ref.shape[1]
    for b in range(SUBLANES):
        sh_ref[b] = ext_ref[b:b + span, :]
    base = CONV_HALO - CONV_PAD
    done = []
    for lc in range(cw // LANES):
        ls = slice(lc * LANES, (lc + 1) * LANES)
        for rc in range(tm // rows):
            r0 = rc * rows
            acc = jnp.zeros((rows, LANES), F32)
            for t in range(CONV_KERNEL):
                off = base + t
                a0 = r0 + SUBLANES * (off // SUBLANES)
                acc = acc + sh_ref[off % SUBLANES, a0:a0 + rows, ls] * w_ref[t:t + 1, ls]
            y_ref[r0:r0 + rows, ls] = acc
            done.append(_dependent_zero(acc))
    y = _layer_norm(y_ref[...] + cb_ref[...], g_ref[...], b_ref[...])
    y = y * _sigmoid(y)
    o_ref[...] = y.astype(BF16)
    done.append(_dependent_zero(y))
    return done


def _conv_specs(s, cw, tm, tile_index):
    nh = tm // CONV_HALO
    last = s // CONV_HALO - 1
    return [pl.BlockSpec((CONV_HALO, cw), lambda *g: (jnp.maximum(tile_index(*g) * nh - 1, 0), 0)),
            pl.BlockSpec((tm, cw), lambda *g: (tile_index(*g), 0)),
            pl.BlockSpec((CONV_HALO, cw), lambda *g: (jnp.minimum((tile_index(*g) + 1) * nh, last), 0))]


def _conv_scratch(tm, cw):
    return [pltpu.VMEM((tm + 2 * CONV_HALO, cw), F32),
            pltpu.VMEM((SUBLANES, tm + 2 * CONV_HALO - SUBLANES, cw), F32),
            pltpu.VMEM((tm, cw), F32)]


def _mixer_kernel(qt_ref, k_ref, vt_ref, kn_ref, lq1_ref, lk1_ref, lq2_ref, lk2_ref, g_ref,
                  up_ref, uc_ref, un_ref, cw_ref, cb_ref, cg_ref, cbeta_ref,
                  o_ref, conv_ref,
                  rhs_ref, acc_ref, l_ref, kmax_ref, ext_ref, sh_ref, y_ref, *, tk, conv_rows, lam_init):
    step = pl.program_id(0) * pl.num_programs(1) + pl.program_id(1)
    n_steps = pl.num_programs(0) * pl.num_programs(1)

    hd2, tq = qt_ref.shape[1], qt_ref.shape[2]
    n = 2 * tq
    n_kv = k_ref.shape[0] // tk
    qt = qt_ref[0]
    row = lax.broadcasted_iota(jnp.int32, qt.shape, 0)
    zero = jnp.zeros_like(qt)
    rhs_ref[:, :tq] = jnp.where(row < DIFF_HEAD_DIM, qt, zero)
    rhs_ref[:, tq:] = jnp.where(row >= DIFF_HEAD_DIM, qt, zero)

    def k_block(j):
        return k_ref[pl.ds(pl.multiple_of(j * tk, tk), tk), :]

    @pl.when(pl.program_id(1) == 0)
    def _key_norm_bound():
        kn = kn_ref[pl.ds(pl.program_id(0), 1), :]
        col = lax.broadcasted_iota(jnp.int32, (1, n), 1)
        kmax_ref[...] = jnp.sqrt(jnp.where(col < tq, kn[:, 0:1], kn[:, DIFF_HEAD_DIM:DIFF_HEAD_DIM + 1]))

    conv_done = _conv_tile(step, n_steps, up_ref, uc_ref, un_ref, cw_ref, cb_ref, cg_ref, cbeta_ref, conv_ref,
                           ext_ref, sh_ref, y_ref, conv_rows)
    assert CONV_FIRST_BLOCK + pl.cdiv(len(conv_done), CONV_PIECES_PER_BLOCK) <= n_kv

    r32 = rhs_ref[...].astype(F32)
    qn = jnp.sqrt(jnp.sum(r32 * r32, axis=0, keepdims=True))
    m = qn * kmax_ref[...] * SHIFT_SLACK
    l8 = jnp.zeros((8, n), F32)
    acc = jnp.zeros((hd2, n), F32)
    e_prev = None
    for j in range(n_kv):
        s = jnp.dot(k_ref[j * tk:(j + 1) * tk, :], rhs_ref[...], preferred_element_type=F32)
        if e_prev is not None:
            acc = acc + jnp.dot(vt_ref[0, j - 1], e_prev, preferred_element_type=F32)
        if j >= CONV_FIRST_BLOCK:
            for piece in conv_done[:CONV_PIECES_PER_BLOCK]:
                m = m + jnp.concatenate([piece] * (n // LANES), axis=1)
            del conv_done[:CONV_PIECES_PER_BLOCK]
        e = jnp.exp2(s - m)
        l8 = l8 + jnp.sum(e.reshape(tk // 8, 8, n), axis=0)
        e_prev = e.astype(BF16)
    acc = acc + jnp.dot(vt_ref[0, n_kv - 1], e_prev, preferred_element_type=F32)
    l = jnp.sum(l8, axis=0, keepdims=True)
    acc_ref[...] = acc
    l_ref[...] = l

    @pl.when(jnp.logical_not(jnp.min(l) >= L_FLOOR))
    def _running_max_fallback():
        acc_ref[...] = jnp.zeros_like(acc_ref)

        def body(j, carry):
            m_run, l_run = carry
            s = jnp.dot(k_block(j), rhs_ref[...], preferred_element_type=F32)
            m_new = jnp.maximum(m_run, jnp.max(s, axis=0, keepdims=True))
            alpha = jnp.exp2(m_run - m_new)
            e = jnp.exp2(s - m_new)
            pv = jnp.dot(vt_ref[0, j], e.astype(BF16), preferred_element_type=F32)
            acc_ref[...] = alpha * acc_ref[...] + pv
            return m_new, alpha * l_run + jnp.sum(e, axis=0, keepdims=True)

        init = (jnp.full((1, n), -jnp.inf, F32), jnp.zeros((1, n), F32))
        _, l_run = lax.fori_loop(0, n_kv, body, init)
        l_ref[...] = l_run

    o = acc_ref[...] * (1.0 / l_ref[...])
    lam = (jnp.exp(jnp.sum(lq1_ref[...] * lk1_ref[...])) - jnp.exp(jnp.sum(lq2_ref[...] * lk2_ref[...]))
           + lam_init)
    o = o[:, :tq] - lam * o[:, tq:]
    ms = jnp.mean(o * o, axis=0, keepdims=True)
    o = o * lax.rsqrt(ms + LN_EPS) * g_ref[...] * (1.0 - lam_init)
    o_ref[...] = o.T.astype(BF16)


def _mixer(qt, k, vt, kn, lq1, lk1, lq2, lk2, g_col, u, conv_w, conv_b, conv_g, conv_beta, *, tq, tk, conv_rows,
           lam_init):
    n_heads, hd2, s = qt.shape
    cw = u.shape[1]
    nq = s // tq
    tc = s // (n_heads * nq)
    assert tc % conv_rows == 0 and tc % CONV_HALO == 0
    lam_spec = _const_spec(lq1.shape)
    tile = lambda h, i: h * nq + i
    return pl.pallas_call(
        functools.partial(_mixer_kernel, tk=tk, conv_rows=conv_rows, lam_init=lam_init),
        grid=(n_heads, nq),
        in_specs=[pl.BlockSpec((1, hd2, tq), lambda h, i: (h, 0, i)),
                  pl.BlockSpec((s, hd2), lambda h, i: (0, h)),
                  pl.BlockSpec((1, s // tk, hd2, tk), lambda h, i: (h, 0, 0, 0)),
                  _const_spec(kn.shape),
                  lam_spec, lam_spec, lam_spec, lam_spec,
                  _const_spec(g_col.shape),
                  *_conv_specs(s, cw, tc, tile),
                  _const_spec(conv_w.shape),
                  _const_spec(conv_b.shape),
                  _const_spec(conv_g.shape),
                  _const_spec(conv_beta.shape)],
        out_specs=[pl.BlockSpec((tq, hd2), lambda h, i: (i, h)),
                   pl.BlockSpec((tc, cw), lambda h, i: (tile(h, i), 0))],
        out_shape=[jax.ShapeDtypeStruct((s, n_heads * hd2), BF16),
                   jax.ShapeDtypeStruct((s, cw), BF16)],
        scratch_shapes=[pltpu.VMEM((hd2, 2 * tq), BF16),
                        pltpu.VMEM((hd2, 2 * tq), F32),
                        pltpu.VMEM((1, 2 * tq), F32),
                        pltpu.VMEM((1, 2 * tq), F32),
                        *_conv_scratch(tc, cw)],
        compiler_params=pltpu.CompilerParams(dimension_semantics=("arbitrary", "arbitrary"),
                                             vmem_limit_bytes=VMEM_LIMIT),
        name="mixer",
    )(qt, k, vt, kn, lq1, lk1, lq2, lk2, g_col, u, u, u, conv_w, conv_b, conv_g, conv_beta)


def _mix_out_kernel(conv_ref, attn_ref, x_ref, mod_ref, w_ref, g_ref, b_ref, o_ref, *, sub, n_part):
    cw = conv_ref.shape[1]
    tp = x_ref.shape[0] // n_part
    gate_c = mod_ref[3 * sub + 2:3 * sub + 3, :]
    ys = []
    for p in range(n_part):
        rows = slice(p * tp, (p + 1) * tp)
        ys.append(jnp.dot(conv_ref[rows, :], w_ref[:cw, :], preferred_element_type=F32)
                  + jnp.dot(attn_ref[rows, :], w_ref[cw:, :], preferred_element_type=F32))
    for p in range(n_part):
        rows = slice(p * tp, (p + 1) * tp)
        z = ALPHA * x_ref[rows, :] + (1.0 + gate_c) * ys[p]
        o_ref[rows, :] = _layer_norm(z, g_ref[...], b_ref[...])


def _mix_out(conv, attn, x, mod9, w, g, b, *, tm, n_part):
    s, d = x.shape
    row = lambda i: (i, 0)
    return pl.pallas_call(
        functools.partial(_mix_out_kernel, sub=1, n_part=n_part),
        grid=(s // tm,),
        in_specs=[pl.BlockSpec((tm, conv.shape[1]), row),
                  pl.BlockSpec((tm, attn.shape[1]), row),
                  pl.BlockSpec((tm, d), row),
                  _const_spec(mod9.shape),
                  _const_spec(w.shape),
                  _const_spec(g.shape),
                  _const_spec(b.shape)],
        out_specs=pl.BlockSpec((tm, d), row),
        out_shape=jax.ShapeDtypeStruct((s, d), F32),
        compiler_params=pltpu.CompilerParams(dimension_semantics=("arbitrary",),
                                             vmem_limit_bytes=VMEM_LIMIT),
        name="mix_out",
    )(conv, attn, x, mod9, w, g, b)


def _rope_rows():
    inv_freq = ROPE_THETA ** (-jnp.arange(0, ROT_DIM, 2, dtype=F32) / ROT_DIM)
    half = ROT_DIM // 2
    zeros_h = jnp.zeros((half,), F32)
    zeros_p = jnp.zeros((DIFF_HEAD_DIM - ROT_DIM,), F32)
    ones_h = jnp.ones((half,), F32)
    reps = LANES // DIFF_HEAD_DIM
    freq = jnp.tile(jnp.concatenate([inv_freq, inv_freq, zeros_p]), reps)
    neg_first = jnp.tile(jnp.concatenate([-ones_h, zeros_h, zeros_p]), reps)
    pos_second = jnp.tile(jnp.concatenate([zeros_h, ones_h, zeros_p]), reps)
    return jnp.stack([freq, neg_first, pos_second])


def kernel(x, c, w_ada, b_ada, ffn1_w_in, ffn1_w_out, ln1_g, ln1_b, mix_w_in, conv_w, conv_b, conv_ln_g,
           conv_ln_b, lambda_q1, lambda_k1, lambda_q2, lambda_k2, subln_g, mix_w_out, ln2_g, ln2_b,
           ffn2_w_in, ffn2_w_out, ln3_g, ln3_b):
    batch, s, d = x.shape
    assert batch == 1 and w_ada.shape[0] == DEPTH == 1
    cw = conv_w.shape[2]
    aw = (mix_w_in.shape[2] - 2 * cw) // 3
    hd2 = 2 * DIFF_HEAD_DIM
    n_heads = aw // hd2
    lam_init = 0.8 - 0.6 * math.exp(-0.3 * 0)
    q_scale = math.log2(math.e) / math.sqrt(DIFF_HEAD_DIM)
    t = TILES
    for rows in (t.ffn_rows, t.proj_rows, t.attn_q, t.attn_kv):
        assert s % rows == 0

    mod9 = _ada(c.reshape(d, 1), w_ada[0], b_ada, tn=t.ada_cols).reshape(9, d)
    x0 = x[0]
    x1 = _ffn(x0, mod9, ffn1_w_in[0], ffn1_w_out[0], ln1_g, ln1_b,
              sub=0, weight=0.5, tm=t.ffn_rows, tf=t.ffn_cols)

    u, qt, k, vt, kn = _mix_in(x1, mod9, mix_w_in[0], _rope_rows(),
                               cw=cw, aw=aw, q_scale=q_scale, tm=t.proj_rows, tk=t.attn_kv, n_part=t.proj_parts)
    attn, conv = _mixer(qt, k, vt, kn, lambda_q1, lambda_k1, lambda_q2, lambda_k2, subln_g.reshape(hd2, 1),
                        u, conv_w[0], conv_b, conv_ln_g, conv_ln_b,
                        tq=t.attn_q, tk=t.attn_kv, conv_rows=t.conv_rows, lam_init=lam_init)
    x2 = _mix_out(conv, attn, x1, mod9, mix_w_out[0].astype(BF16), ln2_g, ln2_b,
                  tm=t.proj_rows, n_part=t.proj_parts)

    x3 = _ffn(x2, mod9, ffn2_w_in[0], ffn2_w_out[0], ln3_g, ln3_b,
              sub=2, weight=0.5, tm=t.ffn_rows, tf=t.ffn_cols)
    return x3[None]
```

```python
import functools
import math
from typing import NamedTuple

import jax
import jax.numpy as jnp
from jax import lax
from jax.experimental import pallas as pl
from jax.experimental.pallas import tpu as pltpu

F32 = jnp.float32
BF16 = jnp.bfloat16

DEPTH = 1
ALPHA = (2.0 * DEPTH) ** 0.25
LN_EPS = 1e-5
DIFF_HEAD_DIM = 64
ROT_DIM = DIFF_HEAD_DIM // 4
ROPE_THETA = 500000.0
CONV_KERNEL = 31
CONV_PAD = (CONV_KERNEL - 1) // 2
CONV_HALO = 16
LANES = 128
SUBLANES = 8
SHIFT_SLACK = 1.0 + 2.0 ** -10
L_FLOOR = 2.0 ** -80
CONV_FIRST_BLOCK = 32
CONV_PIECES_PER_BLOCK = 1
VMEM_LIMIT = 56 * 1024 * 1024


class _Tiles(NamedTuple):
    ada_cols: int = 1152
    ffn_rows: int = 512
    ffn_cols: int = 256
    proj_rows: int = 1024
    proj_parts: int = 4
    attn_q: int = 512
    attn_kv: int = 256
    conv_rows: int = 32


TILES = _Tiles()


def _sigmoid(x):
    return 1.0 / (1.0 + jnp.exp(-x))


def _layer_norm(z, g, b):
    mu = jnp.mean(z, axis=-1, keepdims=True)
    zc = z - mu
    var = jnp.mean(zc * zc, axis=-1, keepdims=True)
    return zc * lax.rsqrt(var + LN_EPS) * g + b


def _modulate(x, mod_ref, sub):
    shift = mod_ref[3 * sub:3 * sub + 1, :]
    scale = mod_ref[3 * sub + 1:3 * sub + 2, :]
    return x * (1.0 + scale) + shift


def _const_spec(shape):
    return pl.BlockSpec(shape, lambda *_: (0,) * len(shape), pipeline_mode=pl.Buffered(1))


def _ada_kernel(c_ref, w_ref, b_ref, o_ref):
    c = c_ref[...]
    ca = c * _sigmoid(c)
    o_ref[...] = jnp.sum(ca * w_ref[...], axis=0, keepdims=True) + b_ref[...]


def _ada(c_col, w, b_row, tn):
    d, n = w.shape
    assert n % tn == 0 and tn % LANES == 0
    return pl.pallas_call(
        _ada_kernel,
        grid=(n // tn,),
        in_specs=[pl.BlockSpec((d, 1), lambda j: (0, 0)),
                  pl.BlockSpec((d, tn), lambda j: (0, j)),
                  pl.BlockSpec((1, tn), lambda j: (0, j))],
        out_specs=pl.BlockSpec((1, tn), lambda j: (0, j)),
        out_shape=jax.ShapeDtypeStruct((1, n), F32),
        compiler_params=pltpu.CompilerParams(dimension_semantics=("arbitrary",),
                                             vmem_limit_bytes=VMEM_LIMIT),
        name="ada",
    )(c_col, w, b_row)


def _ffn_kernel(x_ref, mod_ref, win_ref, wout_ref, g_ref, b_ref, o_ref, act_ref, *, sub, weight, tf):
    x = x_ref[...]
    d_ff = wout_ref.shape[0]
    h = _modulate(x, mod_ref, sub).astype(win_ref.dtype)
    for c in range(d_ff // tf):
        gate = jnp.dot(h, win_ref[:, c * tf:(c + 1) * tf], preferred_element_type=F32)
        up = jnp.dot(h, win_ref[:, d_ff + c * tf:d_ff + (c + 1) * tf], preferred_element_type=F32)
        act_ref[:, c * tf:(c + 1) * tf] = (gate * _sigmoid(gate) * up).astype(act_ref.dtype)
    y = jnp.dot(act_ref[...], wout_ref[...], preferred_element_type=F32)
    gate_c = mod_ref[3 * sub + 2:3 * sub + 3, :]
    z = ALPHA * x + weight * (1.0 + gate_c) * y
    o_ref[...] = _layer_norm(z, g_ref[...], b_ref[...])


def _ffn(x, mod9, w_in, w_out, g, b, *, sub, weight, tm, tf):
    s, d = x.shape
    d_ff = w_out.shape[0]
    assert s % tm == 0 and d_ff % tf == 0 and w_in.shape == (d, 2 * d_ff)
    return pl.pallas_call(
        functools.partial(_ffn_kernel, sub=sub, weight=weight, tf=tf),
        grid=(s // tm,),
        in_specs=[pl.BlockSpec((tm, d), lambda i: (i, 0)),
                  _const_spec(mod9.shape),
                  _const_spec(w_in.shape),
                  _const_spec(w_out.shape),
                  _const_spec(g.shape),
                  _const_spec(b.shape)],
        out_specs=pl.BlockSpec((tm, d), lambda i: (i, 0)),
        out_shape=jax.ShapeDtypeStruct((s, d), F32),
        scratch_shapes=[pltpu.VMEM((tm, d_ff), w_out.dtype)],
        compiler_params=pltpu.CompilerParams(dimension_semantics=("arbitrary",),
                                             vmem_limit_bytes=VMEM_LIMIT),
        name=f"ffn{sub}",
    )(x, mod9, w_in, w_out, g, b)


def _mix_in_kernel(x_ref, mod_ref, w_ref, rope_ref, u_ref, qt_ref, k_ref, vt_ref, kn_ref, cos_ref, sin_ref, *,
                   sub, cw, aw, q_scale, n_part):
    tm = x_ref.shape[0]
    tp = tm // n_part
    tk = vt_ref.shape[3]
    freq = rope_ref[0:1, :]

    @pl.when(pl.program_id(0) == 0)
    def _in_tile_angles():
        ang = lax.broadcasted_iota(jnp.int32, (tm, LANES), 0).astype(F32) * freq
        cos_ref[...] = jnp.cos(ang)
        sin_ref[...] = jnp.sin(ang)
        kn_ref[...] = jnp.zeros_like(kn_ref)

    comp0 = lax.broadcasted_iota(jnp.int32, (tp, LANES), 1) < DIFF_HEAD_DIM
    projs = []
    for p in range(n_part):
        h = _modulate(x_ref[p * tp:(p + 1) * tp, :], mod_ref, sub).astype(w_ref.dtype)
        projs.append(jnp.dot(h, w_ref[...], preferred_element_type=F32))

    ang0 = (pl.program_id(0) * tm).astype(F32) * freq
    c0, s0 = jnp.cos(ang0), jnp.sin(ang0)
    half = ROT_DIM // 2
    q0, k0, v0 = 2 * cw, 2 * cw + aw, 2 * cw + 2 * aw
    for p, proj in enumerate(projs):
        rows = slice(p * tp, (p + 1) * tp)
        u_ref[rows, :] = proj[:, :cw] * _sigmoid(proj[:, cw:2 * cw])
        rc = c0 * cos_ref[rows, :] - s0 * sin_ref[rows, :]
        sin = s0 * cos_ref[rows, :] + c0 * sin_ref[rows, :]
        rs1, rs2 = sin * rope_ref[1:2, :], sin * rope_ref[2:3, :]

        def rope(t):
            return t * rc + pltpu.roll(t, LANES - half, 1) * rs1 + pltpu.roll(t, half, 1) * rs2

        for g in range(aw // LANES):
            sl = slice(g * LANES, (g + 1) * LANES)
            qt_ref[g, :, rows] = (rope(proj[:, q0 + g * LANES:q0 + (g + 1) * LANES]) * q_scale).T.astype(BF16)
            kb = rope(proj[:, k0 + g * LANES:k0 + (g + 1) * LANES]).astype(BF16)
            k_ref[rows, sl] = kb
            kf = kb.astype(F32)
            sq = kf * kf
            n0 = jnp.max(jnp.sum(jnp.where(comp0, sq, 0.0), axis=1, keepdims=True), axis=0, keepdims=True)
            n1 = jnp.max(jnp.sum(jnp.where(comp0, 0.0, sq), axis=1, keepdims=True), axis=0, keepdims=True)
            kn_ref[g:g + 1, :] = jnp.maximum(kn_ref[g:g + 1, :], jnp.where(comp0[0:1, :], n0, n1))
            for c in range(tp // tk):
                vt_ref[g, p * (tp // tk) + c] = (
                    proj[c * tk:(c + 1) * tk, v0 + g * LANES:v0 + (g + 1) * LANES].T.astype(BF16))


def _mix_in(x, mod9, w, rope_rows, *, cw, aw, q_scale, tm, tk, n_part):
    s, d = x.shape
    n_heads = aw // LANES
    assert (tm // n_part) % tk == 0 and n_heads <= SUBLANES
    row = lambda i: (i, 0)
    return pl.pallas_call(
        functools.partial(_mix_in_kernel, sub=1, cw=cw, aw=aw, q_scale=q_scale, n_part=n_part),
        grid=(s // tm,),
        in_specs=[pl.BlockSpec((tm, d), row),
                  _const_spec(mod9.shape),
                  _const_spec(w.shape),
                  _const_spec(rope_rows.shape)],
        out_specs=[pl.BlockSpec((tm, cw), row),
                   pl.BlockSpec((n_heads, LANES, tm), lambda i: (0, 0, i)),
                   pl.BlockSpec((tm, aw), row),
                   pl.BlockSpec((n_heads, tm // tk, LANES, tk), lambda i: (0, i, 0, 0)),
                   pl.BlockSpec((SUBLANES, LANES), lambda i: (0, 0))],
        out_shape=[jax.ShapeDtypeStruct((s, cw), F32),
                   jax.ShapeDtypeStruct((n_heads, LANES, s), BF16),
                   jax.ShapeDtypeStruct((s, aw), BF16),
                   jax.ShapeDtypeStruct((n_heads, s // tk, LANES, tk), BF16),
                   jax.ShapeDtypeStruct((SUBLANES, LANES), F32)],
        scratch_shapes=[pltpu.VMEM((tm, LANES), F32),
                        pltpu.VMEM((tm, LANES), F32)],
        compiler_params=pltpu.CompilerParams(dimension_semantics=("arbitrary",),
                                             vmem_limit_bytes=VMEM_LIMIT),
        name="mix_in",
    )(x, mod9, w, rope_rows)


def _dependent_zero(v):
    r, c = v.shape
    folded = jnp.sum(v.reshape(r // SUBLANES, SUBLANES, c), axis=0)
    folded = sum(folded[:, g * LANES:(g + 1) * LANES] for g in range(c // LANES))
    bits = lax.bitcast_convert_type(folded[0:1, :], jnp.uint32)
    return lax.bitcast_convert_type((bits >> 16) >> 16, F32)


def _conv_tile(i, n_tiles, prev_ref, cur_ref, next_ref, w_ref, cb_ref, g_ref, b_ref, o_ref, ext_ref, sh_ref, y_ref,
               rows):
    tm, cw = cur_ref.shape
    ext_ref[0:CONV_HALO, :] = jnp.where(i > 0, prev_ref[...], 0.0)
    ext_ref[CONV_HALO:CONV_HALO + tm, :] = cur_ref[...]
    ext_ref[CONV_HALO + tm:, :] = jnp.where(i < n_tiles - 1, next_ref[...], 0.0)
    span = sh_ref.shape[1]
    for b in range(SUBLANES):
        sh_ref[b] = ext_ref[b:b + span, :]
    base = CONV_HALO - CONV_PAD
    done = []
    for lc in range(cw // LANES):
        ls = slice(lc * LANES, (lc + 1) * LANES)
        for rc in range(tm // rows):
            r0 = rc * rows
            acc = jnp.zeros((rows, LANES), F32)
            for t in range(CONV_KERNEL):
                off = base + t
                a0 = r0 + SUBLANES * (off // SUBLANES)
                acc = acc + sh_ref[off % SUBLANES, a0:a0 + rows, ls] * w_ref[t:t + 1, ls]
            y_ref[r0:r0 + rows, ls] = acc
            done.append(_dependent_zero(acc))
    y = _layer_norm(y_ref[...] + cb_ref[...], g_ref[...], b_ref[...])
    y = y * _sigmoid(y)
    o_ref[...] = y.astype(BF16)
    done.append(_dependent_zero(y))
    return done


def _conv_specs(s, cw, tm, tile_index):
    nh = tm // CONV_HALO
    last = s // CONV_HALO - 1
    return [pl.BlockSpec((CONV_HALO, cw), lambda *g: (jnp.maximum(tile_index(*g) * nh - 1, 0), 0)),
            pl.BlockSpec((tm, cw), lambda *g: (tile_index(*g), 0)),
            pl.BlockSpec((CONV_HALO, cw), lambda *g: (jnp.minimum((tile_index(*g) + 1) * nh, last), 0))]


def _conv_scratch(tm, cw):
    return [pltpu.VMEM((tm + 2 * CONV_HALO, cw), F32),
            pltpu.VMEM((SUBLANES, tm + 2 * CONV_HALO - SUBLANES, cw), F32),
            pltpu.VMEM((tm, cw), F32)]


def _mixer_kernel(qt_ref, k_ref, vt_ref, kn_ref, lq1_ref, lk1_ref, lq2_ref, lk2_ref, g_ref,
                  up_ref, uc_ref, un_ref, cw_ref, cb_ref, cg_ref, cbeta_ref,
                  o_ref, conv_ref,
                  rhs_ref, acc_ref, l_ref, kmax_ref, ext_ref, sh_ref, y_ref, *, tk, conv_rows, lam_init):
    step = pl.program_id(0) * pl.num_programs(1) + pl.program_id(1)
    n_steps = pl.num_programs(0) * pl.num_programs(1)

    hd2, tq = qt_ref.shape[1], qt_ref.shape[2]
    n = 2 * tq
    n_kv = k_ref.shape[0] // tk
    qt = qt_ref[0]
    row = lax.broadcasted_iota(jnp.int32, qt.shape, 0)
    zero = jnp.zeros_like(qt)
    rhs_ref[:, :tq] = jnp.where(row < DIFF_HEAD_DIM, qt, zero)
    rhs_ref[:, tq:] = jnp.where(row >= DIFF_HEAD_DIM, qt, zero)

    def k_block(j):
        return k_ref[pl.ds(pl.multiple_of(j * tk, tk), tk), :]

    @pl.when(pl.program_id(1) == 0)
    def _key_norm_bound():
        kn = kn_ref[pl.ds(pl.program_id(0), 1), :]
        col = lax.broadcasted_iota(jnp.int32, (1, n), 1)
        kmax_ref[...] = jnp.sqrt(jnp.where(col < tq, kn[:, 0:1], kn[:, DIFF_HEAD_DIM:DIFF_HEAD_DIM + 1]))

    conv_done = _conv_tile(step, n_steps, up_ref, uc_ref, un_ref, cw_ref, cb_ref, cg_ref, cbeta_ref, conv_ref,
                           ext_ref, sh_ref, y_ref, conv_rows)
    assert CONV_FIRST_BLOCK + pl.cdiv(len(conv_done), CONV_PIECES_PER_BLOCK) <= n_kv

    r32 = rhs_ref[...].astype(F32)
    qn = jnp.sqrt(jnp.sum(r32 * r32, axis=0, keepdims=True))
    m = qn * kmax_ref[...] * SHIFT_SLACK
    l8 = jnp.zeros((8, n), F32)
    acc = jnp.zeros((hd2, n), F32)
    e_prev = None
    for j in range(n_kv):
        s = jnp.dot(k_ref[j * tk:(j + 1) * tk, :], rhs_ref[...], preferred_element_type=F32)
        if e_prev is not None:
            acc = acc + jnp.dot(vt_ref[0, j - 1], e_prev, preferred_element_type=F32)
        if j >= CONV_FIRST_BLOCK:
            for piece in conv_done[:CONV_PIECES_PER_BLOCK]:
                m = m + jnp.concatenate([piece] * (n // LANES), axis=1)
            del conv_done[:CONV_PIECES_PER_BLOCK]
        e = jnp.exp2(s - m)
        l8 = l8 + jnp.sum(e.reshape(tk // 8, 8, n), axis=0)
        e_prev = e.astype(BF16)
    acc = acc + jnp.dot(vt_ref[0, n_kv - 1], e_prev, preferred_element_type=F32)
    l = jnp.sum(l8, axis=0, keepdims=True)
    acc_ref[...] = acc
    l_ref[...] = l

    @pl.when(jnp.logical_not(jnp.min(l) >= L_FLOOR))
    def _running_max_fallback():
        acc_ref[...] = jnp.zeros_like(acc_ref)

        def body(j, carry):
            m_run, l_run = carry
            s = jnp.dot(k_block(j), rhs_ref[...], preferred_element_type=F32)
            m_new = jnp.maximum(m_run, jnp.max(s, axis=0, keepdims=True))
            alpha = jnp.exp2(m_run - m_new)
            e = jnp.exp2(s - m_new)
            pv = jnp.dot(vt_ref[0, j], e.astype(BF16), preferred_element_type=F32)
            acc_ref[...] = alpha * acc_ref[...] + pv
            return m_new, alpha * l_run + jnp.sum(e, axis=0, keepdims=True)

        init = (jnp.full((1, n), -jnp.inf, F32), jnp.zeros((1, n), F32))
        _, l_run = lax.fori_loop(0, n_kv, body, init)
        l_ref[...] = l_run

    o = acc_ref[...] * (1.0 / l_ref[...])
    lam = (jnp.exp(jnp.sum(lq1_ref[...] * lk1_ref[...])) - jnp.exp(jnp.sum(lq2_ref[...] * lk2_ref[...]))
           + lam_init)
    o = o[:, :tq] - lam * o[:, tq:]
    ms = jnp.mean(o * o, axis=0, keepdims=True)
    o = o * lax.rsqrt(ms + LN_EPS) * g_ref[...] * (1.0 - lam_init)
    o_ref[...] = o.T.astype(BF16)


def _mixer(qt, k, vt, kn, lq1, lk1, lq2, lk2, g_col, u, conv_w, conv_b, conv_g, conv_beta, *, tq, tk, conv_rows,
           lam_init):
    n_heads, hd2, s = qt.shape
    cw = u.shape[1]
    nq = s // tq
    tc = s // (n_heads * nq)
    assert tc % conv_rows == 0 and tc % CONV_HALO == 0
    lam_spec = _const_spec(lq1.shape)
    tile = lambda h, i: h * nq + i
    return pl.pallas_call(
        functools.partial(_mixer_kernel, tk=tk, conv_rows=conv_rows, lam_init=lam_init),
        grid=(n_heads, nq),
        in_specs=[pl.BlockSpec((1, hd2, tq), lambda h, i: (h, 0, i)),
                  pl.BlockSpec((s, hd2), lambda h, i: (0, h)),
                  pl.BlockSpec((1, s // tk, hd2, tk), lambda h, i: (h, 0, 0, 0)),
                  _const_spec(kn.shape),
                  lam_spec, lam_spec, lam_spec, lam_spec,
                  _const_spec(g_col.shape),
                  *_conv_specs(s, cw, tc, tile),
                  _const_spec(conv_w.shape),
                  _const_spec(conv_b.shape),
                  _const_spec(conv_g.shape),
                  _const_spec(conv_beta.shape)],
        out_specs=[pl.BlockSpec((tq, hd2), lambda h, i: (i, h)),
                   pl.BlockSpec((tc, cw), lambda h, i: (tile(h, i), 0))],
        out_shape=[jax.ShapeDtypeStruct((s, n_heads * hd2), BF16),
                   jax.ShapeDtypeStruct((s, cw), BF16)],
        scratch_shapes=[pltpu.VMEM((hd2, 2 * tq), BF16),
                        pltpu.VMEM((hd2, 2 * tq), F32),
                        pltpu.VMEM((1, 2 * tq), F32),
                        pltpu.VMEM((1, 2 * tq), F32),
                        *_conv_scratch(tc, cw)],
        compiler_params=pltpu.CompilerParams(dimension_semantics=("arbitrary", "arbitrary"),
                                             vmem_limit_bytes=VMEM_LIMIT),
        name="mixer",
    )(qt, k, vt, kn, lq1, lk1, lq2, lk2, g_col, u, u, u, conv_w, conv_b, conv_g, conv_beta)


def _mix_out_kernel(conv_ref, attn_ref, x_ref, mod_ref, w_ref, g_ref, b_ref, o_ref, *, sub, n_part):
    cw = conv_ref.shape[1]
    tp = x_ref.shape[0] // n_part
    gate_c = mod_ref[3 * sub + 2:3 * sub + 3, :]
    ys = []
    for p in range(n_part):
        rows = slice(p * tp, (p + 1) * tp)
        ys.append(jnp.dot(conv_ref[rows, :], w_ref[:cw, :], preferred_element_type=F32)
                  + jnp.dot(attn_ref[rows, :], w_ref[cw:, :], preferred_element_type=F32))
    for p in range(n_part):
        rows = slice(p * tp, (p + 1) * tp)
        z = ALPHA * x_ref[rows, :] + (1.0 + gate_c) * ys[p]
        o_ref[rows, :] = _layer_norm(z, g_ref[...], b_ref[...])


def _mix_out(conv, attn, x, mod9, w, g, b, *, tm, n_part):
    s, d = x.shape
    row = lambda i: (i, 0)
    return pl.pallas_call(
        functools.partial(_mix_out_kernel, sub=1, n_part=n_part),
        grid=(s // tm,),
        in_specs=[pl.BlockSpec((tm, conv.shape[1]), row),
                  pl.BlockSpec((tm, attn.shape[1]), row),
                  pl.BlockSpec((tm, d), row),
                  _const_spec(mod9.shape),
                  _const_spec(w.shape),
                  _const_spec(g.shape),
                  _const_spec(b.shape)],
        out_specs=pl.BlockSpec((tm, d), row),
        out_shape=jax.ShapeDtypeStruct((s, d), F32),
        compiler_params=pltpu.CompilerParams(dimension_semantics=("arbitrary",),
                                             vmem_limit_bytes=VMEM_LIMIT),
        name="mix_out",
    )(conv, attn, x, mod9, w, g, b)


def _rope_rows():
    inv_freq = ROPE_THETA ** (-jnp.arange(0, ROT_DIM, 2, dtype=F32) / ROT_DIM)
    half = ROT_DIM // 2
    zeros_h = jnp.zeros((half,), F32)
    zeros_p = jnp.zeros((DIFF_HEAD_DIM - ROT_DIM,), F32)
    ones_h = jnp.ones((half,), F32)
    reps = LANES // DIFF_HEAD_DIM
    freq = jnp.tile(jnp.concatenate([inv_freq, inv_freq, zeros_p]), reps)
    neg_first = jnp.tile(jnp.concatenate([-ones_h, zeros_h, zeros_p]), reps)
    pos_second = jnp.tile(jnp.concatenate([zeros_h, ones_h, zeros_p]), reps)
    return jnp.stack([freq, neg_first, pos_second])


def kernel(x, c, w_ada, b_ada, ffn1_w_in, ffn1_w_out, ln1_g, ln1_b, mix_w_in, conv_w, conv_b, conv_ln_g,
           conv_ln_b, lambda_q1, lambda_k1, lambda_q2, lambda_k2, subln_g, mix_w_out, ln2_g, ln2_b,
           ffn2_w_in, ffn2_w_out, ln3_g, ln3_b):
    batch, s, d = x.shape
    assert batch == 1 and w_ada.shape[0] == DEPTH == 1
    cw = conv_w.shape[2]
    aw = (mix_w_in.shape[2] - 2 * cw) // 3
    hd2 = 2 * DIFF_HEAD_DIM
    n_heads = aw // hd2
    lam_init = 0.8 - 0.6 * math.exp(-0.3 * 0)
    q_scale = math.log2(math.e) / math.sqrt(DIFF_HEAD_DIM)
    t = TILES
    for rows in (t.ffn_rows, t.proj_rows, t.attn_q, t.attn_kv):
        assert s % rows == 0

    mod9 = _ada(c.reshape(d, 1), w_ada[0], b_ada, tn=t.ada_cols).reshape(9, d)
    x0 = x[0]
    x1 = _ffn(x0, mod9, ffn1_w_in[0], ffn1_w_out[0], ln1_g, ln1_b,
              sub=0, weight=0.5, tm=t.ffn_rows, tf=t.ffn_cols)

    u, qt, k, vt, kn = _mix_in(x1, mod9, mix_w_in[0], _rope_rows(),
                               cw=cw, aw=aw, q_scale=q_scale, tm=t.proj_rows, tk=t.attn_kv, n_part=t.proj_parts)
    attn, conv = _mixer(qt, k, vt, kn, lambda_q1, lambda_k1, lambda_q2, lambda_k2, subln_g.reshape(hd2, 1),
                        u, conv_w[0], conv_b, conv_ln_g, conv_ln_b,
                        tq=t.attn_q, tk=t.attn_kv, conv_rows=t.conv_rows, lam_init=lam_init)
    x2 = _mix_out(conv, attn, x1, mod9, mix_w_out[0].astype(BF16), ln2_g, ln2_b,
                  tm=t.proj_rows, n_part=t.proj_parts)

    x3 = _ffn(x2, mod9, ffn2_w_in[0], ffn2_w_out[0], ln3_g, ln3_b,
              sub=2, weight=0.5, tm=t.ffn_rows, tf=t.ffn_cols)
    return x3[None]
```

```python
import functools
import math
from typing import NamedTuple

import jax
import jax.numpy as jnp
from jax import lax
from jax.experimental import pallas as pl
from jax.experimental.pallas import tpu as pltpu

F32 = jnp.float32
BF16 = jnp.bfloat16

DEPTH = 1
ALPHA = (2.0 * DEPTH) ** 0.25
LN_EPS = 1e-5
DIFF_HEAD_DIM = 64
ROT_DIM = DIFF_HEAD_DIM // 4
ROPE_THETA = 500000.0
CONV_KERNEL = 31
CONV_PAD = (CONV_KERNEL - 1) // 2
CONV_HALO = 16
LANES = 128
SUBLANES = 8
SHIFT_SLACK = 1.0 + 2.0 ** -10
L_FLOOR = 2.0 ** -80
CONV_FIRST_BLOCK = 28
CONV_PIECES_PER_BLOCK = 1
VMEM_LIMIT = 56 * 1024 * 1024


class _Tiles(NamedTuple):
    ada_cols: int = 1152
    ffn_rows: int = 512
    ffn_cols: int = 256
    proj_rows: int = 1024
    proj_parts: int = 4
    attn_q: int = 512
    attn_kv: int = 256
    conv_rows: int = 16


TILES = _Tiles()


def _sigmoid(x):
    return 1.0 / (1.0 + jnp.exp(-x))


def _layer_norm(z, g, b):
    mu = jnp.mean(z, axis=-1, keepdims=True)
    zc = z - mu
    var = jnp.mean(zc * zc, axis=-1, keepdims=True)
    return zc * lax.rsqrt(var + LN_EPS) * g + b


def _modulate(x, mod_ref, sub):
    shift = mod_ref[3 * sub:3 * sub + 1, :]
    scale = mod_ref[3 * sub + 1:3 * sub + 2, :]
    return x * (1.0 + scale) + shift


def _const_spec(shape):
    return pl.BlockSpec(shape, lambda *_: (0,) * len(shape), pipeline_mode=pl.Buffered(1))


def _ada_kernel(c_ref, w_ref, b_ref, o_ref):
    c = c_ref[...]
    ca = c * _sigmoid(c)
    o_ref[...] = jnp.sum(ca * w_ref[...], axis=0, keepdims=True) + b_ref[...]


def _ada(c_col, w, b_row, tn):
    d, n = w.shape
    assert n % tn == 0 and tn % LANES == 0
    return pl.pallas_call(
        _ada_kernel,
        grid=(n // tn,),
        in_specs=[pl.BlockSpec((d, 1), lambda j: (0, 0)),
                  pl.BlockSpec((d, tn), lambda j: (0, j)),
                  pl.BlockSpec((1, tn), lambda j: (0, j))],
        out_specs=pl.BlockSpec((1, tn), lambda j: (0, j)),
        out_shape=jax.ShapeDtypeStruct((1, n), F32),
        compiler_params=pltpu.CompilerParams(dimension_semantics=("arbitrary",),
                                             vmem_limit_bytes=VMEM_LIMIT),
        name="ada",
    )(c_col, w, b_row)


def _ffn_kernel(x_ref, mod_ref, win_ref, wout_ref, g_ref, b_ref, o_ref, act_ref, *, sub, weight, tf):
    x = x_ref[...]
    d_ff = wout_ref.shape[0]
    h = _modulate(x, mod_ref, sub).astype(win_ref.dtype)
    for c in range(d_ff // tf):
        gate = jnp.dot(h, win_ref[:, c * tf:(c + 1) * tf], preferred_element_type=F32)
        up = jnp.dot(h, win_ref[:, d_ff + c * tf:d_ff + (c + 1) * tf], preferred_element_type=F32)
        act_ref[:, c * tf:(c + 1) * tf] = (gate * _sigmoid(gate) * up).astype(act_ref.dtype)
    y = jnp.dot(act_ref[...], wout_ref[...], preferred_element_type=F32)
    gate_c = mod_ref[3 * sub + 2:3 * sub + 3, :]
    z = ALPHA * x + weight * (1.0 + gate_c) * y
    o_ref[...] = _layer_norm(z, g_ref[...], b_ref[...])


def _ffn(x, mod9, w_in, w_out, g, b, *, sub, weight, tm, tf):
    s, d = x.shape
    d_ff = w_out.shape[0]
    assert s % tm == 0 and d_ff % tf == 0 and w_in.shape == (d, 2 * d_ff)
    return pl.pallas_call(
        functools.partial(_ffn_kernel, sub=sub, weight=weight, tf=tf),
        grid=(s // tm,),
        in_specs=[pl.BlockSpec((tm, d), lambda i: (i, 0)),
                  _const_spec(mod9.shape),
                  _const_spec(w_in.shape),
                  _const_spec(w_out.shape),
                  _const_spec(g.shape),
                  _const_spec(b.shape)],
        out_specs=pl.BlockSpec((tm, d), lambda i: (i, 0)),
        out_shape=jax.ShapeDtypeStruct((s, d), F32),
        scratch_shapes=[pltpu.VMEM((tm, d_ff), w_out.dtype)],
        compiler_params=pltpu.CompilerParams(dimension_semantics=("arbitrary",),
                                             vmem_limit_bytes=VMEM_LIMIT),
        name=f"ffn{sub}",
    )(x, mod9, w_in, w_out, g, b)


def _mix_in_kernel(x_ref, mod_ref, w_ref, rope_ref, u_ref, qt_ref, k_ref, vt_ref, kn_ref, cos_ref, sin_ref, *,
                   sub, cw, aw, q_scale, n_part):
    tm = x_ref.shape[0]
    tp = tm // n_part
    tk = vt_ref.shape[3]
    freq = rope_ref[0:1, :]

    @pl.when(pl.program_id(0) == 0)
    def _in_tile_angles():
        ang = lax.broadcasted_iota(jnp.int32, (tm, LANES), 0).astype(F32) * freq
        cos_ref[...] = jnp.cos(ang)
        sin_ref[...] = jnp.sin(ang)
        kn_ref[...] = jnp.zeros_like(kn_ref)

    comp0 = lax.broadcasted_iota(jnp.int32, (tp, LANES), 1) < DIFF_HEAD_DIM
    projs = []
    for p in range(n_part):
        h = _modulate(x_ref[p * tp:(p + 1) * tp, :], mod_ref, sub).astype(w_ref.dtype)
        projs.append(jnp.dot(h, w_ref[...], preferred_element_type=F32))

    ang0 = (pl.program_id(0) * tm).astype(F32) * freq
    c0, s0 = jnp.cos(ang0), jnp.sin(ang0)
    half = ROT_DIM // 2
    q0, k0, v0 = 2 * cw, 2 * cw + aw, 2 * cw + 2 * aw
    for p, proj in enumerate(projs):
        rows = slice(p * tp, (p + 1) * tp)
        u_ref[rows, :] = proj[:, :cw] * _sigmoid(proj[:, cw:2 * cw])
        rc = c0 * cos_ref[rows, :] - s0 * sin_ref[rows, :]
        sin = s0 * cos_ref[rows, :] + c0 * sin_ref[rows, :]
        rs1, rs2 = sin * rope_ref[1:2, :], sin * rope_ref[2:3, :]

        def rope(t):
            return t * rc + pltpu.roll(t, LANES - half, 1) * rs1 + pltpu.roll(t, half, 1) * rs2

        for g in range(aw // LANES):
            sl = slice(g * LANES, (g + 1) * LANES)
            qt_ref[g, :, rows] = (rope(proj[:, q0 + g * LANES:q0 + (g + 1) * LANES]) * q_scale).T.astype(BF16)
            kb = rope(proj[:, k0 + g * LANES:k0 + (g + 1) * LANES]).astype(BF16)
            k_ref[rows, sl] = kb
            kf = kb.astype(F32)
            sq = kf * kf
            n0 = jnp.max(jnp.sum(jnp.where(comp0, sq, 0.0), axis=1, keepdims=True), axis=0, keepdims=True)
            n1 = jnp.max(jnp.sum(jnp.where(comp0, 0.0, sq), axis=1, keepdims=True), axis=0, keepdims=True)
            kn_ref[g:g + 1, :] = jnp.maximum(kn_ref[g:g + 1, :], jnp.where(comp0[0:1, :], n0, n1))
            for c in range(tp // tk):
                vt_ref[g, p * (tp // tk) + c] = (
                    proj[c * tk:(c + 1) * tk, v0 + g * LANES:v0 + (g + 1) * LANES].T.astype(BF16))


def _mix_in(x, mod9, w, rope_rows, *, cw, aw, q_scale, tm, tk, n_part):
    s, d = x.shape
    n_heads = aw // LANES
    assert (tm // n_part) % tk == 0 and n_heads <= SUBLANES
    row = lambda i: (i, 0)
    return pl.pallas_call(
        functools.partial(_mix_in_kernel, sub=1, cw=cw, aw=aw, q_scale=q_scale, n_part=n_part),
        grid=(s // tm,),
        in_specs=[pl.BlockSpec((tm, d), row),
                  _const_spec(mod9.shape),
                  _const_spec(w.shape),
                  _const_spec(rope_rows.shape)],
        out_specs=[pl.BlockSpec((tm, cw), row),
                   pl.BlockSpec((n_heads, LANES, tm), lambda i: (0, 0, i)),
                   pl.BlockSpec((tm, aw), row),
                   pl.BlockSpec((n_heads, tm // tk, LANES, tk), lambda i: (0, i, 0, 0)),
                   pl.BlockSpec((SUBLANES, LANES), lambda i: (0, 0))],
        out_shape=[jax.ShapeDtypeStruct((s, cw), F32),
                   jax.ShapeDtypeStruct((n_heads, LANES, s), BF16),
                   jax.ShapeDtypeStruct((s, aw), BF16),
                   jax.ShapeDtypeStruct((n_heads, s // tk, LANES, tk), BF16),
                   jax.ShapeDtypeStruct((SUBLANES, LANES), F32)],
        scratch_shapes=[pltpu.VMEM((tm, LANES), F32),
                        pltpu.VMEM((tm, LANES), F32)],
        compiler_params=pltpu.CompilerParams(dimension_semantics=("arbitrary",),
                                             vmem_limit_bytes=VMEM_LIMIT),
        name="mix_in",
    )(x, mod9, w, rope_rows)


def _dependent_zero(v):
    r, c = v.shape
    folded = jnp.sum(v.reshape(r // SUBLANES, SUBLANES, c), axis=0)
    folded = sum(folded[:, g * LANES:(g + 1) * LANES] for g in range(c // LANES))
    bits = lax.bitcast_convert_type(folded[0:1, :], jnp.uint32)
    return lax.bitcast_convert_type((bits >> 16) >> 16, F32)


def _conv_tile(i, n_tiles, prev_ref, cur_ref, next_ref, w_ref, cb_ref, g_ref, b_ref, o_ref, ext_ref, sh_ref, y_ref,
               rows):
    tm, cw = cur_ref.shape
    ext_ref[0:CONV_HALO, :] = jnp.where(i > 0, prev_ref[...], 0.0)
    ext_ref[CONV_HALO:CONV_HALO + tm, :] = cur_ref[...]
    ext_ref[CONV_HALO + tm:, :] = jnp.where(i < n_tiles - 1, next_ref[...], 0.0)
    span = sh_ref.shape[1]
    for b in range(SUBLANES):
        sh_ref[b] = ext_ref[b:b + span, :]
    base = CONV_HALO - CONV_PAD
    done = []
    for lc in range(cw // LANES):
        ls = slice(lc * LANES, (lc + 1) * LANES)
        for rc in range(tm // rows):
            r0 = rc * rows
            acc = jnp.zeros((rows, LANES), F32)
            for t in range(CONV_KERNEL):
                off = base + t
                a0 = r0 + SUBLANES * (off // SUBLANES)
                acc = acc + sh_ref[off % SUBLANES, a0:a0 + rows, ls] * w_ref[t:t + 1, ls]
            y_ref[r0:r0 + rows, ls] = acc
            done.append(_dependent_zero(acc))
    y = _layer_norm(y_ref[...] + cb_ref[...], g_ref[...], b_ref[...])
    y = y * _sigmoid(y)
    o_ref[...] = y.astype(BF16)
    done.append(_dependent_zero(y))
    return done


def _conv_specs(s, cw, tm, tile_index):
    nh = tm // CONV_HALO
    last = s // CONV_HALO - 1
    return [pl.BlockSpec((CONV_HALO, cw), lambda *g: (jnp.maximum(tile_index(*g) * nh - 1, 0), 0)),
            pl.BlockSpec((tm, cw), lambda *g: (tile_index(*g), 0)),
            pl.BlockSpec((CONV_HALO, cw), lambda *g: (jnp.minimum((tile_index(*g) + 1) * nh, last), 0))]


def _conv_scratch(tm, cw):
    return [pltpu.VMEM((tm + 2 * CONV_HALO, cw), F32),
            pltpu.VMEM((SUBLANES, tm + 2 * CONV_HALO - SUBLANES, cw), F32),
            pltpu.VMEM((tm, cw), F32)]


def _mixer_kernel(qt_ref, k_ref, vt_ref, kn_ref, lq1_ref, lk1_ref, lq2_ref, lk2_ref, g_ref,
                  up_ref, uc_ref, un_ref, cw_ref, cb_ref, cg_ref, cbeta_ref,
                  o_ref, conv_ref,
                  rhs_ref, acc_ref, l_ref, kmax_ref, ext_ref, sh_ref, y_ref, *, tk, conv_rows, lam_init):
    step = pl.program_id(0) * pl.num_programs(1) + pl.program_id(1)
    n_steps = pl.num_programs(0) * pl.num_programs(1)

    hd2, tq = qt_ref.shape[1], qt_ref.shape[2]
    n = 2 * tq
    n_kv = k_ref.shape[0] // tk
    qt = qt_ref[0]
    row = lax.broadcasted_iota(jnp.int32, qt.shape, 0)
    zero = jnp.zeros_like(qt)
    rhs_ref[:, :tq] = jnp.where(row < DIFF_HEAD_DIM, qt, zero)
    rhs_ref[:, tq:] = jnp.where(row >= DIFF_HEAD_DIM, qt, zero)

    def k_block(j):
        return k_ref[pl.ds(pl.multiple_of(j * tk, tk), tk), :]

    @pl.when(pl.program_id(1) == 0)
    def _key_norm_bound():
        kn = kn_ref[pl.ds(pl.program_id(0), 1), :]
        col = lax.broadcasted_iota(jnp.int32, (1, n), 1)
        kmax_ref[...] = jnp.sqrt(jnp.where(col < tq, kn[:, 0:1], kn[:, DIFF_HEAD_DIM:DIFF_HEAD_DIM + 1]))

    conv_done = _conv_tile(step, n_steps, up_ref, uc_ref, un_ref, cw_ref, cb_ref, cg_ref, cbeta_ref, conv_ref,
                           ext_ref, sh_ref, y_ref, conv_rows)
    assert CONV_FIRST_BLOCK + pl.cdiv(len(conv_done), CONV_PIECES_PER_BLOCK) <= n_kv

    r32 = rhs_ref[...].astype(F32)
    qn = jnp.sqrt(jnp.sum(r32 * r32, axis=0, keepdims=True))
    m = qn * kmax_ref[...] * SHIFT_SLACK
    l8 = jnp.zeros((8, n), F32)
    acc = jnp.zeros((hd2, n), F32)
    e_prev = None
    for j in range(n_kv):
        s = jnp.dot(k_ref[j * tk:(j + 1) * tk, :], rhs_ref[...], preferred_element_type=F32)
        if e_prev is not None:
            acc = acc + jnp.dot(vt_ref[0, j - 1], e_prev, preferred_element_type=F32)
        if j >= CONV_FIRST_BLOCK:
            for piece in conv_done[:CONV_PIECES_PER_BLOCK]:
                m = m + jnp.concatenate([piece] * (n // LANES), axis=1)
            del conv_done[:CONV_PIECES_PER_BLOCK]
        e = jnp.exp2(s - m)
        l8 = l8 + jnp.sum(e.reshape(tk // 8, 8, n), axis=0)
        e_prev = e.astype(BF16)
    acc = acc + jnp.dot(vt_ref[0, n_kv - 1], e_prev, preferred_element_type=F32)
    l = jnp.sum(l8, axis=0, keepdims=True)
    acc_ref[...] = acc
    l_ref[...] = l

    @pl.when(jnp.logical_not(jnp.min(l) >= L_FLOOR))
    def _running_max_fallback():
        acc_ref[...] = jnp.zeros_like(acc_ref)

        def body(j, carry):
            m_run, l_run = carry
            s = jnp.dot(k_block(j), rhs_ref[...], preferred_element_type=F32)
            m_new = jnp.maximum(m_run, jnp.max(s, axis=0, keepdims=True))
            alpha = jnp.exp2(m_run - m_new)
            e = jnp.exp2(s - m_new)
            pv = jnp.dot(vt_ref[0, j], e.astype(BF16), preferred_element_type=F32)
            acc_ref[...] = alpha * acc_ref[...] + pv
            return m_new, alpha * l_run + jnp.sum(e, axis=0, keepdims=True)

        init = (jnp.full((1, n), -jnp.inf, F32), jnp.zeros((1, n), F32))
        _, l_run = lax.fori_loop(0, n_kv, body, init)
        l_ref[...] = l_run

    o = acc_ref[...] * (1.0 / l_ref[...])
    lam = (jnp.exp(jnp.sum(lq1_ref[...] * lk1_ref[...])) - jnp.exp(jnp.sum(lq2_ref[...] * lk2_ref[...]))
           + lam_init)
    o = o[:, :tq] - lam * o[:, tq:]
    ms = jnp.mean(o * o, axis=0, keepdims=True)
    o = o * lax.rsqrt(ms + LN_EPS) * g_ref[...] * (1.0 - lam_init)
    o_ref[...] = o.T.astype(BF16)


def _mixer(qt, k, vt, kn, lq1, lk1, lq2, lk2, g_col, u, conv_w, conv_b, conv_g, conv_beta, *, tq, tk, conv_rows,
           lam_init):
    n_heads, hd2, s = qt.shape
    cw = u.shape[1]
    nq = s // tq
    tc = s // (n_heads * nq)
    assert tc % conv_rows == 0 and tc % CONV_HALO == 0
    lam_spec = _const_spec(lq1.shape)
    tile = lambda h, i: h * nq + i
    return pl.pallas_call(
        functools.partial(_mixer_kernel, tk=tk, conv_rows=conv_rows, lam_init=lam_init),
        grid=(n_heads, nq),
        in_specs=[pl.BlockSpec((1, hd2, tq), lambda h, i: (h, 0, i)),
                  pl.BlockSpec((s, hd2), lambda h, i: (0, h)),
                  pl.BlockSpec((1, s // tk, hd2, tk), lambda h, i: (h, 0, 0, 0)),
                  _const_spec(kn.shape),
                  lam_spec, lam_spec, lam_spec, lam_spec,
                  _const_spec(g_col.shape),
                  *_conv_specs(s, cw, tc, tile),
                  _const_spec(conv_w.shape),
                  _const_spec(conv_b.shape),
                  _const_spec(conv_g.shape),
                  _const_spec(conv_beta.shape)],
        out_specs=[pl.BlockSpec((tq, hd2), lambda h, i: (i, h)),
                   pl.BlockSpec((tc, cw), lambda h, i: (tile(h, i), 0))],
        out_shape=[jax.ShapeDtypeStruct((s, n_heads * hd2), BF16),
                   jax.ShapeDtypeStruct((s, cw), BF16)],
        scratch_shapes=[pltpu.VMEM((hd2, 2 * tq), BF16),
                        pltpu.VMEM((hd2, 2 * tq), F32),
                        pltpu.VMEM((1, 2 * tq), F32),
                        pltpu.VMEM((1, 2 * tq), F32),
                        *_conv_scratch(tc, cw)],
        compiler_params=pltpu.CompilerParams(dimension_semantics=("arbitrary", "arbitrary"),
                                             vmem_limit_bytes=VMEM_LIMIT),
        name="mixer",
    )(qt, k, vt, kn, lq1, lk1, lq2, lk2, g_col, u, u, u, conv_w, conv_b, conv_g, conv_beta)


def _mix_out_kernel(conv_ref, attn_ref, x_ref, mod_ref, w_ref, g_ref, b_ref, o_ref, *, sub, n_part):
    cw = conv_ref.shape[1]
    tp = x_ref.shape[0] // n_part
    gate_c = mod_ref[3 * sub + 2:3 * sub + 3, :]
    ys = []
    for p in range(n_part):
        rows = slice(p * tp, (p + 1) * tp)
        ys.append(jnp.dot(conv_ref[rows, :], w_ref[:cw, :], preferred_element_type=F32)
                  + jnp.dot(attn_ref[rows, :], w_ref[cw:, :], preferred_element_type=F32))
    for p in range(n_part):
        rows = slice(p * tp, (p + 1) * tp)
        z = ALPHA * x_ref[rows, :] + (1.0 + gate_c) * ys[p]
        o_ref[rows, :] = _layer_norm(z, g_ref[...], b_ref[...])


def _mix_out(conv, attn, x, mod9, w, g, b, *, tm, n_part):
    s, d = x.shape
    row = lambda i: (i, 0)
    return pl.pallas_call(
        functools.partial(_mix_out_kernel, sub=1, n_part=n_part),
        grid=(s // tm,),
        in_specs=[pl.BlockSpec((tm, conv.shape[1]), row),
                  pl.BlockSpec((tm, attn.shape[1]), row),
                  pl.BlockSpec((tm, d), row),
                  _const_spec(mod9.shape),
                  _const_spec(w.shape),
                  _const_spec(g.shape),
                  _const_spec(b.shape)],
        out_specs=pl.BlockSpec((tm, d), row),
        out_shape=jax.ShapeDtypeStruct((s, d), F32),
        compiler_params=pltpu.CompilerParams(dimension_semantics=("arbitrary",),
                                             vmem_limit_bytes=VMEM_LIMIT),
        name="mix_out",
    )(conv, attn, x, mod9, w, g, b)


def _rope_rows():
    inv_freq = ROPE_THETA ** (-jnp.arange(0, ROT_DIM, 2, dtype=F32) / ROT_DIM)
    half = ROT_DIM // 2
    zeros_h = jnp.zeros((half,), F32)
    zeros_p = jnp.zeros((DIFF_HEAD_DIM - ROT_DIM,), F32)
    ones_h = jnp.ones((half,), F32)
    reps = LANES // DIFF_HEAD_DIM
    freq = jnp.tile(jnp.concatenate([inv_freq, inv_freq, zeros_p]), reps)
    neg_first = jnp.tile(jnp.concatenate([-ones_h, zeros_h, zeros_p]), reps)
    pos_second = jnp.tile(jnp.concatenate([zeros_h, ones_h, zeros_p]), reps)
    return jnp.stack([freq, neg_first, pos_second])


def kernel(x, c, w_ada, b_ada, ffn1_w_in, ffn1_w_out, ln1_g, ln1_b, mix_w_in, conv_w, conv_b, conv_ln_g,
           conv_ln_b, lambda_q1, lambda_k1, lambda_q2, lambda_k2, subln_g, mix_w_out, ln2_g, ln2_b,
           ffn2_w_in, ffn2_w_out, ln3_g, ln3_b):
    batch, s, d = x.shape
    assert batch == 1 and w_ada.shape[0] == DEPTH == 1
    cw = conv_w.shape[2]
    aw = (mix_w_in.shape[2] - 2 * cw) // 3
    hd2 = 2 * DIFF_HEAD_DIM
    n_heads = aw // hd2
    lam_init = 0.8 - 0.6 * math.exp(-0.3 * 0)
    q_scale = math.log2(math.e) / math.sqrt(DIFF_HEAD_DIM)
    t = TILES
    for rows in (t.ffn_rows, t.proj_rows, t.attn_q, t.attn_kv):
        assert s % rows == 0

    mod9 = _ada(c.reshape(d, 1), w_ada[0], b_ada, tn=t.ada_cols).reshape(9, d)
    x0 = x[0]
    x1 = _ffn(x0, mod9, ffn1_w_in[0], ffn1_w_out[0], ln1_g, ln1_b,
              sub=0, weight=0.5, tm=t.ffn_rows, tf=t.ffn_cols)

    u, qt, k, vt, kn = _mix_in(x1, mod9, mix_w_in[0], _rope_rows(),
                               cw=cw, aw=aw, q_scale=q_scale, tm=t.proj_rows, tk=t.attn_kv, n_part=t.proj_parts)
    attn, conv = _mixer(qt, k, vt, kn, lambda_q1, lambda_k1, lambda_q2, lambda_k2, subln_g.reshape(hd2, 1),
                        u, conv_w[0], conv_b, conv_ln_g, conv_ln_b,
                        tq=t.attn_q, tk=t.attn_kv, conv_rows=t.conv_rows, lam_init=lam_init)
    x2 = _mix_out(conv, attn, x1, mod9, mix_w_out[0].astype(BF16), ln2_g, ln2_b,
                  tm=t.proj_rows, n_part=t.proj_parts)

    x3 = _ffn(x2, mod9, ffn2_w_in[0], ffn2_w_out[0], ln3_g, ln3_b,
              sub=2, weight=0.5, tm=t.ffn_rows, tf=t.ffn_cols)
    return x3[None]
```

```python
import functools
import math
from typing import NamedTuple

import jax
import jax.numpy as jnp
from jax import lax
from jax.experimental import pallas as pl
from jax.experimental.pallas import tpu as pltpu

F32 = jnp.float32
BF16 = jnp.bfloat16

DEPTH = 1
ALPHA = (2.0 * DEPTH) ** 0.25
LN_EPS = 1e-5
DIFF_HEAD_DIM = 64
ROT_DIM = DIFF_HEAD_DIM // 4
ROPE_THETA = 500000.0
CONV_KERNEL = 31
CONV_PAD = (CONV_KERNEL - 1) // 2
CONV_HALO = 16
LANES = 128
SUBLANES = 8
SHIFT_SLACK = 1.0 + 2.0 ** -10
L_FLOOR = 2.0 ** -80
CONV_FIRST_BLOCK = 32
CONV_PIECES_PER_BLOCK = 1
VMEM_LIMIT = 56 * 1024 * 1024


class _Tiles(NamedTuple):
    ada_cols: int = 1152
    ffn_rows: int = 512
    ffn_cols: int = 256
    proj_rows: int = 1024
    proj_parts: int = 4
    attn_q: int = 512
    attn_kv: int = 256
    conv_rows: int = 32


TILES = _Tiles()


def _sigmoid(x):
    return 1.0 / (1.0 + jnp.exp(-x))


def _layer_norm(z, g, b):
    mu = jnp.mean(z, axis=-1, keepdims=True)
    zc = z - mu
    var = jnp.mean(zc * zc, axis=-1, keepdims=True)
    return zc * lax.rsqrt(var + LN_EPS) * g + b


def _modulate(x, mod_ref, sub):
    shift = mod_ref[3 * sub:3 * sub + 1, :]
    scale = mod_ref[3 * sub + 1:3 * sub + 2, :]
    return x * (1.0 + scale) + shift


def _const_spec(shape):
    return pl.BlockSpec(shape, lambda *_: (0,) * len(shape), pipeline_mode=pl.Buffered(1))


def _ada_kernel(c_ref, w_ref, b_ref, o_ref):
    c = c_ref[...]
    ca = c * _sigmoid(c)
    o_ref[...] = jnp.sum(ca * w_ref[...], axis=0, keepdims=True) + b_ref[...]


def _ada(c_col, w, b_row, tn):
    d, n = w.shape
    assert n % tn == 0 and tn % LANES == 0
    return pl.pallas_call(
        _ada_kernel,
        grid=(n // tn,),
        in_specs=[pl.BlockSpec((d, 1), lambda j: (0, 0)),
                  pl.BlockSpec((d, tn), lambda j: (0, j)),
                  pl.BlockSpec((1, tn), lambda j: (0, j))],
        out_specs=pl.BlockSpec((1, tn), lambda j: (0, j)),
        out_shape=jax.ShapeDtypeStruct((1, n), F32),
        compiler_params=pltpu.CompilerParams(dimension_semantics=("arbitrary",),
                                             vmem_limit_bytes=VMEM_LIMIT),
        name="ada",
    )(c_col, w, b_row)


def _ffn_kernel(x_ref, mod_ref, win_ref, wout_ref, g_ref, b_ref, o_ref, act_ref, *, sub, weight, tf):
    x = x_ref[...]
    d_ff = wout_ref.shape[0]
    h = _modulate(x, mod_ref, sub).astype(win_ref.dtype)
    for c in range(d_ff // tf):
        gate = jnp.dot(h, win_ref[:, c * tf:(c + 1) * tf], preferred_element_type=F32)
        up = jnp.dot(h, win_ref[:, d_ff + c * tf:d_ff + (c + 1) * tf], preferred_element_type=F32)
        act_ref[:, c * tf:(c + 1) * tf] = (gate * _sigmoid(gate) * up).astype(act_ref.dtype)
    y = jnp.dot(act_ref[...], wout_ref[...], preferred_element_type=F32)
    gate_c = mod_ref[3 * sub + 2:3 * sub + 3, :]
    z = ALPHA * x + weight * (1.0 + gate_c) * y
    o_ref[...] = _layer_norm(z, g_ref[...], b_ref[...])


def _ffn(x, mod9, w_in, w_out, g, b, *, sub, weight, tm, tf):
    s, d = x.shape
    d_ff = w_out.shape[0]
    assert s % tm == 0 and d_ff % tf == 0 and w_in.shape == (d, 2 * d_ff)
    return pl.pallas_call(
        functools.partial(_ffn_kernel, sub=sub, weight=weight, tf=tf),
        grid=(s // tm,),
        in_specs=[pl.BlockSpec((tm, d), lambda i: (i, 0)),
                  _const_spec(mod9.shape),
                  _const_spec(w_in.shape),
                  _const_spec(w_out.shape),
                  _const_spec(g.shape),
                  _const_spec(b.shape)],
        out_specs=pl.BlockSpec((tm, d), lambda i: (i, 0)),
        out_shape=jax.ShapeDtypeStruct((s, d), F32),
        scratch_shapes=[pltpu.VMEM((tm, d_ff), w_out.dtype)],
        compiler_params=pltpu.CompilerParams(dimension_semantics=("arbitrary",),
                                             vmem_limit_bytes=VMEM_LIMIT),
        name=f"ffn{sub}",
    )(x, mod9, w_in, w_out, g, b)


def _mix_in_kernel(x_ref, mod_ref, w_ref, rope_ref, u_ref, qt_ref, k_ref, vt_ref, kn_ref, cos_ref, sin_ref, *,
                   sub, cw, aw, q_scale, n_part):
    tm = x_ref.shape[0]
    tp = tm // n_part
    tk = vt_ref.shape[3]
    freq = rope_ref[0:1, :]

    @pl.when(pl.program_id(0) == 0)
    def _in_tile_angles():
        ang = lax.broadcasted_iota(jnp.int32, (tm, LANES), 0).astype(F32) * freq
        cos_ref[...] = jnp.cos(ang)
        sin_ref[...] = jnp.sin(ang)
        kn_ref[...] = jnp.zeros_like(kn_ref)

    comp0 = lax.broadcasted_iota(jnp.int32, (tp, LANES), 1) < DIFF_HEAD_DIM
    projs = []
    for p in range(n_part):
        h = _modulate(x_ref[p * tp:(p + 1) * tp, :], mod_ref, sub).astype(w_ref.dtype)
        projs.append(jnp.dot(h, w_ref[...], preferred_element_type=F32))

    ang0 = (pl.program_id(0) * tm).astype(F32) * freq
    c0, s0 = jnp.cos(ang0), jnp.sin(ang0)
    half = ROT_DIM // 2
    q0, k0, v0 = 2 * cw, 2 * cw + aw, 2 * cw + 2 * aw
    for p, proj in enumerate(projs):
        rows = slice(p * tp, (p + 1) * tp)
        u_ref[rows, :] = proj[:, :cw] * _sigmoid(proj[:, cw:2 * cw])
        rc = c0 * cos_ref[rows, :] - s0 * sin_ref[rows, :]
        sin = s0 * cos_ref[rows, :] + c0 * sin_ref[rows, :]
        rs1, rs2 = sin * rope_ref[1:2, :], sin * rope_ref[2:3, :]

        def rope(t):
            return t * rc + pltpu.roll(t, LANES - half, 1) * rs1 + pltpu.roll(t, half, 1) * rs2

        for g in range(aw // LANES):
            sl = slice(g * LANES, (g + 1) * LANES)
            qt_ref[g, :, rows] = (rope(proj[:, q0 + g * LANES:q0 + (g + 1) * LANES]) * q_scale).T.astype(BF16)
            kb = rope(proj[:, k0 + g * LANES:k0 + (g + 1) * LANES]).astype(BF16)
            k_ref[rows, sl] = kb
            kf = kb.astype(F32)
            sq = kf * kf
            n0 = jnp.max(jnp.sum(jnp.where(comp0, sq, 0.0), axis=1, keepdims=True), axis=0, keepdims=True)
            n1 = jnp.max(jnp.sum(jnp.where(comp0, 0.0, sq), axis=1, keepdims=True), axis=0, keepdims=True)
            kn_ref[g:g + 1, :] = jnp.maximum(kn_ref[g:g + 1, :], jnp.where(comp0[0:1, :], n0, n1))
            for c in range(tp // tk):
                vt_ref[g, p * (tp // tk) + c] = (
                    proj[c * tk:(c + 1) * tk, v0 + g * LANES:v0 + (g + 1) * LANES].T.astype(BF16))


def _mix_in(x, mod9, w, rope_rows, *, cw, aw, q_scale, tm, tk, n_part):
    s, d = x.shape
    n_heads = aw // LANES
    assert (tm // n_part) % tk == 0 and n_heads <= SUBLANES
    row = lambda i: (i, 0)
    return pl.pallas_call(
        functools.partial(_mix_in_kernel, sub=1, cw=cw, aw=aw, q_scale=q_scale, n_part=n_part),
        grid=(s // tm,),
        in_specs=[pl.BlockSpec((tm, d), row),
                  _const_spec(mod9.shape),
                  _const_spec(w.shape),
                  _const_spec(rope_rows.shape)],
        out_specs=[pl.BlockSpec((tm, cw), row),
                   pl.BlockSpec((n_heads, LANES, tm), lambda i: (0, 0, i)),
                   pl.BlockSpec((tm, aw), row),
                   pl.BlockSpec((n_heads, tm // tk, LANES, tk), lambda i: (0, i, 0, 0)),
                   pl.BlockSpec((SUBLANES, LANES), lambda i: (0, 0))],
        out_shape=[jax.ShapeDtypeStruct((s, cw), F32),
                   jax.ShapeDtypeStruct((n_heads, LANES, s), BF16),
                   jax.ShapeDtypeStruct((s, aw), BF16),
                   jax.ShapeDtypeStruct((n_heads, s // tk, LANES, tk), BF16),
                   jax.ShapeDtypeStruct((SUBLANES, LANES), F32)],
        scratch_shapes=[pltpu.VMEM((tm, LANES), F32),
                        pltpu.VMEM((tm, LANES), F32)],
        compiler_params=pltpu.CompilerParams(dimension_semantics=("arbitrary",),
                                             vmem_limit_bytes=VMEM_LIMIT),
        name="mix_in",
    )(x, mod9, w, rope_rows)


def _dependent_zero(v):
    r, c = v.shape
    folded = jnp.sum(v.reshape(r // SUBLANES, SUBLANES, c), axis=0)
    folded = sum(folded[:, g * LANES:(g + 1) * LANES] for g in range(c // LANES))
    bits = lax.bitcast_convert_type(folded[0:1, :], jnp.uint32)
    return lax.bitcast_convert_type((bits >> 16) >> 16, F32)


def _conv_tile(i, n_tiles, prev_ref, cur_ref, next_ref, w_ref, cb_ref, g_ref, b_ref, o_ref, ext_ref, sh_ref, y_ref,
               rows):
    tm, cw = cur_ref.shape
    ext_ref[0:CONV_HALO, :] = jnp.where(i > 0, prev_ref[...], 0.0)
    ext_ref[CONV_HALO:CONV_HALO + tm, :] = cur_ref[...]
    ext_ref[CONV_HALO + tm:, :] = jnp.where(i < n_tiles - 1, next_ref[...], 0.0)
    span = sh_ref.shape[1]
    for b in range(SUBLANES):
        sh_ref[b] = ext_ref[b:b + span, :]
    base = CONV_HALO - CONV_PAD
    done = []
    for lc in range(cw // LANES):
        ls = slice(lc * LANES, (lc + 1) * LANES)
        for rc in range(tm // rows):
            r0 = rc * rows
            acc = jnp.zeros((rows, LANES), F32)
            for t in range(CONV_KERNEL):
                off = base + t
                a0 = r0 + SUBLANES * (off // SUBLANES)
                acc = acc + sh_ref[off % SUBLANES, a0:a0 + rows, ls] * w_ref[t:t + 1, ls]
            y_ref[r0:r0 + rows, ls] = acc
            done.append(_dependent_zero(acc))
    y = _layer_norm(y_ref[...] + cb_ref[...], g_ref[...], b_ref[...])
    y = y * _sigmoid(y)
    o_ref[...] = y.astype(BF16)
    done.append(_dependent_zero(y))
    return done


def _conv_specs(s, cw, tm, tile_index):
    nh = tm // CONV_HALO
    last = s // CONV_HALO - 1
    return [pl.BlockSpec((CONV_HALO, cw), lambda *g: (jnp.maximum(tile_index(*g) * nh - 1, 0), 0)),
            pl.BlockSpec((tm, cw), lambda *g: (tile_index(*g), 0)),
            pl.BlockSpec((CONV_HALO, cw), lambda *g: (jnp.minimum((tile_index(*g) + 1) * nh, last), 0))]


def _conv_scratch(tm, cw):
    return [pltpu.VMEM((tm + 2 * CONV_HALO, cw), F32),
            pltpu.VMEM((SUBLANES, tm + 2 * CONV_HALO - SUBLANES, cw), F32),
            pltpu.VMEM((tm, cw), F32)]


def _mixer_kernel(qt_ref, k_ref, vt_ref, kn_ref, lq1_ref, lk1_ref, lq2_ref, lk2_ref, g_ref,
                  up_ref, uc_ref, un_ref, cw_ref, cb_ref, cg_ref, cbeta_ref,
                  o_ref, conv_ref,
                  rhs_ref, acc_ref, kmax_ref, ext_ref, sh_ref, y_ref, *, tk, conv_rows, lam_init):
    step = pl.program_id(0) * pl.num_programs(1) + pl.program_id(1)
    n_steps = pl.num_programs(0) * pl.num_programs(1)

    hd2, tq = qt_ref.shape[1], qt_ref.shape[2]
    n = 2 * tq
    n_kv = k_ref.shape[0] // tk
    qt = qt_ref[0]
    row = lax.broadcasted_iota(jnp.int32, qt.shape, 0)
    zero = jnp.zeros_like(qt)
    rhs_ref[:, :tq] = jnp.where(row < DIFF_HEAD_DIM, qt, zero)
    rhs_ref[:, tq:] = jnp.where(row >= DIFF_HEAD_DIM, qt, zero)

    def k_block(j):
        return k_ref[pl.ds(pl.multiple_of(j * tk, tk), tk), :]

    @pl.when(pl.program_id(1) == 0)
    def _key_norm_bound():
        kn = kn_ref[pl.ds(pl.program_id(0), 1), :]
        col = lax.broadcasted_iota(jnp.int32, (1, n), 1)
        kmax_ref[...] = jnp.sqrt(jnp.where(col < tq, kn[:, 0:1], kn[:, DIFF_HEAD_DIM:DIFF_HEAD_DIM + 1]))

    conv_done = _conv_tile(step, n_steps, up_ref, uc_ref, un_ref, cw_ref, cb_ref, cg_ref, cbeta_ref, conv_ref,
                           ext_ref, sh_ref, y_ref, conv_rows)
    assert CONV_FIRST_BLOCK + pl.cdiv(len(conv_done), CONV_PIECES_PER_BLOCK) <= n_kv

    r32 = rhs_ref[...].astype(F32)
    qn = jnp.sqrt(jnp.sum(r32 * r32, axis=0, keepdims=True))
    m = qn * kmax_ref[...] * SHIFT_SLACK
    l8 = jnp.zeros((8, n), F32)
    acc = jnp.zeros((hd2, n), F32)
    e_prev = None
    for j in range(n_kv):
        s = jnp.dot(k_ref[j * tk:(j + 1) * tk, :], rhs_ref[...], preferred_element_type=F32)
        if e_prev is not None:
            acc = acc + jnp.dot(vt_ref[0, j - 1], e_prev, preferred_element_type=F32)
        if j >= CONV_FIRST_BLOCK:
            for piece in conv_done[:CONV_PIECES_PER_BLOCK]:
                m = m + jnp.concatenate([piece] * (n // LANES), axis=1)
            del conv_done[:CONV_PIECES_PER_BLOCK]
        e = jnp.exp2(s - m)
        l8 = l8 + jnp.sum(e.reshape(tk // 8, 8, n), axis=0)
        e_prev = e.astype(BF16)
    acc = acc + jnp.dot(vt_ref[0, n_kv - 1], e_prev, preferred_element_type=F32)
    l = jnp.sum(l8, axis=0, keepdims=True)

    def finish(acc, l):
        o = acc * (1.0 / l)
        lam = (jnp.exp(jnp.sum(lq1_ref[...] * lk1_ref[...])) - jnp.exp(jnp.sum(lq2_ref[...] * lk2_ref[...]))
               + lam_init)
        o = o[:, :tq] - lam * o[:, tq:]
        ms = jnp.mean(o * o, axis=0, keepdims=True)
        o = o * lax.rsqrt(ms + LN_EPS) * g_ref[...] * (1.0 - lam_init)
        o_ref[...] = o.T.astype(BF16)

    finish(acc, l)

    @pl.when(jnp.logical_not(jnp.min(l) >= L_FLOOR))
    def _running_max_fallback():
        acc_ref[...] = jnp.zeros_like(acc_ref)

        def body(j, carry):
            m_run, l_run = carry
            s = jnp.dot(k_block(j), rhs_ref[...], preferred_element_type=F32)
            m_new = jnp.maximum(m_run, jnp.max(s, axis=0, keepdims=True))
            alpha = jnp.exp2(m_run - m_new)
            e = jnp.exp2(s - m_new)
            pv = jnp.dot(vt_ref[0, j], e.astype(BF16), preferred_element_type=F32)
            acc_ref[...] = alpha * acc_ref[...] + pv
            return m_new, alpha * l_run + jnp.sum(e, axis=0, keepdims=True)

        init = (jnp.full((1, n), -jnp.inf, F32), jnp.zeros((1, n), F32))
        _, l_run = lax.fori_loop(0, n_kv, body, init)
        finish(acc_ref[...], l_run)


def _mixer(qt, k, vt, kn, lq1, lk1, lq2, lk2, g_col, u, conv_w, conv_b, conv_g, conv_beta, *, tq, tk, conv_rows,
           lam_init):
    n_heads, hd2, s = qt.shape
    cw = u.shape[1]
    nq = s // tq
    tc = s // (n_heads * nq)
    assert tc % conv_rows == 0 and tc % CONV_HALO == 0
    lam_spec = _const_spec(lq1.shape)
    tile = lambda h, i: h * nq + i
    return pl.pallas_call(
        functools.partial(_mixer_kernel, tk=tk, conv_rows=conv_rows, lam_init=lam_init),
        grid=(n_heads, nq),
        in_specs=[pl.BlockSpec((1, hd2, tq), lambda h, i: (h, 0, i)),
                  pl.BlockSpec((s, hd2), lambda h, i: (0, h)),
                  pl.BlockSpec((1, s // tk, hd2, tk), lambda h, i: (h, 0, 0, 0)),
                  _const_spec(kn.shape),
                  lam_spec, lam_spec, lam_spec, lam_spec,
                  _const_spec(g_col.shape),
                  *_conv_specs(s, cw, tc, tile),
                  _const_spec(conv_w.shape),
                  _const_spec(conv_b.shape),
                  _const_spec(conv_g.shape),
                  _const_spec(conv_beta.shape)],
        out_specs=[pl.BlockSpec((tq, hd2), lambda h, i: (i, h)),
                   pl.BlockSpec((tc, cw), lambda h, i: (tile(h, i), 0))],
        out_shape=[jax.ShapeDtypeStruct((s, n_heads * hd2), BF16),
                   jax.ShapeDtypeStruct((s, cw), BF16)],
        scratch_shapes=[pltpu.VMEM((hd2, 2 * tq), BF16),
                        pltpu.VMEM((hd2, 2 * tq), F32),
                        pltpu.VMEM((1, 2 * tq), F32),
                        *_conv_scratch(tc, cw)],
        compiler_params=pltpu.CompilerParams(dimension_semantics=("arbitrary", "arbitrary"),
                                             vmem_limit_bytes=VMEM_LIMIT),
        name="mixer",
    )(qt, k, vt, kn, lq1, lk1, lq2, lk2, g_col, u, u, u, conv_w, conv_b, conv_g, conv_beta)


def _mix_out_kernel(conv_ref, attn_ref, x_ref, mod_ref, w_ref, g_ref, b_ref, o_ref, *, sub, n_part):
    cw = conv_ref.shape[1]
    tp = x_ref.shape[0] // n_part
    gate_c = mod_ref[3 * sub + 2:3 * sub + 3, :]
    ys = []
    for p in range(n_part):
        rows = slice(p * tp, (p + 1) * tp)
        ys.append(jnp.dot(conv_ref[rows, :], w_ref[:cw, :], preferred_element_type=F32)
                  + jnp.dot(attn_ref[rows, :], w_ref[cw:, :], preferred_element_type=F32))
    for p in range(n_part):
        rows = slice(p * tp, (p + 1) * tp)
        z = ALPHA * x_ref[rows, :] + (1.0 + gate_c) * ys[p]
        o_ref[rows, :] = _layer_norm(z, g_ref[...], b_ref[...])


def _mix_out(conv, attn, x, mod9, w, g, b, *, tm, n_part):
    s, d = x.shape
    row = lambda i: (i, 0)
    return pl.pallas_call(
        functools.partial(_mix_out_kernel, sub=1, n_part=n_part),
        grid=(s // tm,),
        in_specs=[pl.BlockSpec((tm, conv.shape[1]), row),
                  pl.BlockSpec((tm, attn.shape[1]), row),
                  pl.BlockSpec((tm, d), row),
                  _const_spec(mod9.shape),
                  _const_spec(w.shape),
                  _const_spec(g.shape),
                  _const_spec(b.shape)],
        out_specs=pl.BlockSpec((tm, d), row),
        out_shape=jax.ShapeDtypeStruct((s, d), F32),
        compiler_params=pltpu.CompilerParams(dimension_semantics=("arbitrary",),
                                             vmem_limit_bytes=VMEM_LIMIT),
        name="mix_out",
    )(conv, attn, x, mod9, w, g, b)


def _rope_rows():
    inv_freq = ROPE_THETA ** (-jnp.arange(0, ROT_DIM, 2, dtype=F32) / ROT_DIM)
    half = ROT_DIM // 2
    zeros_h = jnp.zeros((half,), F32)
    zeros_p = jnp.zeros((DIFF_HEAD_DIM - ROT_DIM,), F32)
    ones_h = jnp.ones((half,), F32)
    reps = LANES // DIFF_HEAD_DIM
    freq = jnp.tile(jnp.concatenate([inv_freq, inv_freq, zeros_p]), reps)
    neg_first = jnp.tile(jnp.concatenate([-ones_h, zeros_h, zeros_p]), reps)
    pos_second = jnp.tile(jnp.concatenate([zeros_h, ones_h, zeros_p]), reps)
    return jnp.stack([freq, neg_first, pos_second])


def kernel(x, c, w_ada, b_ada, ffn1_w_in, ffn1_w_out, ln1_g, ln1_b, mix_w_in, conv_w, conv_b, conv_ln_g,
           conv_ln_b, lambda_q1, lambda_k1, lambda_q2, lambda_k2, subln_g, mix_w_out, ln2_g, ln2_b,
           ffn2_w_in, ffn2_w_out, ln3_g, ln3_b):
    batch, s, d = x.shape
    assert batch == 1 and w_ada.shape[0] == DEPTH == 1
    cw = conv_w.shape[2]
    aw = (mix_w_in.shape[2] - 2 * cw) // 3
    hd2 = 2 * DIFF_HEAD_DIM
    n_heads = aw // hd2
    lam_init = 0.8 - 0.6 * math.exp(-0.3 * 0)
    q_scale = math.log2(math.e) / math.sqrt(DIFF_HEAD_DIM)
    t = TILES
    for rows in (t.ffn_rows, t.proj_rows, t.attn_q, t.attn_kv):
        assert s % rows == 0

    mod9 = _ada(c.reshape(d, 1), w_ada[0], b_ada, tn=t.ada_cols).reshape(9, d)
    x0 = x[0]
    x1 = _ffn(x0, mod9, ffn1_w_in[0], ffn1_w_out[0], ln1_g, ln1_b,
              sub=0, weight=0.5, tm=t.ffn_rows, tf=t.ffn_cols)

    u, qt, k, vt, kn = _mix_in(x1, mod9, mix_w_in[0], _rope_rows(),
                               cw=cw, aw=aw, q_scale=q_scale, tm=t.proj_rows, tk=t.attn_kv, n_part=t.proj_parts)
    attn, conv = _mixer(qt, k, vt, kn, lambda_q1, lambda_k1, lambda_q2, lambda_k2, subln_g.reshape(hd2, 1),
                        u, conv_w[0], conv_b, conv_ln_g, conv_ln_b,
                        tq=t.attn_q, tk=t.attn_kv, conv_rows=t.conv_rows, lam_init=lam_init)
    x2 = _mix_out(conv, attn, x1, mod9, mix_w_out[0].astype(BF16), ln2_g, ln2_b,
                  tm=t.proj_rows, n_part=t.proj_parts)

    x3 = _ffn(x2, mod9, ffn2_w_in[0], ffn2_w_out[0], ln3_g, ln3_b,
              sub=2, weight=0.5, tm=t.ffn_rows, tf=t.ffn_cols)
    return x3[None]
```

```python
import functools
import math
from typing import NamedTuple

import jax
import jax.numpy as jnp
from jax import lax
from jax.experimental import pallas as pl
from jax.experimental.pallas import tpu as pltpu

F32 = jnp.float32
BF16 = jnp.bfloat16

DEPTH = 1
ALPHA = (2.0 * DEPTH) ** 0.25
LN_EPS = 1e-5
DIFF_HEAD_DIM = 64
ROT_DIM = DIFF_HEAD_DIM // 4
ROPE_THETA = 500000.0
CONV_KERNEL = 31
CONV_PAD = (CONV_KERNEL - 1) // 2
CONV_HALO = 16
LANES = 128
SUBLANES = 8
SHIFT_SLACK = 1.0 + 2.0 ** -10
L_FLOOR = 2.0 ** -80
CONV_FIRST_BLOCK = 32
CONV_PIECES_PER_BLOCK = 1
VMEM_LIMIT = 56 * 1024 * 1024


class _Tiles(NamedTuple):
    ada_cols: int = 1152
    ffn_rows: int = 512
    ffn_cols: int = 256
    ffn_out_parts: int = 2
    proj_rows: int = 1024
    proj_parts: int = 4
    attn_q: int = 512
    attn_kv: int = 256
    conv_rows: int = 32


TILES = _Tiles()


def _sigmoid(x):
    return 1.0 / (1.0 + jnp.exp(-x))


def _layer_norm(z, g, b):
    mu = jnp.mean(z, axis=-1, keepdims=True)
    zc = z - mu
    var = jnp.mean(zc * zc, axis=-1, keepdims=True)
    return zc * lax.rsqrt(var + LN_EPS) * g + b


def _modulate(x, mod_ref, sub):
    shift = mod_ref[3 * sub:3 * sub + 1, :]
    scale = mod_ref[3 * sub + 1:3 * sub + 2, :]
    return x * (1.0 + scale) + shift


def _const_spec(shape):
    return pl.BlockSpec(shape, lambda *_: (0,) * len(shape), pipeline_mode=pl.Buffered(1))


def _ada_kernel(c_ref, w_ref, b_ref, o_ref):
    c = c_ref[...]
    ca = c * _sigmoid(c)
    o_ref[...] = jnp.sum(ca * w_ref[...], axis=0, keepdims=True) + b_ref[...]


def _ada(c_col, w, b_row, tn):
    d, n = w.shape
    assert n % tn == 0 and tn % LANES == 0
    return pl.pallas_call(
        _ada_kernel,
        grid=(n // tn,),
        in_specs=[pl.BlockSpec((d, 1), lambda j: (0, 0)),
                  pl.BlockSpec((d, tn), lambda j: (0, j)),
                  pl.BlockSpec((1, tn), lambda j: (0, j))],
        out_specs=pl.BlockSpec((1, tn), lambda j: (0, j)),
        out_shape=jax.ShapeDtypeStruct((1, n), F32),
        compiler_params=pltpu.CompilerParams(dimension_semantics=("arbitrary",),
                                             vmem_limit_bytes=VMEM_LIMIT),
        name="ada",
    )(c_col, w, b_row)


def _ffn_kernel(x_ref, mod_ref, win_ref, wout_ref, g_ref, b_ref, o_ref, act_ref, *, sub, weight, tf, n_out_part):
    x = x_ref[...]
    d_ff = wout_ref.shape[0]
    h = _modulate(x, mod_ref, sub).astype(win_ref.dtype)
    for c in range(d_ff // tf):
        gate = jnp.dot(h, win_ref[:, c * tf:(c + 1) * tf], preferred_element_type=F32)
        up = jnp.dot(h, win_ref[:, d_ff + c * tf:d_ff + (c + 1) * tf], preferred_element_type=F32)
        act_ref[:, c * tf:(c + 1) * tf] = (gate * _sigmoid(gate) * up).astype(act_ref.dtype)
    gate_c = mod_ref[3 * sub + 2:3 * sub + 3, :]
    tp = x.shape[0] // n_out_part
    ys = [jnp.dot(act_ref[p * tp:(p + 1) * tp, :], wout_ref[...], preferred_element_type=F32)
          for p in range(n_out_part)]
    for p, y in enumerate(ys):
        rows = slice(p * tp, (p + 1) * tp)
        z = ALPHA * x[rows, :] + weight * (1.0 + gate_c) * y
        o_ref[rows, :] = _layer_norm(z, g_ref[...], b_ref[...])


def _ffn(x, mod9, w_in, w_out, g, b, *, sub, weight, tm, tf, n_out_part):
    s, d = x.shape
    d_ff = w_out.shape[0]
    assert s % tm == 0 and d_ff % tf == 0 and w_in.shape == (d, 2 * d_ff) and tm % (SUBLANES * n_out_part) == 0
    return pl.pallas_call(
        functools.partial(_ffn_kernel, sub=sub, weight=weight, tf=tf, n_out_part=n_out_part),
        grid=(s // tm,),
        in_specs=[pl.BlockSpec((tm, d), lambda i: (i, 0)),
                  _const_spec(mod9.shape),
                  _const_spec(w_in.shape),
                  _const_spec(w_out.shape),
                  _const_spec(g.shape),
                  _const_spec(b.shape)],
        out_specs=pl.BlockSpec((tm, d), lambda i: (i, 0)),
        out_shape=jax.ShapeDtypeStruct((s, d), F32),
        scratch_shapes=[pltpu.VMEM((tm, d_ff), w_out.dtype)],
        compiler_params=pltpu.CompilerParams(dimension_semantics=("arbitrary",),
                                             vmem_limit_bytes=VMEM_LIMIT),
        name=f"ffn{sub}",
    )(x, mod9, w_in, w_out, g, b)


def _mix_in_kernel(x_ref, mod_ref, w_ref, rope_ref, u_ref, qt_ref, k_ref, vt_ref, kn_ref, cos_ref, sin_ref, *,
                   sub, cw, aw, q_scale, n_part):
    tm = x_ref.shape[0]
    tp = tm // n_part
    tk = vt_ref.shape[3]
    freq = rope_ref[0:1, :]

    @pl.when(pl.program_id(0) == 0)
    def _in_tile_angles():
        ang = lax.broadcasted_iota(jnp.int32, (tm, LANES), 0).astype(F32) * freq
        cos_ref[...] = jnp.cos(ang)
        sin_ref[...] = jnp.sin(ang)
        kn_ref[...] = jnp.zeros_like(kn_ref)

    comp0 = lax.broadcasted_iota(jnp.int32, (tp, LANES), 1) < DIFF_HEAD_DIM
    projs = []
    for p in range(n_part):
        h = _modulate(x_ref[p * tp:(p + 1) * tp, :], mod_ref, sub).astype(w_ref.dtype)
        projs.append(jnp.dot(h, w_ref[...], preferred_element_type=F32))

    ang0 = (pl.program_id(0) * tm).astype(F32) * freq
    c0, s0 = jnp.cos(ang0), jnp.sin(ang0)
    half = ROT_DIM // 2
    q0, k0, v0 = 2 * cw, 2 * cw + aw, 2 * cw + 2 * aw
    for p, proj in enumerate(projs):
        rows = slice(p * tp, (p + 1) * tp)
        u_ref[rows, :] = proj[:, :cw] * _sigmoid(proj[:, cw:2 * cw])
        rc = c0 * cos_ref[rows, :] - s0 * sin_ref[rows, :]
        sin = s0 * cos_ref[rows, :] + c0 * sin_ref[rows, :]
        rs1, rs2 = sin * rope_ref[1:2, :], sin * rope_ref[2:3, :]

        def rope(t):
            return t * rc + pltpu.roll(t, LANES - half, 1) * rs1 + pltpu.roll(t, half, 1) * rs2

        for g in range(aw // LANES):
            sl = slice(g * LANES, (g + 1) * LANES)
            qt_ref[g, :, rows] = (rope(proj[:, q0 + g * LANES:q0 + (g + 1) * LANES]) * q_scale).T.astype(BF16)
            kb = rope(proj[:, k0 + g * LANES:k0 + (g + 1) * LANES]).astype(BF16)
            k_ref[rows, sl] = kb
            kf = kb.astype(F32)
            sq = kf * kf
            n0 = jnp.max(jnp.sum(jnp.where(comp0, sq, 0.0), axis=1, keepdims=True), axis=0, keepdims=True)
            n1 = jnp.max(jnp.sum(jnp.where(comp0, 0.0, sq), axis=1, keepdims=True), axis=0, keepdims=True)
            kn_ref[g:g + 1, :] = jnp.maximum(kn_ref[g:g + 1, :], jnp.where(comp0[0:1, :], n0, n1))
            for c in range(tp // tk):
                vt_ref[g, p * (tp // tk) + c] = (
                    proj[c * tk:(c + 1) * tk, v0 + g * LANES:v0 + (g + 1) * LANES].T.astype(BF16))


def _mix_in(x, mod9, w, rope_rows, *, cw, aw, q_scale, tm, tk, n_part):
    s, d = x.shape
    n_heads = aw // LANES
    assert (tm // n_part) % tk == 0 and n_heads <= SUBLANES
    row = lambda i: (i, 0)
    return pl.pallas_call(
        functools.partial(_mix_in_kernel, sub=1, cw=cw, aw=aw, q_scale=q_scale, n_part=n_part),
        grid=(s // tm,),
        in_specs=[pl.BlockSpec((tm, d), row),
                  _const_spec(mod9.shape),
                  _const_spec(w.shape),
                  _const_spec(rope_rows.shape)],
        out_specs=[pl.BlockSpec((tm, cw), row),
                   pl.BlockSpec((n_heads, LANES, tm), lambda i: (0, 0, i)),
                   pl.BlockSpec((tm, aw), row),
                   pl.BlockSpec((n_heads, tm // tk, LANES, tk), lambda i: (0, i, 0, 0)),
                   pl.BlockSpec((SUBLANES, LANES), lambda i: (0, 0))],
        out_shape=[jax.ShapeDtypeStruct((s, cw), F32),
                   jax.ShapeDtypeStruct((n_heads, LANES, s), BF16),
                   jax.ShapeDtypeStruct((s, aw), BF16),
                   jax.ShapeDtypeStruct((n_heads, s // tk, LANES, tk), BF16),
                   jax.ShapeDtypeStruct((SUBLANES, LANES), F32)],
        scratch_shapes=[pltpu.VMEM((tm, LANES), F32),
                        pltpu.VMEM((tm, LANES), F32)],
        compiler_params=pltpu.CompilerParams(dimension_semantics=("arbitrary",),
                                             vmem_limit_bytes=VMEM_LIMIT),
        name="mix_in",
    )(x, mod9, w, rope_rows)


def _dependent_zero(v):
    r, c = v.shape
    folded = jnp.sum(v.reshape(r // SUBLANES, SUBLANES, c), axis=0)
    folded = sum(folded[:, g * LANES:(g + 1) * LANES] for g in range(c // LANES))
    bits = lax.bitcast_convert_type(folded[0:1, :], jnp.uint32)
    return lax.bitcast_convert_type((bits >> 16) >> 16, F32)


def _conv_tile(i, n_tiles, prev_ref, cur_ref, next_ref, w_ref, cb_ref, g_ref, b_ref, o_ref, ext_ref, sh_ref, y_ref,
               rows):
    tm, cw = cur_ref.shape
    ext_ref[0:CONV_HALO, :] = jnp.where(i > 0, prev_ref[...], 0.0)
    ext_ref[CONV_HALO:CONV_HALO + tm, :] = cur_ref[...]
    ext_ref[CONV_HALO + tm:, :] = jnp.where(i < n_tiles - 1, next_ref[...], 0.0)
    span = sh_ref.shape[1]
    for b in range(SUBLANES):
        sh_ref[b] = ext_ref[b:b + span, :]
    base = CONV_HALO - CONV_PAD
    done = []
    for lc in range(cw // LANES):
        ls = slice(lc * LANES, (lc + 1) * LANES)
        for rc in range(tm // rows):
            r0 = rc * rows
            acc = jnp.zeros((rows, LANES), F32)
            for t in range(CONV_KERNEL):
                off = base + t
                a0 = r0 + SUBLANES * (off // SUBLANES)
                acc = acc + sh_ref[off % SUBLANES, a0:a0 + rows, ls] * w_ref[t:t + 1, ls]
            y_ref[r0:r0 + rows, ls] = acc
            done.append(_dependent_zero(acc))
    y = _layer_norm(y_ref[...] + cb_ref[...], g_ref[...], b_ref[...])
    y = y * _sigmoid(y)
    o_ref[...] = y.astype(BF16)
    done.append(_dependent_zero(y))
    return done


def _conv_specs(s, cw, tm, tile_index):
    nh = tm // CONV_HALO
    last = s // CONV_HALO - 1
    return [pl.BlockSpec((CONV_HALO, cw), lambda *g: (jnp.maximum(tile_index(*g) * nh - 1, 0), 0)),
            pl.BlockSpec((tm, cw), lambda *g: (tile_index(*g), 0)),
            pl.BlockSpec((CONV_HALO, cw), lambda *g: (jnp.minimum((tile_index(*g) + 1) * nh, last), 0))]


def _conv_scratch(tm, cw):
    return [pltpu.VMEM((tm + 2 * CONV_HALO, cw), F32),
            pltpu.VMEM((SUBLANES, tm + 2 * CONV_HALO - SUBLANES, cw), F32),
            pltpu.VMEM((tm, cw), F32)]


def _mixer_kernel(qt_ref, k_ref, vt_ref, kn_ref, lq1_ref, lk1_ref, lq2_ref, lk2_ref, g_ref,
                  up_ref, uc_ref, un_ref, cw_ref, cb_ref, cg_ref, cbeta_ref,
                  o_ref, conv_ref,
                  rhs_ref, acc_ref, kmax_ref, ext_ref, sh_ref, y_ref, *, tk, conv_rows, lam_init):
    step = pl.program_id(0) * pl.num_programs(1) + pl.program_id(1)
    n_steps = pl.num_programs(0) * pl.num_programs(1)

    hd2, tq = qt_ref.shape[1], qt_ref.shape[2]
    n = 2 * tq
    n_kv = k_ref.shape[0] // tk
    qt = qt_ref[0]
    row = lax.broadcasted_iota(jnp.int32, qt.shape, 0)
    zero = jnp.zeros_like(qt)
    rhs_ref[:, :tq] = jnp.where(row < DIFF_HEAD_DIM, qt, zero)
    rhs_ref[:, tq:] = jnp.where(row >= DIFF_HEAD_DIM, qt, zero)

    def k_block(j):
        return k_ref[pl.ds(pl.multiple_of(j * tk, tk), tk), :]

    @pl.when(pl.program_id(1) == 0)
    def _key_norm_bound():
        kn = kn_ref[pl.ds(pl.program_id(0), 1), :]
        col = lax.broadcasted_iota(jnp.int32, (1, n), 1)
        kmax_ref[...] = jnp.sqrt(jnp.where(col < tq, kn[:, 0:1], kn[:, DIFF_HEAD_DIM:DIFF_HEAD_DIM + 1]))

    conv_done = _conv_tile(step, n_steps, up_ref, uc_ref, un_ref, cw_ref, cb_ref, cg_ref, cbeta_ref, conv_ref,
                           ext_ref, sh_ref, y_ref, conv_rows)
    assert CONV_FIRST_BLOCK + pl.cdiv(len(conv_done), CONV_PIECES_PER_BLOCK) <= n_kv

    r32 = rhs_ref[...].astype(F32)
    qn = jnp.sqrt(jnp.sum(r32 * r32, axis=0, keepdims=True))
    m = qn * kmax_ref[...] * SHIFT_SLACK
    l8 = jnp.zeros((8, n), F32)
    acc = jnp.zeros((hd2, n), F32)
    e_prev = None
    for j in range(n_kv):
        s = jnp.dot(k_ref[j * tk:(j + 1) * tk, :], rhs_ref[...], preferred_element_type=F32)
        if e_prev is not None:
            acc = acc + jnp.dot(vt_ref[0, j - 1], e_prev, preferred_element_type=F32)
        if j >= CONV_FIRST_BLOCK:
            for piece in conv_done[:CONV_PIECES_PER_BLOCK]:
                m = m + jnp.concatenate([piece] * (n // LANES), axis=1)
            del conv_done[:CONV_PIECES_PER_BLOCK]
        e = jnp.exp2(s - m)
        l8 = l8 + jnp.sum(e.reshape(tk // 8, 8, n), axis=0)
        e_prev = e.astype(BF16)
    acc = acc + jnp.dot(vt_ref[0, n_kv - 1], e_prev, preferred_element_type=F32)
    l = jnp.sum(l8, axis=0, keepdims=True)

    def finish(acc, l):
        o = acc * (1.0 / l)
        lam = (jnp.exp(jnp.sum(lq1_ref[...] * lk1_ref[...])) - jnp.exp(jnp.sum(lq2_ref[...] * lk2_ref[...]))
               + lam_init)
        o = o[:, :tq] - lam * o[:, tq:]
        ms = jnp.mean(o * o, axis=0, keepdims=True)
        o = o * lax.rsqrt(ms + LN_EPS) * g_ref[...] * (1.0 - lam_init)
        o_ref[...] = o.T.astype(BF16)

    finish(acc, l)

    @pl.when(jnp.logical_not(jnp.min(l) >= L_FLOOR))
    def _running_max_fallback():
        acc_ref[...] = jnp.zeros_like(acc_ref)

        def body(j, carry):
            m_run, l_run = carry
            s = jnp.dot(k_block(j), rhs_ref[...], preferred_element_type=F32)
            m_new = jnp.maximum(m_run, jnp.max(s, axis=0, keepdims=True))
            alpha = jnp.exp2(m_run - m_new)
            e = jnp.exp2(s - m_new)
            pv = jnp.dot(vt_ref[0, j], e.astype(BF16), preferred_element_type=F32)
            acc_ref[...] = alpha * acc_ref[...] + pv
            return m_new, alpha * l_run + jnp.sum(e, axis=0, keepdims=True)

        init = (jnp.full((1, n), -jnp.inf, F32), jnp.zeros((1, n), F32))
        _, l_run = lax.fori_loop(0, n_kv, body, init)
        finish(acc_ref[...], l_run)


def _mixer(qt, k, vt, kn, lq1, lk1, lq2, lk2, g_col, u, conv_w, conv_b, conv_g, conv_beta, *, tq, tk, conv_rows,
           lam_init):
    n_heads, hd2, s = qt.shape
    cw = u.shape[1]
    nq = s // tq
    tc = s // (n_heads * nq)
    assert tc % conv_rows == 0 and tc % CONV_HALO == 0
    lam_spec = _const_spec(lq1.shape)
    tile = lambda h, i: h * nq + i
    return pl.pallas_call(
        functools.partial(_mixer_kernel, tk=tk, conv_rows=conv_rows, lam_init=lam_init),
        grid=(n_heads, nq),
        in_specs=[pl.BlockSpec((1, hd2, tq), lambda h, i: (h, 0, i)),
                  pl.BlockSpec((s, hd2), lambda h, i: (0, h)),
                  pl.BlockSpec((1, s // tk, hd2, tk), lambda h, i: (h, 0, 0, 0)),
                  _const_spec(kn.shape),
                  lam_spec, lam_spec, lam_spec, lam_spec,
                  _const_spec(g_col.shape),
                  *_conv_specs(s, cw, tc, tile),
                  _const_spec(conv_w.shape),
                  _const_spec(conv_b.shape),
                  _const_spec(conv_g.shape),
                  _const_spec(conv_beta.shape)],
        out_specs=[pl.BlockSpec((tq, hd2), lambda h, i: (i, h)),
                   pl.BlockSpec((tc, cw), lambda h, i: (tile(h, i), 0))],
        out_shape=[jax.ShapeDtypeStruct((s, n_heads * hd2), BF16),
                   jax.ShapeDtypeStruct((s, cw), BF16)],
        scratch_shapes=[pltpu.VMEM((hd2, 2 * tq), BF16),
                        pltpu.VMEM((hd2, 2 * tq), F32),
                        pltpu.VMEM((1, 2 * tq), F32),
                        *_conv_scratch(tc, cw)],
        compiler_params=pltpu.CompilerParams(dimension_semantics=("arbitrary", "arbitrary"),
                                             vmem_limit_bytes=VMEM_LIMIT),
        name="mixer",
    )(qt, k, vt, kn, lq1, lk1, lq2, lk2, g_col, u, u, u, conv_w, conv_b, conv_g, conv_beta)


def _mix_out_kernel(conv_ref, attn_ref, x_ref, mod_ref, w_ref, g_ref, b_ref, o_ref, *, sub, n_part):
    cw = conv_ref.shape[1]
    tp = x_ref.shape[0] // n_part
    gate_c = mod_ref[3 * sub + 2:3 * sub + 3, :]
    ys = []
    for p in range(n_part):
        rows = slice(p * tp, (p + 1) * tp)
        ys.append(jnp.dot(conv_ref[rows, :], w_ref[:cw, :], preferred_element_type=F32)
                  + jnp.dot(attn_ref[rows, :], w_ref[cw:, :], preferred_element_type=F32))
    for p in range(n_part):
        rows = slice(p * tp, (p + 1) * tp)
        z = ALPHA * x_ref[rows, :] + (1.0 + gate_c) * ys[p]
        o_ref[rows, :] = _layer_norm(z, g_ref[...], b_ref[...])


def _mix_out(conv, attn, x, mod9, w, g, b, *, tm, n_part):
    s, d = x.shape
    row = lambda i: (i, 0)
    return pl.pallas_call(
        functools.partial(_mix_out_kernel, sub=1, n_part=n_part),
        grid=(s // tm,),
        in_specs=[pl.BlockSpec((tm, conv.shape[1]), row),
                  pl.BlockSpec((tm, attn.shape[1]), row),
                  pl.BlockSpec((tm, d), row),
                  _const_spec(mod9.shape),
                  _const_spec(w.shape),
                  _const_spec(g.shape),
                  _const_spec(b.shape)],
        out_specs=pl.BlockSpec((tm, d), row),
        out_shape=jax.ShapeDtypeStruct((s, d), F32),
        compiler_params=pltpu.CompilerParams(dimension_semantics=("arbitrary",),
                                             vmem_limit_bytes=VMEM_LIMIT),
        name="mix_out",
    )(conv, attn, x, mod9, w, g, b)


def _rope_rows():
    inv_freq = ROPE_THETA ** (-jnp.arange(0, ROT_DIM, 2, dtype=F32) / ROT_DIM)
    half = ROT_DIM // 2
    zeros_h = jnp.zeros((half,), F32)
    zeros_p = jnp.zeros((DIFF_HEAD_DIM - ROT_DIM,), F32)
    ones_h = jnp.ones((half,), F32)
    reps = LANES // DIFF_HEAD_DIM
    freq = jnp.tile(jnp.concatenate([inv_freq, inv_freq, zeros_p]), reps)
    neg_first = jnp.tile(jnp.concatenate([-ones_h, zeros_h, zeros_p]), reps)
    pos_second = jnp.tile(jnp.concatenate([zeros_h, ones_h, zeros_p]), reps)
    return jnp.stack([freq, neg_first, pos_second])


def kernel(x, c, w_ada, b_ada, ffn1_w_in, ffn1_w_out, ln1_g, ln1_b, mix_w_in, conv_w, conv_b, conv_ln_g,
           conv_ln_b, lambda_q1, lambda_k1, lambda_q2, lambda_k2, subln_g, mix_w_out, ln2_g, ln2_b,
           ffn2_w_in, ffn2_w_out, ln3_g, ln3_b):
    batch, s, d = x.shape
    assert batch == 1 and w_ada.shape[0] == DEPTH == 1
    cw = conv_w.shape[2]
    aw = (mix_w_in.shape[2] - 2 * cw) // 3
    hd2 = 2 * DIFF_HEAD_DIM
    n_heads = aw // hd2
    lam_init = 0.8 - 0.6 * math.exp(-0.3 * 0)
    q_scale = math.log2(math.e) / math.sqrt(DIFF_HEAD_DIM)
    t = TILES
    for rows in (t.ffn_rows, t.proj_rows, t.attn_q, t.attn_kv):
        assert s % rows == 0

    mod9 = _ada(c.reshape(d, 1), w_ada[0], b_ada, tn=t.ada_cols).reshape(9, d)
    x0 = x[0]
    x1 = _ffn(x0, mod9, ffn1_w_in[0], ffn1_w_out[0], ln1_g, ln1_b,
              sub=0, weight=0.5, tm=t.ffn_rows, tf=t.ffn_cols, n_out_part=t.ffn_out_parts)

    u, qt, k, vt, kn = _mix_in(x1, mod9, mix_w_in[0], _rope_rows(),
                               cw=cw, aw=aw, q_scale=q_scale, tm=t.proj_rows, tk=t.attn_kv, n_part=t.proj_parts)
    attn, conv = _mixer(qt, k, vt, kn, lambda_q1, lambda_k1, lambda_q2, lambda_k2, subln_g.reshape(hd2, 1),
                        u, conv_w[0], conv_b, conv_ln_g, conv_ln_b,
                        tq=t.attn_q, tk=t.attn_kv, conv_rows=t.conv_rows, lam_init=lam_init)
    x2 = _mix_out(conv, attn, x1, mod9, mix_w_out[0].astype(BF16), ln2_g, ln2_b,
                  tm=t.proj_rows, n_part=t.proj_parts)

    x3 = _ffn(x2, mod9, ffn2_w_in[0], ffn2_w_out[0], ln3_g, ln3_b,
              sub=2, weight=0.5, tm=t.ffn_rows, tf=t.ffn_cols, n_out_part=t.ffn_out_parts)
    return x3[None]
```

```python
import functools
import math
from typing import NamedTuple

import jax
import jax.numpy as jnp
from jax import lax
from jax.experimental import pallas as pl
from jax.experimental.pallas import tpu as pltpu

F32 = jnp.float32
BF16 = jnp.bfloat16

DEPTH = 1
ALPHA = (2.0 * DEPTH) ** 0.25
LN_EPS = 1e-5
DIFF_HEAD_DIM = 64
ROT_DIM = DIFF_HEAD_DIM // 4
ROPE_THETA = 500000.0
CONV_KERNEL = 31
CONV_PAD = (CONV_KERNEL - 1) // 2
CONV_HALO = 16
LANES = 128
SUBLANES = 8
SHIFT_SLACK = 1.0 + 2.0 ** -10
L_FLOOR = 2.0 ** -80
CONV_FIRST_BLOCK = 32
CONV_PIECES_PER_BLOCK = 1
VMEM_LIMIT = 56 * 1024 * 1024


class _Tiles(NamedTuple):
    ada_cols: int = 1152
    ffn_rows: int = 512
    ffn_cols: int = 256
    ffn_out_parts: int = 2
    proj_rows: int = 1024
    proj_parts: int = 4
    attn_q: int = 512
    attn_kv: int = 256
    conv_rows: int = 32


TILES = _Tiles()


def _sigmoid(x):
    return 1.0 / (1.0 + jnp.exp(-x))


def _layer_norm(z, g, b):
    mu = jnp.mean(z, axis=-1, keepdims=True)
    zc = z - mu
    var = jnp.mean(zc * zc, axis=-1, keepdims=True)
    return zc * lax.rsqrt(var + LN_EPS) * g + b


def _modulate(x, mod_ref, sub):
    shift = mod_ref[3 * sub:3 * sub + 1, :]
    scale = mod_ref[3 * sub + 1:3 * sub + 2, :]
    return x * (1.0 + scale) + shift


def _const_spec(shape):
    return pl.BlockSpec(shape, lambda *_: (0,) * len(shape), pipeline_mode=pl.Buffered(1))


def _ada_kernel(c_ref, w_ref, b_ref, o_ref):
    c = c_ref[...]
    ca = c * _sigmoid(c)
    o_ref[...] = jnp.sum(ca * w_ref[...], axis=0, keepdims=True) + b_ref[...]


def _ada(c_col, w, b_row, tn):
    d, n = w.shape
    assert n % tn == 0 and tn % LANES == 0
    return pl.pallas_call(
        _ada_kernel,
        grid=(n // tn,),
        in_specs=[pl.BlockSpec((d, 1), lambda j: (0, 0)),
                  pl.BlockSpec((d, tn), lambda j: (0, j)),
                  pl.BlockSpec((1, tn), lambda j: (0, j))],
        out_specs=pl.BlockSpec((1, tn), lambda j: (0, j)),
        out_shape=jax.ShapeDtypeStruct((1, n), F32),
        compiler_params=pltpu.CompilerParams(dimension_semantics=("arbitrary",),
                                             vmem_limit_bytes=VMEM_LIMIT),
        name="ada",
    )(c_col, w, b_row)


def _ffn_kernel(x_ref, mod_ref, win_ref, wout_ref, g_ref, b_ref, o_ref, act_ref, *, sub, weight, tf, n_out_part):
    x = x_ref[...]
    d_ff = wout_ref.shape[0]
    h = _modulate(x, mod_ref, sub).astype(win_ref.dtype)
    for c in range(d_ff // tf):
        gate = jnp.dot(h, win_ref[:, c * tf:(c + 1) * tf], preferred_element_type=F32)
        up = jnp.dot(h, win_ref[:, d_ff + c * tf:d_ff + (c + 1) * tf], preferred_element_type=F32)
        act_ref[:, c * tf:(c + 1) * tf] = (gate * _sigmoid(gate) * up).astype(act_ref.dtype)
    gate_c = mod_ref[3 * sub + 2:3 * sub + 3, :]
    tp = x.shape[0] // n_out_part
    ys = [jnp.dot(act_ref[p * tp:(p + 1) * tp, :], wout_ref[...], preferred_element_type=F32)
          for p in range(n_out_part)]
    for p, y in enumerate(ys):
        rows = slice(p * tp, (p + 1) * tp)
        z = ALPHA * x[rows, :] + weight * (1.0 + gate_c) * y
        o_ref[rows, :] = _layer_norm(z, g_ref[...], b_ref[...])


def _ffn(x, mod9, w_in, w_out, g, b, *, sub, weight, tm, tf, n_out_part):
    s, d = x.shape
    d_ff = w_out.shape[0]
    assert s % tm == 0 and d_ff % tf == 0 and w_in.shape == (d, 2 * d_ff) and tm % (SUBLANES * n_out_part) == 0
    return pl.pallas_call(
        functools.partial(_ffn_kernel, sub=sub, weight=weight, tf=tf, n_out_part=n_out_part),
        grid=(s // tm,),
        in_specs=[pl.BlockSpec((tm, d), lambda i: (i, 0)),
                  _const_spec(mod9.shape),
                  _const_spec(w_in.shape),
                  _const_spec(w_out.shape),
                  _const_spec(g.shape),
                  _const_spec(b.shape)],
        out_specs=pl.BlockSpec((tm, d), lambda i: (i, 0)),
        out_shape=jax.ShapeDtypeStruct((s, d), F32),
        scratch_shapes=[pltpu.VMEM((tm, d_ff), w_out.dtype)],
        compiler_params=pltpu.CompilerParams(dimension_semantics=("arbitrary",),
                                             vmem_limit_bytes=VMEM_LIMIT),
        name=f"ffn{sub}",
    )(x, mod9, w_in, w_out, g, b)


def _mix_in_kernel(x_ref, mod_ref, w_ref, rope_ref, u_ref, qt_ref, k_ref, vt_ref, kn_ref, cos_ref, sin_ref, *,
                   sub, cw, aw, q_scale, n_part):
    tm = x_ref.shape[0]
    tp = tm // n_part
    tk = vt_ref.shape[3]
    freq = rope_ref[0:1, :]

    @pl.when(pl.program_id(0) == 0)
    def _in_tile_angles():
        ang = lax.broadcasted_iota(jnp.int32, (tm, LANES), 0).astype(F32) * freq
        cos_ref[...] = jnp.cos(ang)
        sin_ref[...] = jnp.sin(ang)
        kn_ref[...] = jnp.zeros_like(kn_ref)

    comp0 = lax.broadcasted_iota(jnp.int32, (tp, LANES), 1) < DIFF_HEAD_DIM
    projs = []
    for p in range(n_part):
        h = _modulate(x_ref[p * tp:(p + 1) * tp, :], mod_ref, sub).astype(w_ref.dtype)
        projs.append(jnp.dot(h, w_ref[...], preferred_element_type=F32))

    ang0 = (pl.program_id(0) * tm).astype(F32) * freq
    c0, s0 = jnp.cos(ang0), jnp.sin(ang0)
    half = ROT_DIM // 2
    q0, k0, v0 = 2 * cw, 2 * cw + aw, 2 * cw + 2 * aw
    for p, proj in enumerate(projs):
        rows = slice(p * tp, (p + 1) * tp)
        u_ref[rows, :] = proj[:, :cw] * _sigmoid(proj[:, cw:2 * cw])
        rc = c0 * cos_ref[rows, :] - s0 * sin_ref[rows, :]
        sin = s0 * cos_ref[rows, :] + c0 * sin_ref[rows, :]
        rs1, rs2 = sin * rope_ref[1:2, :], sin * rope_ref[2:3, :]

        def rope(t):
            return t * rc + pltpu.roll(t, LANES - half, 1) * rs1 + pltpu.roll(t, half, 1) * rs2

        for g in range(aw // LANES):
            sl = slice(g * LANES, (g + 1) * LANES)
            qt_ref[g, :, rows] = (rope(proj[:, q0 + g * LANES:q0 + (g + 1) * LANES]) * q_scale).T.astype(BF16)
            kb = rope(proj[:, k0 + g * LANES:k0 + (g + 1) * LANES]).astype(BF16)
            k_ref[rows, sl] = kb
            kf = kb.astype(F32)
            sq = kf * kf
            n0 = jnp.max(jnp.sum(jnp.where(comp0, sq, 0.0), axis=1, keepdims=True), axis=0, keepdims=True)
            n1 = jnp.max(jnp.sum(jnp.where(comp0, 0.0, sq), axis=1, keepdims=True), axis=0, keepdims=True)
            kn_ref[g:g + 1, :] = jnp.maximum(kn_ref[g:g + 1, :], jnp.where(comp0[0:1, :], n0, n1))
            for c in range(tp // tk):
                vt_ref[g, p * (tp // tk) + c] = (
                    proj[c * tk:(c + 1) * tk, v0 + g * LANES:v0 + (g + 1) * LANES].T.astype(BF16))


def _mix_in(x, mod9, w, rope_rows, *, cw, aw, q_scale, tm, tk, n_part):
    s, d = x.shape
    n_heads = aw // LANES
    assert (tm // n_part) % tk == 0 and n_heads <= SUBLANES
    row = lambda i: (i, 0)
    return pl.pallas_call(
        functools.partial(_mix_in_kernel, sub=1, cw=cw, aw=aw, q_scale=q_scale, n_part=n_part),
        grid=(s // tm,),
        in_specs=[pl.BlockSpec((tm, d), row),
                  _const_spec(mod9.shape),
                  _const_spec(w.shape),
                  _const_spec(rope_rows.shape)],
        out_specs=[pl.BlockSpec((tm, cw), row),
                   pl.BlockSpec((n_heads, LANES, tm), lambda i: (0, 0, i)),
                   pl.BlockSpec((tm, aw), row),
                   pl.BlockSpec((n_heads, tm // tk, LANES, tk), lambda i: (0, i, 0, 0)),
                   pl.BlockSpec((SUBLANES, LANES), lambda i: (0, 0))],
        out_shape=[jax.ShapeDtypeStruct((s, cw), F32),
                   jax.ShapeDtypeStruct((n_heads, LANES, s), BF16),
                   jax.ShapeDtypeStruct((s, aw), BF16),
                   jax.ShapeDtypeStruct((n_heads, s // tk, LANES, tk), BF16),
                   jax.ShapeDtypeStruct((SUBLANES, LANES), F32)],
        scratch_shapes=[pltpu.VMEM((tm, LANES), F32),
                        pltpu.VMEM((tm, LANES), F32)],
        compiler_params=pltpu.CompilerParams(dimension_semantics=("arbitrary",),
                                             vmem_limit_bytes=VMEM_LIMIT),
        name="mix_in",
    )(x, mod9, w, rope_rows)


def _dependent_zero(v):
    r, c = v.shape
    folded = jnp.sum(v.reshape(r // SUBLANES, SUBLANES, c), axis=0)
    folded = sum(folded[:, g * LANES:(g + 1) * LANES] for g in range(c // LANES))
    bits = lax.bitcast_convert_type(folded[0:1, :], jnp.uint32)
    return lax.bitcast_convert_type((bits >> 16) >> 16, F32)


def _conv_tile(i, n_tiles, prev_ref, cur_ref, next_ref, w_ref, cb_ref, g_ref, b_ref, o_ref, ext_ref, sh_ref, y_ref,
               rows):
    tm, cw = cur_ref.shape
    ext_ref[0:CONV_HALO, :] = jnp.where(i > 0, prev_ref[...], 0.0)
    ext_ref[CONV_HALO:CONV_HALO + tm, :] = cur_ref[...]
    ext_ref[CONV_HALO + tm:, :] = jnp.where(i < n_tiles - 1, next_ref[...], 0.0)
    span = sh_ref.shape[1]
    for b in range(SUBLANES):
        sh_ref[b] = ext_ref[b:b + span, :]
    base = CONV_HALO - CONV_PAD
    done = []
    for lc in range(cw // LANES):
        ls = slice(lc * LANES, (lc + 1) * LANES)
        for rc in range(tm // rows):
            r0 = rc * rows
            acc = jnp.zeros((rows, LANES), F32)
            for t in range(CONV_KERNEL):
                off = base + t
                a0 = r0 + SUBLANES * (off // SUBLANES)
                acc = acc + sh_ref[off % SUBLANES, a0:a0 + rows, ls] * w_ref[t:t + 1, ls]
            y_ref[r0:r0 + rows, ls] = acc
            done.append(_dependent_zero(acc))
    y = _layer_norm(y_ref[...] + cb_ref[...], g_ref[...], b_ref[...])
    y = y * _sigmoid(y)
    o_ref[...] = y.astype(BF16)
    done.append(_dependent_zero(y))
    return done


def _conv_specs(s, cw, tm, tile_index):
    nh = tm // CONV_HALO
    last = s // CONV_HALO - 1
    return [pl.BlockSpec((CONV_HALO, cw), lambda *g: (jnp.maximum(tile_index(*g) * nh - 1, 0), 0)),
            pl.BlockSpec((tm, cw), lambda *g: (tile_index(*g), 0)),
            pl.BlockSpec((CONV_HALO, cw), lambda *g: (jnp.minimum((tile_index(*g) + 1) * nh, last), 0))]


def _conv_scratch(tm, cw):
    return [pltpu.VMEM((tm + 2 * CONV_HALO, cw), F32),
            pltpu.VMEM((SUBLANES, tm + 2 * CONV_HALO - SUBLANES, cw), F32),
            pltpu.VMEM((tm, cw), F32)]


def _mixer_kernel(qt_ref, k_ref, vt_ref, kn_ref, lq1_ref, lk1_ref, lq2_ref, lk2_ref, g_ref,
                  up_ref, uc_ref, un_ref, cw_ref, cb_ref, cg_ref, cbeta_ref,
                  o_ref, conv_ref,
                  rhs_ref, acc_ref, kmax_ref, ext_ref, sh_ref, y_ref, *, tk, conv_rows, lam_init):
    step = pl.program_id(0) * pl.num_programs(1) + pl.program_id(1)
    n_steps = pl.num_programs(0) * pl.num_programs(1)

    hd2, tq = qt_ref.shape[1], qt_ref.shape[2]
    n = 2 * tq
    n_kv = k_ref.shape[0] // tk
    qt = qt_ref[0]
    row = lax.broadcasted_iota(jnp.int32, qt.shape, 0)
    zero = jnp.zeros_like(qt)
    rhs_ref[:, :tq] = jnp.where(row < DIFF_HEAD_DIM, qt, zero)
    rhs_ref[:, tq:] = jnp.where(row >= DIFF_HEAD_DIM, qt, zero)

    def k_block(j):
        return k_ref[pl.ds(pl.multiple_of(j * tk, tk), tk), :]

    @pl.when(pl.program_id(1) == 0)
    def _key_norm_bound():
        kn = kn_ref[pl.ds(pl.program_id(0), 1), :]
        col = lax.broadcasted_iota(jnp.int32, (1, n), 1)
        kmax_ref[...] = jnp.sqrt(jnp.where(col < tq, kn[:, 0:1], kn[:, DIFF_HEAD_DIM:DIFF_HEAD_DIM + 1]))

    conv_done = _conv_tile(step, n_steps, up_ref, uc_ref, un_ref, cw_ref, cb_ref, cg_ref, cbeta_ref, conv_ref,
                           ext_ref, sh_ref, y_ref, conv_rows)
    assert CONV_FIRST_BLOCK + pl.cdiv(len(conv_done), CONV_PIECES_PER_BLOCK) <= n_kv

    r32 = rhs_ref[...].astype(F32)
    qn = jnp.sqrt(jnp.sum(r32 * r32, axis=0, keepdims=True))
    m = qn * kmax_ref[...] * SHIFT_SLACK
    l8 = [jnp.zeros((8, tq), F32)] * 2
    acc = [jnp.zeros((hd2, tq), F32)] * 2
    e_prev = [None, None]
    for j in range(n_kv):
        if j >= CONV_FIRST_BLOCK:
            for piece in conv_done[:CONV_PIECES_PER_BLOCK]:
                m = m + jnp.concatenate([piece] * (n // LANES), axis=1)
            del conv_done[:CONV_PIECES_PER_BLOCK]
        for c in range(2):
            cols = slice(c * tq, (c + 1) * tq)
            s = jnp.dot(k_ref[j * tk:(j + 1) * tk, :], rhs_ref[:, cols], preferred_element_type=F32)
            if e_prev[c] is not None:
                acc[c] = acc[c] + jnp.dot(vt_ref[0, j - 1], e_prev[c], preferred_element_type=F32)
            e = jnp.exp2(s - m[:, cols])
            l8[c] = l8[c] + jnp.sum(e.reshape(tk // 8, 8, tq), axis=0)
            e_prev[c] = e.astype(BF16)
    acc = jnp.concatenate([acc[c] + jnp.dot(vt_ref[0, n_kv - 1], e_prev[c], preferred_element_type=F32)
                           for c in range(2)], axis=1)
    l = jnp.concatenate([jnp.sum(l8[c], axis=0, keepdims=True) for c in range(2)], axis=1)

    def finish(acc, l):
        o = acc * (1.0 / l)
        lam = (jnp.exp(jnp.sum(lq1_ref[...] * lk1_ref[...])) - jnp.exp(jnp.sum(lq2_ref[...] * lk2_ref[...]))
               + lam_init)
        o = o[:, :tq] - lam * o[:, tq:]
        ms = jnp.mean(o * o, axis=0, keepdims=True)
        o = o * lax.rsqrt(ms + LN_EPS) * g_ref[...] * (1.0 - lam_init)
        o_ref[...] = o.T.astype(BF16)

    finish(acc, l)

    @pl.when(jnp.logical_not(jnp.min(l) >= L_FLOOR))
    def _running_max_fallback():
        acc_ref[...] = jnp.zeros_like(acc_ref)

        def body(j, carry):
            m_run, l_run = carry
            s = jnp.dot(k_block(j), rhs_ref[...], preferred_element_type=F32)
            m_new = jnp.maximum(m_run, jnp.max(s, axis=0, keepdims=True))
            alpha = jnp.exp2(m_run - m_new)
            e = jnp.exp2(s - m_new)
            pv = jnp.dot(vt_ref[0, j], e.astype(BF16), preferred_element_type=F32)
            acc_ref[...] = alpha * acc_ref[...] + pv
            return m_new, alpha * l_run + jnp.sum(e, axis=0, keepdims=True)

        init = (jnp.full((1, n), -jnp.inf, F32), jnp.zeros((1, n), F32))
        _, l_run = lax.fori_loop(0, n_kv, body, init)
        finish(acc_ref[...], l_run)


def _mixer(qt, k, vt, kn, lq1, lk1, lq2, lk2, g_col, u, conv_w, conv_b, conv_g, conv_beta, *, tq, tk, conv_rows,
           lam_init):
    n_heads, hd2, s = qt.shape
    cw = u.shape[1]
    nq = s // tq
    tc = s // (n_heads * nq)
    assert tc % conv_rows == 0 and tc % CONV_HALO == 0
    lam_spec = _const_spec(lq1.shape)
    tile = lambda h, i: h * nq + i
    return pl.pallas_call(
        functools.partial(_mixer_kernel, tk=tk, conv_rows=conv_rows, lam_init=lam_init),
        grid=(n_heads, nq),
        in_specs=[pl.BlockSpec((1, hd2, tq), lambda h, i: (h, 0, i)),
                  pl.BlockSpec((s, hd2), lambda h, i: (0, h)),
                  pl.BlockSpec((1, s // tk, hd2, tk), lambda h, i: (h, 0, 0, 0)),
                  _const_spec(kn.shape),
                  lam_spec, lam_spec, lam_spec, lam_spec,
                  _const_spec(g_col.shape),
                  *_conv_specs(s, cw, tc, tile),
                  _const_spec(conv_w.shape),
                  _const_spec(conv_b.shape),
                  _const_spec(conv_g.shape),
                  _const_spec(conv_beta.shape)],
        out_specs=[pl.BlockSpec((tq, hd2), lambda h, i: (i, h)),
                   pl.BlockSpec((tc, cw), lambda h, i: (tile(h, i), 0))],
        out_shape=[jax.ShapeDtypeStruct((s, n_heads * hd2), BF16),
                   jax.ShapeDtypeStruct((s, cw), BF16)],
        scratch_shapes=[pltpu.VMEM((hd2, 2 * tq), BF16),
                        pltpu.VMEM((hd2, 2 * tq), F32),
                        pltpu.VMEM((1, 2 * tq), F32),
                        *_conv_scratch(tc, cw)],
        compiler_params=pltpu.CompilerParams(dimension_semantics=("arbitrary", "arbitrary"),
                                             vmem_limit_bytes=VMEM_LIMIT),
        name="mixer",
    )(qt, k, vt, kn, lq1, lk1, lq2, lk2, g_col, u, u, u, conv_w, conv_b, conv_g, conv_beta)


def _mix_out_kernel(conv_ref, attn_ref, x_ref, mod_ref, w_ref, g_ref, b_ref, o_ref, *, sub, n_part):
    cw = conv_ref.shape[1]
    tp = x_ref.shape[0] // n_part
    gate_c = mod_ref[3 * sub + 2:3 * sub + 3, :]
    ys = []
    for p in range(n_part):
        rows = slice(p * tp, (p + 1) * tp)
        ys.append(jnp.dot(conv_ref[rows, :], w_ref[:cw, :], preferred_element_type=F32)
                  + jnp.dot(attn_ref[rows, :], w_ref[cw:, :], preferred_element_type=F32))
    for p in range(n_part):
        rows = slice(p * tp, (p + 1) * tp)
        z = ALPHA * x_ref[rows, :] + (1.0 + gate_c) * ys[p]
        o_ref[rows, :] = _layer_norm(z, g_ref[...], b_ref[...])


def _mix_out(conv, attn, x, mod9, w, g, b, *, tm, n_part):
    s, d = x.shape
    row = lambda i: (i, 0)
    return pl.pallas_call(
        functools.partial(_mix_out_kernel, sub=1, n_part=n_part),
        grid=(s // tm,),
        in_specs=[pl.BlockSpec((tm, conv.shape[1]), row),
                  pl.BlockSpec((tm, attn.shape[1]), row),
                  pl.BlockSpec((tm, d), row),
                  _const_spec(mod9.shape),
                  _const_spec(w.shape),
                  _const_spec(g.shape),
                  _const_spec(b.shape)],
        out_specs=pl.BlockSpec((tm, d), row),
        out_shape=jax.ShapeDtypeStruct((s, d), F32),
        compiler_params=pltpu.CompilerParams(dimension_semantics=("arbitrary",),
                                             vmem_limit_bytes=VMEM_LIMIT),
        name="mix_out",
    )(conv, attn, x, mod9, w, g, b)


def _rope_rows():
    inv_freq = ROPE_THETA ** (-jnp.arange(0, ROT_DIM, 2, dtype=F32) / ROT_DIM)
    half = ROT_DIM // 2
    zeros_h = jnp.zeros((half,), F32)
    zeros_p = jnp.zeros((DIFF_HEAD_DIM - ROT_DIM,), F32)
    ones_h = jnp.ones((half,), F32)
    reps = LANES // DIFF_HEAD_DIM
    freq = jnp.tile(jnp.concatenate([inv_freq, inv_freq, zeros_p]), reps)
    neg_first = jnp.tile(jnp.concatenate([-ones_h, zeros_h, zeros_p]), reps)
    pos_second = jnp.tile(jnp.concatenate([zeros_h, ones_h, zeros_p]), reps)
    return jnp.stack([freq, neg_first, pos_second])


def kernel(x, c, w_ada, b_ada, ffn1_w_in, ffn1_w_out, ln1_g, ln1_b, mix_w_in, conv_w, conv_b, conv_ln_g,
           conv_ln_b, lambda_q1, lambda_k1, lambda_q2, lambda_k2, subln_g, mix_w_out, ln2_g, ln2_b,
           ffn2_w_in, ffn2_w_out, ln3_g, ln3_b):
    batch, s, d = x.shape
    assert batch == 1 and w_ada.shape[0] == DEPTH == 1
    cw = conv_w.shape[2]
    aw = (mix_w_in.shape[2] - 2 * cw) // 3
    hd2 = 2 * DIFF_HEAD_DIM
    n_heads = aw // hd2
    lam_init = 0.8 - 0.6 * math.exp(-0.3 * 0)
    q_scale = math.log2(math.e) / math.sqrt(DIFF_HEAD_DIM)
    t = TILES
    for rows in (t.ffn_rows, t.proj_rows, t.attn_q, t.attn_kv):
        assert s % rows == 0

    mod9 = _ada(c.reshape(d, 1), w_ada[0], b_ada, tn=t.ada_cols).reshape(9, d)
    x0 = x[0]
    x1 = _ffn(x0, mod9, ffn1_w_in[0], ffn1_w_out[0], ln1_g, ln1_b,
              sub=0, weight=0.5, tm=t.ffn_rows, tf=t.ffn_cols, n_out_part=t.ffn_out_parts)

    u, qt, k, vt, kn = _mix_in(x1, mod9, mix_w_in[0], _rope_rows(),
                               cw=cw, aw=aw, q_scale=q_scale, tm=t.proj_rows, tk=t.attn_kv, n_part=t.proj_parts)
    attn, conv = _mixer(qt, k, vt, kn, lambda_q1, lambda_k1, lambda_q2, lambda_k2, subln_g.reshape(hd2, 1),
                        u, conv_w[0], conv_b, conv_ln_g, conv_ln_b,
                        tq=t.attn_q, tk=t.attn_kv, conv_rows=t.conv_rows, lam_init=lam_init)
    x2 = _mix_out(conv, attn, x1, mod9, mix_w_out[0].astype(BF16), ln2_g, ln2_b,
                  tm=t.proj_rows, n_part=t.proj_parts)

    x3 = _ffn(x2, mod9, ffn2_w_in[0], ffn2_w_out[0], ln3_g, ln3_b,
              sub=2, weight=0.5, tm=t.ffn_rows, tf=t.ffn_cols, n_out_part=t.ffn_out_parts)
    return x3[None]
```

```python
import functools
import math
from typing import NamedTuple

import jax
import jax.numpy as jnp
from jax import lax
from jax.experimental import pallas as pl
from jax.experimental.pallas import tpu as pltpu

F32 = jnp.float32
BF16 = jnp.bfloat16

DEPTH = 1
ALPHA = (2.0 * DEPTH) ** 0.25
LN_EPS = 1e-5
DIFF_HEAD_DIM = 64
ROT_DIM = DIFF_HEAD_DIM // 4
ROPE_THETA = 500000.0
CONV_KERNEL = 31
CONV_PAD = (CONV_KERNEL - 1) // 2
CONV_HALO = 16
LANES = 128
SUBLANES = 8
SHIFT_SLACK = 1.0 + 2.0 ** -10
L_FLOOR = 2.0 ** -80
CONV_FIRST_BLOCK = 12
CONV_PIECES_PER_BLOCK = 1
VMEM_LIMIT = 56 * 1024 * 1024


class _Tiles(NamedTuple):
    ada_cols: int = 1152
    ffn_rows: int = 512
    ffn_cols: int = 256
    ffn_out_parts: int = 2
    proj_rows: int = 1024
    proj_parts: int = 4
    attn_q: int = 512
    attn_kv: int = 256
    conv_rows: int = 32


TILES = _Tiles()


def _sigmoid(x):
    return 1.0 / (1.0 + jnp.exp(-x))


def _layer_norm(z, g, b):
    mu = jnp.mean(z, axis=-1, keepdims=True)
    zc = z - mu
    var = jnp.mean(zc * zc, axis=-1, keepdims=True)
    return zc * lax.rsqrt(var + LN_EPS) * g + b


def _modulate(x, mod_ref, sub):
    shift = mod_ref[3 * sub:3 * sub + 1, :]
    scale = mod_ref[3 * sub + 1:3 * sub + 2, :]
    return x * (1.0 + scale) + shift


def _const_spec(shape):
    return pl.BlockSpec(shape, lambda *_: (0,) * len(shape), pipeline_mode=pl.Buffered(1))


def _ada_kernel(c_ref, w_ref, b_ref, o_ref):
    c = c_ref[...]
    ca = c * _sigmoid(c)
    o_ref[...] = jnp.sum(ca * w_ref[...], axis=0, keepdims=True) + b_ref[...]


def _ada(c_col, w, b_row, tn):
    d, n = w.shape
    assert n % tn == 0 and tn % LANES == 0
    return pl.pallas_call(
        _ada_kernel,
        grid=(n // tn,),
        in_specs=[pl.BlockSpec((d, 1), lambda j: (0, 0)),
                  pl.BlockSpec((d, tn), lambda j: (0, j)),
                  pl.BlockSpec((1, tn), lambda j: (0, j))],
        out_specs=pl.BlockSpec((1, tn), lambda j: (0, j)),
        out_shape=jax.ShapeDtypeStruct((1, n), F32),
        compiler_params=pltpu.CompilerParams(dimension_semantics=("arbitrary",),
                                             vmem_limit_bytes=VMEM_LIMIT),
        name="ada",
    )(c_col, w, b_row)


def _ffn_kernel(x_ref, mod_ref, win_ref, wout_ref, g_ref, b_ref, o_ref, act_ref, *, sub, weight, tf, n_out_part):
    x = x_ref[...]
    d_ff = wout_ref.shape[0]
    h = _modulate(x, mod_ref, sub).astype(win_ref.dtype)
    for c in range(d_ff // tf):
        gate = jnp.dot(h, win_ref[:, c * tf:(c + 1) * tf], preferred_element_type=F32)
        up = jnp.dot(h, win_ref[:, d_ff + c * tf:d_ff + (c + 1) * tf], preferred_element_type=F32)
        act_ref[:, c * tf:(c + 1) * tf] = (gate * _sigmoid(gate) * up).astype(act_ref.dtype)
    gate_c = mod_ref[3 * sub + 2:3 * sub + 3, :]
    tp = x.shape[0] // n_out_part
    ys = [jnp.dot(act_ref[p * tp:(p + 1) * tp, :], wout_ref[...], preferred_element_type=F32)
          for p in range(n_out_part)]
    for p, y in enumerate(ys):
        rows = slice(p * tp, (p + 1) * tp)
        z = ALPHA * x[rows, :] + weight * (1.0 + gate_c) * y
        o_ref[rows, :] = _layer_norm(z, g_ref[...], b_ref[...])


def _ffn(x, mod9, w_in, w_out, g, b, *, sub, weight, tm, tf, n_out_part):
    s, d = x.shape
    d_ff = w_out.shape[0]
    assert s % tm == 0 and d_ff % tf == 0 and w_in.shape == (d, 2 * d_ff) and tm % (SUBLANES * n_out_part) == 0
    return pl.pallas_call(
        functools.partial(_ffn_kernel, sub=sub, weight=weight, tf=tf, n_out_part=n_out_part),
        grid=(s // tm,),
        in_specs=[pl.BlockSpec((tm, d), lambda i: (i, 0)),
                  _const_spec(mod9.shape),
                  _const_spec(w_in.shape),
                  _const_spec(w_out.shape),
                  _const_spec(g.shape),
                  _const_spec(b.shape)],
        out_specs=pl.BlockSpec((tm, d), lambda i: (i, 0)),
        out_shape=jax.ShapeDtypeStruct((s, d), F32),
        scratch_shapes=[pltpu.VMEM((tm, d_ff), w_out.dtype)],
        compiler_params=pltpu.CompilerParams(dimension_semantics=("arbitrary",),
                                             vmem_limit_bytes=VMEM_LIMIT),
        name=f"ffn{sub}",
    )(x, mod9, w_in, w_out, g, b)


def _mix_in_kernel(x_ref, mod_ref, w_ref, rope_ref, u_ref, qt_ref, k_ref, vt_ref, kn_ref, cos_ref, sin_ref, *,
                   sub, cw, aw, q_scale, n_part):
    tm = x_ref.shape[0]
    tp = tm // n_part
    tk = vt_ref.shape[3]
    freq = rope_ref[0:1, :]

    @pl.when(pl.program_id(0) == 0)
    def _in_tile_angles():
        ang = lax.broadcasted_iota(jnp.int32, (tm, LANES), 0).astype(F32) * freq
        cos_ref[...] = jnp.cos(ang)
        sin_ref[...] = jnp.sin(ang)
        kn_ref[...] = jnp.zeros_like(kn_ref)

    comp0 = lax.broadcasted_iota(jnp.int32, (tp, LANES), 1) < DIFF_HEAD_DIM
    projs = []
    for p in range(n_part):
        h = _modulate(x_ref[p * tp:(p + 1) * tp, :], mod_ref, sub).astype(w_ref.dtype)
        projs.append(jnp.dot(h, w_ref[...], preferred_element_type=F32))

    ang0 = (pl.program_id(0) * tm).astype(F32) * freq
    c0, s0 = jnp.cos(ang0), jnp.sin(ang0)
    half = ROT_DIM // 2
    q0, k0, v0 = 2 * cw, 2 * cw + aw, 2 * cw + 2 * aw
    for p, proj in enumerate(projs):
        rows = slice(p * tp, (p + 1) * tp)
        u_ref[rows, :] = proj[:, :cw] * _sigmoid(proj[:, cw:2 * cw])
        rc = c0 * cos_ref[rows, :] - s0 * sin_ref[rows, :]
        sin = s0 * cos_ref[rows, :] + c0 * sin_ref[rows, :]
        rs1, rs2 = sin * rope_ref[1:2, :], sin * rope_ref[2:3, :]

        def rope(t):
            return t * rc + pltpu.roll(t, LANES - half, 1) * rs1 + pltpu.roll(t, half, 1) * rs2

        for g in range(aw // LANES):
            sl = slice(g * LANES, (g + 1) * LANES)
            qt_ref[g, :, rows] = (rope(proj[:, q0 + g * LANES:q0 + (g + 1) * LANES]) * q_scale).T.astype(BF16)
            kb = rope(proj[:, k0 + g * LANES:k0 + (g + 1) * LANES]).astype(BF16)
            k_ref[rows, sl] = kb
            kf = kb.astype(F32)
            sq = kf * kf
            n0 = jnp.max(jnp.sum(jnp.where(comp0, sq, 0.0), axis=1, keepdims=True), axis=0, keepdims=True)
            n1 = jnp.max(jnp.sum(jnp.where(comp0, 0.0, sq), axis=1, keepdims=True), axis=0, keepdims=True)
            kn_ref[g:g + 1, :] = jnp.maximum(kn_ref[g:g + 1, :], jnp.where(comp0[0:1, :], n0, n1))
            for c in range(tp // tk):
                vt_ref[g, p * (tp // tk) + c] = (
                    proj[c * tk:(c + 1) * tk, v0 + g * LANES:v0 + (g + 1) * LANES].T.astype(BF16))


def _mix_in(x, mod9, w, rope_rows, *, cw, aw, q_scale, tm, tk, n_part):
    s, d = x.shape
    n_heads = aw // LANES
    assert (tm // n_part) % tk == 0 and n_heads <= SUBLANES
    row = lambda i: (i, 0)
    return pl.pallas_call(
        functools.partial(_mix_in_kernel, sub=1, cw=cw, aw=aw, q_scale=q_scale, n_part=n_part),
        grid=(s // tm,),
        in_specs=[pl.BlockSpec((tm, d), row),
                  _const_spec(mod9.shape),
                  _const_spec(w.shape),
                  _const_spec(rope_rows.shape)],
        out_specs=[pl.BlockSpec((tm, cw), row),
                   pl.BlockSpec((n_heads, LANES, tm), lambda i: (0, 0, i)),
                   pl.BlockSpec((tm, aw), row),
                   pl.BlockSpec((n_heads, tm // tk, LANES, tk), lambda i: (0, i, 0, 0)),
                   pl.BlockSpec((SUBLANES, LANES), lambda i: (0, 0))],
        out_shape=[jax.ShapeDtypeStruct((s, cw), F32),
                   jax.ShapeDtypeStruct((n_heads, LANES, s), BF16),
                   jax.ShapeDtypeStruct((s, aw), BF16),
                   jax.ShapeDtypeStruct((n_heads, s // tk, LANES, tk), BF16),
                   jax.ShapeDtypeStruct((SUBLANES, LANES), F32)],
        scratch_shapes=[pltpu.VMEM((tm, LANES), F32),
                        pltpu.VMEM((tm, LANES), F32)],
        compiler_params=pltpu.CompilerParams(dimension_semantics=("arbitrary",),
                                             vmem_limit_bytes=VMEM_LIMIT),
        name="mix_in",
    )(x, mod9, w, rope_rows)


def _dependent_zero(v):
    r, c = v.shape
    folded = jnp.sum(v.reshape(r // SUBLANES, SUBLANES, c), axis=0)
    folded = sum(folded[:, g * LANES:(g + 1) * LANES] for g in range(c // LANES))
    bits = lax.bitcast_convert_type(folded[0:1, :], jnp.uint32)
    return lax.bitcast_convert_type((bits >> 16) >> 16, F32)


def _conv_tile(i, n_tiles, prev_ref, cur_ref, next_ref, w_ref, cb_ref, g_ref, b_ref, o_ref, ext_ref, sh_ref, y_ref,
               rows):
    tm, cw = cur_ref.shape
    ext_ref[0:CONV_HALO, :] = jnp.where(i > 0, prev_ref[...], 0.0)
    ext_ref[CONV_HALO:CONV_HALO + tm, :] = cur_ref[...]
    ext_ref[CONV_HALO + tm:, :] = jnp.where(i < n_tiles - 1, next_ref[...], 0.0)
    span = sh_ref.shape[1]
    for b in range(SUBLANES):
        sh_ref[b] = ext_ref[b:b + span, :]
    base = CONV_HALO - CONV_PAD
    done = []
    for lc in range(cw // LANES):
        ls = slice(lc * LANES, (lc + 1) * LANES)
        for rc in range(tm // rows):
            r0 = rc * rows
            acc = jnp.zeros((rows, LANES), F32)
            for t in range(CONV_KERNEL):
                off = base + t
                a0 = r0 + SUBLANES * (off // SUBLANES)
                acc = acc + sh_ref[off % SUBLANES, a0:a0 + rows, ls] * w_ref[t:t + 1, ls]
            y_ref[r0:r0 + rows, ls] = acc
            done.append(_dependent_zero(acc))
    y = _layer_norm(y_ref[...] + cb_ref[...], g_ref[...], b_ref[...])
    y = y * _sigmoid(y)
    o_ref[...] = y.astype(BF16)
    done.append(_dependent_zero(y))
    return done


def _conv_specs(s, cw, tm, tile_index):
    nh = tm // CONV_HALO
    last = s // CONV_HALO - 1
    return [pl.BlockSpec((CONV_HALO, cw), lambda *g: (jnp.maximum(tile_index(*g) * nh - 1, 0), 0)),
            pl.BlockSpec((tm, cw), lambda *g: (tile_index(*g), 0)),
            pl.BlockSpec((CONV_HALO, cw), lambda *g: (jnp.minimum((tile_index(*g) + 1) * nh, last), 0))]


def _conv_scratch(tm, cw):
    return [pltpu.VMEM((tm + 2 * CONV_HALO, cw), F32),
            pltpu.VMEM((SUBLANES, tm + 2 * CONV_HALO - SUBLANES, cw), F32),
            pltpu.VMEM((tm, cw), F32)]


def _mixer_kernel(qt_ref, k_ref, vt_ref, kn_ref, lq1_ref, lk1_ref, lq2_ref, lk2_ref, g_ref,
                  up_ref, uc_ref, un_ref, cw_ref, cb_ref, cg_ref, cbeta_ref,
                  o_ref, conv_ref,
                  rhs_ref, acc_ref, kmax_ref, ext_ref, sh_ref, y_ref, *, tk, conv_rows, lam_init):
    step = pl.program_id(0) * pl.num_programs(1) + pl.program_id(1)
    n_steps = pl.num_programs(0) * pl.num_programs(1)

    hd2, tq = qt_ref.shape[1], qt_ref.shape[2]
    n = 2 * tq
    n_kv = k_ref.shape[0] // tk
    qt = qt_ref[0]
    row = lax.broadcasted_iota(jnp.int32, qt.shape, 0)
    zero = jnp.zeros_like(qt)
    rhs_ref[:, :tq] = jnp.where(row < DIFF_HEAD_DIM, qt, zero)
    rhs_ref[:, tq:] = jnp.where(row >= DIFF_HEAD_DIM, qt, zero)

    def k_block(j):
        return k_ref[pl.ds(pl.multiple_of(j * tk, tk), tk), :]

    @pl.when(pl.program_id(1) == 0)
    def _key_norm_bound():
        kn = kn_ref[pl.ds(pl.program_id(0), 1), :]
        col = lax.broadcasted_iota(jnp.int32, (1, n), 1)
        kmax_ref[...] = jnp.sqrt(jnp.where(col < tq, kn[:, 0:1], kn[:, DIFF_HEAD_DIM:DIFF_HEAD_DIM + 1]))

    conv_done = _conv_tile(step, n_steps, up_ref, uc_ref, un_ref, cw_ref, cb_ref, cg_ref, cbeta_ref, conv_ref,
                           ext_ref, sh_ref, y_ref, conv_rows)
    assert CONV_FIRST_BLOCK + pl.cdiv(len(conv_done), CONV_PIECES_PER_BLOCK) <= n_kv

    r32 = rhs_ref[...].astype(F32)
    qn = jnp.sqrt(jnp.sum(r32 * r32, axis=0, keepdims=True))
    m = qn * kmax_ref[...] * SHIFT_SLACK
    l8 = jnp.zeros((8, n), F32)
    acc = jnp.zeros((hd2, n), F32)
    e_prev = None
    for j in range(n_kv):
        s = jnp.dot(k_ref[j * tk:(j + 1) * tk, :], rhs_ref[...], preferred_element_type=F32)
        if e_prev is not None:
            acc = acc + jnp.dot(vt_ref[0, j - 1], e_prev, preferred_element_type=F32)
        if j >= CONV_FIRST_BLOCK:
            for piece in conv_done[:CONV_PIECES_PER_BLOCK]:
                m = m + jnp.concatenate([piece] * (n // LANES), axis=1)
            del conv_done[:CONV_PIECES_PER_BLOCK]
        e = jnp.exp2(s - m)
        l8 = l8 + jnp.sum(e.reshape(tk // 8, 8, n), axis=0)
        e_prev = e.astype(BF16)
    acc = acc + jnp.dot(vt_ref[0, n_kv - 1], e_prev, preferred_element_type=F32)
    l = jnp.sum(l8, axis=0, keepdims=True)

    def finish(acc, l):
        o = acc * (1.0 / l)
        lam = (jnp.exp(jnp.sum(lq1_ref[...] * lk1_ref[...])) - jnp.exp(jnp.sum(lq2_ref[...] * lk2_ref[...]))
               + lam_init)
        o = o[:, :tq] - lam * o[:, tq:]
        ms = jnp.mean(o * o, axis=0, keepdims=True)
        o = o * lax.rsqrt(ms + LN_EPS) * g_ref[...] * (1.0 - lam_init)
        o_ref[...] = o.T.astype(BF16)

    finish(acc, l)

    @pl.when(jnp.logical_not(jnp.min(l) >= L_FLOOR))
    def _running_max_fallback():
        acc_ref[...] = jnp.zeros_like(acc_ref)

        def body(j, carry):
            m_run, l_run = carry
            s = jnp.dot(k_block(j), rhs_ref[...], preferred_element_type=F32)
            m_new = jnp.maximum(m_run, jnp.max(s, axis=0, keepdims=True))
            alpha = jnp.exp2(m_run - m_new)
            e = jnp.exp2(s - m_new)
            pv = jnp.dot(vt_ref[0, j], e.astype(BF16), preferred_element_type=F32)
            acc_ref[...] = alpha * acc_ref[...] + pv
            return m_new, alpha * l_run + jnp.sum(e, axis=0, keepdims=True)

        init = (jnp.full((1, n), -jnp.inf, F32), jnp.zeros((1, n), F32))
        _, l_run = lax.fori_loop(0, n_kv, body, init)
        finish(acc_ref[...], l_run)


def _mixer(qt, k, vt, kn, lq1, lk1, lq2, lk2, g_col, u, conv_w, conv_b, conv_g, conv_beta, *, tq, tk, conv_rows,
           lam_init):
    n_heads, hd2, s = qt.shape
    cw = u.shape[1]
    nq = s // tq
    tc = s // (n_heads * nq)
    assert tc % conv_rows == 0 and tc % CONV_HALO == 0
    lam_spec = _const_spec(lq1.shape)
    tile = lambda h, i: h * nq + i
    return pl.pallas_call(
        functools.partial(_mixer_kernel, tk=tk, conv_rows=conv_rows, lam_init=lam_init),
        grid=(n_heads, nq),
        in_specs=[pl.BlockSpec((1, hd2, tq), lambda h, i: (h, 0, i)),
                  pl.BlockSpec((s, hd2), lambda h, i: (0, h)),
                  pl.BlockSpec((1, s // tk, hd2, tk), lambda h, i: (h, 0, 0, 0)),
                  _const_spec(kn.shape),
                  lam_spec, lam_spec, lam_spec, lam_spec,
                  _const_spec(g_col.shape),
                  *_conv_specs(s, cw, tc, tile),
                  _const_spec(conv_w.shape),
                  _const_spec(conv_b.shape),
                  _const_spec(conv_g.shape),
                  _const_spec(conv_beta.shape)],
        out_specs=[pl.BlockSpec((tq, hd2), lambda h, i: (i, h)),
                   pl.BlockSpec((tc, cw), lambda h, i: (tile(h, i), 0))],
        out_shape=[jax.ShapeDtypeStruct((s, n_heads * hd2), BF16),
                   jax.ShapeDtypeStruct((s, cw), BF16)],
        scratch_shapes=[pltpu.VMEM((hd2, 2 * tq), BF16),
                        pltpu.VMEM((hd2, 2 * tq), F32),
                        pltpu.VMEM((1, 2 * tq), F32),
                        *_conv_scratch(tc, cw)],
        compiler_params=pltpu.CompilerParams(dimension_semantics=("arbitrary", "arbitrary"),
                                             vmem_limit_bytes=VMEM_LIMIT),
        name="mixer",
    )(qt, k, vt, kn, lq1, lk1, lq2, lk2, g_col, u, u, u, conv_w, conv_b, conv_g, conv_beta)


def _mix_out_kernel(conv_ref, attn_ref, x_ref, mod_ref, w_ref, g_ref, b_ref, o_ref, *, sub, n_part):
    cw = conv_ref.shape[1]
    tp = x_ref.shape[0] // n_part
    gate_c = mod_ref[3 * sub + 2:3 * sub + 3, :]
    ys = []
    for p in range(n_part):
        rows = slice(p * tp, (p + 1) * tp)
        ys.append(jnp.dot(conv_ref[rows, :], w_ref[:cw, :], preferred_element_type=F32)
                  + jnp.dot(attn_ref[rows, :], w_ref[cw:, :], preferred_element_type=F32))
    for p in range(n_part):
        rows = slice(p * tp, (p + 1) * tp)
        z = ALPHA * x_ref[rows, :] + (1.0 + gate_c) * ys[p]
        o_ref[rows, :] = _layer_norm(z, g_ref[...], b_ref[...])


def _mix_out(conv, attn, x, mod9, w, g, b, *, tm, n_part):
    s, d = x.shape
    row = lambda i: (i, 0)
    return pl.pallas_call(
        functools.partial(_mix_out_kernel, sub=1, n_part=n_part),
        grid=(s // tm,),
        in_specs=[pl.BlockSpec((tm, conv.shape[1]), row),
                  pl.BlockSpec((tm, attn.shape[1]), row),
                  pl.BlockSpec((tm, d), row),
                  _const_spec(mod9.shape),
                  _const_spec(w.shape),
                  _const_spec(g.shape),
                  _const_spec(b.shape)],
        out_specs=pl.BlockSpec((tm, d), row),
        out_shape=jax.ShapeDtypeStruct((s, d), F32),
        compiler_params=pltpu.CompilerParams(dimension_semantics=("arbitrary",),
                                             vmem_limit_bytes=VMEM_LIMIT),
        name="mix_out",
    )(conv, attn, x, mod9, w, g, b)


def _rope_rows():
    inv_freq = ROPE_THETA ** (-jnp.arange(0, ROT_DIM, 2, dtype=F32) / ROT_DIM)
    half = ROT_DIM // 2
    zeros_h = jnp.zeros((half,), F32)
    zeros_p = jnp.zeros((DIFF_HEAD_DIM - ROT_DIM,), F32)
    ones_h = jnp.ones((half,), F32)
    reps = LANES // DIFF_HEAD_DIM
    freq = jnp.tile(jnp.concatenate([inv_freq, inv_freq, zeros_p]), reps)
    neg_first = jnp.tile(jnp.concatenate([-ones_h, zeros_h, zeros_p]), reps)
    pos_second = jnp.tile(jnp.concatenate([zeros_h, ones_h, zeros_p]), reps)
    return jnp.stack([freq, neg_first, pos_second])


def kernel(x, c, w_ada, b_ada, ffn1_w_in, ffn1_w_out, ln1_g, ln1_b, mix_w_in, conv_w, conv_b, conv_ln_g,
           conv_ln_b, lambda_q1, lambda_k1, lambda_q2, lambda_k2, subln_g, mix_w_out, ln2_g, ln2_b,
           ffn2_w_in, ffn2_w_out, ln3_g, ln3_b):
    batch, s, d = x.shape
    assert batch == 1 and w_ada.shape[0] == DEPTH == 1
    cw = conv_w.shape[2]
    aw = (mix_w_in.shape[2] - 2 * cw) // 3
    hd2 = 2 * DIFF_HEAD_DIM
    n_heads = aw // hd2
    lam_init = 0.8 - 0.6 * math.exp(-0.3 * 0)
    q_scale = math.log2(math.e) / math.sqrt(DIFF_HEAD_DIM)
    t = TILES
    for rows in (t.ffn_rows, t.proj_rows, t.attn_q, t.attn_kv):
        assert s % rows == 0

    mod9 = _ada(c.reshape(d, 1), w_ada[0], b_ada, tn=t.ada_cols).reshape(9, d)
    x0 = x[0]
    x1 = _ffn(x0, mod9, ffn1_w_in[0], ffn1_w_out[0], ln1_g, ln1_b,
              sub=0, weight=0.5, tm=t.ffn_rows, tf=t.ffn_cols, n_out_part=t.ffn_out_parts)

    u, qt, k, vt, kn = _mix_in(x1, mod9, mix_w_in[0], _rope_rows(),
                               cw=cw, aw=aw, q_scale=q_scale, tm=t.proj_rows, tk=t.attn_kv, n_part=t.proj_parts)
    attn, conv = _mixer(qt, k, vt, kn, lambda_q1, lambda_k1, lambda_q2, lambda_k2, subln_g.reshape(hd2, 1),
                        u, conv_w[0], conv_b, conv_ln_g, conv_ln_b,
                        tq=t.attn_q, tk=t.attn_kv, conv_rows=t.conv_rows, lam_init=lam_init)
    x2 = _mix_out(conv, attn, x1, mod9, mix_w_out[0].astype(BF16), ln2_g, ln2_b,
                  tm=t.proj_rows, n_part=t.proj_parts)

    x3 = _ffn(x2, mod9, ffn2_w_in[0], ffn2_w_out[0], ln3_g, ln3_b,
              sub=2, weight=0.5, tm=t.ffn_rows, tf=t.ffn_cols, n_out_part=t.ffn_out_parts)
    return x3[None]
```

```python
import functools
import math
from typing import NamedTuple

import jax
import jax.numpy as jnp
from jax import lax
from jax.experimental import pallas as pl
from jax.experimental.pallas import tpu as pltpu

F32 = jnp.float32
BF16 = jnp.bfloat16

DEPTH = 1
ALPHA = (2.0 * DEPTH) ** 0.25
LN_EPS = 1e-5
DIFF_HEAD_DIM = 64
ROT_DIM = DIFF_HEAD_DIM // 4
ROPE_THETA = 500000.0
CONV_KERNEL = 31
CONV_PAD = (CONV_KERNEL - 1) // 2
CONV_HALO = 16
LANES = 128
SUBLANES = 8
SHIFT_SLACK = 1.0 + 2.0 ** -10
L_FLOOR = 2.0 ** -80
CONV_FIRST_BLOCK = 12
CONV_PIECES_PER_BLOCK = 1
VMEM_LIMIT = 56 * 1024 * 1024


class _Tiles(NamedTuple):
    ada_cols: int = 1152
    ffn_rows: int = 512
    ffn_cols: int = 256
    ffn_out_parts: int = 2
    proj_rows: int = 1024
    proj_parts: int = 4
    attn_q: int = 512
    attn_kv: int = 256
    conv_rows: int = 32


TILES = _Tiles()


def _sigmoid(x):
    return 1.0 / (1.0 + jnp.exp(-x))


def _layer_norm(z, g, b):
    mu = jnp.mean(z, axis=-1, keepdims=True)
    zc = z - mu
    var = jnp.mean(zc * zc, axis=-1, keepdims=True)
    return zc * lax.rsqrt(var + LN_EPS) * g + b


def _modulate(x, mod_ref, sub):
    shift = mod_ref[3 * sub:3 * sub + 1, :]
    scale = mod_ref[3 * sub + 1:3 * sub + 2, :]
    return x * (1.0 + scale) + shift


def _const_spec(shape):
    return pl.BlockSpec(shape, lambda *_: (0,) * len(shape), pipeline_mode=pl.Buffered(1))


def _ada_kernel(c_ref, w_ref, b_ref, o_ref):
    c = c_ref[...]
    ca = c * _sigmoid(c)
    o_ref[...] = jnp.sum(ca * w_ref[...], axis=0, keepdims=True) + b_ref[...]


def _ada(c_col, w, b_row, tn):
    d, n = w.shape
    assert n % tn == 0 and tn % LANES == 0
    return pl.pallas_call(
        _ada_kernel,
        grid=(n // tn,),
        in_specs=[pl.BlockSpec((d, 1), lambda j: (0, 0)),
                  pl.BlockSpec((d, tn), lambda j: (0, j)),
                  pl.BlockSpec((1, tn), lambda j: (0, j))],
        out_specs=pl.BlockSpec((1, tn), lambda j: (0, j)),
        out_shape=jax.ShapeDtypeStruct((1, n), F32),
        compiler_params=pltpu.CompilerParams(dimension_semantics=("arbitrary",),
                                             vmem_limit_bytes=VMEM_LIMIT),
        name="ada",
    )(c_col, w, b_row)


def _ffn_kernel(x_ref, mod_ref, win_ref, wout_ref, g_ref, b_ref, o_ref, act_ref, *, sub, weight, tf, n_out_part):
    x = x_ref[...]
    d_ff = wout_ref.shape[0]
    h = _modulate(x, mod_ref, sub).astype(win_ref.dtype)
    for c in range(d_ff // tf):
        gate = jnp.dot(h, win_ref[:, c * tf:(c + 1) * tf], preferred_element_type=F32)
        up = jnp.dot(h, win_ref[:, d_ff + c * tf:d_ff + (c + 1) * tf], preferred_element_type=F32)
        act_ref[:, c * tf:(c + 1) * tf] = (gate * _sigmoid(gate) * up).astype(act_ref.dtype)
    gate_c = mod_ref[3 * sub + 2:3 * sub + 3, :]
    tp = x.shape[0] // n_out_part
    ys = [jnp.dot(act_ref[p * tp:(p + 1) * tp, :], wout_ref[...], preferred_element_type=F32)
          for p in range(n_out_part)]
    for p, y in enumerate(ys):
        rows = slice(p * tp, (p + 1) * tp)
        z = ALPHA * x[rows, :] + weight * (1.0 + gate_c) * y
        o_ref[rows, :] = _layer_norm(z, g_ref[...], b_ref[...])


def _ffn(x, mod9, w_in, w_out, g, b, *, sub, weight, tm, tf, n_out_part):
    s, d = x.shape
    d_ff = w_out.shape[0]
    assert s % tm == 0 and d_ff % tf == 0 and w_in.shape == (d, 2 * d_ff) and tm % (SUBLANES * n_out_part) == 0
    return pl.pallas_call(
        functools.partial(_ffn_kernel, sub=sub, weight=weight, tf=tf, n_out_part=n_out_part),
        grid=(s // tm,),
        in_specs=[pl.BlockSpec((tm, d), lambda i: (i, 0)),
                  _const_spec(mod9.shape),
                  _const_spec(w_in.shape),
                  _const_spec(w_out.shape),
                  _const_spec(g.shape),
                  _const_spec(b.shape)],
        out_specs=pl.BlockSpec((tm, d), lambda i: (i, 0)),
        out_shape=jax.ShapeDtypeStruct((s, d), F32),
        scratch_shapes=[pltpu.VMEM((tm, d_ff), w_out.dtype)],
        compiler_params=pltpu.CompilerParams(dimension_semantics=("arbitrary",),
                                             vmem_limit_bytes=VMEM_LIMIT),
        name=f"ffn{sub}",
    )(x, mod9, w_in, w_out, g, b)


def _mix_in_kernel(x_ref, mod_ref, w_ref, rope_ref, u_ref, qt_ref, k_ref, vt_ref, kn_ref, cos_ref, sin_ref, *,
                   sub, cw, aw, q_scale, n_part):
    tm = x_ref.shape[0]
    tp = tm // n_part
    tk = vt_ref.shape[3]
    freq = rope_ref[0:1, :]

    @pl.when(pl.program_id(0) == 0)
    def _in_tile_angles():
        ang = lax.broadcasted_iota(jnp.int32, (tm, LANES), 0).astype(F32) * freq
        cos_ref[...] = jnp.cos(ang)
        sin_ref[...] = jnp.sin(ang)
        kn_ref[...] = jnp.zeros_like(kn_ref)

    comp0 = lax.broadcasted_iota(jnp.int32, (tp, LANES), 1) < DIFF_HEAD_DIM
    projs = []
    for p in range(n_part):
        h = _modulate(x_ref[p * tp:(p + 1) * tp, :], mod_ref, sub).astype(w_ref.dtype)
        projs.append(jnp.dot(h, w_ref[...], preferred_element_type=F32))

    ang0 = (pl.program_id(0) * tm).astype(F32) * freq
    c0, s0 = jnp.cos(ang0), jnp.sin(ang0)
    half = ROT_DIM // 2
    q0, k0, v0 = 2 * cw, 2 * cw + aw, 2 * cw + 2 * aw
    for p, proj in enumerate(projs):
        rows = slice(p * tp, (p + 1) * tp)
        u_ref[rows, :] = proj[:, :cw] * _sigmoid(proj[:, cw:2 * cw])
        rc = c0 * cos_ref[rows, :] - s0 * sin_ref[rows, :]
        sin = s0 * cos_ref[rows, :] + c0 * sin_ref[rows, :]
        rs1, rs2 = sin * rope_ref[1:2, :], sin * rope_ref[2:3, :]

        def rope(t):
            return t * rc + pltpu.roll(t, LANES - half, 1) * rs1 + pltpu.roll(t, half, 1) * rs2

        for g in range(aw // LANES):
            sl = slice(g * LANES, (g + 1) * LANES)
            qt_ref[g, :, rows] = (rope(proj[:, q0 + g * LANES:q0 + (g + 1) * LANES]) * q_scale).T.astype(BF16)
            kb = rope(proj[:, k0 + g * LANES:k0 + (g + 1) * LANES]).astype(BF16)
            k_ref[rows, sl] = kb
            kf = kb.astype(F32)
            sq = kf * kf
            n0 = jnp.max(jnp.sum(jnp.where(comp0, sq, 0.0), axis=1, keepdims=True), axis=0, keepdims=True)
            n1 = jnp.max(jnp.sum(jnp.where(comp0, 0.0, sq), axis=1, keepdims=True), axis=0, keepdims=True)
            kn_ref[g:g + 1, :] = jnp.maximum(kn_ref[g:g + 1, :], jnp.where(comp0[0:1, :], n0, n1))
            for c in range(tp // tk):
                vt_ref[g, p * (tp // tk) + c] = (
                    proj[c * tk:(c + 1) * tk, v0 + g * LANES:v0 + (g + 1) * LANES].T.astype(BF16))


def _mix_in(x, mod9, w, rope_rows, *, cw, aw, q_scale, tm, tk, n_part):
    s, d = x.shape
    n_heads = aw // LANES
    assert (tm // n_part) % tk == 0 and n_heads <= SUBLANES
    row = lambda i: (i, 0)
    return pl.pallas_call(
        functools.partial(_mix_in_kernel, sub=1, cw=cw, aw=aw, q_scale=q_scale, n_part=n_part),
        grid=(s // tm,),
        in_specs=[pl.BlockSpec((tm, d), row),
                  _const_spec(mod9.shape),
                  _const_spec(w.shape),
                  _const_spec(rope_rows.shape)],
        out_specs=[pl.BlockSpec((tm, cw), row),
                   pl.BlockSpec((n_heads, LANES, tm), lambda i: (0, 0, i)),
                   pl.BlockSpec((tm, aw), row),
                   pl.BlockSpec((n_heads, tm // tk, LANES, tk), lambda i: (0, i, 0, 0)),
                   pl.BlockSpec((SUBLANES, LANES), lambda i: (0, 0))],
        out_shape=[jax.ShapeDtypeStruct((s, cw), F32),
                   jax.ShapeDtypeStruct((n_heads, LANES, s), BF16),
                   jax.ShapeDtypeStruct((s, aw), BF16),
                   jax.ShapeDtypeStruct((n_heads, s // tk, LANES, tk), BF16),
                   jax.ShapeDtypeStruct((SUBLANES, LANES), F32)],
        scratch_shapes=[pltpu.VMEM((tm, LANES), F32),
                        pltpu.VMEM((tm, LANES), F32)],
        compiler_params=pltpu.CompilerParams(dimension_semantics=("arbitrary",),
                                             vmem_limit_bytes=VMEM_LIMIT),
        name="mix_in",
    )(x, mod9, w, rope_rows)


def _dependent_zero(v):
    r, c = v.shape
    folded = jnp.sum(v.reshape(r // SUBLANES, SUBLANES, c), axis=0)
    folded = sum(folded[:, g * LANES:(g + 1) * LANES] for g in range(c // LANES))
    bits = lax.bitcast_convert_type(folded[0:1, :], jnp.uint32)
    return lax.bitcast_convert_type((bits >> 16) >> 16, F32)


def _conv_tile(i, n_tiles, prev_ref, cur_ref, next_ref, w_ref, cb_ref, g_ref, b_ref, o_ref, ext_ref, sh_ref, y_ref,
               rows):
    tm, cw = cur_ref.shape
    ext_ref[0:CONV_HALO, :] = jnp.where(i > 0, prev_ref[...], 0.0)
    ext_ref[CONV_HALO:CONV_HALO + tm, :] = cur_ref[...]
    ext_ref[CONV_HALO + tm:, :] = jnp.where(i < n_tiles - 1, next_ref[...], 0.0)
    span = sh_ref.shape[1]
    for b in range(SUBLANES):
        sh_ref[b] = ext_ref[b:b + span, :]
    base = CONV_HALO - CONV_PAD
    done = []
    for lc in range(cw // LANES):
        ls = slice(lc * LANES, (lc + 1) * LANES)
        for rc in range(tm // rows):
            r0 = rc * rows
            acc = jnp.zeros((rows, LANES), F32)
            for t in range(CONV_KERNEL):
                off = base + t
                a0 = r0 + SUBLANES * (off // SUBLANES)
                acc = acc + sh_ref[off % SUBLANES, a0:a0 + rows, ls] * w_ref[t:t + 1, ls]
            y_ref[r0:r0 + rows, ls] = acc
            done.append(_dependent_zero(acc))
    y = _layer_norm(y_ref[...] + cb_ref[...], g_ref[...], b_ref[...])
    y = y * _sigmoid(y)
    o_ref[...] = y.astype(BF16)
    done.append(_dependent_zero(y))
    return done


def _conv_specs(s, cw, tm, tile_index):
    nh = tm // CONV_HALO
    last = s // CONV_HALO - 1
    return [pl.BlockSpec((CONV_HALO, cw), lambda *g: (jnp.maximum(tile_index(*g) * nh - 1, 0), 0)),
            pl.BlockSpec((tm, cw), lambda *g: (tile_index(*g), 0)),
            pl.BlockSpec((CONV_HALO, cw), lambda *g: (jnp.minimum((tile_index(*g) + 1) * nh, last), 0))]


def _conv_scratch(tm, cw):
    return [pltpu.VMEM((tm + 2 * CONV_HALO, cw), F32),
            pltpu.VMEM((SUBLANES, tm + 2 * CONV_HALO - SUBLANES, cw), F32),
            pltpu.VMEM((tm, cw), F32)]


def _mixer_kernel(qt_ref, k_ref, vt_ref, kn_ref, lq1_ref, lk1_ref, lq2_ref, lk2_ref, g_ref,
                  up_ref, uc_ref, un_ref, cw_ref, cb_ref, cg_ref, cbeta_ref,
                  o_ref, conv_ref,
                  rhs_ref, acc_ref, kmax_ref, ext_ref, sh_ref, y_ref, *, tk, conv_rows, lam_init):
    step = pl.program_id(0) * pl.num_programs(1) + pl.program_id(1)
    n_steps = pl.num_programs(0) * pl.num_programs(1)

    hd2, tq = qt_ref.shape[1], qt_ref.shape[2]
    n = 2 * tq
    n_kv = k_ref.shape[0] // tk
    qt = qt_ref[0]
    row = lax.broadcasted_iota(jnp.int32, qt.shape, 0)
    zero = jnp.zeros_like(qt)
    rhs_ref[:, :tq] = jnp.where(row < DIFF_HEAD_DIM, qt, zero)
    rhs_ref[:, tq:] = jnp.where(row >= DIFF_HEAD_DIM, qt, zero)

    def k_block(j):
        return k_ref[pl.ds(pl.multiple_of(j * tk, tk), tk), :]

    @pl.when(pl.program_id(1) == 0)
    def _key_norm_bound():
        kn = kn_ref[pl.ds(pl.program_id(0), 1), :]
        col = lax.broadcasted_iota(jnp.int32, (1, n), 1)
        kmax_ref[...] = jnp.sqrt(jnp.where(col < tq, kn[:, 0:1], kn[:, DIFF_HEAD_DIM:DIFF_HEAD_DIM + 1]))

    conv_done = _conv_tile(step, n_steps, up_ref, uc_ref, un_ref, cw_ref, cb_ref, cg_ref, cbeta_ref, conv_ref,
                           ext_ref, sh_ref, y_ref, conv_rows)
    assert CONV_FIRST_BLOCK + pl.cdiv(len(conv_done), CONV_PIECES_PER_BLOCK) <= n_kv

    r32 = rhs_ref[...].astype(F32)
    qn = jnp.sqrt(jnp.sum(r32 * r32, axis=0, keepdims=True))
    m = qn * kmax_ref[...] * SHIFT_SLACK
    l8 = jnp.zeros((8, n), F32)
    acc = jnp.zeros((hd2, n), F32)
    e_prev = None
    for j in range(n_kv):
        s = jnp.dot(k_ref[j * tk:(j + 1) * tk, :], rhs_ref[...], preferred_element_type=F32)
        if e_prev is not None:
            acc = acc + jnp.dot(vt_ref[0, j - 1], e_prev, preferred_element_type=F32)
        if j >= n_kv:
            for piece in conv_done[:CONV_PIECES_PER_BLOCK]:
                m = m + jnp.concatenate([piece] * (n // LANES), axis=1)
            del conv_done[:CONV_PIECES_PER_BLOCK]
        e = jnp.exp2(s - m)
        l8 = l8 + jnp.sum(e.reshape(tk // 8, 8, n), axis=0)
        e_prev = e.astype(BF16)
    acc = acc + jnp.dot(vt_ref[0, n_kv - 1], e_prev, preferred_element_type=F32)
    l = jnp.sum(l8, axis=0, keepdims=True)

    def finish(acc, l):
        o = acc * (1.0 / l)
        lam = (jnp.exp(jnp.sum(lq1_ref[...] * lk1_ref[...])) - jnp.exp(jnp.sum(lq2_ref[...] * lk2_ref[...]))
               + lam_init)
        o = o[:, :tq] - lam * o[:, tq:]
        ms = jnp.mean(o * o, axis=0, keepdims=True)
        o = o * lax.rsqrt(ms + LN_EPS) * g_ref[...] * (1.0 - lam_init)
        o_ref[...] = o.T.astype(BF16)

    finish(acc, l)

    @pl.when(jnp.logical_not(jnp.min(l) >= L_FLOOR))
    def _running_max_fallback():
        acc_ref[...] = jnp.zeros_like(acc_ref)

        def body(j, carry):
            m_run, l_run = carry
            s = jnp.dot(k_block(j), rhs_ref[...], preferred_element_type=F32)
            m_new = jnp.maximum(m_run, jnp.max(s, axis=0, keepdims=True))
            alpha = jnp.exp2(m_run - m_new)
            e = jnp.exp2(s - m_new)
            pv = jnp.dot(vt_ref[0, j], e.astype(BF16), preferred_element_type=F32)
            acc_ref[...] = alpha * acc_ref[...] + pv
            return m_new, alpha * l_run + jnp.sum(e, axis=0, keepdims=True)

        init = (jnp.full((1, n), -jnp.inf, F32), jnp.zeros((1, n), F32))
        _, l_run = lax.fori_loop(0, n_kv, body, init)
        finish(acc_ref[...], l_run)


def _mixer(qt, k, vt, kn, lq1, lk1, lq2, lk2, g_col, u, conv_w, conv_b, conv_g, conv_beta, *, tq, tk, conv_rows,
           lam_init):
    n_heads, hd2, s = qt.shape
    cw = u.shape[1]
    nq = s // tq
    tc = s // (n_heads * nq)
    assert tc % conv_rows == 0 and tc % CONV_HALO == 0
    lam_spec = _const_spec(lq1.shape)
    tile = lambda h, i: h * nq + i
    return pl.pallas_call(
        functools.partial(_mixer_kernel, tk=tk, conv_rows=conv_rows, lam_init=lam_init),
        grid=(n_heads, nq),
        in_specs=[pl.BlockSpec((1, hd2, tq), lambda h, i: (h, 0, i)),
                  pl.BlockSpec((s, hd2), lambda h, i: (0, h)),
                  pl.BlockSpec((1, s // tk, hd2, tk), lambda h, i: (h, 0, 0, 0)),
                  _const_spec(kn.shape),
                  lam_spec, lam_spec, lam_spec, lam_spec,
                  _const_spec(g_col.shape),
                  *_conv_specs(s, cw, tc, tile),
                  _const_spec(conv_w.shape),
                  _const_spec(conv_b.shape),
                  _const_spec(conv_g.shape),
                  _const_spec(conv_beta.shape)],
        out_specs=[pl.BlockSpec((tq, hd2), lambda h, i: (i, h)),
                   pl.BlockSpec((tc, cw), lambda h, i: (tile(h, i), 0))],
        out_shape=[jax.ShapeDtypeStruct((s, n_heads * hd2), BF16),
                   jax.ShapeDtypeStruct((s, cw), BF16)],
        scratch_shapes=[pltpu.VMEM((hd2, 2 * tq), BF16),
                        pltpu.VMEM((hd2, 2 * tq), F32),
                        pltpu.VMEM((1, 2 * tq), F32),
                        *_conv_scratch(tc, cw)],
        compiler_params=pltpu.CompilerParams(dimension_semantics=("arbitrary", "arbitrary"),
                                             vmem_limit_bytes=VMEM_LIMIT),
        name="mixer",
    )(qt, k, vt, kn, lq1, lk1, lq2, lk2, g_col, u, u, u, conv_w, conv_b, conv_g, conv_beta)


def _mix_out_kernel(conv_ref, attn_ref, x_ref, mod_ref, w_ref, g_ref, b_ref, o_ref, *, sub, n_part):
    cw = conv_ref.shape[1]
    tp = x_ref.shape[0] // n_part
    gate_c = mod_ref[3 * sub + 2:3 * sub + 3, :]
    ys = []
    for p in range(n_part):
        rows = slice(p * tp, (p + 1) * tp)
        ys.append(jnp.dot(conv_ref[rows, :], w_ref[:cw, :], preferred_element_type=F32)
                  + jnp.dot(attn_ref[rows, :], w_ref[cw:, :], preferred_element_type=F32))
    for p in range(n_part):
        rows = slice(p * tp, (p + 1) * tp)
        z = ALPHA * x_ref[rows, :] + (1.0 + gate_c) * ys[p]
        o_ref[rows, :] = _layer_norm(z, g_ref[...], b_ref[...])


def _mix_out(conv, attn, x, mod9, w, g, b, *, tm, n_part):
    s, d = x.shape
    row = lambda i: (i, 0)
    return pl.pallas_call(
        functools.partial(_mix_out_kernel, sub=1, n_part=n_part),
        grid=(s // tm,),
        in_specs=[pl.BlockSpec((tm, conv.shape[1]), row),
                  pl.BlockSpec((tm, attn.shape[1]), row),
                  pl.BlockSpec((tm, d), row),
                  _const_spec(mod9.shape),
                  _const_spec(w.shape),
                  _const_spec(g.shape),
                  _const_spec(b.shape)],
        out_specs=pl.BlockSpec((tm, d), row),
        out_shape=jax.ShapeDtypeStruct((s, d), F32),
        compiler_params=pltpu.CompilerParams(dimension_semantics=("arbitrary",),
                                             vmem_limit_bytes=VMEM_LIMIT),
        name="mix_out",
    )(conv, attn, x, mod9, w, g, b)


def _rope_rows():
    inv_freq = ROPE_THETA ** (-jnp.arange(0, ROT_DIM, 2, dtype=F32) / ROT_DIM)
    half = ROT_DIM // 2
    zeros_h = jnp.zeros((half,), F32)
    zeros_p = jnp.zeros((DIFF_HEAD_DIM - ROT_DIM,), F32)
    ones_h = jnp.ones((half,), F32)
    reps = LANES // DIFF_HEAD_DIM
    freq = jnp.tile(jnp.concatenate([inv_freq, inv_freq, zeros_p]), reps)
    neg_first = jnp.tile(jnp.concatenate([-ones_h, zeros_h, zeros_p]), reps)
    pos_second = jnp.tile(jnp.concatenate([zeros_h, ones_h, zeros_p]), reps)
    return jnp.stack([freq, neg_first, pos_second])


def kernel(x, c, w_ada, b_ada, ffn1_w_in, ffn1_w_out, ln1_g, ln1_b, mix_w_in, conv_w, conv_b, conv_ln_g,
           conv_ln_b, lambda_q1, lambda_k1, lambda_q2, lambda_k2, subln_g, mix_w_out, ln2_g, ln2_b,
           ffn2_w_in, ffn2_w_out, ln3_g, ln3_b):
    batch, s, d = x.shape
    assert batch == 1 and w_ada.shape[0] == DEPTH == 1
    cw = conv_w.shape[2]
    aw = (mix_w_in.shape[2] - 2 * cw) // 3
    hd2 = 2 * DIFF_HEAD_DIM
    n_heads = aw // hd2
    lam_init = 0.8 - 0.6 * math.exp(-0.3 * 0)
    q_scale = math.log2(math.e) / math.sqrt(DIFF_HEAD_DIM)
    t = TILES
    for rows in (t.ffn_rows, t.proj_rows, t.attn_q, t.attn_kv):
        assert s % rows == 0

    mod9 = _ada(c.reshape(d, 1), w_ada[0], b_ada, tn=t.ada_cols).reshape(9, d)
    x0 = x[0]
    x1 = _ffn(x0, mod9, ffn1_w_in[0], ffn1_w_out[0], ln1_g, ln1_b,
              sub=0, weight=0.5, tm=t.ffn_rows, tf=t.ffn_cols, n_out_part=t.ffn_out_parts)

    u, qt, k, vt, kn = _mix_in(x1, mod9, mix_w_in[0], _rope_rows(),
                               cw=cw, aw=aw, q_scale=q_scale, tm=t.proj_rows, tk=t.attn_kv, n_part=t.proj_parts)
    attn, conv = _mixer(qt, k, vt, kn, lambda_q1, lambda_k1, lambda_q2, lambda_k2, subln_g.reshape(hd2, 1),
                        u, conv_w[0], conv_b, conv_ln_g, conv_ln_b,
                        tq=t.attn_q, tk=t.attn_kv, conv_rows=t.conv_rows, lam_init=lam_init)
    x2 = _mix_out(conv, attn, x1, mod9, mix_w_out[0].astype(BF16), ln2_g, ln2_b,
                  tm=t.proj_rows, n_part=t.proj_parts)

    x3 = _ffn(x2, mod9, ffn2_w_in[0], ffn2_w_out[0], ln3_g, ln3_b,
              sub=2, weight=0.5, tm=t.ffn_rows, tf=t.ffn_cols, n_out_part=t.ffn_out_parts)
    return x3[None]
```

```python
import functools
import math
from typing import NamedTuple

import jax
import jax.numpy as jnp
from jax import lax
from jax.experimental import pallas as pl
from jax.experimental.pallas import tpu as pltpu

F32 = jnp.float32
BF16 = jnp.bfloat16

DEPTH = 1
ALPHA = (2.0 * DEPTH) ** 0.25
LN_EPS = 1e-5
DIFF_HEAD_DIM = 64
ROT_DIM = DIFF_HEAD_DIM // 4
ROPE_THETA = 500000.0
CONV_KERNEL = 31
CONV_PAD = (CONV_KERNEL - 1) // 2
CONV_HALO = 16
LANES = 128
SUBLANES = 8
SHIFT_SLACK = 1.0 + 2.0 ** -10
L_FLOOR = 2.0 ** -80
CONV_FIRST_BLOCK = 12
CONV_PIECES_PER_BLOCK = 1
VMEM_LIMIT = 56 * 1024 * 1024


class _Tiles(NamedTuple):
    ada_cols: int = 1152
    ffn_rows: int = 512
    ffn_cols: int = 256
    ffn_out_parts: int = 2
    proj_rows: int = 1024
    proj_parts: int = 4
    attn_q: int = 512
    attn_kv: int = 256
    conv_rows: int = 32


TILES = _Tiles()


def _sigmoid(x):
    return 1.0 / (1.0 + jnp.exp(-x))


def _layer_norm(z, g, b):
    mu = jnp.mean(z, axis=-1, keepdims=True)
    zc = z - mu
    var = jnp.mean(zc * zc, axis=-1, keepdims=True)
    return zc * lax.rsqrt(var + LN_EPS) * g + b


def _modulate(x, mod_ref, sub):
    shift = mod_ref[3 * sub:3 * sub + 1, :]
    scale = mod_ref[3 * sub + 1:3 * sub + 2, :]
    return x * (1.0 + scale) + shift


def _const_spec(shape):
    return pl.BlockSpec(shape, lambda *_: (0,) * len(shape), pipeline_mode=pl.Buffered(1))


def _ada_kernel(c_ref, w_ref, b_ref, o_ref):
    c = c_ref[...]
    ca = c * _sigmoid(c)
    o_ref[...] = jnp.sum(ca * w_ref[...], axis=0, keepdims=True) + b_ref[...]


def _ada(c_col, w, b_row, tn):
    d, n = w.shape
    assert n % tn == 0 and tn % LANES == 0
    return pl.pallas_call(
        _ada_kernel,
        grid=(n // tn,),
        in_specs=[pl.BlockSpec((d, 1), lambda j: (0, 0)),
                  pl.BlockSpec((d, tn), lambda j: (0, j)),
                  pl.BlockSpec((1, tn), lambda j: (0, j))],
        out_specs=pl.BlockSpec((1, tn), lambda j: (0, j)),
        out_shape=jax.ShapeDtypeStruct((1, n), F32),
        compiler_params=pltpu.CompilerParams(dimension_semantics=("arbitrary",),
                                             vmem_limit_bytes=VMEM_LIMIT),
        name="ada",
    )(c_col, w, b_row)


def _ffn_kernel(x_ref, mod_ref, win_ref, wout_ref, g_ref, b_ref, o_ref, act_ref, *, sub, weight, tf, n_out_part):
    x = x_ref[...]
    d_ff = wout_ref.shape[0]
    h = _modulate(x, mod_ref, sub).astype(win_ref.dtype)
    for c in range(d_ff // tf):
        gate = jnp.dot(h, win_ref[:, c * tf:(c + 1) * tf], preferred_element_type=F32)
        up = jnp.dot(h, win_ref[:, d_ff + c * tf:d_ff + (c + 1) * tf], preferred_element_type=F32)
        act_ref[:, c * tf:(c + 1) * tf] = (gate * _sigmoid(gate) * up).astype(act_ref.dtype)
    gate_c = mod_ref[3 * sub + 2:3 * sub + 3, :]
    tp = x.shape[0] // n_out_part
    ys = [jnp.dot(act_ref[p * tp:(p + 1) * tp, :], wout_ref[...], preferred_element_type=F32)
          for p in range(n_out_part)]
    for p, y in enumerate(ys):
        rows = slice(p * tp, (p + 1) * tp)
        z = ALPHA * x[rows, :] + weight * (1.0 + gate_c) * y
        o_ref[rows, :] = _layer_norm(z, g_ref[...], b_ref[...])


def _ffn(x, mod9, w_in, w_out, g, b, *, sub, weight, tm, tf, n_out_part):
    s, d = x.shape
    d_ff = w_out.shape[0]
    assert s % tm == 0 and d_ff % tf == 0 and w_in.shape == (d, 2 * d_ff) and tm % (SUBLANES * n_out_part) == 0
    return pl.pallas_call(
        functools.partial(_ffn_kernel, sub=sub, weight=weight, tf=tf, n_out_part=n_out_part),
        grid=(s // tm,),
        in_specs=[pl.BlockSpec((tm, d), lambda i: (i, 0)),
                  _const_spec(mod9.shape),
                  _const_spec(w_in.shape),
                  _const_spec(w_out.shape),
                  _const_spec(g.shape),
                  _const_spec(b.shape)],
        out_specs=pl.BlockSpec((tm, d), lambda i: (i, 0)),
        out_shape=jax.ShapeDtypeStruct((s, d), F32),
        scratch_shapes=[pltpu.VMEM((tm, d_ff), w_out.dtype)],
        compiler_params=pltpu.CompilerParams(dimension_semantics=("arbitrary",),
                                             vmem_limit_bytes=VMEM_LIMIT),
        name=f"ffn{sub}",
    )(x, mod9, w_in, w_out, g, b)


def _mix_in_kernel(x_ref, mod_ref, w_ref, rope_ref, u_ref, qt_ref, k_ref, vt_ref, kn_ref, cos_ref, sin_ref, *,
                   sub, cw, aw, q_scale, n_part):
    tm = x_ref.shape[0]
    tp = tm // n_part
    tk = vt_ref.shape[3]
    freq = rope_ref[0:1, :]

    @pl.when(pl.program_id(0) == 0)
    def _in_tile_angles():
        ang = lax.broadcasted_iota(jnp.int32, (tm, LANES), 0).astype(F32) * freq
        cos_ref[...] = jnp.cos(ang)
        sin_ref[...] = jnp.sin(ang)
        kn_ref[...] = jnp.zeros_like(kn_ref)

    comp0 = lax.broadcasted_iota(jnp.int32, (tp, LANES), 1) < DIFF_HEAD_DIM
    projs = []
    for p in range(n_part):
        h = _modulate(x_ref[p * tp:(p + 1) * tp, :], mod_ref, sub).astype(w_ref.dtype)
        projs.append(jnp.dot(h, w_ref[...], preferred_element_type=F32))

    ang0 = (pl.program_id(0) * tm).astype(F32) * freq
    c0, s0 = jnp.cos(ang0), jnp.sin(ang0)
    half = ROT_DIM // 2
    q0, k0, v0 = 2 * cw, 2 * cw + aw, 2 * cw + 2 * aw
    for p, proj in enumerate(projs):
        rows = slice(p * tp, (p + 1) * tp)
        u_ref[rows, :] = proj[:, :cw] * _sigmoid(proj[:, cw:2 * cw])
        rc = c0 * cos_ref[rows, :] - s0 * sin_ref[rows, :]
        sin = s0 * cos_ref[rows, :] + c0 * sin_ref[rows, :]
        rs1, rs2 = sin * rope_ref[1:2, :], sin * rope_ref[2:3, :]

        def rope(t):
            return t * rc + pltpu.roll(t, LANES - half, 1) * rs1 + pltpu.roll(t, half, 1) * rs2

        for g in range(aw // LANES):
            sl = slice(g * LANES, (g + 1) * LANES)
            qt_ref[g, :, rows] = (rope(proj[:, q0 + g * LANES:q0 + (g + 1) * LANES]) * q_scale).T.astype(BF16)
            kb = rope(proj[:, k0 + g * LANES:k0 + (g + 1) * LANES]).astype(BF16)
            k_ref[rows, sl] = kb
            kf = kb.astype(F32)
            sq = kf * kf
            n0 = jnp.max(jnp.sum(jnp.where(comp0, sq, 0.0), axis=1, keepdims=True), axis=0, keepdims=True)
            n1 = jnp.max(jnp.sum(jnp.where(comp0, 0.0, sq), axis=1, keepdims=True), axis=0, keepdims=True)
            kn_ref[g:g + 1, :] = jnp.maximum(kn_ref[g:g + 1, :], jnp.where(comp0[0:1, :], n0, n1))
            for c in range(tp // tk):
                vt_ref[g, p * (tp // tk) + c] = (
                    proj[c * tk:(c + 1) * tk, v0 + g * LANES:v0 + (g + 1) * LANES].T.astype(BF16))


def _mix_in(x, mod9, w, rope_rows, *, cw, aw, q_scale, tm, tk, n_part):
    s, d = x.shape
    n_heads = aw // LANES
    assert (tm // n_part) % tk == 0 and n_heads <= SUBLANES
    row = lambda i: (i, 0)
    return pl.pallas_call(
        functools.partial(_mix_in_kernel, sub=1, cw=cw, aw=aw, q_scale=q_scale, n_part=n_part),
        grid=(s // tm,),
        in_specs=[pl.BlockSpec((tm, d), row),
                  _const_spec(mod9.shape),
                  _const_spec(w.shape),
                  _const_spec(rope_rows.shape)],
        out_specs=[pl.BlockSpec((tm, cw), row),
                   pl.BlockSpec((n_heads, LANES, tm), lambda i: (0, 0, i)),
                   pl.BlockSpec((tm, aw), row),
                   pl.BlockSpec((n_heads, tm // tk, LANES, tk), lambda i: (0, i, 0, 0)),
                   pl.BlockSpec((SUBLANES, LANES), lambda i: (0, 0))],
        out_shape=[jax.ShapeDtypeStruct((s, cw), F32),
                   jax.ShapeDtypeStruct((n_heads, LANES, s), BF16),
                   jax.ShapeDtypeStruct((s, aw), BF16),
                   jax.ShapeDtypeStruct((n_heads, s // tk, LANES, tk), BF16),
                   jax.ShapeDtypeStruct((SUBLANES, LANES), F32)],
        scratch_shapes=[pltpu.VMEM((tm, LANES), F32),
                        pltpu.VMEM((tm, LANES), F32)],
        compiler_params=pltpu.CompilerParams(dimension_semantics=("arbitrary",),
                                             vmem_limit_bytes=VMEM_LIMIT),
        name="mix_in",
    )(x, mod9, w, rope_rows)


def _dependent_zero(v):
    r, c = v.shape
    folded = jnp.sum(v.reshape(r // SUBLANES, SUBLANES, c), axis=0)
    folded = sum(folded[:, g * LANES:(g + 1) * LANES] for g in range(c // LANES))
    bits = lax.bitcast_convert_type(folded[0:1, :], jnp.uint32)
    return lax.bitcast_convert_type((bits >> 16) >> 16, F32)


def _conv_tile(i, n_tiles, prev_ref, cur_ref, next_ref, w_ref, cb_ref, g_ref, b_ref, o_ref, ext_ref, sh_ref, y_ref,
               rows):
    tm, cw = cur_ref.shape
    ext_ref[0:CONV_HALO, :] = jnp.where(i > 0, prev_ref[...], 0.0)
    ext_ref[CONV_HALO:CONV_HALO + tm, :] = cur_ref[...]
    ext_ref[CONV_HALO + tm:, :] = jnp.where(i < n_tiles - 1, next_ref[...], 0.0)
    span = sh_ref.shape[1]
    for b in range(SUBLANES):
        sh_ref[b] = ext_ref[b:b + span, :]
    base = CONV_HALO - CONV_PAD
    done = []
    for lc in range(cw // LANES):
        ls = slice(lc * LANES, (lc + 1) * LANES)
        for rc in range(tm // rows):
            r0 = rc * rows
            acc = jnp.zeros((rows, LANES), F32)
            for t in range(CONV_KERNEL):
                off = base + t
                a0 = r0 + SUBLANES * (off // SUBLANES)
                acc = acc + sh_ref[off % SUBLANES, a0:a0 + rows, ls] * w_ref[t:t + 1, ls]
            y_ref[r0:r0 + rows, ls] = acc
            done.append(_dependent_zero(acc))
    y = _layer_norm(y_ref[...] + cb_ref[...], g_ref[...], b_ref[...])
    y = y * _sigmoid(y)
    o_ref[...] = y.astype(BF16)
    done.append(_dependent_zero(y))
    return done


def _conv_specs(s, cw, tm, tile_index):
    nh = tm // CONV_HALO
    last = s // CONV_HALO - 1
    return [pl.BlockSpec((CONV_HALO, cw), lambda *g: (jnp.maximum(tile_index(*g) * nh - 1, 0), 0)),
            pl.BlockSpec((tm, cw), lambda *g: (tile_index(*g), 0)),
            pl.BlockSpec((CONV_HALO, cw), lambda *g: (jnp.minimum((tile_index(*g) + 1) * nh, last), 0))]


def _conv_scratch(tm, cw):
    return [pltpu.VMEM((tm + 2 * CONV_HALO, cw), F32),
            pltpu.VMEM((SUBLANES, tm + 2 * CONV_HALO - SUBLANES, cw), F32),
            pltpu.VMEM((tm, cw), F32)]


def _mixer_kernel(qt_ref, k_ref, vt_ref, kn_ref, lq1_ref, lk1_ref, lq2_ref, lk2_ref, g_ref,
                  up_ref, uc_ref, un_ref, cw_ref, cb_ref, cg_ref, cbeta_ref,
                  o_ref, conv_ref,
                  rhs_ref, acc_ref, kmax_ref, ext_ref, sh_ref, y_ref, *, tk, conv_rows, lam_init):
    step = pl.program_id(0) * pl.num_programs(1) + pl.program_id(1)
    n_steps = pl.num_programs(0) * pl.num_programs(1)

    hd2, tq = qt_ref.shape[1], qt_ref.shape[2]
    n = 2 * tq
    n_kv = k_ref.shape[0] // tk
    qt = qt_ref[0]
    row = lax.broadcasted_iota(jnp.int32, qt.shape, 0)
    zero = jnp.zeros_like(qt)
    rhs_ref[:, :tq] = jnp.where(row < DIFF_HEAD_DIM, qt, zero)
    rhs_ref[:, tq:] = jnp.where(row >= DIFF_HEAD_DIM, qt, zero)

    def k_block(j):
        return k_ref[pl.ds(pl.multiple_of(j * tk, tk), tk), :]

    @pl.when(pl.program_id(1) == 0)
    def _key_norm_bound():
        kn = kn_ref[pl.ds(pl.program_id(0), 1), :]
        col = lax.broadcasted_iota(jnp.int32, (1, n), 1)
        kmax_ref[...] = jnp.sqrt(jnp.where(col < tq, kn[:, 0:1], kn[:, DIFF_HEAD_DIM:DIFF_HEAD_DIM + 1]))

    conv_done = _conv_tile(step, n_steps, up_ref, uc_ref, un_ref, cw_ref, cb_ref, cg_ref, cbeta_ref, conv_ref,
                           ext_ref, sh_ref, y_ref, conv_rows)
    conv_done[-1:-1] = [None, None]
    assert CONV_FIRST_BLOCK + pl.cdiv(len(conv_done), CONV_PIECES_PER_BLOCK) <= n_kv

    r32 = rhs_ref[...].astype(F32)
    qn = jnp.sqrt(jnp.sum(r32 * r32, axis=0, keepdims=True))
    m = qn * kmax_ref[...] * SHIFT_SLACK
    l8 = jnp.zeros((8, n), F32)
    acc = jnp.zeros((hd2, n), F32)
    e_prev = None
    for j in range(n_kv):
        s = jnp.dot(k_ref[j * tk:(j + 1) * tk, :], rhs_ref[...], preferred_element_type=F32)
        if e_prev is not None:
            acc = acc + jnp.dot(vt_ref[0, j - 1], e_prev, preferred_element_type=F32)
        if j >= CONV_FIRST_BLOCK:
            for piece in conv_done[:CONV_PIECES_PER_BLOCK]:
                if piece is not None:
                    m = m + jnp.concatenate([piece] * (n // LANES), axis=1)
            del conv_done[:CONV_PIECES_PER_BLOCK]
        e = jnp.exp2(s - m)
        l8 = l8 + jnp.sum(e.reshape(tk // 8, 8, n), axis=0)
        e_prev = e.astype(BF16)
    acc = acc + jnp.dot(vt_ref[0, n_kv - 1], e_prev, preferred_element_type=F32)
    l = jnp.sum(l8, axis=0, keepdims=True)

    def finish(acc, l):
        o = acc * (1.0 / l)
        lam = (jnp.exp(jnp.sum(lq1_ref[...] * lk1_ref[...])) - jnp.exp(jnp.sum(lq2_ref[...] * lk2_ref[...]))
               + lam_init)
        o = o[:, :tq] - lam * o[:, tq:]
        ms = jnp.mean(o * o, axis=0, keepdims=True)
        o = o * lax.rsqrt(ms + LN_EPS) * g_ref[...] * (1.0 - lam_init)
        o_ref[...] = o.T.astype(BF16)

    finish(acc, l)

    @pl.when(jnp.logical_not(jnp.min(l) >= L_FLOOR))
    def _running_max_fallback():
        acc_ref[...] = jnp.zeros_like(acc_ref)

        def body(j, carry):
            m_run, l_run = carry
            s = jnp.dot(k_block(j), rhs_ref[...], preferred_element_type=F32)
            m_new = jnp.maximum(m_run, jnp.max(s, axis=0, keepdims=True))
            alpha = jnp.exp2(m_run - m_new)
            e = jnp.exp2(s - m_new)
            pv = jnp.dot(vt_ref[0, j], e.astype(BF16), preferred_element_type=F32)
            acc_ref[...] = alpha * acc_ref[...] + pv
            return m_new, alpha * l_run + jnp.sum(e, axis=0, keepdims=True)

        init = (jnp.full((1, n), -jnp.inf, F32), jnp.zeros((1, n), F32))
        _, l_run = lax.fori_loop(0, n_kv, body, init)
        finish(acc_ref[...], l_run)


def _mixer(qt, k, vt, kn, lq1, lk1, lq2, lk2, g_col, u, conv_w, conv_b, conv_g, conv_beta, *, tq, tk, conv_rows,
           lam_init):
    n_heads, hd2, s = qt.shape
    cw = u.shape[1]
    nq = s // tq
    tc = s // (n_heads * nq)
    assert tc % conv_rows == 0 and tc % CONV_HALO == 0
    lam_spec = _const_spec(lq1.shape)
    tile = lambda h, i: h * nq + i
    return pl.pallas_call(
        functools.partial(_mixer_kernel, tk=tk, conv_rows=conv_rows, lam_init=lam_init),
        grid=(n_heads, nq),
        in_specs=[pl.BlockSpec((1, hd2, tq), lambda h, i: (h, 0, i)),
                  pl.BlockSpec((s, hd2), lambda h, i: (0, h)),
                  pl.BlockSpec((1, s // tk, hd2, tk), lambda h, i: (h, 0, 0, 0)),
                  _const_spec(kn.shape),
                  lam_spec, lam_spec, lam_spec, lam_spec,
                  _const_spec(g_col.shape),
                  *_conv_specs(s, cw, tc, tile),
                  _const_spec(conv_w.shape),
                  _const_spec(conv_b.shape),
                  _const_spec(conv_g.shape),
                  _const_spec(conv_beta.shape)],
        out_specs=[pl.BlockSpec((tq, hd2), lambda h, i: (i, h)),
                   pl.BlockSpec((tc, cw), lambda h, i: (tile(h, i), 0))],
        out_shape=[jax.ShapeDtypeStruct((s, n_heads * hd2), BF16),
                   jax.ShapeDtypeStruct((s, cw), BF16)],
        scratch_shapes=[pltpu.VMEM((hd2, 2 * tq), BF16),
                        pltpu.VMEM((hd2, 2 * tq), F32),
                        pltpu.VMEM((1, 2 * tq), F32),
                        *_conv_scratch(tc, cw)],
        compiler_params=pltpu.CompilerParams(dimension_semantics=("arbitrary", "arbitrary"),
                                             vmem_limit_bytes=VMEM_LIMIT),
        name="mixer",
    )(qt, k, vt, kn, lq1, lk1, lq2, lk2, g_col, u, u, u, conv_w, conv_b, conv_g, conv_beta)


def _mix_out_kernel(conv_ref, attn_ref, x_ref, mod_ref, w_ref, g_ref, b_ref, o_ref, *, sub, n_part):
    cw = conv_ref.shape[1]
    tp = x_ref.shape[0] // n_part
    gate_c = mod_ref[3 * sub + 2:3 * sub + 3, :]
    ys = []
    for p in range(n_part):
        rows = slice(p * tp, (p + 1) * tp)
        ys.append(jnp.dot(conv_ref[rows, :], w_ref[:cw, :], preferred_element_type=F32)
                  + jnp.dot(attn_ref[rows, :], w_ref[cw:, :], preferred_element_type=F32))
    for p in range(n_part):
        rows = slice(p * tp, (p + 1) * tp)
        z = ALPHA * x_ref[rows, :] + (1.0 + gate_c) * ys[p]
        o_ref[rows, :] = _layer_norm(z, g_ref[...], b_ref[...])


def _mix_out(conv, attn, x, mod9, w, g, b, *, tm, n_part):
    s, d = x.shape
    row = lambda i: (i, 0)
    return pl.pallas_call(
        functools.partial(_mix_out_kernel, sub=1, n_part=n_part),
        grid=(s // tm,),
        in_specs=[pl.BlockSpec((tm, conv.shape[1]), row),
                  pl.BlockSpec((tm, attn.shape[1]), row),
                  pl.BlockSpec((tm, d), row),
                  _const_spec(mod9.shape),
                  _const_spec(w.shape),
                  _const_spec(g.shape),
                  _const_spec(b.shape)],
        out_specs=pl.BlockSpec((tm, d), row),
        out_shape=jax.ShapeDtypeStruct((s, d), F32),
        compiler_params=pltpu.CompilerParams(dimension_semantics=("arbitrary",),
                                             vmem_limit_bytes=VMEM_LIMIT),
        name="mix_out",
    )(conv, attn, x, mod9, w, g, b)


def _rope_rows():
    inv_freq = ROPE_THETA ** (-jnp.arange(0, ROT_DIM, 2, dtype=F32) / ROT_DIM)
    half = ROT_DIM // 2
    zeros_h = jnp.zeros((half,), F32)
    zeros_p = jnp.zeros((DIFF_HEAD_DIM - ROT_DIM,), F32)
    ones_h = jnp.ones((half,), F32)
    reps = LANES // DIFF_HEAD_DIM
    freq = jnp.tile(jnp.concatenate([inv_freq, inv_freq, zeros_p]), reps)
    neg_first = jnp.tile(jnp.concatenate([-ones_h, zeros_h, zeros_p]), reps)
    pos_second = jnp.tile(jnp.concatenate([zeros_h, ones_h, zeros_p]), reps)
    return jnp.stack([freq, neg_first, pos_second])


def kernel(x, c, w_ada, b_ada, ffn1_w_in, ffn1_w_out, ln1_g, ln1_b, mix_w_in, conv_w, conv_b, conv_ln_g,
           conv_ln_b, lambda_q1, lambda_k1, lambda_q2, lambda_k2, subln_g, mix_w_out, ln2_g, ln2_b,
           ffn2_w_in, ffn2_w_out, ln3_g, ln3_b):
    batch, s, d = x.shape
    assert batch == 1 and w_ada.shape[0] == DEPTH == 1
    cw = conv_w.shape[2]
    aw = (mix_w_in.shape[2] - 2 * cw) // 3
    hd2 = 2 * DIFF_HEAD_DIM
    n_heads = aw // hd2
    lam_init = 0.8 - 0.6 * math.exp(-0.3 * 0)
    q_scale = math.log2(math.e) / math.sqrt(DIFF_HEAD_DIM)
    t = TILES
    for rows in (t.ffn_rows, t.proj_rows, t.attn_q, t.attn_kv):
        assert s % rows == 0

    mod9 = _ada(c.reshape(d, 1), w_ada[0], b_ada, tn=t.ada_cols).reshape(9, d)
    x0 = x[0]
    x1 = _ffn(x0, mod9, ffn1_w_in[0], ffn1_w_out[0], ln1_g, ln1_b,
              sub=0, weight=0.5, tm=t.ffn_rows, tf=t.ffn_cols, n_out_part=t.ffn_out_parts)

    u, qt, k, vt, kn = _mix_in(x1, mod9, mix_w_in[0], _rope_rows(),
                               cw=cw, aw=aw, q_scale=q_scale, tm=t.proj_rows, tk=t.attn_kv, n_part=t.proj_parts)
    attn, conv = _mixer(qt, k, vt, kn, lambda_q1, lambda_k1, lambda_q2, lambda_k2, subln_g.reshape(hd2, 1),
                        u, conv_w[0], conv_b, conv_ln_g, conv_ln_b,
                        tq=t.attn_q, tk=t.attn_kv, conv_rows=t.conv_rows, lam_init=lam_init)
    x2 = _mix_out(conv, attn, x1, mod9, mix_w_out[0].astype(BF16), ln2_g, ln2_b,
                  tm=t.proj_rows, n_part=t.proj_parts)

    x3 = _ffn(x2, mod9, ffn2_w_in[0], ffn2_w_out[0], ln3_g, ln3_b,
              sub=2, weight=0.5, tm=t.ffn_rows, tf=t.ffn_cols, n_out_part=t.ffn_out_parts)
    return x3[None]
```

```python
import functools
import math
from typing import NamedTuple

import jax
import jax.numpy as jnp
from jax import lax
from jax.experimental import pallas as pl
from jax.experimental.pallas import tpu as pltpu

F32 = jnp.float32
BF16 = jnp.bfloat16

DEPTH = 1
ALPHA = (2.0 * DEPTH) ** 0.25
LN_EPS = 1e-5
DIFF_HEAD_DIM = 64
ROT_DIM = DIFF_HEAD_DIM // 4
ROPE_THETA = 500000.0
CONV_KERNEL = 31
CONV_PAD = (CONV_KERNEL - 1) // 2
CONV_HALO = 16
LANES = 128
SUBLANES = 8
SHIFT_SLACK = 1.0 + 2.0 ** -10
L_FLOOR = 2.0 ** -80
CONV_FIRST_BLOCK = 12
CONV_PIECES_PER_BLOCK = 1
VMEM_LIMIT = 56 * 1024 * 1024


class _Tiles(NamedTuple):
    ada_cols: int = 1152
    ffn_rows: int = 512
    ffn_cols: int = 256
    ffn_out_parts: int = 2
    proj_rows: int = 1024
    proj_parts: int = 4
    attn_q: int = 512
    attn_kv: int = 256
    conv_rows: int = 32


TILES = _Tiles()


def _sigmoid(x):
    return 1.0 / (1.0 + jnp.exp(-x))


def _layer_norm(z, g, b):
    mu = jnp.mean(z, axis=-1, keepdims=True)
    zc = z - mu
    var = jnp.mean(zc * zc, axis=-1, keepdims=True)
    return zc * lax.rsqrt(var + LN_EPS) * g + b


def _modulate(x, mod_ref, sub):
    shift = mod_ref[3 * sub:3 * sub + 1, :]
    scale = mod_ref[3 * sub + 1:3 * sub + 2, :]
    return x * (1.0 + scale) + shift


def _const_spec(shape):
    return pl.BlockSpec(shape, lambda *_: (0,) * len(shape), pipeline_mode=pl.Buffered(1))


def _ada_kernel(c_ref, w_ref, b_ref, o_ref):
    c = c_ref[...]
    ca = c * _sigmoid(c)
    o_ref[...] = jnp.sum(ca * w_ref[...], axis=0, keepdims=True) + b_ref[...]


def _ada(c_col, w, b_row, tn):
    d, n = w.shape
    assert n % tn == 0 and tn % LANES == 0
    return pl.pallas_call(
        _ada_kernel,
        grid=(n // tn,),
        in_specs=[pl.BlockSpec((d, 1), lambda j: (0, 0)),
                  pl.BlockSpec((d, tn), lambda j: (0, j)),
                  pl.BlockSpec((1, tn), lambda j: (0, j))],
        out_specs=pl.BlockSpec((1, tn), lambda j: (0, j)),
        out_shape=jax.ShapeDtypeStruct((1, n), F32),
        compiler_params=pltpu.CompilerParams(dimension_semantics=("arbitrary",),
                                             vmem_limit_bytes=VMEM_LIMIT),
        name="ada",
    )(c_col, w, b_row)


def _ffn_kernel(x_ref, mod_ref, win_ref, wout_ref, g_ref, b_ref, o_ref, act_ref, *, sub, weight, tf, n_out_part):
    x = x_ref[...]
    d_ff = wout_ref.shape[0]
    h = _modulate(x, mod_ref, sub).astype(win_ref.dtype)
    for c in range(d_ff // tf):
        gate = jnp.dot(h, win_ref[:, c * tf:(c + 1) * tf], preferred_element_type=F32)
        up = jnp.dot(h, win_ref[:, d_ff + c * tf:d_ff + (c + 1) * tf], preferred_element_type=F32)
        act_ref[:, c * tf:(c + 1) * tf] = (gate * _sigmoid(gate) * up).astype(act_ref.dtype)
    gate_c = mod_ref[3 * sub + 2:3 * sub + 3, :]
    tp = x.shape[0] // n_out_part
    ys = [jnp.dot(act_ref[p * tp:(p + 1) * tp, :], wout_ref[...], preferred_element_type=F32)
          for p in range(n_out_part)]
    for p, y in enumerate(ys):
        rows = slice(p * tp, (p + 1) * tp)
        z = ALPHA * x[rows, :] + weight * (1.0 + gate_c) * y
        o_ref[rows, :] = _layer_norm(z, g_ref[...], b_ref[...])


def _ffn(x, mod9, w_in, w_out, g, b, *, sub, weight, tm, tf, n_out_part):
    s, d = x.shape
    d_ff = w_out.shape[0]
    assert s % tm == 0 and d_ff % tf == 0 and w_in.shape == (d, 2 * d_ff) and tm % (SUBLANES * n_out_part) == 0
    return pl.pallas_call(
        functools.partial(_ffn_kernel, sub=sub, weight=weight, tf=tf, n_out_part=n_out_part),
        grid=(s // tm,),
        in_specs=[pl.BlockSpec((tm, d), lambda i: (i, 0)),
                  _const_spec(mod9.shape),
                  _const_spec(w_in.shape),
                  _const_spec(w_out.shape),
                  _const_spec(g.shape),
                  _const_spec(b.shape)],
        out_specs=pl.BlockSpec((tm, d), lambda i: (i, 0)),
        out_shape=jax.ShapeDtypeStruct((s, d), F32),
        scratch_shapes=[pltpu.VMEM((tm, d_ff), w_out.dtype)],
        compiler_params=pltpu.CompilerParams(dimension_semantics=("arbitrary",),
                                             vmem_limit_bytes=VMEM_LIMIT),
        name=f"ffn{sub}",
    )(x, mod9, w_in, w_out, g, b)


def _mix_in_kernel(x_ref, mod_ref, w_ref, rope_ref, u_ref, qt_ref, k_ref, vt_ref, kn_ref, cos_ref, sin_ref, *,
                   sub, cw, aw, q_scale, n_part):
    tm = x_ref.shape[0]
    tp = tm // n_part
    tk = vt_ref.shape[3]
    freq = rope_ref[0:1, :]

    @pl.when(pl.program_id(0) == 0)
    def _in_tile_angles():
        ang = lax.broadcasted_iota(jnp.int32, (tm, LANES), 0).astype(F32) * freq
        cos_ref[...] = jnp.cos(ang)
        sin_ref[...] = jnp.sin(ang)
        kn_ref[...] = jnp.zeros_like(kn_ref)

    comp0 = lax.broadcasted_iota(jnp.int32, (tp, LANES), 1) < DIFF_HEAD_DIM
    projs = []
    for p in range(n_part):
        h = _modulate(x_ref[p * tp:(p + 1) * tp, :], mod_ref, sub).astype(w_ref.dtype)
        projs.append(jnp.dot(h, w_ref[...], preferred_element_type=F32))

    ang0 = (pl.program_id(0) * tm).astype(F32) * freq
    c0, s0 = jnp.cos(ang0), jnp.sin(ang0)
    half = ROT_DIM // 2
    q0, k0, v0 = 2 * cw, 2 * cw + aw, 2 * cw + 2 * aw
    for p, proj in enumerate(projs):
        rows = slice(p * tp, (p + 1) * tp)
        u_ref[rows, :] = proj[:, :cw] * _sigmoid(proj[:, cw:2 * cw])
        rc = c0 * cos_ref[rows, :] - s0 * sin_ref[rows, :]
        sin = s0 * cos_ref[rows, :] + c0 * sin_ref[rows, :]
        rs1, rs2 = sin * rope_ref[1:2, :], sin * rope_ref[2:3, :]

        def rope(t):
            return t * rc + pltpu.roll(t, LANES - half, 1) * rs1 + pltpu.roll(t, half, 1) * rs2

        for g in range(aw // LANES):
            sl = slice(g * LANES, (g + 1) * LANES)
            qt_ref[g, :, rows] = (rope(proj[:, q0 + g * LANES:q0 + (g + 1) * LANES]) * q_scale).T.astype(BF16)
            kb = rope(proj[:, k0 + g * LANES:k0 + (g + 1) * LANES]).astype(BF16)
            k_ref[rows, sl] = kb
            kf = kb.astype(F32)
            sq = kf * kf
            n0 = jnp.max(jnp.sum(jnp.where(comp0, sq, 0.0), axis=1, keepdims=True), axis=0, keepdims=True)
            n1 = jnp.max(jnp.sum(jnp.where(comp0, 0.0, sq), axis=1, keepdims=True), axis=0, keepdims=True)
            kn_ref[g:g + 1, :] = jnp.maximum(kn_ref[g:g + 1, :], jnp.where(comp0[0:1, :], n0, n1))
            for c in range(tp // tk):
                vt_ref[g, p * (tp // tk) + c] = (
                    proj[c * tk:(c + 1) * tk, v0 + g * LANES:v0 + (g + 1) * LANES].T.astype(BF16))


def _mix_in(x, mod9, w, rope_rows, *, cw, aw, q_scale, tm, tk, n_part):
    s, d = x.shape
    n_heads = aw // LANES
    assert (tm // n_part) % tk == 0 and n_heads <= SUBLANES
    row = lambda i: (i, 0)
    return pl.pallas_call(
        functools.partial(_mix_in_kernel, sub=1, cw=cw, aw=aw, q_scale=q_scale, n_part=n_part),
        grid=(s // tm,),
        in_specs=[pl.BlockSpec((tm, d), row),
                  _const_spec(mod9.shape),
                  _const_spec(w.shape),
                  _const_spec(rope_rows.shape)],
        out_specs=[pl.BlockSpec((tm, cw), row),
                   pl.BlockSpec((n_heads, LANES, tm), lambda i: (0, 0, i)),
                   pl.BlockSpec((tm, aw), row),
                   pl.BlockSpec((n_heads, tm // tk, LANES, tk), lambda i: (0, i, 0, 0)),
                   pl.BlockSpec((SUBLANES, LANES), lambda i: (0, 0))],
        out_shape=[jax.ShapeDtypeStruct((s, cw), F32),
                   jax.ShapeDtypeStruct((n_heads, LANES, s), BF16),
                   jax.ShapeDtypeStruct((s, aw), BF16),
                   jax.ShapeDtypeStruct((n_heads, s // tk, LANES, tk), BF16),
                   jax.ShapeDtypeStruct((SUBLANES, LANES), F32)],
        scratch_shapes=[pltpu.VMEM((tm, LANES), F32),
                        pltpu.VMEM((tm, LANES), F32)],
        compiler_params=pltpu.CompilerParams(dimension_semantics=("arbitrary",),
                                             vmem_limit_bytes=VMEM_LIMIT),
        name="mix_in",
    )(x, mod9, w, rope_rows)


def _dependent_zero(v):
    r, c = v.shape
    folded = jnp.sum(v.reshape(r // SUBLANES, SUBLANES, c), axis=0)
    folded = sum(folded[:, g * LANES:(g + 1) * LANES] for g in range(c // LANES))
    bits = lax.bitcast_convert_type(folded[0:1, :], jnp.uint32)
    return lax.bitcast_convert_type((bits >> 16) >> 16, F32)


def _conv_tile(i, n_tiles, prev_ref, cur_ref, next_ref, w_ref, cb_ref, g_ref, b_ref, o_ref, ext_ref, sh_ref, y_ref,
               rows):
    tm, cw = cur_ref.shape
    ext_ref[0:CONV_HALO, :] = jnp.where(i > 0, prev_ref[...], 0.0)
    ext_ref[CONV_HALO:CONV_HALO + tm, :] = cur_ref[...]
    ext_ref[CONV_HALO + tm:, :] = jnp.where(i < n_tiles - 1, next_ref[...], 0.0)
    span = sh_ref.shape[1]
    for b in range(SUBLANES):
        sh_ref[b] = ext_ref[b:b + span, :]
    base = CONV_HALO - CONV_PAD
    done = []
    for rc in range(tm // rows):
        r0 = rc * rows
        for lc in range(cw // LANES):
            ls = slice(lc * LANES, (lc + 1) * LANES)
            acc = jnp.zeros((rows, LANES), F32)
            for t in range(CONV_KERNEL):
                off = base + t
                a0 = r0 + SUBLANES * (off // SUBLANES)
                acc = acc + sh_ref[off % SUBLANES, a0:a0 + rows, ls] * w_ref[t:t + 1, ls]
            y_ref[r0:r0 + rows, ls] = acc
            done.append(_dependent_zero(acc))
    y = _layer_norm(y_ref[...] + cb_ref[...], g_ref[...], b_ref[...])
    y = y * _sigmoid(y)
    o_ref[...] = y.astype(BF16)
    done.append(_dependent_zero(y))
    return done


def _conv_specs(s, cw, tm, tile_index):
    nh = tm // CONV_HALO
    last = s // CONV_HALO - 1
    return [pl.BlockSpec((CONV_HALO, cw), lambda *g: (jnp.maximum(tile_index(*g) * nh - 1, 0), 0)),
            pl.BlockSpec((tm, cw), lambda *g: (tile_index(*g), 0)),
            pl.BlockSpec((CONV_HALO, cw), lambda *g: (jnp.minimum((tile_index(*g) + 1) * nh, last), 0))]


def _conv_scratch(tm, cw):
    return [pltpu.VMEM((tm + 2 * CONV_HALO, cw), F32),
            pltpu.VMEM((SUBLANES, tm + 2 * CONV_HALO - SUBLANES, cw), F32),
            pltpu.VMEM((tm, cw), F32)]


def _mixer_kernel(qt_ref, k_ref, vt_ref, kn_ref, lq1_ref, lk1_ref, lq2_ref, lk2_ref, g_ref,
                  up_ref, uc_ref, un_ref, cw_ref, cb_ref, cg_ref, cbeta_ref,
                  o_ref, conv_ref,
                  rhs_ref, acc_ref, kmax_ref, ext_ref, sh_ref, y_ref, *, tk, conv_rows, lam_init):
    step = pl.program_id(0) * pl.num_programs(1) + pl.program_id(1)
    n_steps = pl.num_programs(0) * pl.num_programs(1)

    hd2, tq = qt_ref.shape[1], qt_ref.shape[2]
    n = 2 * tq
    n_kv = k_ref.shape[0] // tk
    qt = qt_ref[0]
    row = lax.broadcasted_iota(jnp.int32, qt.shape, 0)
    zero = jnp.zeros_like(qt)
    rhs_ref[:, :tq] = jnp.where(row < DIFF_HEAD_DIM, qt, zero)
    rhs_ref[:, tq:] = jnp.where(row >= DIFF_HEAD_DIM, qt, zero)

    def k_block(j):
        return k_ref[pl.ds(pl.multiple_of(j * tk, tk), tk), :]

    @pl.when(pl.program_id(1) == 0)
    def _key_norm_bound():
        kn = kn_ref[pl.ds(pl.program_id(0), 1), :]
        col = lax.broadcasted_iota(jnp.int32, (1, n), 1)
        kmax_ref[...] = jnp.sqrt(jnp.where(col < tq, kn[:, 0:1], kn[:, DIFF_HEAD_DIM:DIFF_HEAD_DIM + 1]))

    conv_done = _conv_tile(step, n_steps, up_ref, uc_ref, un_ref, cw_ref, cb_ref, cg_ref, cbeta_ref, conv_ref,
                           ext_ref, sh_ref, y_ref, conv_rows)
    assert CONV_FIRST_BLOCK + pl.cdiv(len(conv_done), CONV_PIECES_PER_BLOCK) <= n_kv

    r32 = rhs_ref[...].astype(F32)
    qn = jnp.sqrt(jnp.sum(r32 * r32, axis=0, keepdims=True))
    m = qn * kmax_ref[...] * SHIFT_SLACK
    l8 = jnp.zeros((8, n), F32)
    acc = jnp.zeros((hd2, n), F32)
    e_prev = None
    for j in range(n_kv):
        s = jnp.dot(k_ref[j * tk:(j + 1) * tk, :], rhs_ref[...], preferred_element_type=F32)
        if e_prev is not None:
            acc = acc + jnp.dot(vt_ref[0, j - 1], e_prev, preferred_element_type=F32)
        if j >= CONV_FIRST_BLOCK:
            for piece in conv_done[:CONV_PIECES_PER_BLOCK]:
                m = m + jnp.concatenate([piece] * (n // LANES), axis=1)
            del conv_done[:CONV_PIECES_PER_BLOCK]
        e = jnp.exp2(s - m)
        l8 = l8 + jnp.sum(e.reshape(tk // 8, 8, n), axis=0)
        e_prev = e.astype(BF16)
    acc = acc + jnp.dot(vt_ref[0, n_kv - 1], e_prev, preferred_element_type=F32)
    l = jnp.sum(l8, axis=0, keepdims=True)

    def finish(acc, l):
        o = acc * (1.0 / l)
        lam = (jnp.exp(jnp.sum(lq1_ref[...] * lk1_ref[...])) - jnp.exp(jnp.sum(lq2_ref[...] * lk2_ref[...]))
               + lam_init)
        o = o[:, :tq] - lam * o[:, tq:]
        ms = jnp.mean(o * o, axis=0, keepdims=True)
        o = o * lax.rsqrt(ms + LN_EPS) * g_ref[...] * (1.0 - lam_init)
        o_ref[...] = o.T.astype(BF16)

    finish(acc, l)

    @pl.when(jnp.logical_not(jnp.min(l) >= L_FLOOR))
    def _running_max_fallback():
        acc_ref[...] = jnp.zeros_like(acc_ref)

        def body(j, carry):
            m_run, l_run = carry
            s = jnp.dot(k_block(j), rhs_ref[...], preferred_element_type=F32)
            m_new = jnp.maximum(m_run, jnp.max(s, axis=0, keepdims=True))
            alpha = jnp.exp2(m_run - m_new)
            e = jnp.exp2(s - m_new)
            pv = jnp.dot(vt_ref[0, j], e.astype(BF16), preferred_element_type=F32)
            acc_ref[...] = alpha * acc_ref[...] + pv
            return m_new, alpha * l_run + jnp.sum(e, axis=0, keepdims=True)

        init = (jnp.full((1, n), -jnp.inf, F32), jnp.zeros((1, n), F32))
        _, l_run = lax.fori_loop(0, n_kv, body, init)
        finish(acc_ref[...], l_run)


def _mixer(qt, k, vt, kn, lq1, lk1, lq2, lk2, g_col, u, conv_w, conv_b, conv_g, conv_beta, *, tq, tk, conv_rows,
           lam_init):
    n_heads, hd2, s = qt.shape
    cw = u.shape[1]
    nq = s // tq
    tc = s // (n_heads * nq)
    assert tc % conv_rows == 0 and tc % CONV_HALO == 0
    lam_spec = _const_spec(lq1.shape)
    tile = lambda h, i: h * nq + i
    return pl.pallas_call(
        functools.partial(_mixer_kernel, tk=tk, conv_rows=conv_rows, lam_init=lam_init),
        grid=(n_heads, nq),
        in_specs=[pl.BlockSpec((1, hd2, tq), lambda h, i: (h, 0, i)),
                  pl.BlockSpec((s, hd2), lambda h, i: (0, h)),
                  pl.BlockSpec((1, s // tk, hd2, tk), lambda h, i: (h, 0, 0, 0)),
                  _const_spec(kn.shape),
                  lam_spec, lam_spec, lam_spec, lam_spec,
                  _const_spec(g_col.shape),
                  *_conv_specs(s, cw, tc, tile),
                  _const_spec(conv_w.shape),
                  _const_spec(conv_b.shape),
                  _const_spec(conv_g.shape),
                  _const_spec(conv_beta.shape)],
        out_specs=[pl.BlockSpec((tq, hd2), lambda h, i: (i, h)),
                   pl.BlockSpec((tc, cw), lambda h, i: (tile(h, i), 0))],
        out_shape=[jax.ShapeDtypeStruct((s, n_heads * hd2), BF16),
                   jax.ShapeDtypeStruct((s, cw), BF16)],
        scratch_shapes=[pltpu.VMEM((hd2, 2 * tq), BF16),
                        pltpu.VMEM((hd2, 2 * tq), F32),
                        pltpu.VMEM((1, 2 * tq), F32),
                        *_conv_scratch(tc, cw)],
        compiler_params=pltpu.CompilerParams(dimension_semantics=("arbitrary", "arbitrary"),
                                             vmem_limit_bytes=VMEM_LIMIT),
        name="mixer",
    )(qt, k, vt, kn, lq1, lk1, lq2, lk2, g_col, u, u, u, conv_w, conv_b, conv_g, conv_beta)


def _mix_out_kernel(conv_ref, attn_ref, x_ref, mod_ref, w_ref, g_ref, b_ref, o_ref, *, sub, n_part):
    cw = conv_ref.shape[1]
    tp = x_ref.shape[0] // n_part
    gate_c = mod_ref[3 * sub + 2:3 * sub + 3, :]
    ys = []
    for p in range(n_part):
        rows = slice(p * tp, (p + 1) * tp)
        ys.append(jnp.dot(conv_ref[rows, :], w_ref[:cw, :], preferred_element_type=F32)
                  + jnp.dot(attn_ref[rows, :], w_ref[cw:, :], preferred_element_type=F32))
    for p in range(n_part):
        rows = slice(p * tp, (p + 1) * tp)
        z = ALPHA * x_ref[rows, :] + (1.0 + gate_c) * ys[p]
        o_ref[rows, :] = _layer_norm(z, g_ref[...], b_ref[...])


def _mix_out(conv, attn, x, mod9, w, g, b, *, tm, n_part):
    s, d = x.shape
    row = lambda i: (i, 0)
    return pl.pallas_call(
        functools.partial(_mix_out_kernel, sub=1, n_part=n_part),
        grid=(s // tm,),
        in_specs=[pl.BlockSpec((tm, conv.shape[1]), row),
                  pl.BlockSpec((tm, attn.shape[1]), row),
                  pl.BlockSpec((tm, d), row),
                  _const_spec(mod9.shape),
                  _const_spec(w.shape),
                  _const_spec(g.shape),
                  _const_spec(b.shape)],
        out_specs=pl.BlockSpec((tm, d), row),
        out_shape=jax.ShapeDtypeStruct((s, d), F32),
        compiler_params=pltpu.CompilerParams(dimension_semantics=("arbitrary",),
                                             vmem_limit_bytes=VMEM_LIMIT),
        name="mix_out",
    )(conv, attn, x, mod9, w, g, b)


def _rope_rows():
    inv_freq = ROPE_THETA ** (-jnp.arange(0, ROT_DIM, 2, dtype=F32) / ROT_DIM)
    half = ROT_DIM // 2
    zeros_h = jnp.zeros((half,), F32)
    zeros_p = jnp.zeros((DIFF_HEAD_DIM - ROT_DIM,), F32)
    ones_h = jnp.ones((half,), F32)
    reps = LANES // DIFF_HEAD_DIM
    freq = jnp.tile(jnp.concatenate([inv_freq, inv_freq, zeros_p]), reps)
    neg_first = jnp.tile(jnp.concatenate([-ones_h, zeros_h, zeros_p]), reps)
    pos_second = jnp.tile(jnp.concatenate([zeros_h, ones_h, zeros_p]), reps)
    return jnp.stack([freq, neg_first, pos_second])


def kernel(x, c, w_ada, b_ada, ffn1_w_in, ffn1_w_out, ln1_g, ln1_b, mix_w_in, conv_w, conv_b, conv_ln_g,
           conv_ln_b, lambda_q1, lambda_k1, lambda_q2, lambda_k2, subln_g, mix_w_out, ln2_g, ln2_b,
           ffn2_w_in, ffn2_w_out, ln3_g, ln3_b):
    batch, s, d = x.shape
    assert batch == 1 and w_ada.shape[0] == DEPTH == 1
    cw = conv_w.shape[2]
    aw = (mix_w_in.shape[2] - 2 * cw) // 3
    hd2 = 2 * DIFF_HEAD_DIM
    n_heads = aw // hd2
    lam_init = 0.8 - 0.6 * math.exp(-0.3 * 0)
    q_scale = math.log2(math.e) / math.sqrt(DIFF_HEAD_DIM)
    t = TILES
    for rows in (t.ffn_rows, t.proj_rows, t.attn_q, t.attn_kv):
        assert s % rows == 0

    mod9 = _ada(c.reshape(d, 1), w_ada[0], b_ada, tn=t.ada_cols).reshape(9, d)
    x0 = x[0]
    x1 = _ffn(x0, mod9, ffn1_w_in[0], ffn1_w_out[0], ln1_g, ln1_b,
              sub=0, weight=0.5, tm=t.ffn_rows, tf=t.ffn_cols, n_out_part=t.ffn_out_parts)

    u, qt, k, vt, kn = _mix_in(x1, mod9, mix_w_in[0], _rope_rows(),
                               cw=cw, aw=aw, q_scale=q_scale, tm=t.proj_rows, tk=t.attn_kv, n_part=t.proj_parts)
    attn, conv = _mixer(qt, k, vt, kn, lambda_q1, lambda_k1, lambda_q2, lambda_k2, subln_g.reshape(hd2, 1),
                        u, conv_w[0], conv_b, conv_ln_g, conv_ln_b,
                        tq=t.attn_q, tk=t.attn_kv, conv_rows=t.conv_rows, lam_init=lam_init)
    x2 = _mix_out(conv, attn, x1, mod9, mix_w_out[0].astype(BF16), ln2_g, ln2_b,
                  tm=t.proj_rows, n_part=t.proj_parts)

    x3 = _ffn(x2, mod9, ffn2_w_in[0], ffn2_w_out[0], ln3_g, ln3_b,
              sub=2, weight=0.5, tm=t.ffn_rows, tf=t.ffn_cols, n_out_part=t.ffn_out_parts)
    return x3[None]
```
